```python
import jax, jax.numpy as jnp
from jax import lax
import numpy as np

D_MODEL = 1024
BATCH = 16
SEQ = 256
DEPTH = 2
DEC_BATCH = 2
DEC_SEQ = 4096
PAST_LEN = 256

GRID_W = 64
POS_BASE = 10000.0
MIX_W = 256
N_GROUPS = 4
GROUP_W = MIX_W // N_GROUPS
POOL_WINDOWS = (2, 4, 8, 16)
RWKV_HEADS = 4
RWKV_HEAD = MIX_W // RWKV_HEADS
RWKV_W_RANK = 64
RWKV_A_RANK = 32
RWKV_G_RANK = 64
RWKV_DECAY_SCALE = 0.606531
RWKV_GN_EPS = 64e-5
GLA_HEADS = 4
GLA_DK = 32
GLA_DV = MIX_W // GLA_HEADS
GLA_RANK = 16
GLA_CHUNK = 64
GLA_GATE_NORM = 16.0
N_BRANCH = 4
D_FF = 4 * D_MODEL
EPS = 1e-6
IN_SIZES = (MIX_W, MIX_W, 3 * MIX_W, 2 * RWKV_W_RANK, 2 * RWKV_A_RANK, RWKV_G_RANK,
            GLA_HEADS * GLA_DK, GLA_HEADS * GLA_DK, GLA_HEADS * GLA_DV, GLA_HEADS * GLA_DV,
            2 * GLA_RANK, N_BRANCH * D_MODEL)
P_IN = sum(IN_SIZES)

kernel_name = 'hybrid_pool_fourier_rwkv7_gla_diffusion_step'

F32 = jnp.float32


def _rmsnorm(x, g):
    xf = x.astype(F32)
    return xf * lax.rsqrt(jnp.mean(xf * xf, axis=-1, keepdims=True) + EPS) * g.astype(F32)


def _split_cols(u, sizes):
    out, start = [], 0
    for s in sizes:
        out.append(u[..., start:start + s])
        start += s
    return out


def _pos_embed_2d(n_tok, d):
    rows = n_tok // GRID_W
    rr, cc = jnp.meshgrid(jnp.arange(rows, dtype=F32), jnp.arange(GRID_W, dtype=F32), indexing='ij')
    rr = rr.reshape(-1)
    cc = cc.reshape(-1)
    quarter = d // 4
    omega = 1.0 / (POS_BASE ** (jnp.arange(quarter, dtype=F32) / quarter))
    ar = rr[:, None] * omega
    ac = cc[:, None] * omega
    return jnp.concatenate([jnp.sin(ar), jnp.cos(ar), jnp.sin(ac), jnp.cos(ac)], axis=-1)


def _pool_mixer(z, w, scale):
    b_, L, _ = z.shape
    csum = jnp.concatenate([jnp.zeros((b_, 1, MIX_W), F32), jnp.cumsum(z, axis=1)], axis=1)
    t = jnp.arange(L)
    parts = []
    for gi, win in enumerate(POOL_WINDOWS):
        sl = slice(gi * GROUP_W, (gi + 1) * GROUP_W)
        lo = jnp.clip(t - win // 2, 0, L - 1)
        hi = jnp.clip(t + (win - win // 2) - 1, 0, L - 1)
        s = jnp.take(csum[..., sl], hi + 1, axis=1) - jnp.take(csum[..., sl], lo, axis=1)
        cnt = (hi - lo + 1).astype(F32)[None, :, None]
        parts.append(s / cnt - z[..., sl])
    pooled = jnp.stack(parts, axis=2)
    y = jnp.einsum('blgc,gcd->blgd', pooled, w)
    return y.reshape(b_, L, MIX_W) * scale


def _fourier_mixer(z):
    b_, L, _ = z.shape
    zg = z.astype(F32).reshape(b_, L, N_GROUPS, GROUP_W)
    f = jnp.fft.fft2(zg, axes=(1, 3), norm='ortho')
    return jnp.real(f).astype(F32).reshape(b_, L, MIX_W)


def _token_shift(z):
    zp = jnp.pad(z, ((0, 0), (1, 1), (0, 0)))
    return 0.5 * (zp[:, :-2] + zp[:, 2:])


def _rwkv_scan(r, k, v, w, kk, a, s0, reverse):
    def step(S, inp):
        r_t, k_t, v_t, w_t, kk_t, a_t = inp
        sa = jnp.einsum('bhvk,bhk->bhv', S, -kk_t)
        S = S * w_t[:, :, None, :] + sa[..., None] * (kk_t * a_t)[:, :, None, :] + v_t[..., None] * k_t[:, :, None, :]
        return S, jnp.einsum('bhvk,bhk->bhv', S, r_t)
    xs = tuple(jnp.swapaxes(t, 0, 1) for t in (r, k, v, w, kk, a))
    s_fin, o = lax.scan(step, s0, xs, reverse=reverse)
    return jnp.swapaxes(o, 0, 1), s_fin


def _gla_chunked(q, k, v, log_a, s0):
    b_, L, H, _ = q.shape
    dv = v.shape[-1]
    n = L // GLA_CHUNK

    def chunks(t):
        return t.reshape(b_, n, GLA_CHUNK, H, t.shape[-1]).transpose(1, 0, 3, 2, 4)

    causal = jnp.tril(jnp.ones((GLA_CHUNK, GLA_CHUNK), dtype=bool))[None, None, :, :, None]

    def step(S, inp):
        qc, kc, vc, gc = inp
        bcum = jnp.cumsum(gc, axis=2)
        diff = bcum[:, :, :, None, :] - bcum[:, :, None, :, :]
        decay = jnp.exp(jnp.where(causal, diff, -jnp.inf))
        att = jnp.einsum('bhtd,bhsd,bhtsd->bhts', qc, kc, decay)
        o = jnp.einsum('bhts,bhse->bhte', att, vc) + jnp.einsum('bhtd,bhde->bhte', qc * jnp.exp(bcum), S)
        b_last = bcum[:, :, -1:, :]
        S = jnp.exp(b_last[:, :, 0, :])[..., None] * S + jnp.einsum('bhsd,bhse->bhde', kc * jnp.exp(b_last - bcum), vc)
        return S, o

    s_fin, o = lax.scan(step, s0, (chunks(q), chunks(k), chunks(v), chunks(log_a)))
    return o.transpose(1, 0, 3, 2, 4).reshape(b_, L, H, dv), s_fin


def _head_groupnorm(o):
    mu = jnp.mean(o, axis=-1, keepdims=True)
    var = jnp.mean(jnp.square(o - mu), axis=-1, keepdims=True)
    return (o - mu) * lax.rsqrt(var + RWKV_GN_EPS)


def _mixer_block(h, p, s_rwkv0, s_gla0):
    b_, L, _ = h.shape
    u = jnp.einsum('bld,dp->blp', h, p['w_in']).astype(F32)
    (z_pool, z_four, z_rkv, c_w, c_a, c_g, gq, gk, gv, g_out, c_al, m_log) = _split_cols(u, IN_SIZES)

    y_a = _pool_mixer(z_pool, p['pool_w'], p['pool_scale'])

    y_b = _fourier_mixer(z_four)

    hn = (b_, L, RWKV_HEADS, RWKV_HEAD)
    z_rkv = z_rkv + p['rwkv_mu'] * (_token_shift(z_rkv) - z_rkv)
    r = z_rkv[..., :MIX_W].reshape(hn)
    k = z_rkv[..., MIX_W:2 * MIX_W].reshape(hn)
    v = z_rkv[..., 2 * MIX_W:].reshape(hn)
    kk = k * p['rwkv_kk'].reshape(RWKV_HEADS, RWKV_HEAD)
    kk = kk * lax.rsqrt(jnp.sum(kk * kk, axis=-1, keepdims=True) + EPS)
    w_log = p['rwkv_w0'] + jnp.einsum('bldr,drc->bldc', jnp.tanh(c_w.reshape(b_, L, 2, RWKV_W_RANK)), p['rwkv_bw'])
    decay = jnp.exp(-RWKV_DECAY_SCALE * jax.nn.sigmoid(w_log)).reshape(b_, L, 2, RWKV_HEADS, RWKV_HEAD)
    iclr = jax.nn.sigmoid(p['rwkv_a0'] + jnp.einsum('bldr,drc->bldc', c_a.reshape(b_, L, 2, RWKV_A_RANK), p['rwkv_ba']))
    iclr = iclr.reshape(b_, L, 2, RWKV_HEADS, RWKV_HEAD)
    k_dir = k[:, :, None] * (1.0 + (iclr - 1.0) * p['rwkv_ka'].reshape(RWKV_HEADS, RWKV_HEAD))
    o_f, s_rf = _rwkv_scan(r, k_dir[:, :, 0], v, decay[:, :, 0], kk, iclr[:, :, 0], s_rwkv0[:, 0], False)
    o_bk, s_rb = _rwkv_scan(r, k_dir[:, :, 1], v, decay[:, :, 1], kk, iclr[:, :, 1], s_rwkv0[:, 1], True)
    o_c = _head_groupnorm(o_f + o_bk) * p['rwkv_gn'].reshape(RWKV_HEADS, RWKV_HEAD)
    bonus = jnp.sum(jnp.sum(r[:, :, None] * k_dir * p['rwkv_rk'], axis=-1, keepdims=True), axis=2)
    o_c = o_c + bonus * v
    gate_c = jnp.einsum('blr,rc->blc', jax.nn.sigmoid(c_g), p['rwkv_bg'])
    y_c = o_c.reshape(b_, L, MIX_W) * gate_c

    q = gq.reshape(b_, L, GLA_HEADS, GLA_DK) * (GLA_DK ** -0.5)
    kg = gk.reshape(b_, L, GLA_HEADS, GLA_DK)
    vg = gv.reshape(b_, L, GLA_HEADS, GLA_DV)
    log_a = jax.nn.log_sigmoid(jnp.einsum('bldr,drc->bldc', c_al.reshape(b_, L, 2, GLA_RANK), p['gla_ab'])
                               + p['gla_abias']) / GLA_GATE_NORM
    log_a = log_a.reshape(b_, L, 2, GLA_HEADS, GLA_DK)
    og_f, s_gf = _gla_chunked(q, kg, vg, log_a[:, :, 0], s_gla0[:, 0])
    og_rev, s_gb = _gla_chunked(jnp.flip(q, 1), jnp.flip(kg, 1), jnp.flip(vg, 1), jnp.flip(log_a[:, :, 1], 1), s_gla0[:, 1])
    og = og_f + jnp.flip(og_rev, 1)
    og = og * lax.rsqrt(jnp.mean(og * og, axis=-1, keepdims=True) + EPS) * p['gla_norm'].reshape(GLA_HEADS, GLA_DV)
    y_d = og.reshape(b_, L, MIX_W) * jax.nn.silu(g_out)

    ys = jnp.stack([y_a, y_b, y_c, y_d], axis=2)
    branch = jnp.einsum('blic,icd->blid', ys, p['w_branch'])
    gates = jax.nn.sigmoid(m_log.reshape(b_, L, N_BRANCH, D_MODEL))
    merged = jnp.sum(gates * branch, axis=2)
    out = jnp.einsum('bld,de->ble', merged, p['w_out'])
    return out, jnp.stack([s_rf, s_rb], axis=1), jnp.stack([s_gf, s_gb], axis=1)


def _layer(x, cond, p, s_rwkv0, s_gla0):
    mod = jnp.einsum('bd,de->be', jax.nn.silu(cond.astype(F32)), p['ada_w']) + p['ada_b']
    sh1, sc1, g1, sh2, sc2, g2 = jnp.split(mod[:, None, :].astype(F32), 6, axis=-1)
    h = _rmsnorm(x, p['norm1_g']) * (1.0 + sc1) + sh1
    mix, s_r, s_g = _mixer_block(h, p, s_rwkv0, s_gla0)
    xf = x.astype(F32) + g1 * mix
    h = _rmsnorm(xf, p['norm2_g']) * (1.0 + sc2) + sh2
    ff = jnp.einsum('blf,fd->bld', jnp.square(jax.nn.relu(jnp.einsum('bld,df->blf', h, p['mlp_w1']))), p['mlp_w2'])
    xf = xf + g2 * ff
    return xf.astype(x.dtype), s_r, s_g


def setup_inputs(seed: int = 0) -> dict:
    key = jax.random.key(seed)
    ks = jax.random.split(key, 40)

    def nrm(i, shape, scale):
        return jax.random.normal(ks[i], shape, F32) * scale

    def gain(i, shape):
        return 1.0 + 0.1 * jax.random.normal(ks[i], shape, F32)

    return {
        'x_prompt': nrm(0, (BATCH, SEQ, D_MODEL), 1.0),
        'x_sample': nrm(1, (DEC_BATCH, DEC_SEQ, D_MODEL), 1.0),
        'state_rwkv': nrm(2, (DEC_BATCH, DEPTH, 2, RWKV_HEADS, RWKV_HEAD, RWKV_HEAD), 0.5),
        'state_gla': nrm(3, (DEC_BATCH, DEPTH, 2, GLA_HEADS, GLA_DK, GLA_DV), 1.0),
        'c': nrm(4, (DEC_BATCH, D_MODEL), 1.0),
        'c_ctx': nrm(5, (D_MODEL,), 1.0),
        'ada_w': nrm(6, (DEPTH, D_MODEL, 6 * D_MODEL), 0.5 * D_MODEL ** -0.5),
        'ada_b': nrm(7, (DEPTH, 6 * D_MODEL), 0.02),
        'norm1_g': gain(8, (DEPTH, D_MODEL)),
        'norm2_g': gain(9, (DEPTH, D_MODEL)),
        'w_in': nrm(10, (DEPTH, D_MODEL, P_IN), D_MODEL ** -0.5),
        'pool_w': nrm(11, (DEPTH, N_GROUPS, GROUP_W, GROUP_W), GROUP_W ** -0.5),
        'pool_scale': gain(12, (DEPTH, MIX_W)),
        'rwkv_mu': jax.random.uniform(ks[13], (DEPTH, 3 * MIX_W), F32),
        'rwkv_w0': nrm(14, (DEPTH, 2, MIX_W), 0.5),
        'rwkv_bw': nrm(15, (DEPTH, 2, RWKV_W_RANK, MIX_W), 0.1 * RWKV_W_RANK ** -0.5),
        'rwkv_a0': nrm(16, (DEPTH, 2, MIX_W), 0.5),
        'rwkv_ba': nrm(17, (DEPTH, 2, RWKV_A_RANK, MIX_W), 0.5 * RWKV_A_RANK ** -0.5),
        'rwkv_kk': gain(18, (DEPTH, MIX_W)),
        'rwkv_ka': gain(19, (DEPTH, MIX_W)),
        'rwkv_bg': nrm(20, (DEPTH, RWKV_G_RANK, MIX_W), RWKV_G_RANK ** -0.5),
        'rwkv_rk': nrm(21, (DEPTH, RWKV_HEADS, RWKV_HEAD), 0.1),
        'rwkv_gn': gain(22, (DEPTH, MIX_W)),
        'gla_ab': nrm(23, (DEPTH, 2, GLA_RANK, GLA_HEADS * GLA_DK), 0.5 * GLA_RANK ** -0.5),
        'gla_abias': nrm(24, (DEPTH, 2, GLA_HEADS * GLA_DK), 0.5),
        'gla_norm': gain(25, (DEPTH, MIX_W)),
        'w_branch': nrm(26, (DEPTH, N_BRANCH, MIX_W, D_MODEL), MIX_W ** -0.5),
        'w_out': nrm(27, (DEPTH, D_MODEL, D_MODEL), D_MODEL ** -0.5),
        'mlp_w1': nrm(28, (DEPTH, D_MODEL, D_FF), D_MODEL ** -0.5),
        'mlp_w2': nrm(29, (DEPTH, D_FF, D_MODEL), D_FF ** -0.5),
        'final_g': gain(30, (D_MODEL,)),
    }


def reference(x_prompt, x_sample, state_rwkv, state_gla, c, c_ctx, ada_w, ada_b, norm1_g, norm2_g,
              w_in, pool_w, pool_scale, rwkv_mu, rwkv_w0, rwkv_bw, rwkv_a0, rwkv_ba, rwkv_kk, rwkv_ka,
              rwkv_bg, rwkv_rk, rwkv_gn, gla_ab, gla_abias, gla_norm, w_branch, w_out, mlp_w1, mlp_w2,
              final_g):
    def layer_params(l):
        return {'ada_w': ada_w[l], 'ada_b': ada_b[l], 'norm1_g': norm1_g[l], 'norm2_g': norm2_g[l],
                'w_in': w_in[l], 'pool_w': pool_w[l], 'pool_scale': pool_scale[l], 'rwkv_mu': rwkv_mu[l],
                'rwkv_w0': rwkv_w0[l], 'rwkv_bw': rwkv_bw[l], 'rwkv_a0': rwkv_a0[l], 'rwkv_ba': rwkv_ba[l],
                'rwkv_kk': rwkv_kk[l], 'rwkv_ka': rwkv_ka[l], 'rwkv_bg': rwkv_bg[l], 'rwkv_rk': rwkv_rk[l],
                'rwkv_gn': rwkv_gn[l], 'gla_ab': gla_ab[l], 'gla_abias': gla_abias[l], 'gla_norm': gla_norm[l],
                'w_branch': w_branch[l], 'w_out': w_out[l], 'mlp_w1': mlp_w1[l], 'mlp_w2': mlp_w2[l]}

    xp = x_prompt
    bp = x_prompt.shape[0]
    new_r, new_g = [], []
    for l in range(DEPTH):
        xp, s_r, s_g = _layer(xp, c_ctx[None, :], layer_params(l),
                              jnp.zeros((bp, 2, RWKV_HEADS, RWKV_HEAD, RWKV_HEAD), F32),
                              jnp.zeros((bp, 2, GLA_HEADS, GLA_DK, GLA_DV), F32))
        new_r.append(s_r)
        new_g.append(s_g)
    y_prompt = _rmsnorm(xp, final_g).astype(x_prompt.dtype)
    new_state_rwkv = jnp.stack(new_r, axis=1)
    new_state_gla = jnp.stack(new_g, axis=1)

    n_tok = x_sample.shape[1]
    xs = (x_sample.astype(F32) + _pos_embed_2d(n_tok, D_MODEL)[None]).astype(x_sample.dtype)
    for l in range(DEPTH):
        xs, _, _ = _layer(xs, c, layer_params(l), state_rwkv[:, l].astype(F32), state_gla[:, l].astype(F32))
    y_sample = _rmsnorm(xs, final_g).astype(x_sample.dtype)

    return (y_prompt, y_sample, new_state_rwkv, new_state_gla)
```

```python
import functools

import numpy as np
import jax
import jax.numpy as jnp
from jax import lax
from jax.experimental import pallas as pl
from jax.experimental.pallas import tpu as pltpu

F32 = jnp.float32
BF16 = jnp.bfloat16

D_MODEL = 1024
DEPTH = 2
GRID_W = 64
POS_BASE = 10000.0
MIX_W = 256
POOL_WINDOWS = (2, 4, 8, 16)
POOL_HALO = 8
RWKV_DECAY_SCALE = 0.606531
RWKV_GN_EPS = 64e-5
GLA_DK = 32
GLA_GATE_NORM = 16.0
D_FF = 4 * D_MODEL
EPS = 1e-6
P_MIX = 2336
P_MIX_PAD = 2432
CHUNK = 64
SUB = 16

TM = 512
TB = 256
FFT_N1 = 64
VMEM_LIMIT = 56 * 1024 * 1024


def _cparams(sem):
    return pltpu.CompilerParams(dimension_semantics=sem, vmem_limit_bytes=VMEM_LIMIT)


def _full(shape):
    n = len(shape)
    return pl.BlockSpec(shape, lambda *_: (0,) * n)


def _dot(a, b):
    return jnp.dot(a.astype(BF16), b.astype(BF16), preferred_element_type=F32)


def _dot_nt(a, b):
    return lax.dot_general(a.astype(BF16), b.astype(BF16), (((1,), (1,)), ((), ())),
                           preferred_element_type=F32)


def _dot_tn(a, b):
    return lax.dot_general(a.astype(BF16), b.astype(BF16), (((0,), (0,)), ((), ())),
                           preferred_element_type=F32)


def _split2(a):
    hi = a.astype(BF16)
    lo = (a - hi.astype(F32)).astype(BF16)
    return hi, lo


def _split3(a):
    hi = a.astype(BF16)
    r = a - hi.astype(F32)
    mid = r.astype(BF16)
    lo = (r - mid.astype(F32)).astype(BF16)
    return hi, mid, lo


def _dot_exact_rhs(a, w):
    hi, lo = _split2(a)
    return (jnp.dot(hi, w, preferred_element_type=F32) + jnp.dot(lo, w, preferred_element_type=F32))


def _dot_exact_lhs3(w, a):
    hi, mid, lo = _split3(a)
    return (jnp.dot(w, hi, preferred_element_type=F32) + jnp.dot(w, mid, preferred_element_type=F32)
            + jnp.dot(w, lo, preferred_element_type=F32))


def _seg_ones(n, seg, scale=1.0):
    r = lax.broadcasted_iota(jnp.int32, (n, n), 0)
    c = lax.broadcasted_iota(jnp.int32, (n, n), 1)
    return jnp.where((r // seg) == (c // seg), scale, 0.0).astype(BF16)


def _log_sigmoid(x):
    return jnp.minimum(x, 0.0) - jnp.log(1.0 + jnp.exp(-jnp.abs(x)))


def _rms_mod(x, g, sc, sh):
    ms = jnp.mean(x * x, axis=-1, keepdims=True)
    return x * lax.rsqrt(ms + EPS) * g * (1.0 + sc) + sh


def _ada_kernel(c_ref, w_ref, b_ref, o_ref):
    c = c_ref[...]
    s = c * jax.nn.sigmoid(c)
    hi, lo = _split2(s)
    w = w_ref[0].astype(BF16)
    o_ref[0] = (jnp.dot(hi, w, preferred_element_type=F32) + jnp.dot(lo, w, preferred_element_type=F32)
                + b_ref[0])


def _ada(cond8, ada_w, ada_b):
    tn = 1536
    n = 6 * D_MODEL
    return pl.pallas_call(
        _ada_kernel,
        grid=(DEPTH, n // tn),
        in_specs=[_full((8, D_MODEL)),
                  pl.BlockSpec((1, D_MODEL, tn), lambda l, j: (l, 0, j)),
                  pl.BlockSpec((1, 1, tn), lambda l, j: (l, 0, j))],
        out_specs=pl.BlockSpec((1, 8, tn), lambda l, j: (l, 0, j)),
        out_shape=jax.ShapeDtypeStruct((DEPTH, 8, n), F32),
        compiler_params=_cparams(("parallel", "parallel")),
        name="ada_mod",
    )(cond8, ada_w, ada_b.reshape(DEPTH, 1, n))


def _inproj_kernel(x_ref, mod_ref, g_ref, w_ref, up_ref, uf_ref, ur_ref, ug_ref):
    mod = mod_ref[0]
    h = _rms_mod(x_ref[...], g_ref[...], mod[:, D_MODEL:2 * D_MODEL], mod[:, 0:D_MODEL]).astype(BF16)
    up_ref[...] = jnp.dot(h, w_ref[:, 0:256], preferred_element_type=F32)
    uf_ref[...] = jnp.dot(h, w_ref[:, 256:512], preferred_element_type=F32)
    ur_ref[...] = jnp.dot(h, w_ref[:, 512:1536], preferred_element_type=F32)
    ug_ref[...] = jnp.dot(h, w_ref[:, 1536:P_MIX_PAD], preferred_element_type=F32)


def _cond_map(cond_base, rows_per_cond):
    return lambda i: (cond_base + (i * TM) // rows_per_cond, 0, 0)


def _inproj(x, mod3, g, w, cond_base, rows_per_cond):
    t = x.shape[0]
    row = lambda i: (i, 0)
    return pl.pallas_call(
        _inproj_kernel,
        grid=(t // TM,),
        in_specs=[pl.BlockSpec((TM, D_MODEL), row),
                  pl.BlockSpec((1, 1, 6 * D_MODEL), _cond_map(cond_base, rows_per_cond)),
                  _full((1, D_MODEL)),
                  _full((D_MODEL, P_MIX_PAD))],
        out_specs=[pl.BlockSpec((TM, 256), row), pl.BlockSpec((TM, 256), row),
                   pl.BlockSpec((TM, 1024), row), pl.BlockSpec((TM, 896), row)],
        out_shape=[jax.ShapeDtypeStruct((t, 256), F32), jax.ShapeDtypeStruct((t, 256), F32),
                   jax.ShapeDtypeStruct((t, 1024), F32), jax.ShapeDtypeStruct((t, 896), F32)],
        compiler_params=_cparams(("parallel",)),
        name="in_proj",
    )(x, mod3, g, w)


def _pool_kernel(prev_ref, cur_ref, next_ref, wbd_ref, scale_ref, y_ref, *, seq_len):
    j = pl.program_id(1)
    nb = seq_len // TB
    cur = cur_ref[...]
    prev = jnp.where(j > 0, prev_ref[...], 0.0)
    nxt = jnp.where(j < nb - 1, next_ref[...], 0.0)
    ext = jnp.concatenate([prev, cur, nxt], axis=0)
    ext_hi, ext_lo = _split2(ext)
    ke = TB + 2 * POOL_HALO
    d = (lax.broadcasted_iota(jnp.int32, (TB, ke), 1) - POOL_HALO
         - lax.broadcasted_iota(jnp.int32, (TB, ke), 0))
    pos = j * TB + lax.broadcasted_iota(jnp.int32, (TB, MIX_W), 0)
    group = lax.broadcasted_iota(jnp.int32, (TB, MIX_W), 1) // (MIX_W // len(POOL_WINDOWS))
    pooled = jnp.zeros((TB, MIX_W), F32)
    for gi, win in enumerate(POOL_WINDOWS):
        before, after = win // 2, win - win // 2 - 1
        band = jnp.where((d >= -before) & (d <= after), 1.0, 0.0).astype(BF16)
        s = (jnp.dot(band, ext_hi, preferred_element_type=F32)
             + jnp.dot(band, ext_lo, preferred_element_type=F32))
        cnt = (jnp.minimum(pos + after, seq_len - 1) - jnp.maximum(pos - before, 0) + 1).astype(F32)
        pooled = jnp.where(group == gi, s / cnt - cur, pooled)
    y_ref[...] = _dot(pooled, wbd_ref[...]) * scale_ref[...]


def _pool(u_p, wbd, scale, nseq, seq_len):
    nb = seq_len // TB
    hb = TB // POOL_HALO
    nh = u_p.shape[0] // POOL_HALO
    return pl.pallas_call(
        functools.partial(_pool_kernel, seq_len=seq_len),
        grid=(nseq, nb),
        in_specs=[pl.BlockSpec((POOL_HALO, MIX_W), lambda b, j: (jnp.maximum((b * nb + j) * hb - 1, 0), 0)),
                  pl.BlockSpec((TB, MIX_W), lambda b, j: (b * nb + j, 0)),
                  pl.BlockSpec((POOL_HALO, MIX_W), lambda b, j: (jnp.minimum((b * nb + j + 1) * hb, nh - 1), 0)),
                  _full((MIX_W, MIX_W)), _full((1, MIX_W))],
        out_specs=pl.BlockSpec((TB, MIX_W), lambda b, j: (b * nb + j, 0)),
        out_shape=jax.ShapeDtypeStruct(u_p.shape, F32),
        compiler_params=_cparams(("parallel", "parallel")),
        name="pool_mixer",
    )(u_p, u_p, u_p, wbd, scale)


def _dft_tables(n):
    t = np.arange(n)
    ang = 2.0 * np.pi * ((np.outer(t, t)) % n) / n
    return np.cos(ang), np.sin(ang)


def _bd_np(m, copies):
    k = m.shape[0]
    out = np.zeros((k * copies, k * copies), m.dtype)
    for i in range(copies):
        out[i * k:(i + 1) * k, i * k:(i + 1) * k] = m
    return out


def _four_small_kernel(z_ref, cs_ref, cbd_ref, sbd_ref, y_ref, *, seq_len, norm):
    zf = _dot(cs_ref[...], z_ref[...])
    y = _dot(zf[:seq_len], cbd_ref[...]) + _dot(zf[seq_len:], sbd_ref[...])
    y_ref[...] = y * norm


def _four_small(u_f, nseq, seq_len):
    c, s = _dft_tables(seq_len)
    cs = jnp.asarray(np.concatenate([c, -s], axis=0), F32)
    c64, s64 = _dft_tables(MIX_W // 4)
    cbd = jnp.asarray(_bd_np(c64, 4), F32)
    sbd = jnp.asarray(_bd_np(s64, 4), F32)
    return pl.pallas_call(
        functools.partial(_four_small_kernel, seq_len=seq_len, norm=float((seq_len * 64) ** -0.5)),
        grid=(nseq,),
        in_specs=[pl.BlockSpec((seq_len, MIX_W), lambda b: (b, 0)),
                  _full((2 * seq_len, seq_len)), _full((MIX_W, MIX_W)), _full((MIX_W, MIX_W))],
        out_specs=pl.BlockSpec((seq_len, MIX_W), lambda b: (b, 0)),
        out_shape=jax.ShapeDtypeStruct(u_f.shape, F32),
        compiler_params=_cparams(("parallel",)),
        name="fourier_small",
    )(u_f, cs, cbd, sbd)


def _fft1_kernel(z_ref, fs_ref, a_ref):
    a_ref[...] = _dot(fs_ref[...], z_ref[...])


def _fft2_kernel(a_ref, g_ref, cbd_ref, sbd_ref, y_ref, *, f1b, norm):
    for i in range(f1b):
        a = jnp.concatenate([a_ref[0, i], a_ref[1, i]], axis=0)
        x = _dot(g_ref[i], a)
        y = _dot(x[:FFT_N1], cbd_ref[...]) + _dot(x[FFT_N1:], sbd_ref[...])
        y_ref[:, i * MIX_W:(i + 1) * MIX_W] = y * norm


def _four_large(u_f, nseq, seq_len):
    n1 = FFT_N1
    assert seq_len == n1 * n1
    width = n1 * MIX_W
    c64, s64 = _dft_tables(n1)
    fs = jnp.asarray(np.concatenate([c64, -s64], axis=0), F32)
    f1 = np.arange(n1)[:, None, None]
    f2 = np.arange(n1)[None, :, None]
    t2 = np.arange(n1)[None, None, :]
    ang = 2.0 * np.pi * ((t2 * (f1 + n1 * f2)) % seq_len) / seq_len
    gr, gi = np.cos(ang), -np.sin(ang)
    g = jnp.asarray(np.concatenate([np.concatenate([gr, -gi], axis=2),
                                    np.concatenate([gi, gr], axis=2)], axis=1), F32)
    cbd = jnp.asarray(_bd_np(c64, 4), F32)
    sbd = jnp.asarray(_bd_np(s64, 4), F32)
    nc = 2048
    a = pl.pallas_call(
        _fft1_kernel,
        grid=(nseq, width // nc),
        in_specs=[pl.BlockSpec((None, n1, nc), lambda b, j: (b, 0, j)), _full((2 * n1, n1))],
        out_specs=pl.BlockSpec((None, 2 * n1, nc), lambda b, j: (b, 0, j)),
        out_shape=jax.ShapeDtypeStruct((nseq, 2 * n1, width), F32),
        compiler_params=_cparams(("parallel", "parallel")),
        name="fourier_stage1",
    )(u_f.reshape(nseq, n1, width), fs)
    f1b = 8
    y = pl.pallas_call(
        functools.partial(_fft2_kernel, f1b=f1b, norm=float((seq_len * 64) ** -0.5)),
        grid=(nseq, n1 // f1b),
        in_specs=[pl.BlockSpec((None, 2, f1b, n1, MIX_W), lambda b, j: (b, 0, j, 0, 0)),
                  pl.BlockSpec((f1b, 2 * n1, 2 * n1), lambda b, j: (j, 0, 0)),
                  _full((MIX_W, MIX_W)), _full((MIX_W, MIX_W))],
        out_specs=pl.BlockSpec((None, n1, f1b * MIX_W), lambda b, j: (b, 0, j)),
        out_shape=jax.ShapeDtypeStruct((nseq, n1, width), F32),
        compiler_params=_cparams(("parallel", "parallel")),
        name="fourier_stage2",
    )(a.reshape(nseq, 2, n1, n1, MIX_W), g, cbd, sbd)
    return y.reshape(nseq * seq_len, MIX_W)


def _slab_bd(y):
    lane = lax.broadcasted_iota(jnp.int32, y.shape, 1)
    z = jnp.zeros_like(y)
    return jnp.concatenate([jnp.where(lane < 64, y, z), jnp.where(lane >= 64, y, z)], axis=0)


def _mm(x, y):
    return jnp.dot(x.astype(BF16), _slab_bd(y.astype(BF16)), preferred_element_type=F32)


def _mm3(x, y):
    xh, xl = _split2(x)
    yh, yl = _split2(y)
    bh, bl = _slab_bd(yh), _slab_bd(yl)
    return (jnp.dot(xh, bh, preferred_element_type=F32)
            + (jnp.dot(xh, bl, preferred_element_type=F32) + jnp.dot(xl, bh, preferred_element_type=F32)))


def _mm_nt(x, y):
    return _dot_nt(x, _slab_bd(y.astype(BF16)))


def _tri_inverse(n_mat, eye, blk):
    nd = jnp.where(blk, n_mat, 0.0)
    no = n_mat - nd
    n2 = _mm3(nd, nd)
    n4 = _mm3(n2, n2)
    n8 = _mm3(n4, n4)
    x = n2 - nd - _mm3(nd, n2)
    x = x + n4 + _mm3(x, n4)
    x = x + n8 + _mm3(x, n8)
    td = eye + x
    e = _mm(td, no)
    e2 = _mm(e, e)
    e3 = _mm(e, e2)
    return td + _mm(e2 - e - e3, td)


def _rwkv_kernel(cur_ref, prev_ref, next_ref, mu_ref, kkw_ref, ka_ref, rk_ref, w0_ref, a0_ref,
                 bw_ref, ba_ref, s0_ref, o_ref, ex_ref, sfin_ref,
                 s_scr, r_scr, kd_scr, v_scr, kk_scr, bv_scr, lw_scr, *, rev, nb):
    j = pl.program_id(1)
    jj = (nb - 1 - j) if rev else j

    @pl.when(j == 0)
    def _():
        s_scr[...] = s0_ref[0]

    z = cur_ref[:, 0:768]
    prow = jnp.where(jj > 0, prev_ref[POOL_HALO - 1:POOL_HALO, :], 0.0)
    nrow = jnp.where(jj < nb - 1, next_ref[0:1, :], 0.0)
    rowid = lax.broadcasted_iota(jnp.int32, (TB, 768), 0)
    up = jnp.where(rowid == 0, prow, pltpu.roll(z, 1, 0))
    dn = jnp.where(rowid == TB - 1, nrow, pltpu.roll(z, TB - 1, 0))
    zm = z + mu_ref[...] * (0.5 * (up + dn) - z)
    r = zm[:, 0:256]
    k = zm[:, 256:512]
    v = zm[:, 512:768]
    ones_bd = _seg_ones(MIX_W, 64)
    kk = k * kkw_ref[...]
    kk = kk * lax.rsqrt(_dot_exact_rhs(kk * kk, ones_bd) + EPS)
    w_log = w0_ref[...] + _dot(jnp.tanh(cur_ref[:, 768:896]), bw_ref[...])
    lw = -RWKV_DECAY_SCALE * jax.nn.sigmoid(w_log)
    a = jax.nn.sigmoid(a0_ref[...] + _dot(cur_ref[:, 896:1024], ba_ref[...]))
    kd = k * (1.0 + (a - 1.0) * ka_ref[...])
    ex_ref[...] = _dot_exact_rhs(r * kd * rk_ref[...], ones_bd) * v
    r_scr[...] = r
    kd_scr[...] = kd
    v_scr[...] = v
    kk_scr[...] = kk
    bv_scr[...] = kk * a
    lw_scr[...] = lw

    t_i = lax.broadcasted_iota(jnp.int32, (CHUNK, 128), 0)
    s_i = lax.broadcasted_iota(jnp.int32, (CHUNK, 128), 1) & (CHUNK - 1)
    if rev:
        strict, incl = s_i > t_i, s_i >= t_i
    else:
        strict, incl = s_i < t_i, s_i <= t_i
    blk = (s_i // SUB) == (t_i // SUB)
    eye = jnp.where(s_i == t_i, 1.0, 0.0)
    tr = lax.broadcasted_iota(jnp.int32, (CHUNK, CHUNK), 0)
    tc = lax.broadcasted_iota(jnp.int32, (CHUNK, CHUNK), 1)
    tri = jnp.where((tc >= tr) if rev else (tc <= tr), 1.0, 0.0).astype(BF16)
    bd_r = lax.broadcasted_iota(jnp.int32, (128, 128), 0) // 64
    bd_c = lax.broadcasted_iota(jnp.int32, (128, 128), 1) // 64
    bd_mask = bd_r == bd_c
    last = 0 if rev else CHUNK - 1

    def chunk_body(ci, carry):
        c = (TB // CHUNK - 1 - ci) if rev else ci
        rows = pl.ds(pl.multiple_of(c * CHUNK, CHUNK), CHUNK)
        lw_c = lw_scr[rows, :]
        cum = _dot_exact_lhs3(tri, lw_c)
        e_in = jnp.exp(cum)
        e_out = jnp.exp(-cum)
        rt_a = r_scr[rows, :] * e_in
        kh_a = kd_scr[rows, :] * e_out
        bh_a = bv_scr[rows, :] * e_out
        kt_a = kk_scr[rows, :] * jnp.exp(cum - lw_c)
        v_a = v_scr[rows, :]
        for hp in range(2):
            ln = slice(hp * 128, (hp + 1) * 128)
            rt, kh, bh, kt, vv = rt_a[:, ln], kh_a[:, ln], bh_a[:, ln], kt_a[:, ln], v_a[:, ln]
            gam = e_in[last:last + 1, ln]
            lhs = jnp.concatenate([kt, rt], axis=0)
            mk = _mm_nt(lhs, kh)
            mb = _mm_nt(lhs, bh)
            m_mat = jnp.where(strict, mk[:CHUNK], 0.0)
            p_mat = jnp.where(incl, mk[CHUNK:], 0.0)
            n_mat = jnp.where(strict, mb[:CHUNK], 0.0)
            q_mat = jnp.where(incl, mb[CHUNK:], 0.0)
            t_mat = _tri_inverse(n_mat, eye, blk)
            mpv = _mm(jnp.concatenate([m_mat, p_mat], axis=0), vv)
            mv, pv = mpv[:CHUNK], mpv[CHUNK:]
            tk = _mm(t_mat, kt)
            uv = _mm(t_mat, mv)
            rc = rt - _mm(q_mat, tk)
            oc = pv - _mm(q_mat, uv)
            s_old = s_scr[hp]
            o_ref[rows, ln] = _dot_nt(rc, s_old) + oc
            braw = _dot_tn(jnp.concatenate([vv, uv], axis=0), jnp.concatenate([kh, -bh], axis=0))
            w1 = _dot_nt(s_old, _slab_bd(tk.astype(BF16)))
            w2 = jnp.dot(w1.astype(BF16), _slab_bd(bh.astype(BF16)), preferred_element_type=F32)
            s_scr[hp] = (s_old - w2 + jnp.where(bd_mask, braw, 0.0)) * gam
        return carry

    lax.fori_loop(0, TB // CHUNK, chunk_body, 0)

    @pl.when(j == nb - 1)
    def _():
        sfin_ref[0] = s_scr[...]


def _rwkv_dir(u_r, prm, s0, nseq, seq_len, rev):
    nb = seq_len // TB
    hb = TB // POOL_HALO
    nh = u_r.shape[0] // POOL_HALO
    t = u_r.shape[0]

    def jj(j):
        return (nb - 1 - j) if rev else j

    cur_map = lambda b, j: (b * nb + jj(j), 0)
    st_map = lambda b, j: (b, 0, 0, 0)
    vec = _full((1, MIX_W))
    return pl.pallas_call(
        functools.partial(_rwkv_kernel, rev=rev, nb=nb),
        grid=(nseq, nb),
        in_specs=[pl.BlockSpec((TB, 1024), cur_map),
                  pl.BlockSpec((POOL_HALO, 768), lambda b, j: (jnp.maximum((b * nb + jj(j)) * hb - 1, 0), 0)),
                  pl.BlockSpec((POOL_HALO, 768), lambda b, j: (jnp.minimum((b * nb + jj(j) + 1) * hb, nh - 1), 0)),
                  _full((1, 768)), vec, vec, vec, vec, vec,
                  _full((128, MIX_W)), _full((128, MIX_W)),
                  pl.BlockSpec((1, 2, 128, 128), st_map)],
        out_specs=[pl.BlockSpec((TB, MIX_W), cur_map), pl.BlockSpec((TB, MIX_W), cur_map),
                   pl.BlockSpec((1, 2, 128, 128), st_map)],
        out_shape=[jax.ShapeDtypeStruct((t, MIX_W), F32), jax.ShapeDtypeStruct((t, MIX_W), F32),
                   jax.ShapeDtypeStruct((nseq, 2, 128, 128), F32)],
        scratch_shapes=[pltpu.VMEM((2, 128, 128), F32)] + [pltpu.VMEM((TB, MIX_W), F32)] * 6,
        compiler_params=_cparams(("parallel", "arbitrary")),
        name="rwkv_bwd" if rev else "rwkv_fwd",
    )(u_r, u_r, u_r, prm["mu"], prm["kk"], prm["ka"], prm["rk"], prm["w0"], prm["a0"],
      prm["bw"], prm["ba"], s0)


def _rows_bd4(y, width):
    lane = lax.broadcasted_iota(jnp.int32, y.shape, 1) // width
    z = jnp.zeros_like(y)
    return jnp.concatenate([jnp.where(lane == h, y, z) for h in range(4)], axis=0)


def _gla_kernel(u_ref, ab_ref, abias_ref, s0_ref, o_ref, sfin_ref, s_scr, la_scr, *, rev, nb):
    j = pl.program_id(1)

    @pl.when(j == 0)
    def _():
        s_scr[...] = s0_ref[0]

    la_scr[...] = _log_sigmoid(_dot(u_ref[:, 768:896], ab_ref[...]) + abias_ref[...]) * (1.0 / GLA_GATE_NORM)

    t_i = lax.broadcasted_iota(jnp.int32, (CHUNK, 256), 0)
    s_i = lax.broadcasted_iota(jnp.int32, (CHUNK, 256), 1) & (CHUNK - 1)
    tr = lax.broadcasted_iota(jnp.int32, (CHUNK, CHUNK), 0)
    tc = lax.broadcasted_iota(jnp.int32, (CHUNK, CHUNK), 1)
    tri = jnp.where((tc >= tr) if rev else (tc <= tr), 1.0, 0.0).astype(BF16)
    levels = []
    for lev in range(1, 7):
        size, half = 1 << lev, 1 << (lev - 1)
        same = (t_i // size) == (s_i // size)
        if rev:
            mask = same & ((t_i & (size - 1)) < half) & ((s_i & (size - 1)) >= half)
            ref_row = (tr // size) * size + half
        else:
            mask = same & ((t_i & (size - 1)) >= half) & ((s_i & (size - 1)) < half)
            ref_row = (tr // size) * size + half - 1
        levels.append((mask, jnp.where(tc == ref_row, 1.0, 0.0).astype(BF16)))
    diag = s_i == t_i
    sel_all = jnp.concatenate([sel for _, sel in levels], axis=0)
    bd_r = lax.broadcasted_iota(jnp.int32, (256, 128), 0) // 64
    bd_c = lax.broadcasted_iota(jnp.int32, (256, 128), 1) // GLA_DK
    bd_mask = bd_r == bd_c
    last = 0 if rev else CHUNK - 1
    scale = float(GLA_DK ** -0.5)

    def chunk_body(ci, carry):
        c = (TB // CHUNK - 1 - ci) if rev else ci
        rows = pl.ds(pl.multiple_of(c * CHUNK, CHUNK), CHUNK)
        q = u_ref[rows, 0:128] * scale
        kg = u_ref[rows, 128:256]
        vg = u_ref[rows, 256:512]
        b = _dot_exact_lhs3(tri, la_scr[rows, :])
        bref = _dot_exact_lhs3(sel_all, b)
        att = jnp.where(diag, _dot_nt(q, _rows_bd4(kg.astype(BF16), GLA_DK)), 0.0)
        for li, (mask, _) in enumerate(levels):
            br = bref[li * CHUNK:(li + 1) * CHUNK]
            qe = q * jnp.exp(jnp.minimum(b - br, 0.0))
            ke = kg * jnp.exp(jnp.minimum(br - b, 0.0))
            att = att + jnp.where(mask, _dot_nt(qe, _rows_bd4(ke.astype(BF16), GLA_DK)), 0.0)
        s_old = s_scr[...]
        o = _dot(att, _rows_bd4(vg.astype(BF16), 64)) + _dot_nt(q * jnp.exp(b), s_old)
        o_ref[rows, :] = o
        b_last = b[last:last + 1, :]
        inc = _dot_tn(vg, kg * jnp.exp(b_last - b))
        s_scr[...] = s_old * jnp.exp(b_last) + jnp.where(bd_mask, inc, 0.0)
        return carry

    lax.fori_loop(0, TB // CHUNK, chunk_body, 0)

    @pl.when(j == nb - 1)
    def _():
        sfin_ref[0] = s_scr[...]


def _gla_dir(u_g, ab, abias, s0, nseq, seq_len, rev):
    nb = seq_len // TB
    t = u_g.shape[0]
    cur_map = (lambda b, j: (b * nb + nb - 1 - j, 0)) if rev else (lambda b, j: (b * nb + j, 0))
    st_map = lambda b, j: (b, 0, 0)
    return pl.pallas_call(
        functools.partial(_gla_kernel, rev=rev, nb=nb),
        grid=(nseq, nb),
        in_specs=[pl.BlockSpec((TB, 896), cur_map), _full((128, 128)), _full((1, 128)),
                  pl.BlockSpec((1, 256, 128), st_map)],
        out_specs=[pl.BlockSpec((TB, MIX_W), cur_map), pl.BlockSpec((1, 256, 128), st_map)],
        out_shape=[jax.ShapeDtypeStruct((t, MIX_W), F32), jax.ShapeDtypeStruct((nseq, 256, 128), F32)],
        scratch_shapes=[pltpu.VMEM((256, 128), F32), pltpu.VMEM((TB, 128), F32)],
        compiler_params=_cparams(("parallel", "arbitrary")),
        name="gla_bwd" if rev else "gla_fwd",
    )(u_g, ab, abias, s0)


def _merge_kernel(x_ref, mod_ref, g_ref, ya_ref, yb_ref, of_ref, ob_ref, exf_ref, exb_ref, cag_ref,
                  gf_ref, gb_ref, gout_ref, gn_ref, bg_ref, gnorm_ref, wg_ref, wb_ref, wo_ref, x1_ref):
    mod = mod_ref[0]
    x = x_ref[...]
    h = _rms_mod(x, g_ref[...], mod[:, D_MODEL:2 * D_MODEL], mod[:, 0:D_MODEL]).astype(BF16)
    mean64 = _seg_ones(MIX_W, 64, 1.0 / 64.0)
    o = of_ref[...] + ob_ref[...]
    mu = _dot_exact_rhs(o, mean64)
    oc = o - mu
    var = _dot_exact_rhs(oc * oc, mean64)
    gate_c = _dot(jax.nn.sigmoid(cag_ref[...]), bg_ref[...])
    y_c = (oc * lax.rsqrt(var + RWKV_GN_EPS) * gn_ref[...] + exf_ref[...] + exb_ref[...]) * gate_c
    og = gf_ref[...] + gb_ref[...]
    gout = gout_ref[...]
    y_d = (og * lax.rsqrt(_dot_exact_rhs(og * og, mean64) + EPS) * gnorm_ref[...]
           * (gout * jax.nn.sigmoid(gout)))
    ys = (ya_ref[...], yb_ref[...], y_c, y_d)
    merged = jnp.zeros((TM, D_MODEL), F32)
    for i in range(4):
        gate = jax.nn.sigmoid(jnp.dot(h, wg_ref[:, i * D_MODEL:(i + 1) * D_MODEL], preferred_element_type=F32))
        merged = merged + gate * _dot(ys[i], wb_ref[i])
    out = _dot(merged, wo_ref[...])
    x1_ref[...] = x + mod[:, 2 * D_MODEL:3 * D_MODEL] * out


def _merge(x, mod3, g, y_a, y_b, o_f, o_b, ex_f, ex_b, u_r, g_f, g_b, u_g, prm, cond_base, rows_per_cond):
    t = x.shape[0]
    row = lambda i: (i, 0)
    mix = pl.BlockSpec((TM, MIX_W), row)
    vec = _full((1, MIX_W))
    return pl.pallas_call(
        _merge_kernel,
        grid=(t // TM,),
        in_specs=[pl.BlockSpec((TM, D_MODEL), row),
                  pl.BlockSpec((1, 1, 6 * D_MODEL), _cond_map(cond_base, rows_per_cond)),
                  _full((1, D_MODEL)),
                  mix, mix, mix, mix, mix, mix,
                  pl.BlockSpec((TM, 128), lambda i: (i, 7)),
                  mix, mix,
                  pl.BlockSpec((TM, MIX_W), lambda i: (i, 2)),
                  vec, _full((128, MIX_W)), vec,
                  _full((D_MODEL, 4 * D_MODEL)), _full((4, MIX_W, D_MODEL)), _full((D_MODEL, D_MODEL))],
        out_specs=pl.BlockSpec((TM, D_MODEL), row),
        out_shape=jax.ShapeDtypeStruct((t, D_MODEL), F32),
        compiler_params=_cparams(("parallel",)),
        name="merge_out",
    )(x, mod3, g, y_a, y_b, o_f, o_b, ex_f, ex_b, u_r, g_f, g_b, u_g,
      prm["gn"], prm["bg"], prm["gla_norm"], prm["w_gate"], prm["w_branch"], prm["w_out"])


def _mlp_kernel(x_ref, mod_ref, g_ref, w1_ref, w2_ref, fg_ref, x2_ref, y_ref):
    mod = mod_ref[0]
    x = x_ref[...]
    h = _rms_mod(x, g_ref[...], mod[:, 4 * D_MODEL:5 * D_MODEL], mod[:, 3 * D_MODEL:4 * D_MODEL]).astype(BF16)
    ff = jnp.zeros((TM, D_MODEL), F32)
    for c in range(D_FF // D_MODEL):
        cols = slice(c * D_MODEL, (c + 1) * D_MODEL)
        a = jnp.maximum(jnp.dot(h, w1_ref[:, cols], preferred_element_type=F32), 0.0)
        ff = ff + _dot(a * a, w2_ref[cols, :])
    x2 = x + mod[:, 5 * D_MODEL:6 * D_MODEL] * ff
    x2_ref[...] = x2
    y_ref[...] = x2 * lax.rsqrt(jnp.mean(x2 * x2, axis=-1, keepdims=True) + EPS) * fg_ref[...]


def _mlp(x1, mod3, g, w1, w2, final_g, cond_base, rows_per_cond):
    t = x1.shape[0]
    row = lambda i: (i, 0)
    return pl.pallas_call(
        _mlp_kernel,
        grid=(t // TM,),
        in_specs=[pl.BlockSpec((TM, D_MODEL), row),
                  pl.BlockSpec((1, 1, 6 * D_MODEL), _cond_map(cond_base, rows_per_cond)),
                  _full((1, D_MODEL)), _full((D_MODEL, D_FF)), _full((D_FF, D_MODEL)), _full((1, D_MODEL))],
        out_specs=[pl.BlockSpec((TM, D_MODEL), row), pl.BlockSpec((TM, D_MODEL), row)],
        out_shape=[jax.ShapeDtypeStruct((t, D_MODEL), F32), jax.ShapeDtypeStruct((t, D_MODEL), F32)],
        compiler_params=_cparams(("parallel",)),
        name="mlp",
    )(x1, mod3, g, w1, w2, final_g)


def _pos_embed_2d(n_tok, d):
    rows = n_tok // GRID_W
    quarter = d // 4
    omega = 1.0 / (POS_BASE ** (jnp.arange(quarter, dtype=F32) / quarter))
    ar = jnp.arange(rows, dtype=F32)[:, None] * omega
    ac = jnp.arange(GRID_W, dtype=F32)[:, None] * omega
    shape = (rows, GRID_W, quarter)
    return jnp.concatenate([jnp.broadcast_to(jnp.sin(ar)[:, None], shape), jnp.broadcast_to(jnp.cos(ar)[:, None], shape),
                            jnp.broadcast_to(jnp.sin(ac)[None], shape), jnp.broadcast_to(jnp.cos(ac)[None], shape)],
                           axis=-1)


def _rwkv_state_to_bd(s):
    lead = s.shape[:-3]
    s2 = s.reshape(lead + (2, 2, 64, 64))
    bd = jnp.einsum("...phvk,hg->...phvgk", s2, jnp.eye(2, dtype=s.dtype))
    return bd.reshape(lead + (2, 128, 128))


def _rwkv_state_from_bd(bd):
    lead = bd.shape[:-3]
    b5 = bd.reshape(lead + (2, 2, 64, 2, 64))
    s = jnp.stack([b5[..., 0, :, 0, :], b5[..., 1, :, 1, :]], axis=-3)
    return s.reshape(lead + (4, 64, 64))


def _gla_state_to_bd(s):
    lead = s.shape[:-3]
    bd = jnp.einsum("...hde,hg->...hegd", s, jnp.eye(4, dtype=s.dtype))
    return bd.reshape(lead + (256, 128))


def _gla_state_from_bd(bd):
    lead = bd.shape[:-2]
    b4 = bd.reshape(lead + (4, 64, 4, GLA_DK))
    s = jnp.stack([b4[..., h, :, h, :] for h in range(4)], axis=-3)
    return jnp.swapaxes(s, -1, -2)


def _pad_rows(m, start, total):
    return jnp.zeros((total, m.shape[1]), m.dtype).at[start:start + m.shape[0]].set(m)


def _layer_params(l, p):
    w_in = p["w_in"][l]
    out = {
        "w_mix": jnp.pad(w_in[:, :P_MIX], ((0, 0), (0, P_MIX_PAD - P_MIX))).astype(BF16),
        "w_gate": w_in[:, P_MIX:].astype(BF16),
        "w_branch": p["w_branch"][l].astype(BF16),
        "w_out": p["w_out"][l].astype(BF16),
        "w1": p["mlp_w1"][l].astype(BF16),
        "w2": p["mlp_w2"][l].astype(BF16),
        "n1": p["norm1_g"][l].reshape(1, D_MODEL),
        "n2": p["norm2_g"][l].reshape(1, D_MODEL),
        "pool_w": jnp.einsum("gcd,gh->gchd", p["pool_w"][l], jnp.eye(4, dtype=F32)).reshape(MIX_W, MIX_W).astype(BF16),
        "pool_scale": p["pool_scale"][l].reshape(1, MIX_W),
        "gn": p["rwkv_gn"][l].reshape(1, MIX_W),
        "bg": _pad_rows(p["rwkv_bg"][l], 64, 128).astype(BF16),
        "gla_norm": p["gla_norm"][l].reshape(1, MIX_W),
    }
    for d in range(2):
        out["rwkv%d" % d] = {
            "mu": p["rwkv_mu"][l].reshape(1, 768),
            "kk": p["rwkv_kk"][l].reshape(1, MIX_W),
            "ka": p["rwkv_ka"][l].reshape(1, MIX_W),
            "rk": p["rwkv_rk"][l].reshape(1, MIX_W),
            "w0": p["rwkv_w0"][l, d].reshape(1, MIX_W),
            "a0": p["rwkv_a0"][l, d].reshape(1, MIX_W),
            "bw": _pad_rows(p["rwkv_bw"][l, d], 64 * d, 128).astype(BF16),
            "ba": _pad_rows(p["rwkv_ba"][l, d], 32 * d, 128).astype(BF16),
        }
        out["gla_ab%d" % d] = _pad_rows(p["gla_ab"][l, d], 16 * d, 128).astype(BF16)
        out["gla_abias%d" % d] = p["gla_abias"][l, d].reshape(1, 128)
    return out


def _run_layer(x, mod3, prm, final_g, s_rwkv0, s_gla0, nseq, seq_len, cond_base, rows_per_cond):
    u_p, u_f, u_r, u_g = _inproj(x, mod3, prm["n1"], prm["w_mix"], cond_base, rows_per_cond)
    y_a = _pool(u_p, prm["pool_w"], prm["pool_scale"], nseq, seq_len)
    if seq_len == FFT_N1 * FFT_N1:
        y_b = _four_large(u_f, nseq, seq_len)
    else:
        y_b = _four_small(u_f, nseq, seq_len)
    o_f, ex_f, sr_f = _rwkv_dir(u_r, prm["rwkv0"], s_rwkv0[0], nseq, seq_len, False)
    o_b, ex_b, sr_b = _rwkv_dir(u_r, prm["rwkv1"], s_rwkv0[1], nseq, seq_len, True)
    g_f, sg_f = _gla_dir(u_g, prm["gla_ab0"], prm["gla_abias0"], s_gla0[0], nseq, seq_len, False)
    g_b, sg_b = _gla_dir(u_g, prm["gla_ab1"], prm["gla_abias1"], s_gla0[1], nseq, seq_len, True)
    x1 = _merge(x, mod3, prm["n1"], y_a, y_b, o_f, o_b, ex_f, ex_b, u_r, g_f, g_b, u_g, prm,
                cond_base, rows_per_cond)
    x2, y = _mlp(x1, mod3, prm["n2"], prm["w1"], prm["w2"], final_g, cond_base, rows_per_cond)
    return x2, y, jnp.stack([sr_f, sr_b], axis=1), jnp.stack([sg_f, sg_b], axis=1)


def kernel(x_prompt, x_sample, state_rwkv, state_gla, c, c_ctx, ada_w, ada_b, norm1_g, norm2_g, w_in, pool_w, pool_scale, rwkv_mu, rwkv_w0, rwkv_bw, rwkv_a0, rwkv_ba, rwkv_kk, rwkv_ka, rwkv_bg, rwkv_rk, rwkv_gn, gla_ab, gla_abias, gla_norm, w_branch, w_out, mlp_w1, mlp_w2, final_g):
    p = dict(w_in=w_in, pool_w=pool_w, pool_scale=pool_scale, rwkv_mu=rwkv_mu, rwkv_w0=rwkv_w0,
             rwkv_bw=rwkv_bw, rwkv_a0=rwkv_a0, rwkv_ba=rwkv_ba, rwkv_kk=rwkv_kk, rwkv_ka=rwkv_ka,
             rwkv_bg=rwkv_bg, rwkv_rk=rwkv_rk, rwkv_gn=rwkv_gn, gla_ab=gla_ab, gla_abias=gla_abias,
             gla_norm=gla_norm, w_branch=w_branch, w_out=w_out, mlp_w1=mlp_w1, mlp_w2=mlp_w2,
             norm1_g=norm1_g, norm2_g=norm2_g)
    bp, lp, _ = x_prompt.shape
    bs, ls, _ = x_sample.shape
    cond8 = jnp.zeros((8, D_MODEL), F32).at[0].set(c_ctx).at[1:1 + bs].set(c)
    mods = _ada(cond8, ada_w, ada_b)
    fg = final_g.reshape(1, D_MODEL)

    xp = x_prompt.reshape(bp * lp, D_MODEL)
    xs = (x_sample + _pos_embed_2d(ls, D_MODEL).reshape(1, ls, D_MODEL)).reshape(bs * ls, D_MODEL)
    zr = jnp.zeros((2, bp, 2, 128, 128), F32)
    zg = jnp.zeros((2, bp, 256, 128), F32)
    sr_in = _rwkv_state_to_bd(state_rwkv)
    sg_in = _gla_state_to_bd(state_gla)
    new_r, new_g = [], []
    for l in range(DEPTH):
        prm = _layer_params(l, p)
        mod3 = mods[l].reshape(8, 1, 6 * D_MODEL)
        xp, yp, s_r, s_g = _run_layer(xp, mod3, prm, fg, zr, zg, bp, lp, 0, bp * lp)
        new_r.append(_rwkv_state_from_bd(s_r))
        new_g.append(_gla_state_from_bd(s_g))
        xs, ys, _, _ = _run_layer(xs, mod3, prm, fg, jnp.swapaxes(sr_in[:, l], 0, 1),
                                  jnp.swapaxes(sg_in[:, l], 0, 1), bs, ls, 1, ls)
    y_prompt = yp.reshape(bp, lp, D_MODEL)
    y_sample = ys.reshape(bs, ls, D_MODEL)
    return (y_prompt, y_sample, jnp.stack(new_r, axis=1), jnp.stack(new_g, axis=1))
```

```python
import functools

import numpy as np
import jax
import jax.numpy as jnp
from jax import lax
from jax.experimental import pallas as pl
from jax.experimental.pallas import tpu as pltpu

F32 = jnp.float32
BF16 = jnp.bfloat16

D_MODEL = 1024
DEPTH = 2
GRID_W = 64
POS_BASE = 10000.0
MIX_W = 256
POOL_WINDOWS = (2, 4, 8, 16)
POOL_HALO = 8
RWKV_DECAY_SCALE = 0.606531
RWKV_GN_EPS = 64e-5
GLA_DK = 32
GLA_GATE_NORM = 16.0
D_FF = 4 * D_MODEL
EPS = 1e-6
P_MIX = 2336
P_MIX_PAD = 2432
CHUNK = 64
SUB = 16

TM = 512
TB = 256
FFT_N1 = 64
VMEM_LIMIT = 56 * 1024 * 1024


def _cparams(sem):
    return pltpu.CompilerParams(dimension_semantics=sem, vmem_limit_bytes=VMEM_LIMIT)


def _full(shape):
    n = len(shape)
    return pl.BlockSpec(shape, lambda *_: (0,) * n)


def _dot(a, b):
    return jnp.dot(a.astype(BF16), b.astype(BF16), preferred_element_type=F32)


def _dot_nt(a, b):
    return lax.dot_general(a.astype(BF16), b.astype(BF16), (((1,), (1,)), ((), ())),
                           preferred_element_type=F32)


def _dot_tn(a, b):
    return lax.dot_general(a.astype(BF16), b.astype(BF16), (((0,), (0,)), ((), ())),
                           preferred_element_type=F32)


def _split2(a):
    hi = a.astype(BF16)
    lo = (a - hi.astype(F32)).astype(BF16)
    return hi, lo


def _split3(a):
    hi = a.astype(BF16)
    r = a - hi.astype(F32)
    mid = r.astype(BF16)
    lo = (r - mid.astype(F32)).astype(BF16)
    return hi, mid, lo


def _dot_exact_rhs(a, w):
    hi, lo = _split2(a)
    return (jnp.dot(hi, w, preferred_element_type=F32) + jnp.dot(lo, w, preferred_element_type=F32))


def _dot_exact_lhs3(w, a):
    hi, mid, lo = _split3(a)
    return (jnp.dot(w, hi, preferred_element_type=F32) + jnp.dot(w, mid, preferred_element_type=F32)
            + jnp.dot(w, lo, preferred_element_type=F32))


def _seg_ones(n, seg, scale=1.0):
    r = lax.broadcasted_iota(jnp.int32, (n, n), 0)
    c = lax.broadcasted_iota(jnp.int32, (n, n), 1)
    return jnp.where((r // seg) == (c // seg), scale, 0.0).astype(BF16)


def _log_sigmoid(x):
    return jnp.minimum(x, 0.0) - jnp.log(1.0 + jnp.exp(-jnp.abs(x)))


def _rms_mod(x, g, sc, sh):
    ms = jnp.mean(x * x, axis=-1, keepdims=True)
    return x * lax.rsqrt(ms + EPS) * g * (1.0 + sc) + sh


def _ada_kernel(c_ref, w_ref, b_ref, o_ref):
    c = c_ref[...]
    s = c * jax.nn.sigmoid(c)
    hi, lo = _split2(s)
    w = w_ref[0].astype(BF16)
    o_ref[0] = (jnp.dot(hi, w, preferred_element_type=F32) + jnp.dot(lo, w, preferred_element_type=F32)
                + b_ref[0])


def _ada(cond8, ada_w, ada_b):
    tn = 1536
    n = 6 * D_MODEL
    return pl.pallas_call(
        _ada_kernel,
        grid=(DEPTH, n // tn),
        in_specs=[_full((8, D_MODEL)),
                  pl.BlockSpec((1, D_MODEL, tn), lambda l, j: (l, 0, j)),
                  pl.BlockSpec((1, 1, tn), lambda l, j: (l, 0, j))],
        out_specs=pl.BlockSpec((1, 8, tn), lambda l, j: (l, 0, j)),
        out_shape=jax.ShapeDtypeStruct((DEPTH, 8, n), F32),
        compiler_params=_cparams(("parallel", "parallel")),
        name="ada_mod",
    )(cond8, ada_w, ada_b.reshape(DEPTH, 1, n))


def _inproj_kernel(x_ref, mod_ref, g_ref, w_ref, up_ref, uf_ref, ur_ref, ug_ref):
    mod = mod_ref[0]
    h = _rms_mod(x_ref[...], g_ref[...], mod[:, D_MODEL:2 * D_MODEL], mod[:, 0:D_MODEL]).astype(BF16)
    up_ref[...] = jnp.dot(h, w_ref[:, 0:256], preferred_element_type=F32)
    uf_ref[...] = jnp.dot(h, w_ref[:, 256:512], preferred_element_type=F32)
    ur_ref[...] = jnp.dot(h, w_ref[:, 512:1536], preferred_element_type=F32)
    ug_ref[...] = jnp.dot(h, w_ref[:, 1536:P_MIX_PAD], preferred_element_type=F32)


def _cond_map(cond_base, rows_per_cond):
    return lambda i: (cond_base + (i * TM) // rows_per_cond, 0, 0)


def _inproj(x, mod3, g, w, cond_base, rows_per_cond):
    t = x.shape[0]
    row = lambda i: (i, 0)
    return pl.pallas_call(
        _inproj_kernel,
        grid=(t // TM,),
        in_specs=[pl.BlockSpec((TM, D_MODEL), row),
                  pl.BlockSpec((1, 1, 6 * D_MODEL), _cond_map(cond_base, rows_per_cond)),
                  _full((1, D_MODEL)),
                  _full((D_MODEL, P_MIX_PAD))],
        out_specs=[pl.BlockSpec((TM, 256), row), pl.BlockSpec((TM, 256), row),
                   pl.BlockSpec((TM, 1024), row), pl.BlockSpec((TM, 896), row)],
        out_shape=[jax.ShapeDtypeStruct((t, 256), F32), jax.ShapeDtypeStruct((t, 256), F32),
                   jax.ShapeDtypeStruct((t, 1024), F32), jax.ShapeDtypeStruct((t, 896), F32)],
        compiler_params=_cparams(("parallel",)),
        name="in_proj",
    )(x, mod3, g, w)


def _pool_kernel(prev_ref, cur_ref, next_ref, wbd_ref, scale_ref, y_ref, *, seq_len):
    j = pl.program_id(1)
    nb = seq_len // TB
    cur = cur_ref[...]
    prev = jnp.where(j > 0, prev_ref[...], 0.0)
    nxt = jnp.where(j < nb - 1, next_ref[...], 0.0)
    ext = jnp.concatenate([prev, cur, nxt], axis=0)
    ext_hi, ext_lo = _split2(ext)
    ke = TB + 2 * POOL_HALO
    d = (lax.broadcasted_iota(jnp.int32, (TB, ke), 1) - POOL_HALO
         - lax.broadcasted_iota(jnp.int32, (TB, ke), 0))
    pos = j * TB + lax.broadcasted_iota(jnp.int32, (TB, MIX_W), 0)
    group = lax.broadcasted_iota(jnp.int32, (TB, MIX_W), 1) // (MIX_W // len(POOL_WINDOWS))
    pooled = jnp.zeros((TB, MIX_W), F32)
    for gi, win in enumerate(POOL_WINDOWS):
        before, after = win // 2, win - win // 2 - 1
        band = jnp.where((d >= -before) & (d <= after), 1.0, 0.0).astype(BF16)
        s = (jnp.dot(band, ext_hi, preferred_element_type=F32)
             + jnp.dot(band, ext_lo, preferred_element_type=F32))
        cnt = (jnp.minimum(pos + after, seq_len - 1) - jnp.maximum(pos - before, 0) + 1).astype(F32)
        pooled = jnp.where(group == gi, s / cnt - cur, pooled)
    y_ref[...] = _dot(pooled, wbd_ref[...]) * scale_ref[...]


def _pool(u_p, wbd, scale, nseq, seq_len):
    nb = seq_len // TB
    hb = TB // POOL_HALO
    nh = u_p.shape[0] // POOL_HALO
    return pl.pallas_call(
        functools.partial(_pool_kernel, seq_len=seq_len),
        grid=(nseq, nb),
        in_specs=[pl.BlockSpec((POOL_HALO, MIX_W), lambda b, j: (jnp.maximum((b * nb + j) * hb - 1, 0), 0)),
                  pl.BlockSpec((TB, MIX_W), lambda b, j: (b * nb + j, 0)),
                  pl.BlockSpec((POOL_HALO, MIX_W), lambda b, j: (jnp.minimum((b * nb + j + 1) * hb, nh - 1), 0)),
                  _full((MIX_W, MIX_W)), _full((1, MIX_W))],
        out_specs=pl.BlockSpec((TB, MIX_W), lambda b, j: (b * nb + j, 0)),
        out_shape=jax.ShapeDtypeStruct(u_p.shape, F32),
        compiler_params=_cparams(("parallel", "parallel")),
        name="pool_mixer",
    )(u_p, u_p, u_p, wbd, scale)


def _dft_tables(n):
    t = np.arange(n)
    ang = 2.0 * np.pi * ((np.outer(t, t)) % n) / n
    return np.cos(ang), np.sin(ang)


def _bd_np(m, copies):
    k = m.shape[0]
    out = np.zeros((k * copies, k * copies), m.dtype)
    for i in range(copies):
        out[i * k:(i + 1) * k, i * k:(i + 1) * k] = m
    return out


def _four_small_kernel(z_ref, cs_ref, cbd_ref, sbd_ref, y_ref, *, seq_len, norm):
    zf = _dot(cs_ref[...], z_ref[...])
    y = _dot(zf[:seq_len], cbd_ref[...]) + _dot(zf[seq_len:], sbd_ref[...])
    y_ref[...] = y * norm


def _four_small(u_f, nseq, seq_len):
    c, s = _dft_tables(seq_len)
    cs = jnp.asarray(np.concatenate([c, -s], axis=0), F32)
    c64, s64 = _dft_tables(MIX_W // 4)
    cbd = jnp.asarray(_bd_np(c64, 4), F32)
    sbd = jnp.asarray(_bd_np(s64, 4), F32)
    return pl.pallas_call(
        functools.partial(_four_small_kernel, seq_len=seq_len, norm=float((seq_len * 64) ** -0.5)),
        grid=(nseq,),
        in_specs=[pl.BlockSpec((seq_len, MIX_W), lambda b: (b, 0)),
                  _full((2 * seq_len, seq_len)), _full((MIX_W, MIX_W)), _full((MIX_W, MIX_W))],
        out_specs=pl.BlockSpec((seq_len, MIX_W), lambda b: (b, 0)),
        out_shape=jax.ShapeDtypeStruct(u_f.shape, F32),
        compiler_params=_cparams(("parallel",)),
        name="fourier_small",
    )(u_f, cs, cbd, sbd)


def _fft1_kernel(z_ref, fs_ref, a_ref):
    a_ref[...] = _dot(fs_ref[...], z_ref[...])


def _fft2_kernel(a_ref, g_ref, cbd_ref, sbd_ref, y_ref, *, f1b, norm):
    for i in range(f1b):
        a = jnp.concatenate([a_ref[0, i], a_ref[1, i]], axis=0)
        x = _dot(g_ref[i], a)
        y = _dot(x[:FFT_N1], cbd_ref[...]) + _dot(x[FFT_N1:], sbd_ref[...])
        y_ref[:, i * MIX_W:(i + 1) * MIX_W] = y * norm


def _four_large(u_f, nseq, seq_len):
    n1 = FFT_N1
    assert seq_len == n1 * n1
    width = n1 * MIX_W
    c64, s64 = _dft_tables(n1)
    fs = jnp.asarray(np.concatenate([c64, -s64], axis=0), F32)
    f1 = np.arange(n1)[:, None, None]
    f2 = np.arange(n1)[None, :, None]
    t2 = np.arange(n1)[None, None, :]
    ang = 2.0 * np.pi * ((t2 * (f1 + n1 * f2)) % seq_len) / seq_len
    gr, gi = np.cos(ang), -np.sin(ang)
    g = jnp.asarray(np.concatenate([np.concatenate([gr, -gi], axis=2),
                                    np.concatenate([gi, gr], axis=2)], axis=1), F32)
    cbd = jnp.asarray(_bd_np(c64, 4), F32)
    sbd = jnp.asarray(_bd_np(s64, 4), F32)
    nc = 2048
    a = pl.pallas_call(
        _fft1_kernel,
        grid=(nseq, width // nc),
        in_specs=[pl.BlockSpec((None, n1, nc), lambda b, j: (b, 0, j)), _full((2 * n1, n1))],
        out_specs=pl.BlockSpec((None, 2 * n1, nc), lambda b, j: (b, 0, j)),
        out_shape=jax.ShapeDtypeStruct((nseq, 2 * n1, width), F32),
        compiler_params=_cparams(("parallel", "parallel")),
        name="fourier_stage1",
    )(u_f.reshape(nseq, n1, width), fs)
    f1b = 8
    y = pl.pallas_call(
        functools.partial(_fft2_kernel, f1b=f1b, norm=float((seq_len * 64) ** -0.5)),
        grid=(nseq, n1 // f1b),
        in_specs=[pl.BlockSpec((None, 2, f1b, n1, MIX_W), lambda b, j: (b, 0, j, 0, 0)),
                  pl.BlockSpec((f1b, 2 * n1, 2 * n1), lambda b, j: (j, 0, 0)),
                  _full((MIX_W, MIX_W)), _full((MIX_W, MIX_W))],
        out_specs=pl.BlockSpec((None, n1, f1b * MIX_W), lambda b, j: (b, 0, j)),
        out_shape=jax.ShapeDtypeStruct((nseq, n1, width), F32),
        compiler_params=_cparams(("parallel", "parallel")),
        name="fourier_stage2",
    )(a.reshape(nseq, 2, n1, n1, MIX_W), g, cbd, sbd)
    return y.reshape(nseq * seq_len, MIX_W)


def _bmm(x, y):
    return lax.dot_general(x.astype(BF16), y.astype(BF16), (((2,), (1,)), ((0,), (0,))),
                           preferred_element_type=F32)


def _bmm_nt(x, y):
    return lax.dot_general(x.astype(BF16), y.astype(BF16), (((2,), (2,)), ((0,), (0,))),
                           preferred_element_type=F32)


def _bmm_tn(x, y):
    return lax.dot_general(x.astype(BF16), y.astype(BF16), (((1,), (1,)), ((0,), (0,))),
                           preferred_element_type=F32)


def _slab_bd(y):
    lane = lax.broadcasted_iota(jnp.int32, y.shape, 2)
    z = jnp.zeros_like(y)
    return jnp.concatenate([jnp.where(lane < 64, y, z), jnp.where(lane >= 64, y, z)], axis=1)


def _mm(x, y):
    return _bmm(x, _slab_bd(y.astype(BF16)))


def _mm3(x, y):
    xh, xl = _split2(x)
    yh, yl = _split2(y)
    bh, bl = _slab_bd(yh), _slab_bd(yl)
    return _bmm(xh, bh) + (_bmm(xh, bl) + _bmm(xl, bh))


def _mm_nt(x, y):
    return _bmm_nt(x, _slab_bd(y.astype(BF16)))


def _to_slabs(x):
    x4 = x.reshape(TB // CHUNK, CHUNK, MIX_W)
    return jnp.stack([x4[:, :, :128], x4[:, :, 128:]], axis=1).reshape(2 * TB // CHUNK, CHUNK, 128)


def _tri_inverse(n_mat, eye, blk):
    nd = jnp.where(blk, n_mat, 0.0)
    no = n_mat - nd
    n2 = _mm3(nd, nd)
    n4 = _mm3(n2, n2)
    n8 = _mm3(n4, n4)
    x = n2 - nd - _mm3(nd, n2)
    x = x + n4 + _mm3(x, n4)
    x = x + n8 + _mm3(x, n8)
    td = eye + x
    e = _mm(td, no)
    e2 = _mm(e, e)
    e3 = _mm(e, e2)
    return td + _mm(e2 - e - e3, td)


def _rwkv_kernel(cur_ref, prev_ref, next_ref, mu_ref, kkw_ref, ka_ref, rk_ref, w0_ref, a0_ref,
                 bw_ref, ba_ref, s0_ref, o_ref, ex_ref, sfin_ref, s_scr, *, rev, nb):
    j = pl.program_id(1)
    jj = (nb - 1 - j) if rev else j

    @pl.when(j == 0)
    def _():
        s_scr[...] = s0_ref[0]

    z = cur_ref[:, 0:768]
    prow = jnp.where(jj > 0, prev_ref[POOL_HALO - 1:POOL_HALO, :], 0.0)
    nrow = jnp.where(jj < nb - 1, next_ref[0:1, :], 0.0)
    rowid = lax.broadcasted_iota(jnp.int32, (TB, 768), 0)
    up = jnp.where(rowid == 0, prow, pltpu.roll(z, 1, 0))
    dn = jnp.where(rowid == TB - 1, nrow, pltpu.roll(z, TB - 1, 0))
    zm = z + mu_ref[...] * (0.5 * (up + dn) - z)
    r = zm[:, 0:256]
    k = zm[:, 256:512]
    v = zm[:, 512:768]
    ones_bd = _seg_ones(MIX_W, 64)
    kk = k * kkw_ref[...]
    kk = kk * lax.rsqrt(_dot_exact_rhs(kk * kk, ones_bd) + EPS)
    w_log = w0_ref[...] + _dot(jnp.tanh(cur_ref[:, 768:896]), bw_ref[...])
    lw = -RWKV_DECAY_SCALE * jax.nn.sigmoid(w_log)
    a = jax.nn.sigmoid(a0_ref[...] + _dot(cur_ref[:, 896:1024], ba_ref[...]))
    kd = k * (1.0 + (a - 1.0) * ka_ref[...])
    ex_ref[...] = _dot_exact_rhs(r * kd * rk_ref[...], ones_bd) * v

    nch = TB // CHUNK
    tr = lax.broadcasted_iota(jnp.int32, (TB, TB), 0)
    tc = lax.broadcasted_iota(jnp.int32, (TB, TB), 1)
    same_chunk = (tr // CHUNK) == (tc // CHUNK)
    tri = jnp.where(same_chunk & ((tc >= tr) if rev else (tc <= tr)), 1.0, 0.0).astype(BF16)
    cum = _dot_exact_lhs3(tri, lw)
    e_in = jnp.exp(cum)
    e_out = jnp.exp(-cum)
    last = 0 if rev else CHUNK - 1
    rt = _to_slabs(r * e_in)
    kh = _to_slabs(kd * e_out)
    bh = _to_slabs(kk * a * e_out)
    kt = _to_slabs(kk * jnp.exp(cum - lw))
    vv = _to_slabs(v)
    gam = _to_slabs(e_in)[:, last:last + 1, :]

    t_i = lax.broadcasted_iota(jnp.int32, (1, CHUNK, 128), 1)
    s_i = lax.broadcasted_iota(jnp.int32, (1, CHUNK, 128), 2) & (CHUNK - 1)
    if rev:
        strict, incl = s_i > t_i, s_i >= t_i
    else:
        strict, incl = s_i < t_i, s_i <= t_i
    blk = (s_i // SUB) == (t_i // SUB)
    eye = jnp.where(s_i == t_i, 1.0, 0.0)

    lhs = jnp.concatenate([kt, rt], axis=1)
    mk = _mm_nt(lhs, kh)
    mb = _mm_nt(lhs, bh)
    m_mat = jnp.where(strict, mk[:, :CHUNK], 0.0)
    p_mat = jnp.where(incl, mk[:, CHUNK:], 0.0)
    n_mat = jnp.where(strict, mb[:, :CHUNK], 0.0)
    q_mat = jnp.where(incl, mb[:, CHUNK:], 0.0)
    t_mat = _tri_inverse(n_mat, eye, blk)
    mpv = _mm(jnp.concatenate([m_mat, p_mat], axis=1), vv)
    mv, pv = mpv[:, :CHUNK], mpv[:, CHUNK:]
    tk = _mm(t_mat, kt)
    uv = _mm(t_mat, mv)
    rc = rt - _mm(q_mat, tk)
    oc = pv - _mm(q_mat, uv)
    bd_r = lax.broadcasted_iota(jnp.int32, (1, 128, 128), 1) // 64
    bd_c = lax.broadcasted_iota(jnp.int32, (1, 128, 128), 2) // 64
    braw = jnp.where(bd_r == bd_c,
                     _bmm_tn(jnp.concatenate([vv, uv], axis=1), jnp.concatenate([kh, -bh], axis=1)), 0.0)
    tk_bd = _slab_bd(tk.astype(BF16))
    bh_bd = _slab_bd(bh.astype(BF16))

    s_cur = s_scr[...]
    for ci in range(nch):
        c = (nch - 1 - ci) if rev else ci
        g = slice(2 * c, 2 * c + 2)
        o = _bmm_nt(rc[g], s_cur) + oc[g]
        o_ref[c * CHUNK:(c + 1) * CHUNK, 0:128] = o[0]
        o_ref[c * CHUNK:(c + 1) * CHUNK, 128:256] = o[1]
        w2 = _bmm(_bmm_nt(s_cur, tk_bd[g]), bh_bd[g])
        s_cur = (s_cur - w2 + braw[g]) * gam[g]
    s_scr[...] = s_cur

    @pl.when(j == nb - 1)
    def _():
        sfin_ref[0] = s_scr[...]


def _rwkv_dir(u_r, prm, s0, nseq, seq_len, rev):
    nb = seq_len // TB
    hb = TB // POOL_HALO
    nh = u_r.shape[0] // POOL_HALO
    t = u_r.shape[0]

    def jj(j):
        return (nb - 1 - j) if rev else j

    cur_map = lambda b, j: (b * nb + jj(j), 0)
    st_map = lambda b, j: (b, 0, 0, 0)
    vec = _full((1, MIX_W))
    return pl.pallas_call(
        functools.partial(_rwkv_kernel, rev=rev, nb=nb),
        grid=(nseq, nb),
        in_specs=[pl.BlockSpec((TB, 1024), cur_map),
                  pl.BlockSpec((POOL_HALO, 768), lambda b, j: (jnp.maximum((b * nb + jj(j)) * hb - 1, 0), 0)),
                  pl.BlockSpec((POOL_HALO, 768), lambda b, j: (jnp.minimum((b * nb + jj(j) + 1) * hb, nh - 1), 0)),
                  _full((1, 768)), vec, vec, vec, vec, vec,
                  _full((128, MIX_W)), _full((128, MIX_W)),
                  pl.BlockSpec((1, 2, 128, 128), st_map)],
        out_specs=[pl.BlockSpec((TB, MIX_W), cur_map), pl.BlockSpec((TB, MIX_W), cur_map),
                   pl.BlockSpec((1, 2, 128, 128), st_map)],
        out_shape=[jax.ShapeDtypeStruct((t, MIX_W), F32), jax.ShapeDtypeStruct((t, MIX_W), F32),
                   jax.ShapeDtypeStruct((nseq, 2, 128, 128), F32)],
        scratch_shapes=[pltpu.VMEM((2, 128, 128), F32)],
        compiler_params=_cparams(("parallel", "arbitrary")),
        name="rwkv_bwd" if rev else "rwkv_fwd",
    )(u_r, u_r, u_r, prm["mu"], prm["kk"], prm["ka"], prm["rk"], prm["w0"], prm["a0"],
      prm["bw"], prm["ba"], s0)


def _rows_bd4(y, width):
    lane = lax.broadcasted_iota(jnp.int32, y.shape, 2) // width
    z = jnp.zeros_like(y)
    return jnp.concatenate([jnp.where(lane == h, y, z) for h in range(4)], axis=1)


def _gla_kernel(u_ref, ab_ref, abias_ref, s0_ref, o_ref, sfin_ref, s_scr, *, rev, nb):
    j = pl.program_id(1)

    @pl.when(j == 0)
    def _():
        s_scr[...] = s0_ref[0]

    nch = TB // CHUNK
    la = _log_sigmoid(_dot(u_ref[:, 768:896], ab_ref[...]) + abias_ref[...]) * (1.0 / GLA_GATE_NORM)
    q = u_ref[:, 0:128] * float(GLA_DK ** -0.5)
    kg = u_ref[:, 128:256]
    vg3 = u_ref[:, 256:512].reshape(nch, CHUNK, MIX_W)

    tr = lax.broadcasted_iota(jnp.int32, (TB, TB), 0)
    tc = lax.broadcasted_iota(jnp.int32, (TB, TB), 1)
    same_chunk = (tr // CHUNK) == (tc // CHUNK)
    tri = jnp.where(same_chunk & ((tc >= tr) if rev else (tc <= tr)), 1.0, 0.0).astype(BF16)
    b = _dot_exact_lhs3(tri, la)
    t_i = lax.broadcasted_iota(jnp.int32, (1, CHUNK, MIX_W), 1)
    s_i = lax.broadcasted_iota(jnp.int32, (1, CHUNK, MIX_W), 2) & (CHUNK - 1)

    def to3(x):
        return x.reshape(nch, CHUNK, 128)

    att = jnp.where(s_i == t_i, _bmm_nt(to3(q), _rows_bd4(to3(kg).astype(BF16), GLA_DK)), 0.0)
    for lev in range(1, 7):
        size, half = 1 << lev, 1 << (lev - 1)
        same = (t_i // size) == (s_i // size)
        if rev:
            mask = same & ((t_i & (size - 1)) < half) & ((s_i & (size - 1)) >= half)
            ref_row = (tr // size) * size + half
        else:
            mask = same & ((t_i & (size - 1)) >= half) & ((s_i & (size - 1)) < half)
            ref_row = (tr // size) * size + half - 1
        br = _dot_exact_lhs3(jnp.where(tc == ref_row, 1.0, 0.0).astype(BF16), b)
        qe = q * jnp.exp(jnp.minimum(b - br, 0.0))
        ke = kg * jnp.exp(jnp.minimum(br - b, 0.0))
        att = att + jnp.where(mask, _bmm_nt(to3(qe), _rows_bd4(to3(ke).astype(BF16), GLA_DK)), 0.0)
    o_intra = _bmm(att, _rows_bd4(vg3.astype(BF16), 64))

    last = 0 if rev else CHUNK - 1
    b3 = to3(b)
    b_last = b3[:, last:last + 1, :]
    qb = to3(q) * jnp.exp(b3)
    bd_r = lax.broadcasted_iota(jnp.int32, (1, MIX_W, 128), 1) // 64
    bd_c = lax.broadcasted_iota(jnp.int32, (1, MIX_W, 128), 2) // GLA_DK
    inc = jnp.where(bd_r == bd_c, _bmm_tn(vg3, to3(kg) * jnp.exp(b_last - b3)), 0.0)
    decay = jnp.exp(b_last)

    s_cur = s_scr[...]
    for ci in range(nch):
        c = (nch - 1 - ci) if rev else ci
        o_ref[c * CHUNK:(c + 1) * CHUNK, :] = o_intra[c] + _dot_nt(qb[c], s_cur)
        s_cur = s_cur * decay[c] + inc[c]
    s_scr[...] = s_cur

    @pl.when(j == nb - 1)
    def _():
        sfin_ref[0] = s_scr[...]


def _gla_dir(u_g, ab, abias, s0, nseq, seq_len, rev):
    nb = seq_len // TB
    t = u_g.shape[0]
    cur_map = (lambda b, j: (b * nb + nb - 1 - j, 0)) if rev else (lambda b, j: (b * nb + j, 0))
    st_map = lambda b, j: (b, 0, 0)
    return pl.pallas_call(
        functools.partial(_gla_kernel, rev=rev, nb=nb),
        grid=(nseq, nb),
        in_specs=[pl.BlockSpec((TB, 896), cur_map), _full((128, 128)), _full((1, 128)),
                  pl.BlockSpec((1, 256, 128), st_map)],
        out_specs=[pl.BlockSpec((TB, MIX_W), cur_map), pl.BlockSpec((1, 256, 128), st_map)],
        out_shape=[jax.ShapeDtypeStruct((t, MIX_W), F32), jax.ShapeDtypeStruct((nseq, 256, 128), F32)],
        scratch_shapes=[pltpu.VMEM((256, 128), F32)],
        compiler_params=_cparams(("parallel", "arbitrary")),
        name="gla_bwd" if rev else "gla_fwd",
    )(u_g, ab, abias, s0)


def _merge_kernel(x_ref, mod_ref, g_ref, ya_ref, yb_ref, of_ref, ob_ref, exf_ref, exb_ref, cag_ref,
                  gf_ref, gb_ref, gout_ref, gn_ref, bg_ref, gnorm_ref, wg_ref, wb_ref, wo_ref, x1_ref):
    mod = mod_ref[0]
    x = x_ref[...]
    h = _rms_mod(x, g_ref[...], mod[:, D_MODEL:2 * D_MODEL], mod[:, 0:D_MODEL]).astype(BF16)
    mean64 = _seg_ones(MIX_W, 64, 1.0 / 64.0)
    o = of_ref[...] + ob_ref[...]
    mu = _dot_exact_rhs(o, mean64)
    oc = o - mu
    var = _dot_exact_rhs(oc * oc, mean64)
    gate_c = _dot(jax.nn.sigmoid(cag_ref[...]), bg_ref[...])
    y_c = (oc * lax.rsqrt(var + RWKV_GN_EPS) * gn_ref[...] + exf_ref[...] + exb_ref[...]) * gate_c
    og = gf_ref[...] + gb_ref[...]
    gout = gout_ref[...]
    y_d = (og * lax.rsqrt(_dot_exact_rhs(og * og, mean64) + EPS) * gnorm_ref[...]
           * (gout * jax.nn.sigmoid(gout)))
    ys = (ya_ref[...], yb_ref[...], y_c, y_d)
    merged = jnp.zeros((TM, D_MODEL), F32)
    for i in range(4):
        gate = jax.nn.sigmoid(jnp.dot(h, wg_ref[:, i * D_MODEL:(i + 1) * D_MODEL], preferred_element_type=F32))
        merged = merged + gate * _dot(ys[i], wb_ref[i])
    out = _dot(merged, wo_ref[...])
    x1_ref[...] = x + mod[:, 2 * D_MODEL:3 * D_MODEL] * out


def _merge(x, mod3, g, y_a, y_b, o_f, o_b, ex_f, ex_b, u_r, g_f, g_b, u_g, prm, cond_base, rows_per_cond):
    t = x.shape[0]
    row = lambda i: (i, 0)
    mix = pl.BlockSpec((TM, MIX_W), row)
    vec = _full((1, MIX_W))
    return pl.pallas_call(
        _merge_kernel,
        grid=(t // TM,),
        in_specs=[pl.BlockSpec((TM, D_MODEL), row),
                  pl.BlockSpec((1, 1, 6 * D_MODEL), _cond_map(cond_base, rows_per_cond)),
                  _full((1, D_MODEL)),
                  mix, mix, mix, mix, mix, mix,
                  pl.BlockSpec((TM, 128), lambda i: (i, 7)),
                  mix, mix,
                  pl.BlockSpec((TM, MIX_W), lambda i: (i, 2)),
                  vec, _full((128, MIX_W)), vec,
                  _full((D_MODEL, 4 * D_MODEL)), _full((4, MIX_W, D_MODEL)), _full((D_MODEL, D_MODEL))],
        out_specs=pl.BlockSpec((TM, D_MODEL), row),
        out_shape=jax.ShapeDtypeStruct((t, D_MODEL), F32),
        compiler_params=_cparams(("parallel",)),
        name="merge_out",
    )(x, mod3, g, y_a, y_b, o_f, o_b, ex_f, ex_b, u_r, g_f, g_b, u_g,
      prm["gn"], prm["bg"], prm["gla_norm"], prm["w_gate"], prm["w_branch"], prm["w_out"])


def _mlp_kernel(x_ref, mod_ref, g_ref, w1_ref, w2_ref, fg_ref, x2_ref, y_ref):
    mod = mod_ref[0]
    x = x_ref[...]
    h = _rms_mod(x, g_ref[...], mod[:, 4 * D_MODEL:5 * D_MODEL], mod[:, 3 * D_MODEL:4 * D_MODEL]).astype(BF16)
    ff = jnp.zeros((TM, D_MODEL), F32)
    for c in range(D_FF // D_MODEL):
        cols = slice(c * D_MODEL, (c + 1) * D_MODEL)
        a = jnp.maximum(jnp.dot(h, w1_ref[:, cols], preferred_element_type=F32), 0.0)
        ff = ff + _dot(a * a, w2_ref[cols, :])
    x2 = x + mod[:, 5 * D_MODEL:6 * D_MODEL] * ff
    x2_ref[...] = x2
    y_ref[...] = x2 * lax.rsqrt(jnp.mean(x2 * x2, axis=-1, keepdims=True) + EPS) * fg_ref[...]


def _mlp(x1, mod3, g, w1, w2, final_g, cond_base, rows_per_cond):
    t = x1.shape[0]
    row = lambda i: (i, 0)
    return pl.pallas_call(
        _mlp_kernel,
        grid=(t // TM,),
        in_specs=[pl.BlockSpec((TM, D_MODEL), row),
                  pl.BlockSpec((1, 1, 6 * D_MODEL), _cond_map(cond_base, rows_per_cond)),
                  _full((1, D_MODEL)), _full((D_MODEL, D_FF)), _full((D_FF, D_MODEL)), _full((1, D_MODEL))],
        out_specs=[pl.BlockSpec((TM, D_MODEL), row), pl.BlockSpec((TM, D_MODEL), row)],
        out_shape=[jax.ShapeDtypeStruct((t, D_MODEL), F32), jax.ShapeDtypeStruct((t, D_MODEL), F32)],
        compiler_params=_cparams(("parallel",)),
        name="mlp",
    )(x1, mod3, g, w1, w2, final_g)


def _pos_embed_2d(n_tok, d):
    rows = n_tok // GRID_W
    quarter = d // 4
    omega = 1.0 / (POS_BASE ** (jnp.arange(quarter, dtype=F32) / quarter))
    ar = jnp.arange(rows, dtype=F32)[:, None] * omega
    ac = jnp.arange(GRID_W, dtype=F32)[:, None] * omega
    shape = (rows, GRID_W, quarter)
    return jnp.concatenate([jnp.broadcast_to(jnp.sin(ar)[:, None], shape), jnp.broadcast_to(jnp.cos(ar)[:, None], shape),
                            jnp.broadcast_to(jnp.sin(ac)[None], shape), jnp.broadcast_to(jnp.cos(ac)[None], shape)],
                           axis=-1)


def _rwkv_state_to_bd(s):
    lead = s.shape[:-3]
    s2 = s.reshape(lead + (2, 2, 64, 64))
    bd = jnp.einsum("...phvk,hg->...phvgk", s2, jnp.eye(2, dtype=s.dtype))
    return bd.reshape(lead + (2, 128, 128))


def _rwkv_state_from_bd(bd):
    lead = bd.shape[:-3]
    b5 = bd.reshape(lead + (2, 2, 64, 2, 64))
    s = jnp.stack([b5[..., 0, :, 0, :], b5[..., 1, :, 1, :]], axis=-3)
    return s.reshape(lead + (4, 64, 64))


def _gla_state_to_bd(s):
    lead = s.shape[:-3]
    bd = jnp.einsum("...hde,hg->...hegd", s, jnp.eye(4, dtype=s.dtype))
    return bd.reshape(lead + (256, 128))


def _gla_state_from_bd(bd):
    lead = bd.shape[:-2]
    b4 = bd.reshape(lead + (4, 64, 4, GLA_DK))
    s = jnp.stack([b4[..., h, :, h, :] for h in range(4)], axis=-3)
    return jnp.swapaxes(s, -1, -2)


def _pad_rows(m, start, total):
    return jnp.zeros((total, m.shape[1]), m.dtype).at[start:start + m.shape[0]].set(m)


def _layer_params(l, p):
    w_in = p["w_in"][l]
    out = {
        "w_mix": jnp.pad(w_in[:, :P_MIX], ((0, 0), (0, P_MIX_PAD - P_MIX))).astype(BF16),
        "w_gate": w_in[:, P_MIX:].astype(BF16),
        "w_branch": p["w_branch"][l].astype(BF16),
        "w_out": p["w_out"][l].astype(BF16),
        "w1": p["mlp_w1"][l].astype(BF16),
        "w2": p["mlp_w2"][l].astype(BF16),
        "n1": p["norm1_g"][l].reshape(1, D_MODEL),
        "n2": p["norm2_g"][l].reshape(1, D_MODEL),
        "pool_w": jnp.einsum("gcd,gh->gchd", p["pool_w"][l], jnp.eye(4, dtype=F32)).reshape(MIX_W, MIX_W).astype(BF16),
        "pool_scale": p["pool_scale"][l].reshape(1, MIX_W),
        "gn": p["rwkv_gn"][l].reshape(1, MIX_W),
        "bg": _pad_rows(p["rwkv_bg"][l], 64, 128).astype(BF16),
        "gla_norm": p["gla_norm"][l].reshape(1, MIX_W),
    }
    for d in range(2):
        out["rwkv%d" % d] = {
            "mu": p["rwkv_mu"][l].reshape(1, 768),
            "kk": p["rwkv_kk"][l].reshape(1, MIX_W),
            "ka": p["rwkv_ka"][l].reshape(1, MIX_W),
            "rk": p["rwkv_rk"][l].reshape(1, MIX_W),
            "w0": p["rwkv_w0"][l, d].reshape(1, MIX_W),
            "a0": p["rwkv_a0"][l, d].reshape(1, MIX_W),
            "bw": _pad_rows(p["rwkv_bw"][l, d], 64 * d, 128).astype(BF16),
            "ba": _pad_rows(p["rwkv_ba"][l, d], 32 * d, 128).astype(BF16),
        }
        out["gla_ab%d" % d] = _pad_rows(p["gla_ab"][l, d], 16 * d, 128).astype(BF16)
        out["gla_abias%d" % d] = p["gla_abias"][l, d].reshape(1, 128)
    return out


def _run_layer(x, mod3, prm, final_g, s_rwkv0, s_gla0, nseq, seq_len, cond_base, rows_per_cond):
    u_p, u_f, u_r, u_g = _inproj(x, mod3, prm["n1"], prm["w_mix"], cond_base, rows_per_cond)
    y_a = _pool(u_p, prm["pool_w"], prm["pool_scale"], nseq, seq_len)
    if seq_len == FFT_N1 * FFT_N1:
        y_b = _four_large(u_f, nseq, seq_len)
    else:
        y_b = _four_small(u_f, nseq, seq_len)
    o_f, ex_f, sr_f = _rwkv_dir(u_r, prm["rwkv0"], s_rwkv0[0], nseq, seq_len, False)
    o_b, ex_b, sr_b = _rwkv_dir(u_r, prm["rwkv1"], s_rwkv0[1], nseq, seq_len, True)
    g_f, sg_f = _gla_dir(u_g, prm["gla_ab0"], prm["gla_abias0"], s_gla0[0], nseq, seq_len, False)
    g_b, sg_b = _gla_dir(u_g, prm["gla_ab1"], prm["gla_abias1"], s_gla0[1], nseq, seq_len, True)
    x1 = _merge(x, mod3, prm["n1"], y_a, y_b, o_f, o_b, ex_f, ex_b, u_r, g_f, g_b, u_g, prm,
                cond_base, rows_per_cond)
    x2, y = _mlp(x1, mod3, prm["n2"], prm["w1"], prm["w2"], final_g, cond_base, rows_per_cond)
    return x2, y, jnp.stack([sr_f, sr_b], axis=1), jnp.stack([sg_f, sg_b], axis=1)


def kernel(x_prompt, x_sample, state_rwkv, state_gla, c, c_ctx, ada_w, ada_b, norm1_g, norm2_g, w_in, pool_w, pool_scale, rwkv_mu, rwkv_w0, rwkv_bw, rwkv_a0, rwkv_ba, rwkv_kk, rwkv_ka, rwkv_bg, rwkv_rk, rwkv_gn, gla_ab, gla_abias, gla_norm, w_branch, w_out, mlp_w1, mlp_w2, final_g):
    p = dict(w_in=w_in, pool_w=pool_w, pool_scale=pool_scale, rwkv_mu=rwkv_mu, rwkv_w0=rwkv_w0,
             rwkv_bw=rwkv_bw, rwkv_a0=rwkv_a0, rwkv_ba=rwkv_ba, rwkv_kk=rwkv_kk, rwkv_ka=rwkv_ka,
             rwkv_bg=rwkv_bg, rwkv_rk=rwkv_rk, rwkv_gn=rwkv_gn, gla_ab=gla_ab, gla_abias=gla_abias,
             gla_norm=gla_norm, w_branch=w_branch, w_out=w_out, mlp_w1=mlp_w1, mlp_w2=mlp_w2,
             norm1_g=norm1_g, norm2_g=norm2_g)
    bp, lp, _ = x_prompt.shape
    bs, ls, _ = x_sample.shape
    cond8 = jnp.zeros((8, D_MODEL), F32).at[0].set(c_ctx).at[1:1 + bs].set(c)
    mods = _ada(cond8, ada_w, ada_b)
    fg = final_g.reshape(1, D_MODEL)

    xp = x_prompt.reshape(bp * lp, D_MODEL)
    xs = (x_sample + _pos_embed_2d(ls, D_MODEL).reshape(1, ls, D_MODEL)).reshape(bs * ls, D_MODEL)
    zr = jnp.zeros((2, bp, 2, 128, 128), F32)
    zg = jnp.zeros((2, bp, 256, 128), F32)
    sr_in = _rwkv_state_to_bd(state_rwkv)
    sg_in = _gla_state_to_bd(state_gla)
    new_r, new_g = [], []
    for l in range(DEPTH):
        prm = _layer_params(l, p)
        mod3 = mods[l].reshape(8, 1, 6 * D_MODEL)
        xp, yp, s_r, s_g = _run_layer(xp, mod3, prm, fg, zr, zg, bp, lp, 0, bp * lp)
        new_r.append(_rwkv_state_from_bd(s_r))
        new_g.append(_gla_state_from_bd(s_g))
        xs, ys, _, _ = _run_layer(xs, mod3, prm, fg, jnp.swapaxes(sr_in[:, l], 0, 1),
                                  jnp.swapaxes(sg_in[:, l], 0, 1), bs, ls, 1, ls)
    y_prompt = yp.reshape(bp, lp, D_MODEL)
    y_sample = ys.reshape(bs, ls, D_MODEL)
    return (y_prompt, y_sample, jnp.stack(new_r, axis=1), jnp.stack(new_g, axis=1))
```

```python
import functools

import numpy as np
import jax
import jax.numpy as jnp
from jax import lax
from jax.experimental import pallas as pl
from jax.experimental.pallas import tpu as pltpu

F32 = jnp.float32
BF16 = jnp.bfloat16

D_MODEL = 1024
DEPTH = 2
GRID_W = 64
POS_BASE = 10000.0
MIX_W = 256
POOL_WINDOWS = (2, 4, 8, 16)
POOL_HALO = 8
RWKV_DECAY_SCALE = 0.606531
RWKV_GN_EPS = 64e-5
GLA_DK = 32
GLA_GATE_NORM = 16.0
D_FF = 4 * D_MODEL
EPS = 1e-6
P_MIX = 2336
P_MIX_PAD = 2432
CHUNK = 64
SUB = 16

TM = 512
TB = 256
SCAN_TB = 512
FFT_N1 = 64
VMEM_LIMIT = 56 * 1024 * 1024


def _cparams(sem):
    return pltpu.CompilerParams(dimension_semantics=sem, vmem_limit_bytes=VMEM_LIMIT)


def _full(shape):
    n = len(shape)
    return pl.BlockSpec(shape, lambda *_: (0,) * n)


def _dot(a, b):
    return jnp.dot(a.astype(BF16), b.astype(BF16), preferred_element_type=F32)


def _dot_nt(a, b):
    return lax.dot_general(a.astype(BF16), b.astype(BF16), (((1,), (1,)), ((), ())),
                           preferred_element_type=F32)


def _split2(a):
    hi = a.astype(BF16)
    lo = (a - hi.astype(F32)).astype(BF16)
    return hi, lo


def _dot_exact_rhs(a, w):
    hi, lo = _split2(a)
    return (jnp.dot(hi, w, preferred_element_type=F32) + jnp.dot(lo, w, preferred_element_type=F32))


def _dot_exact_lhs(w, a):
    hi, lo = _split2(a)
    return jnp.dot(w, hi, preferred_element_type=F32) + jnp.dot(w, lo, preferred_element_type=F32)


def _seg_ones(n, seg, scale=1.0):
    r = lax.broadcasted_iota(jnp.int32, (n, n), 0)
    c = lax.broadcasted_iota(jnp.int32, (n, n), 1)
    return jnp.where((r // seg) == (c // seg), scale, 0.0).astype(BF16)


def _log_sigmoid(x):
    return jnp.minimum(x, 0.0) - jnp.log(1.0 + jnp.exp(-jnp.abs(x)))


def _rms_mod(x, g, sc, sh):
    ms = jnp.mean(x * x, axis=-1, keepdims=True)
    return x * lax.rsqrt(ms + EPS) * g * (1.0 + sc) + sh


def _ada_kernel(c_ref, w_ref, b_ref, o_ref):
    c = c_ref[...]
    s = c * jax.nn.sigmoid(c)
    hi, lo = _split2(s)
    w = w_ref[0].astype(BF16)
    o_ref[0] = (jnp.dot(hi, w, preferred_element_type=F32) + jnp.dot(lo, w, preferred_element_type=F32)
                + b_ref[0])


def _ada(cond8, ada_w, ada_b):
    tn = 1536
    n = 6 * D_MODEL
    return pl.pallas_call(
        _ada_kernel,
        grid=(DEPTH, n // tn),
        in_specs=[_full((8, D_MODEL)),
                  pl.BlockSpec((1, D_MODEL, tn), lambda l, j: (l, 0, j)),
                  pl.BlockSpec((1, 1, tn), lambda l, j: (l, 0, j))],
        out_specs=pl.BlockSpec((1, 8, tn), lambda l, j: (l, 0, j)),
        out_shape=jax.ShapeDtypeStruct((DEPTH, 8, n), F32),
        compiler_params=_cparams(("parallel", "parallel")),
        name="ada_mod",
    )(cond8, ada_w, ada_b.reshape(DEPTH, 1, n))


def _inproj_kernel(x_ref, mod_ref, g_ref, w_ref, up_ref, uf_ref, ur_ref, ug_ref):
    mod = mod_ref[0]
    h = _rms_mod(x_ref[...], g_ref[...], mod[:, D_MODEL:2 * D_MODEL], mod[:, 0:D_MODEL]).astype(BF16)
    up_ref[...] = jnp.dot(h, w_ref[:, 0:256], preferred_element_type=F32)
    uf_ref[...] = jnp.dot(h, w_ref[:, 256:512], preferred_element_type=F32)
    ur_ref[...] = jnp.dot(h, w_ref[:, 512:1536], preferred_element_type=F32)
    ug_ref[...] = jnp.dot(h, w_ref[:, 1536:P_MIX_PAD], preferred_element_type=F32)


def _cond_map(cond_base, rows_per_cond):
    return lambda i: (cond_base + (i * TM) // rows_per_cond, 0, 0)


def _inproj(x, mod3, g, w, cond_base, rows_per_cond):
    t = x.shape[0]
    row = lambda i: (i, 0)
    return pl.pallas_call(
        _inproj_kernel,
        grid=(t // TM,),
        in_specs=[pl.BlockSpec((TM, D_MODEL), row),
                  pl.BlockSpec((1, 1, 6 * D_MODEL), _cond_map(cond_base, rows_per_cond)),
                  _full((1, D_MODEL)),
                  _full((D_MODEL, P_MIX_PAD))],
        out_specs=[pl.BlockSpec((TM, 256), row), pl.BlockSpec((TM, 256), row),
                   pl.BlockSpec((TM, 1024), row), pl.BlockSpec((TM, 896), row)],
        out_shape=[jax.ShapeDtypeStruct((t, 256), F32), jax.ShapeDtypeStruct((t, 256), F32),
                   jax.ShapeDtypeStruct((t, 1024), F32), jax.ShapeDtypeStruct((t, 896), F32)],
        compiler_params=_cparams(("parallel",)),
        name="in_proj",
    )(x, mod3, g, w)


def _pool_kernel(prev_ref, cur_ref, next_ref, wbd_ref, scale_ref, y_ref, *, seq_len):
    j = pl.program_id(1)
    nb = seq_len // TB
    cur = cur_ref[...]
    prev = jnp.where(j > 0, prev_ref[...], 0.0)
    nxt = jnp.where(j < nb - 1, next_ref[...], 0.0)
    ext = jnp.concatenate([prev, cur, nxt], axis=0)
    ext_hi, ext_lo = _split2(ext)
    ke = TB + 2 * POOL_HALO
    d = (lax.broadcasted_iota(jnp.int32, (TB, ke), 1) - POOL_HALO
         - lax.broadcasted_iota(jnp.int32, (TB, ke), 0))
    pos = j * TB + lax.broadcasted_iota(jnp.int32, (TB, MIX_W), 0)
    group = lax.broadcasted_iota(jnp.int32, (TB, MIX_W), 1) // (MIX_W // len(POOL_WINDOWS))
    pooled = jnp.zeros((TB, MIX_W), F32)
    for gi, win in enumerate(POOL_WINDOWS):
        before, after = win // 2, win - win // 2 - 1
        band = jnp.where((d >= -before) & (d <= after), 1.0, 0.0).astype(BF16)
        s = (jnp.dot(band, ext_hi, preferred_element_type=F32)
             + jnp.dot(band, ext_lo, preferred_element_type=F32))
        cnt = (jnp.minimum(pos + after, seq_len - 1) - jnp.maximum(pos - before, 0) + 1).astype(F32)
        pooled = jnp.where(group == gi, s / cnt - cur, pooled)
    y_ref[...] = _dot(pooled, wbd_ref[...]) * scale_ref[...]


def _pool(u_p, wbd, scale, nseq, seq_len):
    nb = seq_len // TB
    hb = TB // POOL_HALO
    nh = u_p.shape[0] // POOL_HALO
    return pl.pallas_call(
        functools.partial(_pool_kernel, seq_len=seq_len),
        grid=(nseq, nb),
        in_specs=[pl.BlockSpec((POOL_HALO, MIX_W), lambda b, j: (jnp.maximum((b * nb + j) * hb - 1, 0), 0)),
                  pl.BlockSpec((TB, MIX_W), lambda b, j: (b * nb + j, 0)),
                  pl.BlockSpec((POOL_HALO, MIX_W), lambda b, j: (jnp.minimum((b * nb + j + 1) * hb, nh - 1), 0)),
                  _full((MIX_W, MIX_W)), _full((1, MIX_W))],
        out_specs=pl.BlockSpec((TB, MIX_W), lambda b, j: (b * nb + j, 0)),
        out_shape=jax.ShapeDtypeStruct(u_p.shape, F32),
        compiler_params=_cparams(("parallel", "parallel")),
        name="pool_mixer",
    )(u_p, u_p, u_p, wbd, scale)


def _dft_tables(n):
    t = np.arange(n)
    ang = 2.0 * np.pi * ((np.outer(t, t)) % n) / n
    return np.cos(ang), np.sin(ang)


def _bd_np(m, copies):
    k = m.shape[0]
    out = np.zeros((k * copies, k * copies), m.dtype)
    for i in range(copies):
        out[i * k:(i + 1) * k, i * k:(i + 1) * k] = m
    return out


def _four_small_kernel(z_ref, cs_ref, cbd_ref, sbd_ref, y_ref, *, seq_len, norm):
    zf = _dot(cs_ref[...], z_ref[...])
    y = _dot(zf[:seq_len], cbd_ref[...]) + _dot(zf[seq_len:], sbd_ref[...])
    y_ref[...] = y * norm


def _four_small(u_f, nseq, seq_len):
    c, s = _dft_tables(seq_len)
    cs = jnp.asarray(np.concatenate([c, -s], axis=0), F32)
    c64, s64 = _dft_tables(MIX_W // 4)
    cbd = jnp.asarray(_bd_np(c64, 4), F32)
    sbd = jnp.asarray(_bd_np(s64, 4), F32)
    return pl.pallas_call(
        functools.partial(_four_small_kernel, seq_len=seq_len, norm=float((seq_len * 64) ** -0.5)),
        grid=(nseq,),
        in_specs=[pl.BlockSpec((seq_len, MIX_W), lambda b: (b, 0)),
                  _full((2 * seq_len, seq_len)), _full((MIX_W, MIX_W)), _full((MIX_W, MIX_W))],
        out_specs=pl.BlockSpec((seq_len, MIX_W), lambda b: (b, 0)),
        out_shape=jax.ShapeDtypeStruct(u_f.shape, F32),
        compiler_params=_cparams(("parallel",)),
        name="fourier_small",
    )(u_f, cs, cbd, sbd)


def _fft1_kernel(z_ref, fs_ref, a_ref):
    a_ref[...] = _dot(fs_ref[...], z_ref[...])


def _fft2_kernel(a_ref, g_ref, cbd_ref, sbd_ref, y_ref, *, f1b, norm):
    for i in range(f1b):
        a = jnp.concatenate([a_ref[0, i], a_ref[1, i]], axis=0)
        x = _dot(g_ref[i], a)
        y = _dot(x[:FFT_N1], cbd_ref[...]) + _dot(x[FFT_N1:], sbd_ref[...])
        y_ref[:, i * MIX_W:(i + 1) * MIX_W] = y * norm


def _four_large(u_f, nseq, seq_len):
    n1 = FFT_N1
    assert seq_len == n1 * n1
    width = n1 * MIX_W
    c64, s64 = _dft_tables(n1)
    fs = jnp.asarray(np.concatenate([c64, -s64], axis=0), F32)
    f1 = np.arange(n1)[:, None, None]
    f2 = np.arange(n1)[None, :, None]
    t2 = np.arange(n1)[None, None, :]
    ang = 2.0 * np.pi * ((t2 * (f1 + n1 * f2)) % seq_len) / seq_len
    gr, gi = np.cos(ang), -np.sin(ang)
    g = jnp.asarray(np.concatenate([np.concatenate([gr, -gi], axis=2),
                                    np.concatenate([gi, gr], axis=2)], axis=1), F32)
    cbd = jnp.asarray(_bd_np(c64, 4), F32)
    sbd = jnp.asarray(_bd_np(s64, 4), F32)
    nc = 2048
    a = pl.pallas_call(
        _fft1_kernel,
        grid=(nseq, width // nc),
        in_specs=[pl.BlockSpec((None, n1, nc), lambda b, j: (b, 0, j)), _full((2 * n1, n1))],
        out_specs=pl.BlockSpec((None, 2 * n1, nc), lambda b, j: (b, 0, j)),
        out_shape=jax.ShapeDtypeStruct((nseq, 2 * n1, width), F32),
        compiler_params=_cparams(("parallel", "parallel")),
        name="fourier_stage1",
    )(u_f.reshape(nseq, n1, width), fs)
    f1b = 8
    y = pl.pallas_call(
        functools.partial(_fft2_kernel, f1b=f1b, norm=float((seq_len * 64) ** -0.5)),
        grid=(nseq, n1 // f1b),
        in_specs=[pl.BlockSpec((None, 2, f1b, n1, MIX_W), lambda b, j: (b, 0, j, 0, 0)),
                  pl.BlockSpec((f1b, 2 * n1, 2 * n1), lambda b, j: (j, 0, 0)),
                  _full((MIX_W, MIX_W)), _full((MIX_W, MIX_W))],
        out_specs=pl.BlockSpec((None, n1, f1b * MIX_W), lambda b, j: (b, 0, j)),
        out_shape=jax.ShapeDtypeStruct((nseq, n1, width), F32),
        compiler_params=_cparams(("parallel", "parallel")),
        name="fourier_stage2",
    )(a.reshape(nseq, 2, n1, n1, MIX_W), g, cbd, sbd)
    return y.reshape(nseq * seq_len, MIX_W)


def _bmm(x, y):
    return lax.dot_general(x.astype(BF16), y.astype(BF16), (((2,), (1,)), ((0,), (0,))),
                           preferred_element_type=F32)


def _bmm_nt(x, y):
    return lax.dot_general(x.astype(BF16), y.astype(BF16), (((2,), (2,)), ((0,), (0,))),
                           preferred_element_type=F32)


def _bmm_tn(x, y):
    return lax.dot_general(x.astype(BF16), y.astype(BF16), (((1,), (1,)), ((0,), (0,))),
                           preferred_element_type=F32)


def _slab_bd(y):
    lane = lax.broadcasted_iota(jnp.int32, y.shape, 2)
    z = jnp.zeros_like(y)
    return jnp.concatenate([jnp.where(lane < 64, y, z), jnp.where(lane >= 64, y, z)], axis=1)


def _mm(x, y):
    return _bmm(x, _slab_bd(y.astype(BF16)))


def _mm_pair(x, y1, y2):
    w = jnp.concatenate([_slab_bd(y1.astype(BF16)), _slab_bd(y2.astype(BF16))], axis=2)
    out = _bmm(x, w)
    return out[:, :, :128], out[:, :, 128:]


def _mm3(x, y):
    xh, xl = _split2(x)
    yh, yl = _split2(y)
    hh, hl = _mm_pair(xh, yh, yl)
    return hh + (hl + _bmm(xl, _slab_bd(yh)))


def _to_slabs(x):
    nch = x.shape[0] // CHUNK
    x4 = x.reshape(nch, CHUNK, MIX_W)
    return jnp.stack([x4[:, :, :128], x4[:, :, 128:]], axis=1).reshape(2 * nch, CHUNK, 128)


def _tri_inverse(n_mat, eye, blk):
    nd = jnp.where(blk, n_mat, 0.0)
    no = n_mat - nd
    n2 = _mm3(nd, nd)
    n4 = _mm3(n2, n2)
    n8 = _mm3(n4, n4)
    x = n2 - nd - _mm3(nd, n2)
    x = x + n4 + _mm3(x, n4)
    x = x + n8 + _mm3(x, n8)
    td = eye + x
    e = _mm(td, no)
    e2 = _mm(e, e)
    e3 = _mm(e, e2)
    return td + _mm(e2 - e - e3, td)


def _rwkv_kernel(cur_ref, prev_ref, next_ref, mu_ref, kkw_ref, ka_ref, rk_ref, w0_ref, a0_ref,
                 bw_ref, ba_ref, s0_ref, o_ref, ex_ref, sfin_ref, s_scr, *, rev, nb):
    j = pl.program_id(1)
    jj = (nb - 1 - j) if rev else j

    @pl.when(j == 0)
    def _():
        s_scr[...] = s0_ref[0]

    tb = cur_ref.shape[0]
    z = cur_ref[:, 0:768]
    prow = jnp.where(jj > 0, prev_ref[POOL_HALO - 1:POOL_HALO, :], 0.0)
    nrow = jnp.where(jj < nb - 1, next_ref[0:1, :], 0.0)
    rowid = lax.broadcasted_iota(jnp.int32, (tb, 768), 0)
    up = jnp.where(rowid == 0, prow, pltpu.roll(z, 1, 0))
    dn = jnp.where(rowid == tb - 1, nrow, pltpu.roll(z, tb - 1, 0))
    zm = z + mu_ref[...] * (0.5 * (up + dn) - z)
    r = zm[:, 0:256]
    k = zm[:, 256:512]
    v = zm[:, 512:768]
    ones_bd = _seg_ones(MIX_W, 64)
    kk = k * kkw_ref[...]
    kk = kk * lax.rsqrt(_dot_exact_rhs(kk * kk, ones_bd) + EPS)
    w_log = w0_ref[...] + _dot(jnp.tanh(cur_ref[:, 768:896]), bw_ref[...])
    lw = -RWKV_DECAY_SCALE * jax.nn.sigmoid(w_log)
    a = jax.nn.sigmoid(a0_ref[...] + _dot(cur_ref[:, 896:1024], ba_ref[...]))
    kd = k * (1.0 + (a - 1.0) * ka_ref[...])
    ex_ref[...] = _dot_exact_rhs(r * kd * rk_ref[...], ones_bd) * v

    nch = tb // CHUNK
    tr = lax.broadcasted_iota(jnp.int32, (tb, tb), 0)
    tc = lax.broadcasted_iota(jnp.int32, (tb, tb), 1)
    same_chunk = (tr // CHUNK) == (tc // CHUNK)
    tri = jnp.where(same_chunk & ((tc >= tr) if rev else (tc <= tr)), 1.0, 0.0).astype(BF16)
    cum = _dot_exact_lhs(tri, lw)
    e_in = jnp.exp(cum)
    e_out = jnp.exp(-cum)
    last = 0 if rev else CHUNK - 1
    rt = _to_slabs(r * e_in)
    kh = _to_slabs(kd * e_out)
    bh = _to_slabs(kk * a * e_out)
    kt = _to_slabs(kk * jnp.exp(cum - lw))
    vv = _to_slabs(v)
    gam = _to_slabs(e_in)[:, last:last + 1, :]

    t_i = lax.broadcasted_iota(jnp.int32, (1, CHUNK, 128), 1)
    s_i = lax.broadcasted_iota(jnp.int32, (1, CHUNK, 128), 2) & (CHUNK - 1)
    if rev:
        strict, incl = s_i > t_i, s_i >= t_i
    else:
        strict, incl = s_i < t_i, s_i <= t_i
    blk = (s_i // SUB) == (t_i // SUB)
    eye = jnp.where(s_i == t_i, 1.0, 0.0)

    lhs = jnp.concatenate([kt, rt], axis=1)
    kb_bd = jnp.concatenate([_slab_bd(kh.astype(BF16)), _slab_bd(bh.astype(BF16))], axis=1)
    mkb = _bmm_nt(lhs, kb_bd)
    m_mat = jnp.where(strict, mkb[:, :CHUNK, :128], 0.0)
    p_mat = jnp.where(incl, mkb[:, CHUNK:, :128], 0.0)
    n_mat = jnp.where(strict, mkb[:, :CHUNK, 128:], 0.0)
    q_mat = jnp.where(incl, mkb[:, CHUNK:, 128:], 0.0)
    t_mat = _tri_inverse(n_mat, eye, blk)
    mpv = _mm(jnp.concatenate([m_mat, p_mat], axis=1), vv)
    mv, pv = mpv[:, :CHUNK], mpv[:, CHUNK:]
    tk, uv = _mm_pair(t_mat, kt, mv)
    qtk, quv = _mm_pair(q_mat, tk, uv)
    rc = rt - qtk
    oc = pv - quv
    bd_r = lax.broadcasted_iota(jnp.int32, (1, 128, 128), 1) // 64
    bd_c = lax.broadcasted_iota(jnp.int32, (1, 128, 128), 2) // 64
    bd_mask = bd_r == bd_c
    d_mat = jnp.where(bd_mask, _bmm_tn(tk, bh), 0.0)
    braw = jnp.where(bd_mask,
                     _bmm_tn(jnp.concatenate([vv, uv], axis=1), jnp.concatenate([kh, -bh], axis=1)), 0.0)

    s_cur = s_scr[...]
    for ci in range(nch):
        c = (nch - 1 - ci) if rev else ci
        g = slice(2 * c, 2 * c + 2)
        o = _bmm_nt(rc[g], s_cur) + oc[g]
        o_ref[c * CHUNK:(c + 1) * CHUNK, 0:128] = o[0]
        o_ref[c * CHUNK:(c + 1) * CHUNK, 128:256] = o[1]
        s_cur = (s_cur - _bmm(s_cur, d_mat[g]) + braw[g]) * gam[g]
    s_scr[...] = s_cur

    @pl.when(j == nb - 1)
    def _():
        sfin_ref[0] = s_scr[...]


def _rwkv_dir(u_r, prm, s0, nseq, seq_len, rev):
    tb = min(seq_len, SCAN_TB)
    nb = seq_len // tb
    hb = tb // POOL_HALO
    nh = u_r.shape[0] // POOL_HALO
    t = u_r.shape[0]

    def jj(j):
        return (nb - 1 - j) if rev else j

    cur_map = lambda b, j: (b * nb + jj(j), 0)
    st_map = lambda b, j: (b, 0, 0, 0)
    vec = _full((1, MIX_W))
    return pl.pallas_call(
        functools.partial(_rwkv_kernel, rev=rev, nb=nb),
        grid=(nseq, nb),
        in_specs=[pl.BlockSpec((tb, 1024), cur_map),
                  pl.BlockSpec((POOL_HALO, 768), lambda b, j: (jnp.maximum((b * nb + jj(j)) * hb - 1, 0), 0)),
                  pl.BlockSpec((POOL_HALO, 768), lambda b, j: (jnp.minimum((b * nb + jj(j) + 1) * hb, nh - 1), 0)),
                  _full((1, 768)), vec, vec, vec, vec, vec,
                  _full((128, MIX_W)), _full((128, MIX_W)),
                  pl.BlockSpec((1, 2, 128, 128), st_map)],
        out_specs=[pl.BlockSpec((tb, MIX_W), cur_map), pl.BlockSpec((tb, MIX_W), cur_map),
                   pl.BlockSpec((1, 2, 128, 128), st_map)],
        out_shape=[jax.ShapeDtypeStruct((t, MIX_W), F32), jax.ShapeDtypeStruct((t, MIX_W), F32),
                   jax.ShapeDtypeStruct((nseq, 2, 128, 128), F32)],
        scratch_shapes=[pltpu.VMEM((2, 128, 128), F32)],
        compiler_params=_cparams(("parallel", "arbitrary")),
        name="rwkv_bwd" if rev else "rwkv_fwd",
    )(u_r, u_r, u_r, prm["mu"], prm["kk"], prm["ka"], prm["rk"], prm["w0"], prm["a0"],
      prm["bw"], prm["ba"], s0)


def _rows_bd4(y, width):
    lane = lax.broadcasted_iota(jnp.int32, y.shape, 2) // width
    z = jnp.zeros_like(y)
    return jnp.concatenate([jnp.where(lane == h, y, z) for h in range(4)], axis=1)


def _block_scans(x, levels):
    rows, width = x.shape
    row = lax.broadcasted_iota(jnp.int32, (rows, width), 0)
    pre, suf = [x], [x]
    for m in range(1, levels + 1):
        size, half = 1 << m, 1 << (m - 1)
        p, s = pre[-1], suf[-1]
        if half < 8:
            off = row & (size - 1)
            p3 = p.reshape(rows // 8, 8, width)
            s3 = s.reshape(rows // 8, 8, width)
            addp = jnp.zeros_like(x)
            adds = jnp.zeros_like(x)
            for o in range(half):
                addp = jnp.where(off == half + o, pltpu.roll(p3, o + 1, 1).reshape(rows, width), addp)
                adds = jnp.where(off == o, pltpu.roll(s3, 8 - (half - o), 1).reshape(rows, width), adds)
        else:
            p4 = p.reshape(rows // size, size, width)
            s4 = s.reshape(rows // size, size, width)
            second = (row & half) != 0
            addp = jnp.where(second, jnp.broadcast_to(p4[:, half - 1:half, :], p4.shape).reshape(rows, width), 0.0)
            adds = jnp.where(second, 0.0, jnp.broadcast_to(s4[:, half:half + 1, :], s4.shape).reshape(rows, width))
        pre.append(p + addp)
        suf.append(s + adds)
    return pre, suf


def _gla_kernel(u_ref, ab_ref, abias_ref, s0_ref, o_ref, sfin_ref, s_scr, *, rev, nb):
    j = pl.program_id(1)

    @pl.when(j == 0)
    def _():
        s_scr[...] = s0_ref[0]

    nch = u_ref.shape[0] // CHUNK
    la = _log_sigmoid(_dot(u_ref[:, 768:896], ab_ref[...]) + abias_ref[...]) * (1.0 / GLA_GATE_NORM)
    q = u_ref[:, 0:128] * float(GLA_DK ** -0.5)
    kg = u_ref[:, 128:256]
    vg3 = u_ref[:, 256:512].reshape(nch, CHUNK, MIX_W)

    pre, suf = _block_scans(la, 6)
    if rev:
        near, far, b = suf, pre, suf[6]
    else:
        near, far, b = pre, suf, pre[6]
    t_i = lax.broadcasted_iota(jnp.int32, (1, CHUNK, MIX_W), 1)
    s_i = lax.broadcasted_iota(jnp.int32, (1, CHUNK, MIX_W), 2) & (CHUNK - 1)

    def to3(x):
        return x.reshape(nch, CHUNK, 128)

    att = jnp.where(s_i == t_i, _bmm_nt(to3(q), _rows_bd4(to3(kg).astype(BF16), GLA_DK)), 0.0)
    for lev in range(1, 7):
        size, half = 1 << lev, 1 << (lev - 1)
        same = (t_i // size) == (s_i // size)
        if rev:
            mask = same & ((t_i & (size - 1)) < half) & ((s_i & (size - 1)) >= half)
        else:
            mask = same & ((t_i & (size - 1)) >= half) & ((s_i & (size - 1)) < half)
        qe = q * jnp.exp(near[lev - 1])
        ke = kg * jnp.exp(far[lev - 1] - la)
        att = att + jnp.where(mask, _bmm_nt(to3(qe), _rows_bd4(to3(ke).astype(BF16), GLA_DK)), 0.0)
    o_intra = _bmm(att, _rows_bd4(vg3.astype(BF16), 64))

    last = 0 if rev else CHUNK - 1
    b3 = to3(b)
    b_last = b3[:, last:last + 1, :]
    qb = to3(q) * jnp.exp(b3)
    bd_r = lax.broadcasted_iota(jnp.int32, (1, MIX_W, 128), 1) // 64
    bd_c = lax.broadcasted_iota(jnp.int32, (1, MIX_W, 128), 2) // GLA_DK
    inc = jnp.where(bd_r == bd_c, _bmm_tn(vg3, to3(kg) * jnp.exp(b_last - b3)), 0.0)
    decay = jnp.exp(b_last)

    s_cur = s_scr[...]
    for ci in range(nch):
        c = (nch - 1 - ci) if rev else ci
        o_ref[c * CHUNK:(c + 1) * CHUNK, :] = o_intra[c] + _dot_nt(qb[c], s_cur)
        s_cur = s_cur * decay[c] + inc[c]
    s_scr[...] = s_cur

    @pl.when(j == nb - 1)
    def _():
        sfin_ref[0] = s_scr[...]


def _gla_dir(u_g, ab, abias, s0, nseq, seq_len, rev):
    tb = min(seq_len, SCAN_TB)
    nb = seq_len // tb
    t = u_g.shape[0]
    cur_map = (lambda b, j: (b * nb + nb - 1 - j, 0)) if rev else (lambda b, j: (b * nb + j, 0))
    st_map = lambda b, j: (b, 0, 0)
    return pl.pallas_call(
        functools.partial(_gla_kernel, rev=rev, nb=nb),
        grid=(nseq, nb),
        in_specs=[pl.BlockSpec((tb, 896), cur_map), _full((128, 128)), _full((1, 128)),
                  pl.BlockSpec((1, 256, 128), st_map)],
        out_specs=[pl.BlockSpec((tb, MIX_W), cur_map), pl.BlockSpec((1, 256, 128), st_map)],
        out_shape=[jax.ShapeDtypeStruct((t, MIX_W), F32), jax.ShapeDtypeStruct((nseq, 256, 128), F32)],
        scratch_shapes=[pltpu.VMEM((256, 128), F32)],
        compiler_params=_cparams(("parallel", "arbitrary")),
        name="gla_bwd" if rev else "gla_fwd",
    )(u_g, ab, abias, s0)


def _merge_kernel(x_ref, mod_ref, g_ref, ya_ref, yb_ref, of_ref, ob_ref, exf_ref, exb_ref, cag_ref,
                  gf_ref, gb_ref, gout_ref, gn_ref, bg_ref, gnorm_ref, wg_ref, wb_ref, wo_ref, x1_ref):
    mod = mod_ref[0]
    x = x_ref[...]
    h = _rms_mod(x, g_ref[...], mod[:, D_MODEL:2 * D_MODEL], mod[:, 0:D_MODEL]).astype(BF16)
    mean64 = _seg_ones(MIX_W, 64, 1.0 / 64.0)
    o = of_ref[...] + ob_ref[...]
    mu = _dot_exact_rhs(o, mean64)
    oc = o - mu
    var = _dot_exact_rhs(oc * oc, mean64)
    gate_c = _dot(jax.nn.sigmoid(cag_ref[...]), bg_ref[...])
    y_c = (oc * lax.rsqrt(var + RWKV_GN_EPS) * gn_ref[...] + exf_ref[...] + exb_ref[...]) * gate_c
    og = gf_ref[...] + gb_ref[...]
    gout = gout_ref[...]
    y_d = (og * lax.rsqrt(_dot_exact_rhs(og * og, mean64) + EPS) * gnorm_ref[...]
           * (gout * jax.nn.sigmoid(gout)))
    ys = (ya_ref[...], yb_ref[...], y_c, y_d)
    merged = jnp.zeros((TM, D_MODEL), F32)
    for i in range(4):
        gate = jax.nn.sigmoid(jnp.dot(h, wg_ref[:, i * D_MODEL:(i + 1) * D_MODEL], preferred_element_type=F32))
        merged = merged + gate * _dot(ys[i], wb_ref[i])
    out = _dot(merged, wo_ref[...])
    x1_ref[...] = x + mod[:, 2 * D_MODEL:3 * D_MODEL] * out


def _merge(x, mod3, g, y_a, y_b, o_f, o_b, ex_f, ex_b, u_r, g_f, g_b, u_g, prm, cond_base, rows_per_cond):
    t = x.shape[0]
    row = lambda i: (i, 0)
    mix = pl.BlockSpec((TM, MIX_W), row)
    vec = _full((1, MIX_W))
    return pl.pallas_call(
        _merge_kernel,
        grid=(t // TM,),
        in_specs=[pl.BlockSpec((TM, D_MODEL), row),
                  pl.BlockSpec((1, 1, 6 * D_MODEL), _cond_map(cond_base, rows_per_cond)),
                  _full((1, D_MODEL)),
                  mix, mix, mix, mix, mix, mix,
                  pl.BlockSpec((TM, 128), lambda i: (i, 7)),
                  mix, mix,
                  pl.BlockSpec((TM, MIX_W), lambda i: (i, 2)),
                  vec, _full((128, MIX_W)), vec,
                  _full((D_MODEL, 4 * D_MODEL)), _full((4, MIX_W, D_MODEL)), _full((D_MODEL, D_MODEL))],
        out_specs=pl.BlockSpec((TM, D_MODEL), row),
        out_shape=jax.ShapeDtypeStruct((t, D_MODEL), F32),
        compiler_params=_cparams(("parallel",)),
        name="merge_out",
    )(x, mod3, g, y_a, y_b, o_f, o_b, ex_f, ex_b, u_r, g_f, g_b, u_g,
      prm["gn"], prm["bg"], prm["gla_norm"], prm["w_gate"], prm["w_branch"], prm["w_out"])


def _mlp_kernel(x_ref, mod_ref, g_ref, w1_ref, w2_ref, fg_ref, x2_ref, y_ref):
    mod = mod_ref[0]
    x = x_ref[...]
    h = _rms_mod(x, g_ref[...], mod[:, 4 * D_MODEL:5 * D_MODEL], mod[:, 3 * D_MODEL:4 * D_MODEL]).astype(BF16)
    ff = jnp.zeros((TM, D_MODEL), F32)
    for c in range(D_FF // D_MODEL):
        cols = slice(c * D_MODEL, (c + 1) * D_MODEL)
        a = jnp.maximum(jnp.dot(h, w1_ref[:, cols], preferred_element_type=F32), 0.0)
        ff = ff + _dot(a * a, w2_ref[cols, :])
    x2 = x + mod[:, 5 * D_MODEL:6 * D_MODEL] * ff
    x2_ref[...] = x2
    y_ref[...] = x2 * lax.rsqrt(jnp.mean(x2 * x2, axis=-1, keepdims=True) + EPS) * fg_ref[...]


def _mlp(x1, mod3, g, w1, w2, final_g, cond_base, rows_per_cond):
    t = x1.shape[0]
    row = lambda i: (i, 0)
    return pl.pallas_call(
        _mlp_kernel,
        grid=(t // TM,),
        in_specs=[pl.BlockSpec((TM, D_MODEL), row),
                  pl.BlockSpec((1, 1, 6 * D_MODEL), _cond_map(cond_base, rows_per_cond)),
                  _full((1, D_MODEL)), _full((D_MODEL, D_FF)), _full((D_FF, D_MODEL)), _full((1, D_MODEL))],
        out_specs=[pl.BlockSpec((TM, D_MODEL), row), pl.BlockSpec((TM, D_MODEL), row)],
        out_shape=[jax.ShapeDtypeStruct((t, D_MODEL), F32), jax.ShapeDtypeStruct((t, D_MODEL), F32)],
        compiler_params=_cparams(("parallel",)),
        name="mlp",
    )(x1, mod3, g, w1, w2, final_g)


def _pos_embed_2d(n_tok, d):
    rows = n_tok // GRID_W
    quarter = d // 4
    omega = 1.0 / (POS_BASE ** (jnp.arange(quarter, dtype=F32) / quarter))
    ar = jnp.arange(rows, dtype=F32)[:, None] * omega
    ac = jnp.arange(GRID_W, dtype=F32)[:, None] * omega
    shape = (rows, GRID_W, quarter)
    return jnp.concatenate([jnp.broadcast_to(jnp.sin(ar)[:, None], shape), jnp.broadcast_to(jnp.cos(ar)[:, None], shape),
                            jnp.broadcast_to(jnp.sin(ac)[None], shape), jnp.broadcast_to(jnp.cos(ac)[None], shape)],
                           axis=-1)


def _rwkv_state_to_bd(s):
    lead = s.shape[:-3]
    s2 = s.reshape(lead + (2, 2, 64, 64))
    bd = jnp.einsum("...phvk,hg->...phvgk", s2, jnp.eye(2, dtype=s.dtype))
    return bd.reshape(lead + (2, 128, 128))


def _rwkv_state_from_bd(bd):
    lead = bd.shape[:-3]
    b5 = bd.reshape(lead + (2, 2, 64, 2, 64))
    s = jnp.stack([b5[..., 0, :, 0, :], b5[..., 1, :, 1, :]], axis=-3)
    return s.reshape(lead + (4, 64, 64))


def _gla_state_to_bd(s):
    lead = s.shape[:-3]
    bd = jnp.einsum("...hde,hg->...hegd", s, jnp.eye(4, dtype=s.dtype))
    return bd.reshape(lead + (256, 128))


def _gla_state_from_bd(bd):
    lead = bd.shape[:-2]
    b4 = bd.reshape(lead + (4, 64, 4, GLA_DK))
    s = jnp.stack([b4[..., h, :, h, :] for h in range(4)], axis=-3)
    return jnp.swapaxes(s, -1, -2)


def _pad_rows(m, start, total):
    return jnp.zeros((total, m.shape[1]), m.dtype).at[start:start + m.shape[0]].set(m)


def _layer_params(l, p):
    w_in = p["w_in"][l]
    out = {
        "w_mix": jnp.pad(w_in[:, :P_MIX], ((0, 0), (0, P_MIX_PAD - P_MIX))).astype(BF16),
        "w_gate": w_in[:, P_MIX:].astype(BF16),
        "w_branch": p["w_branch"][l].astype(BF16),
        "w_out": p["w_out"][l].astype(BF16),
        "w1": p["mlp_w1"][l].astype(BF16),
        "w2": p["mlp_w2"][l].astype(BF16),
        "n1": p["norm1_g"][l].reshape(1, D_MODEL),
        "n2": p["norm2_g"][l].reshape(1, D_MODEL),
        "pool_w": jnp.einsum("gcd,gh->gchd", p["pool_w"][l], jnp.eye(4, dtype=F32)).reshape(MIX_W, MIX_W).astype(BF16),
        "pool_scale": p["pool_scale"][l].reshape(1, MIX_W),
        "gn": p["rwkv_gn"][l].reshape(1, MIX_W),
        "bg": _pad_rows(p["rwkv_bg"][l], 64, 128).astype(BF16),
        "gla_norm": p["gla_norm"][l].reshape(1, MIX_W),
    }
    for d in range(2):
        out["rwkv%d" % d] = {
            "mu": p["rwkv_mu"][l].reshape(1, 768),
            "kk": p["rwkv_kk"][l].reshape(1, MIX_W),
            "ka": p["rwkv_ka"][l].reshape(1, MIX_W),
            "rk": p["rwkv_rk"][l].reshape(1, MIX_W),
            "w0": p["rwkv_w0"][l, d].reshape(1, MIX_W),
            "a0": p["rwkv_a0"][l, d].reshape(1, MIX_W),
            "bw": _pad_rows(p["rwkv_bw"][l, d], 64 * d, 128).astype(BF16),
            "ba": _pad_rows(p["rwkv_ba"][l, d], 32 * d, 128).astype(BF16),
        }
        out["gla_ab%d" % d] = _pad_rows(p["gla_ab"][l, d], 16 * d, 128).astype(BF16)
        out["gla_abias%d" % d] = p["gla_abias"][l, d].reshape(1, 128)
    return out


def _run_layer(x, mod3, prm, final_g, s_rwkv0, s_gla0, nseq, seq_len, cond_base, rows_per_cond):
    u_p, u_f, u_r, u_g = _inproj(x, mod3, prm["n1"], prm["w_mix"], cond_base, rows_per_cond)
    y_a = _pool(u_p, prm["pool_w"], prm["pool_scale"], nseq, seq_len)
    if seq_len == FFT_N1 * FFT_N1:
        y_b = _four_large(u_f, nseq, seq_len)
    else:
        y_b = _four_small(u_f, nseq, seq_len)
    o_f, ex_f, sr_f = _rwkv_dir(u_r, prm["rwkv0"], s_rwkv0[0], nseq, seq_len, False)
    o_b, ex_b, sr_b = _rwkv_dir(u_r, prm["rwkv1"], s_rwkv0[1], nseq, seq_len, True)
    g_f, sg_f = _gla_dir(u_g, prm["gla_ab0"], prm["gla_abias0"], s_gla0[0], nseq, seq_len, False)
    g_b, sg_b = _gla_dir(u_g, prm["gla_ab1"], prm["gla_abias1"], s_gla0[1], nseq, seq_len, True)
    x1 = _merge(x, mod3, prm["n1"], y_a, y_b, o_f, o_b, ex_f, ex_b, u_r, g_f, g_b, u_g, prm,
                cond_base, rows_per_cond)
    x2, y = _mlp(x1, mod3, prm["n2"], prm["w1"], prm["w2"], final_g, cond_base, rows_per_cond)
    return x2, y, jnp.stack([sr_f, sr_b], axis=1), jnp.stack([sg_f, sg_b], axis=1)


def kernel(x_prompt, x_sample, state_rwkv, state_gla, c, c_ctx, ada_w, ada_b, norm1_g, norm2_g, w_in, pool_w, pool_scale, rwkv_mu, rwkv_w0, rwkv_bw, rwkv_a0, rwkv_ba, rwkv_kk, rwkv_ka, rwkv_bg, rwkv_rk, rwkv_gn, gla_ab, gla_abias, gla_norm, w_branch, w_out, mlp_w1, mlp_w2, final_g):
    p = dict(w_in=w_in, pool_w=pool_w, pool_scale=pool_scale, rwkv_mu=rwkv_mu, rwkv_w0=rwkv_w0,
             rwkv_bw=rwkv_bw, rwkv_a0=rwkv_a0, rwkv_ba=rwkv_ba, rwkv_kk=rwkv_kk, rwkv_ka=rwkv_ka,
             rwkv_bg=rwkv_bg, rwkv_rk=rwkv_rk, rwkv_gn=rwkv_gn, gla_ab=gla_ab, gla_abias=gla_abias,
             gla_norm=gla_norm, w_branch=w_branch, w_out=w_out, mlp_w1=mlp_w1, mlp_w2=mlp_w2,
             norm1_g=norm1_g, norm2_g=norm2_g)
    bp, lp, _ = x_prompt.shape
    bs, ls, _ = x_sample.shape
    cond8 = jnp.zeros((8, D_MODEL), F32).at[0].set(c_ctx).at[1:1 + bs].set(c)
    mods = _ada(cond8, ada_w, ada_b)
    fg = final_g.reshape(1, D_MODEL)

    xp = x_prompt.reshape(bp * lp, D_MODEL)
    xs = (x_sample + _pos_embed_2d(ls, D_MODEL).reshape(1, ls, D_MODEL)).reshape(bs * ls, D_MODEL)
    zr = jnp.zeros((2, bp, 2, 128, 128), F32)
    zg = jnp.zeros((2, bp, 256, 128), F32)
    sr_in = _rwkv_state_to_bd(state_rwkv)
    sg_in = _gla_state_to_bd(state_gla)
    new_r, new_g = [], []
    for l in range(DEPTH):
        prm = _layer_params(l, p)
        mod3 = mods[l].reshape(8, 1, 6 * D_MODEL)
        xp, yp, s_r, s_g = _run_layer(xp, mod3, prm, fg, zr, zg, bp, lp, 0, bp * lp)
        new_r.append(_rwkv_state_from_bd(s_r))
        new_g.append(_gla_state_from_bd(s_g))
        xs, ys, _, _ = _run_layer(xs, mod3, prm, fg, jnp.swapaxes(sr_in[:, l], 0, 1),
                                  jnp.swapaxes(sg_in[:, l], 0, 1), bs, ls, 1, ls)
    y_prompt = yp.reshape(bp, lp, D_MODEL)
    y_sample = ys.reshape(bs, ls, D_MODEL)
    return (y_prompt, y_sample, jnp.stack(new_r, axis=1), jnp.stack(new_g, axis=1))
```

```python
import functools

import numpy as np
import jax
import jax.numpy as jnp
from jax import lax
from jax.experimental import pallas as pl
from jax.experimental.pallas import tpu as pltpu

F32 = jnp.float32
BF16 = jnp.bfloat16

D_MODEL = 1024
DEPTH = 2
GRID_W = 64
POS_BASE = 10000.0
MIX_W = 256
POOL_WINDOWS = (2, 4, 8, 16)
POOL_HALO = 8
RWKV_DECAY_SCALE = 0.606531
RWKV_GN_EPS = 64e-5
GLA_DK = 32
GLA_GATE_NORM = 16.0
D_FF = 4 * D_MODEL
EPS = 1e-6
P_MIX = 2336
P_MIX_PAD = 2432
CHUNK = 64
SUB = 16

TM = 512
TB = 256
SCAN_TB = 512
FFT_N1 = 64
VMEM_LIMIT = 56 * 1024 * 1024


def _cparams(sem):
    return pltpu.CompilerParams(dimension_semantics=sem, vmem_limit_bytes=VMEM_LIMIT)


def _full(shape):
    n = len(shape)
    return pl.BlockSpec(shape, lambda *_: (0,) * n)


def _dot(a, b):
    return jnp.dot(a.astype(BF16), b.astype(BF16), preferred_element_type=F32)


def _dot_nt(a, b):
    return lax.dot_general(a.astype(BF16), b.astype(BF16), (((1,), (1,)), ((), ())),
                           preferred_element_type=F32)


def _split2(a):
    hi = a.astype(BF16)
    lo = (a - hi.astype(F32)).astype(BF16)
    return hi, lo


def _dot_exact_rhs(a, w):
    hi, lo = _split2(a)
    return (jnp.dot(hi, w, preferred_element_type=F32) + jnp.dot(lo, w, preferred_element_type=F32))


def _dot_exact_lhs(w, a):
    hi, lo = _split2(a)
    return jnp.dot(w, hi, preferred_element_type=F32) + jnp.dot(w, lo, preferred_element_type=F32)


def _seg_ones(n, seg, scale=1.0):
    r = lax.broadcasted_iota(jnp.int32, (n, n), 0)
    c = lax.broadcasted_iota(jnp.int32, (n, n), 1)
    return jnp.where((r // seg) == (c // seg), scale, 0.0).astype(BF16)


def _log_sigmoid(x):
    return jnp.minimum(x, 0.0) - jnp.log(1.0 + jnp.exp(-jnp.abs(x)))


def _rms_mod(x, g, sc, sh):
    ms = jnp.mean(x * x, axis=-1, keepdims=True)
    return x * lax.rsqrt(ms + EPS) * g * (1.0 + sc) + sh


def _ada_kernel(c_ref, w_ref, b_ref, o_ref):
    c = c_ref[...]
    s = c * jax.nn.sigmoid(c)
    hi, lo = _split2(s)
    w = w_ref[0].astype(BF16)
    o_ref[0] = (jnp.dot(hi, w, preferred_element_type=F32) + jnp.dot(lo, w, preferred_element_type=F32)
                + b_ref[0])


def _ada(cond8, ada_w, ada_b):
    tn = 1536
    n = 6 * D_MODEL
    return pl.pallas_call(
        _ada_kernel,
        grid=(DEPTH, n // tn),
        in_specs=[_full((8, D_MODEL)),
                  pl.BlockSpec((1, D_MODEL, tn), lambda l, j: (l, 0, j)),
                  pl.BlockSpec((1, 1, tn), lambda l, j: (l, 0, j))],
        out_specs=pl.BlockSpec((1, 8, tn), lambda l, j: (l, 0, j)),
        out_shape=jax.ShapeDtypeStruct((DEPTH, 8, n), F32),
        compiler_params=_cparams(("parallel", "parallel")),
        name="ada_mod",
    )(cond8, ada_w, ada_b.reshape(DEPTH, 1, n))


def _win_kernel(w_ref, mix_ref, gate_ref):
    w = w_ref[0]
    mix_ref[0] = jnp.zeros(mix_ref.shape[1:], BF16)
    mix_ref[0, :, 0:P_MIX] = w[:, 0:P_MIX].astype(BF16)
    gate_ref[0] = w[:, P_MIX:].astype(BF16)


def _split_w_in(w_in):
    rows = 128
    p_in = w_in.shape[2]
    return pl.pallas_call(
        _win_kernel,
        grid=(DEPTH, D_MODEL // rows),
        in_specs=[pl.BlockSpec((1, rows, p_in), lambda l, i: (l, i, 0))],
        out_specs=[pl.BlockSpec((1, rows, P_MIX_PAD), lambda l, i: (l, i, 0)),
                   pl.BlockSpec((1, rows, p_in - P_MIX), lambda l, i: (l, i, 0))],
        out_shape=[jax.ShapeDtypeStruct((DEPTH, D_MODEL, P_MIX_PAD), BF16),
                   jax.ShapeDtypeStruct((DEPTH, D_MODEL, p_in - P_MIX), BF16)],
        compiler_params=_cparams(("parallel", "parallel")),
        name="split_w_in",
    )(w_in)


def _inproj_kernel(x_ref, mod_ref, g_ref, w_ref, up_ref, uf_ref, ur_ref, ug_ref):
    _inproj_body(x_ref[...], mod_ref, g_ref, w_ref, up_ref, uf_ref, ur_ref, ug_ref)


def _inproj_pos_kernel(x_ref, rr_ref, cc_ref, mod_ref, g_ref, w_ref, xo_ref, up_ref, uf_ref, ur_ref, ug_ref,
                       *, seq_len):
    nrow = TM // GRID_W
    g0 = pl.multiple_of(((pl.program_id(0) * TM) % seq_len) // GRID_W, nrow)
    half = D_MODEL // 2
    rpart = jnp.broadcast_to(rr_ref[pl.ds(g0, nrow), :][:, None, :], (nrow, GRID_W, half)).reshape(TM, half)
    cpart = jnp.concatenate([cc_ref[...]] * nrow, axis=0)
    x = x_ref[...] + jnp.concatenate([rpart, cpart], axis=1)
    xo_ref[...] = x
    _inproj_body(x, mod_ref, g_ref, w_ref, up_ref, uf_ref, ur_ref, ug_ref)


def _inproj_body(x, mod_ref, g_ref, w_ref, up_ref, uf_ref, ur_ref, ug_ref):
    mod = mod_ref[0]
    h = _rms_mod(x, g_ref[...], mod[:, D_MODEL:2 * D_MODEL], mod[:, 0:D_MODEL]).astype(BF16)
    up_ref[...] = jnp.dot(h, w_ref[:, 0:256], preferred_element_type=F32)
    uf_ref[...] = jnp.dot(h, w_ref[:, 256:512], preferred_element_type=F32)
    ur_ref[...] = jnp.dot(h, w_ref[:, 512:1536], preferred_element_type=F32)
    ug_ref[...] = jnp.dot(h, w_ref[:, 1536:P_MIX_PAD], preferred_element_type=F32)


def _cond_map(cond_base, rows_per_cond):
    return lambda i: (cond_base + (i * TM) // rows_per_cond, 0, 0)


def _inproj(x, mod3, g, w, cond_base, rows_per_cond, pos_tables=None, seq_len=None):
    t = x.shape[0]
    row = lambda i: (i, 0)
    in_specs = [pl.BlockSpec((1, 1, 6 * D_MODEL), _cond_map(cond_base, rows_per_cond)),
                _full((1, D_MODEL)),
                _full((D_MODEL, P_MIX_PAD))]
    out_specs = [pl.BlockSpec((TM, 256), row), pl.BlockSpec((TM, 256), row),
                 pl.BlockSpec((TM, 1024), row), pl.BlockSpec((TM, 896), row)]
    out_shape = [jax.ShapeDtypeStruct((t, 256), F32), jax.ShapeDtypeStruct((t, 256), F32),
                 jax.ShapeDtypeStruct((t, 1024), F32), jax.ShapeDtypeStruct((t, 896), F32)]
    x_spec = pl.BlockSpec((TM, D_MODEL), row)
    if pos_tables is None:
        return pl.pallas_call(
            _inproj_kernel, grid=(t // TM,), in_specs=[x_spec] + in_specs, out_specs=out_specs,
            out_shape=out_shape, compiler_params=_cparams(("parallel",)), name="in_proj",
        )(x, mod3, g, w)
    tab = _full((GRID_W, D_MODEL // 2))
    return pl.pallas_call(
        functools.partial(_inproj_pos_kernel, seq_len=seq_len), grid=(t // TM,),
        in_specs=[x_spec, tab, tab] + in_specs, out_specs=[x_spec] + out_specs,
        out_shape=[jax.ShapeDtypeStruct((t, D_MODEL), F32)] + out_shape,
        compiler_params=_cparams(("parallel",)), name="in_proj_pos",
    )(x, pos_tables[0], pos_tables[1], mod3, g, w)


def _pool_kernel(prev_ref, cur_ref, next_ref, wbd_ref, scale_ref, y_ref, *, seq_len):
    j = pl.program_id(1)
    nb = seq_len // TB
    cur = cur_ref[...]
    prev = jnp.where(j > 0, prev_ref[...], 0.0)
    nxt = jnp.where(j < nb - 1, next_ref[...], 0.0)
    ext = jnp.concatenate([prev, cur, nxt], axis=0)
    ext_hi, ext_lo = _split2(ext)
    ke = TB + 2 * POOL_HALO
    d = (lax.broadcasted_iota(jnp.int32, (TB, ke), 1) - POOL_HALO
         - lax.broadcasted_iota(jnp.int32, (TB, ke), 0))
    pos = j * TB + lax.broadcasted_iota(jnp.int32, (TB, MIX_W), 0)
    group = lax.broadcasted_iota(jnp.int32, (TB, MIX_W), 1) // (MIX_W // len(POOL_WINDOWS))
    pooled = jnp.zeros((TB, MIX_W), F32)
    for gi, win in enumerate(POOL_WINDOWS):
        before, after = win // 2, win - win // 2 - 1
        band = jnp.where((d >= -before) & (d <= after), 1.0, 0.0).astype(BF16)
        s = (jnp.dot(band, ext_hi, preferred_element_type=F32)
             + jnp.dot(band, ext_lo, preferred_element_type=F32))
        cnt = (jnp.minimum(pos + after, seq_len - 1) - jnp.maximum(pos - before, 0) + 1).astype(F32)
        pooled = jnp.where(group == gi, s / cnt - cur, pooled)
    y_ref[...] = _dot(pooled, wbd_ref[...]) * scale_ref[...]


def _pool(u_p, wbd, scale, nseq, seq_len):
    nb = seq_len // TB
    hb = TB // POOL_HALO
    nh = u_p.shape[0] // POOL_HALO
    return pl.pallas_call(
        functools.partial(_pool_kernel, seq_len=seq_len),
        grid=(nseq, nb),
        in_specs=[pl.BlockSpec((POOL_HALO, MIX_W), lambda b, j: (jnp.maximum((b * nb + j) * hb - 1, 0), 0)),
                  pl.BlockSpec((TB, MIX_W), lambda b, j: (b * nb + j, 0)),
                  pl.BlockSpec((POOL_HALO, MIX_W), lambda b, j: (jnp.minimum((b * nb + j + 1) * hb, nh - 1), 0)),
                  _full((MIX_W, MIX_W)), _full((1, MIX_W))],
        out_specs=pl.BlockSpec((TB, MIX_W), lambda b, j: (b * nb + j, 0)),
        out_shape=jax.ShapeDtypeStruct(u_p.shape, F32),
        compiler_params=_cparams(("parallel", "parallel")),
        name="pool_mixer",
    )(u_p, u_p, u_p, wbd, scale)


def _dft_tables(n):
    t = np.arange(n)
    ang = 2.0 * np.pi * ((np.outer(t, t)) % n) / n
    return np.cos(ang), np.sin(ang)


def _bd_np(m, copies):
    k = m.shape[0]
    out = np.zeros((k * copies, k * copies), m.dtype)
    for i in range(copies):
        out[i * k:(i + 1) * k, i * k:(i + 1) * k] = m
    return out


def _four_small_kernel(z_ref, cs_ref, cbd_ref, sbd_ref, y_ref, *, seq_len, norm):
    zf = _dot(cs_ref[...], z_ref[...])
    y = _dot(zf[:seq_len], cbd_ref[...]) + _dot(zf[seq_len:], sbd_ref[...])
    y_ref[...] = y * norm


def _four_small(u_f, nseq, seq_len):
    c, s = _dft_tables(seq_len)
    cs = jnp.asarray(np.concatenate([c, -s], axis=0), F32)
    c64, s64 = _dft_tables(MIX_W // 4)
    cbd = jnp.asarray(_bd_np(c64, 4), F32)
    sbd = jnp.asarray(_bd_np(s64, 4), F32)
    return pl.pallas_call(
        functools.partial(_four_small_kernel, seq_len=seq_len, norm=float((seq_len * 64) ** -0.5)),
        grid=(nseq,),
        in_specs=[pl.BlockSpec((seq_len, MIX_W), lambda b: (b, 0)),
                  _full((2 * seq_len, seq_len)), _full((MIX_W, MIX_W)), _full((MIX_W, MIX_W))],
        out_specs=pl.BlockSpec((seq_len, MIX_W), lambda b: (b, 0)),
        out_shape=jax.ShapeDtypeStruct(u_f.shape, F32),
        compiler_params=_cparams(("parallel",)),
        name="fourier_small",
    )(u_f, cs, cbd, sbd)


def _fft1_kernel(z_ref, fs_ref, a_ref):
    a_ref[...] = _dot(fs_ref[...], z_ref[...])


def _fft2_kernel(a_ref, g_ref, cbd_ref, sbd_ref, y_ref, *, f1b, norm):
    for i in range(f1b):
        a = jnp.concatenate([a_ref[0, i], a_ref[1, i]], axis=0)
        x = _dot(g_ref[i], a)
        y = _dot(x[:FFT_N1], cbd_ref[...]) + _dot(x[FFT_N1:], sbd_ref[...])
        y_ref[:, i * MIX_W:(i + 1) * MIX_W] = y * norm


def _four_large(u_f, nseq, seq_len):
    n1 = FFT_N1
    assert seq_len == n1 * n1
    width = n1 * MIX_W
    c64, s64 = _dft_tables(n1)
    fs = jnp.asarray(np.concatenate([c64, -s64], axis=0), F32)
    f1 = np.arange(n1)[:, None, None]
    f2 = np.arange(n1)[None, :, None]
    t2 = np.arange(n1)[None, None, :]
    ang = 2.0 * np.pi * ((t2 * (f1 + n1 * f2)) % seq_len) / seq_len
    gr, gi = np.cos(ang), -np.sin(ang)
    g = jnp.asarray(np.concatenate([np.concatenate([gr, -gi], axis=2),
                                    np.concatenate([gi, gr], axis=2)], axis=1), F32)
    cbd = jnp.asarray(_bd_np(c64, 4), F32)
    sbd = jnp.asarray(_bd_np(s64, 4), F32)
    nc = 2048
    a = pl.pallas_call(
        _fft1_kernel,
        grid=(nseq, width // nc),
        in_specs=[pl.BlockSpec((None, n1, nc), lambda b, j: (b, 0, j)), _full((2 * n1, n1))],
        out_specs=pl.BlockSpec((None, 2 * n1, nc), lambda b, j: (b, 0, j)),
        out_shape=jax.ShapeDtypeStruct((nseq, 2 * n1, width), F32),
        compiler_params=_cparams(("parallel", "parallel")),
        name="fourier_stage1",
    )(u_f.reshape(nseq, n1, width), fs)
    f1b = 8
    y = pl.pallas_call(
        functools.partial(_fft2_kernel, f1b=f1b, norm=float((seq_len * 64) ** -0.5)),
        grid=(nseq, n1 // f1b),
        in_specs=[pl.BlockSpec((None, 2, f1b, n1, MIX_W), lambda b, j: (b, 0, j, 0, 0)),
                  pl.BlockSpec((f1b, 2 * n1, 2 * n1), lambda b, j: (j, 0, 0)),
                  _full((MIX_W, MIX_W)), _full((MIX_W, MIX_W))],
        out_specs=pl.BlockSpec((None, n1, f1b * MIX_W), lambda b, j: (b, 0, j)),
        out_shape=jax.ShapeDtypeStruct((nseq, n1, width), F32),
        compiler_params=_cparams(("parallel", "parallel")),
        name="fourier_stage2",
    )(a.reshape(nseq, 2, n1, n1, MIX_W), g, cbd, sbd)
    return y.reshape(nseq * seq_len, MIX_W)


def _bmm(x, y):
    return lax.dot_general(x.astype(BF16), y.astype(BF16), (((2,), (1,)), ((0,), (0,))),
                           preferred_element_type=F32)


def _bmm_nt(x, y):
    return lax.dot_general(x.astype(BF16), y.astype(BF16), (((2,), (2,)), ((0,), (0,))),
                           preferred_element_type=F32)


def _bmm_tn(x, y):
    return lax.dot_general(x.astype(BF16), y.astype(BF16), (((1,), (1,)), ((0,), (0,))),
                           preferred_element_type=F32)


def _slab_bd(y):
    lane = lax.broadcasted_iota(jnp.int32, y.shape, 2)
    z = jnp.zeros_like(y)
    return jnp.concatenate([jnp.where(lane < 64, y, z), jnp.where(lane >= 64, y, z)], axis=1)


def _mm(x, y):
    return _bmm(x, _slab_bd(y.astype(BF16)))


def _mm_pair(x, y1, y2):
    w = jnp.concatenate([_slab_bd(y1.astype(BF16)), _slab_bd(y2.astype(BF16))], axis=2)
    out = _bmm(x, w)
    return out[:, :, :128], out[:, :, 128:]


def _mm3(xs, y):
    yh, yl = _split2(y)
    bh, bl = _slab_bd(yh), _slab_bd(yl)
    parts = [_split2(x) for x in xs]
    xh = jnp.concatenate([p[0] for p in parts], axis=1)
    xl = jnp.concatenate([p[1] for p in parts], axis=1)
    hx = _bmm(xh, jnp.concatenate([bh, bl], axis=2))
    out = hx[:, :, :128] + (hx[:, :, 128:] + _bmm(xl, bh))
    return [out[:, i * CHUNK:(i + 1) * CHUNK] for i in range(len(xs))]


def _to_slabs(x):
    nch = x.shape[0] // CHUNK
    x4 = x.reshape(nch, CHUNK, MIX_W)
    return jnp.stack([x4[:, :, :128], x4[:, :, 128:]], axis=1).reshape(2 * nch, CHUNK, 128)


def _tri_inverse(n_mat, eye, blk):
    nd = jnp.where(blk, n_mat, 0.0)
    no = n_mat - nd
    n2, = _mm3([nd], nd)
    n4, y = _mm3([n2, nd], n2)
    x = n2 - nd - y
    n8, y = _mm3([n4, x], n4)
    x = x + n4 + y
    y, = _mm3([x], n8)
    x = x + n8 + y
    td = eye + x
    e = _mm(td, no)
    e2 = _mm(e, e)
    e3 = _mm(e, e2)
    return td + _mm(e2 - e - e3, td)


def _rwkv_kernel(cur_ref, prev_ref, next_ref, mu_ref, kkw_ref, ka_ref, rk_ref, w0_ref, a0_ref,
                 bw_ref, ba_ref, s0_ref, o_ref, ex_ref, sfin_ref, s_scr, *, rev, nb):
    j = pl.program_id(1)
    jj = (nb - 1 - j) if rev else j

    bd_r2 = lax.broadcasted_iota(jnp.int32, (128, 128), 0) // 64
    bd_c2 = lax.broadcasted_iota(jnp.int32, (128, 128), 1) // 64

    @pl.when(j == 0)
    def _():
        for hp in range(2):
            sp = s0_ref[0, 2 * hp:2 * hp + 2].reshape(128, 64)
            s_scr[hp] = jnp.where(bd_r2 == bd_c2, jnp.concatenate([sp, sp], axis=1), 0.0)

    tb = cur_ref.shape[0]
    z = cur_ref[:, 0:768]
    prow = jnp.where(jj > 0, prev_ref[POOL_HALO - 1:POOL_HALO, :], 0.0)
    nrow = jnp.where(jj < nb - 1, next_ref[0:1, :], 0.0)
    rowid = lax.broadcasted_iota(jnp.int32, (tb, 768), 0)
    up = jnp.where(rowid == 0, prow, pltpu.roll(z, 1, 0))
    dn = jnp.where(rowid == tb - 1, nrow, pltpu.roll(z, tb - 1, 0))
    zm = z + mu_ref[...] * (0.5 * (up + dn) - z)
    r = zm[:, 0:256]
    k = zm[:, 256:512]
    v = zm[:, 512:768]
    ones_bd = _seg_ones(MIX_W, 64)
    kk = k * kkw_ref[...]
    kk = kk * lax.rsqrt(_dot(kk * kk, ones_bd) + EPS)
    w_log = w0_ref[...] + _dot(jnp.tanh(cur_ref[:, 768:896]), bw_ref[...])
    lw = -RWKV_DECAY_SCALE * jax.nn.sigmoid(w_log)
    a = jax.nn.sigmoid(a0_ref[...] + _dot(cur_ref[:, 896:1024], ba_ref[...]))
    kd = k * (1.0 + (a - 1.0) * ka_ref[...])
    ex_ref[...] = _dot(r * kd * rk_ref[...], ones_bd) * v

    nch = tb // CHUNK
    tr = lax.broadcasted_iota(jnp.int32, (tb, tb), 0)
    tc = lax.broadcasted_iota(jnp.int32, (tb, tb), 1)
    same_chunk = (tr // CHUNK) == (tc // CHUNK)
    tri = jnp.where(same_chunk & ((tc >= tr) if rev else (tc <= tr)), 1.0, 0.0).astype(BF16)
    cum = _dot_exact_lhs(tri, lw)
    e_in = jnp.exp(cum)
    e_out = jnp.exp(-cum)
    last = 0 if rev else CHUNK - 1
    rt = _to_slabs(r * e_in)
    kh = _to_slabs(kd * e_out)
    bh = _to_slabs(kk * a * e_out)
    kt = _to_slabs(kk * jnp.exp(cum - lw))
    vv = _to_slabs(v)
    gam = _to_slabs(e_in)[:, last:last + 1, :]

    t_i = lax.broadcasted_iota(jnp.int32, (1, CHUNK, 128), 1)
    s_i = lax.broadcasted_iota(jnp.int32, (1, CHUNK, 128), 2) & (CHUNK - 1)
    if rev:
        strict, incl = s_i > t_i, s_i >= t_i
    else:
        strict, incl = s_i < t_i, s_i <= t_i
    blk = (s_i // SUB) == (t_i // SUB)
    eye = jnp.where(s_i == t_i, 1.0, 0.0)

    lhs = jnp.concatenate([kt, rt], axis=1)
    kb_bd = jnp.concatenate([_slab_bd(kh.astype(BF16)), _slab_bd(bh.astype(BF16))], axis=1)
    mkb = _bmm_nt(lhs, kb_bd)
    m_mat = jnp.where(strict, mkb[:, :CHUNK, :128], 0.0)
    p_mat = jnp.where(incl, mkb[:, CHUNK:, :128], 0.0)
    n_mat = jnp.where(strict, mkb[:, :CHUNK, 128:], 0.0)
    q_mat = jnp.where(incl, mkb[:, CHUNK:, 128:], 0.0)
    t_mat = _tri_inverse(n_mat, eye, blk)
    mpv = _mm(jnp.concatenate([m_mat, p_mat], axis=1), vv)
    mv, pv = mpv[:, :CHUNK], mpv[:, CHUNK:]
    tk, uv = _mm_pair(t_mat, kt, mv)
    qtk, quv = _mm_pair(q_mat, tk, uv)
    rc = rt - qtk
    oc = pv - quv
    bd_r = lax.broadcasted_iota(jnp.int32, (1, 128, 128), 1) // 64
    bd_c = lax.broadcasted_iota(jnp.int32, (1, 128, 128), 2) // 64
    bd_mask = bd_r == bd_c
    d_mat = jnp.where(bd_mask, _bmm_tn(tk, bh), 0.0)
    braw = jnp.where(bd_mask,
                     _bmm_tn(jnp.concatenate([vv, uv], axis=1), jnp.concatenate([kh, -bh], axis=1)), 0.0)

    s_cur = s_scr[...]
    for ci in range(nch):
        c = (nch - 1 - ci) if rev else ci
        g = slice(2 * c, 2 * c + 2)
        o = _bmm_nt(rc[g], s_cur) + oc[g]
        o_ref[c * CHUNK:(c + 1) * CHUNK, 0:128] = o[0]
        o_ref[c * CHUNK:(c + 1) * CHUNK, 128:256] = o[1]
        s_cur = (s_cur - _bmm(s_cur, d_mat[g]) + braw[g]) * gam[g]
    s_scr[...] = s_cur

    @pl.when(j == nb - 1)
    def _():
        for hp in range(2):
            s_hp = s_scr[hp]
            sfin_ref[0, 2 * hp] = s_hp[0:64, 0:64]
            sfin_ref[0, 2 * hp + 1] = s_hp[64:128, 64:128]


def _rwkv_dir(u_r, prm, s0, nseq, seq_len, rev):
    tb = min(seq_len, SCAN_TB)
    nb = seq_len // tb
    hb = tb // POOL_HALO
    nh = u_r.shape[0] // POOL_HALO
    t = u_r.shape[0]

    def jj(j):
        return (nb - 1 - j) if rev else j

    cur_map = lambda b, j: (b * nb + jj(j), 0)
    st_map = lambda b, j: (b, 0, 0, 0)
    vec = _full((1, MIX_W))
    return pl.pallas_call(
        functools.partial(_rwkv_kernel, rev=rev, nb=nb),
        grid=(nseq, nb),
        in_specs=[pl.BlockSpec((tb, 1024), cur_map),
                  pl.BlockSpec((POOL_HALO, 768), lambda b, j: (jnp.maximum((b * nb + jj(j)) * hb - 1, 0), 0)),
                  pl.BlockSpec((POOL_HALO, 768), lambda b, j: (jnp.minimum((b * nb + jj(j) + 1) * hb, nh - 1), 0)),
                  _full((1, 768)), vec, vec, vec, vec, vec,
                  _full((128, MIX_W)), _full((128, MIX_W)),
                  pl.BlockSpec((1, 4, 64, 64), st_map)],
        out_specs=[pl.BlockSpec((tb, MIX_W), cur_map), pl.BlockSpec((tb, MIX_W), cur_map),
                   pl.BlockSpec((1, 4, 64, 64), st_map)],
        out_shape=[jax.ShapeDtypeStruct((t, MIX_W), F32), jax.ShapeDtypeStruct((t, MIX_W), F32),
                   jax.ShapeDtypeStruct((nseq, 4, 64, 64), F32)],
        scratch_shapes=[pltpu.VMEM((2, 128, 128), F32)],
        compiler_params=_cparams(("parallel", "arbitrary")),
        name="rwkv_bwd" if rev else "rwkv_fwd",
    )(u_r, u_r, u_r, prm["mu"], prm["kk"], prm["ka"], prm["rk"], prm["w0"], prm["a0"],
      prm["bw"], prm["ba"], s0)


def _rows_bd4(y, width):
    lane = lax.broadcasted_iota(jnp.int32, y.shape, 2) // width
    z = jnp.zeros_like(y)
    return jnp.concatenate([jnp.where(lane == h, y, z) for h in range(4)], axis=1)


def _block_scans(x, levels):
    rows, width = x.shape
    row = lax.broadcasted_iota(jnp.int32, (rows, width), 0)
    pre, suf = [x], [x]
    for m in range(1, levels + 1):
        size, half = 1 << m, 1 << (m - 1)
        p, s = pre[-1], suf[-1]
        if half < 8:
            off = row & (size - 1)
            p3 = p.reshape(rows // 8, 8, width)
            s3 = s.reshape(rows // 8, 8, width)
            addp = jnp.zeros_like(x)
            adds = jnp.zeros_like(x)
            for o in range(half):
                addp = jnp.where(off == half + o, pltpu.roll(p3, o + 1, 1).reshape(rows, width), addp)
                adds = jnp.where(off == o, pltpu.roll(s3, 8 - (half - o), 1).reshape(rows, width), adds)
        else:
            p4 = p.reshape(rows // size, size, width)
            s4 = s.reshape(rows // size, size, width)
            second = (row & half) != 0
            addp = jnp.where(second, jnp.broadcast_to(p4[:, half - 1:half, :], p4.shape).reshape(rows, width), 0.0)
            adds = jnp.where(second, 0.0, jnp.broadcast_to(s4[:, half:half + 1, :], s4.shape).reshape(rows, width))
        pre.append(p + addp)
        suf.append(s + adds)
    return pre, suf


def _gla_kernel(u_ref, ab_ref, abias_ref, s0_ref, o_ref, sfin_ref, s_scr, *, rev, nb):
    j = pl.program_id(1)

    @pl.when(j == 0)
    def _():
        sp = s0_ref[0].reshape(4 * GLA_DK, 64)
        st = jnp.concatenate([sp, sp], axis=1).T[0:64, :]
        lane_h = lax.broadcasted_iota(jnp.int32, (64, 128), 1) // GLA_DK
        s_scr[...] = jnp.concatenate([jnp.where(lane_h == h, st, 0.0) for h in range(4)], axis=0)

    nch = u_ref.shape[0] // CHUNK
    la = _log_sigmoid(_dot(u_ref[:, 768:896], ab_ref[...]) + abias_ref[...]) * (1.0 / GLA_GATE_NORM)
    q = u_ref[:, 0:128] * float(GLA_DK ** -0.5)
    kg = u_ref[:, 128:256]
    vg3 = u_ref[:, 256:512].reshape(nch, CHUNK, MIX_W)

    pre, suf = _block_scans(la, 6)
    if rev:
        near, far, b = suf, pre, suf[6]
    else:
        near, far, b = pre, suf, pre[6]
    t_i = lax.broadcasted_iota(jnp.int32, (1, CHUNK, MIX_W), 1)
    s_i = lax.broadcasted_iota(jnp.int32, (1, CHUNK, MIX_W), 2) & (CHUNK - 1)

    def to3(x):
        return x.reshape(nch, CHUNK, 128)

    att = jnp.where(s_i == t_i, _bmm_nt(to3(q), _rows_bd4(to3(kg).astype(BF16), GLA_DK)), 0.0)
    for lev in range(1, 7):
        size, half = 1 << lev, 1 << (lev - 1)
        same = (t_i // size) == (s_i // size)
        if rev:
            mask = same & ((t_i & (size - 1)) < half) & ((s_i & (size - 1)) >= half)
        else:
            mask = same & ((t_i & (size - 1)) >= half) & ((s_i & (size - 1)) < half)
        qe = q * jnp.exp(near[lev - 1])
        ke = kg * jnp.exp(far[lev - 1] - la)
        att = att + jnp.where(mask, _bmm_nt(to3(qe), _rows_bd4(to3(ke).astype(BF16), GLA_DK)), 0.0)
    o_intra = _bmm(att, _rows_bd4(vg3.astype(BF16), 64))

    last = 0 if rev else CHUNK - 1
    b3 = to3(b)
    b_last = b3[:, last:last + 1, :]
    qb = to3(q) * jnp.exp(b3)
    bd_r = lax.broadcasted_iota(jnp.int32, (1, MIX_W, 128), 1) // 64
    bd_c = lax.broadcasted_iota(jnp.int32, (1, MIX_W, 128), 2) // GLA_DK
    inc = jnp.where(bd_r == bd_c, _bmm_tn(vg3, to3(kg) * jnp.exp(b_last - b3)), 0.0)
    decay = jnp.exp(b_last)

    s_cur = s_scr[...]
    for ci in range(nch):
        c = (nch - 1 - ci) if rev else ci
        o_ref[c * CHUNK:(c + 1) * CHUNK, :] = o_intra[c] + _dot_nt(qb[c], s_cur)
        s_cur = s_cur * decay[c] + inc[c]
    s_scr[...] = s_cur

    @pl.when(j == nb - 1)
    def _():
        s_t = s_scr[...]
        m = s_t[0:64] + s_t[64:128] + s_t[128:192] + s_t[192:256]
        nat = jnp.concatenate([m, jnp.zeros_like(m)], axis=0).T
        sfin_ref[0] = nat[:, 0:64].reshape(4, GLA_DK, 64)


def _gla_dir(u_g, ab, abias, s0, nseq, seq_len, rev):
    tb = min(seq_len, SCAN_TB)
    nb = seq_len // tb
    t = u_g.shape[0]
    cur_map = (lambda b, j: (b * nb + nb - 1 - j, 0)) if rev else (lambda b, j: (b * nb + j, 0))
    st_map = lambda b, j: (b, 0, 0, 0)
    return pl.pallas_call(
        functools.partial(_gla_kernel, rev=rev, nb=nb),
        grid=(nseq, nb),
        in_specs=[pl.BlockSpec((tb, 896), cur_map), _full((128, 128)), _full((1, 128)),
                  pl.BlockSpec((1, 4, GLA_DK, 64), st_map)],
        out_specs=[pl.BlockSpec((tb, MIX_W), cur_map), pl.BlockSpec((1, 4, GLA_DK, 64), st_map)],
        out_shape=[jax.ShapeDtypeStruct((t, MIX_W), F32), jax.ShapeDtypeStruct((nseq, 4, GLA_DK, 64), F32)],
        scratch_shapes=[pltpu.VMEM((256, 128), F32)],
        compiler_params=_cparams(("parallel", "arbitrary")),
        name="gla_bwd" if rev else "gla_fwd",
    )(u_g, ab, abias, s0)


def _merge_kernel(x_ref, mod_ref, g_ref, ya_ref, yb_ref, of_ref, ob_ref, exf_ref, exb_ref, cag_ref,
                  gf_ref, gb_ref, gout_ref, gn_ref, bg_ref, gnorm_ref, wg_ref, wb_ref, wo_ref, x1_ref):
    mod = mod_ref[0]
    x = x_ref[...]
    h = _rms_mod(x, g_ref[...], mod[:, D_MODEL:2 * D_MODEL], mod[:, 0:D_MODEL]).astype(BF16)
    mean64 = _seg_ones(MIX_W, 64, 1.0 / 64.0)
    o = of_ref[...] + ob_ref[...]
    mu = _dot_exact_rhs(o, mean64)
    oc = o - mu
    var = _dot_exact_rhs(oc * oc, mean64)
    gate_c = _dot(jax.nn.sigmoid(cag_ref[...]), bg_ref[...])
    y_c = (oc * lax.rsqrt(var + RWKV_GN_EPS) * gn_ref[...] + exf_ref[...] + exb_ref[...]) * gate_c
    og = gf_ref[...] + gb_ref[...]
    gout = gout_ref[...]
    y_d = (og * lax.rsqrt(_dot_exact_rhs(og * og, mean64) + EPS) * gnorm_ref[...]
           * (gout * jax.nn.sigmoid(gout)))
    ys = (ya_ref[...], yb_ref[...], y_c, y_d)
    merged = jnp.zeros((TM, D_MODEL), F32)
    for i in range(4):
        gate = jax.nn.sigmoid(jnp.dot(h, wg_ref[:, i * D_MODEL:(i + 1) * D_MODEL], preferred_element_type=F32))
        merged = merged + gate * _dot(ys[i], wb_ref[i])
    out = _dot(merged, wo_ref[...])
    x1_ref[...] = x + mod[:, 2 * D_MODEL:3 * D_MODEL] * out


def _merge(x, mod3, g, y_a, y_b, o_f, o_b, ex_f, ex_b, u_r, g_f, g_b, u_g, prm, cond_base, rows_per_cond):
    t = x.shape[0]
    row = lambda i: (i, 0)
    mix = pl.BlockSpec((TM, MIX_W), row)
    vec = _full((1, MIX_W))
    return pl.pallas_call(
        _merge_kernel,
        grid=(t // TM,),
        in_specs=[pl.BlockSpec((TM, D_MODEL), row),
                  pl.BlockSpec((1, 1, 6 * D_MODEL), _cond_map(cond_base, rows_per_cond)),
                  _full((1, D_MODEL)),
                  mix, mix, mix, mix, mix, mix,
                  pl.BlockSpec((TM, 128), lambda i: (i, 7)),
                  mix, mix,
                  pl.BlockSpec((TM, MIX_W), lambda i: (i, 2)),
                  vec, _full((128, MIX_W)), vec,
                  _full((D_MODEL, 4 * D_MODEL)), _full((4, MIX_W, D_MODEL)), _full((D_MODEL, D_MODEL))],
        out_specs=pl.BlockSpec((TM, D_MODEL), row),
        out_shape=jax.ShapeDtypeStruct((t, D_MODEL), F32),
        compiler_params=_cparams(("parallel",)),
        name="merge_out",
    )(x, mod3, g, y_a, y_b, o_f, o_b, ex_f, ex_b, u_r, g_f, g_b, u_g,
      prm["gn"], prm["bg"], prm["gla_norm"], prm["w_gate"], prm["w_branch"], prm["w_out"])


def _mlp_kernel(x_ref, mod_ref, g_ref, w1_ref, w2_ref, fg_ref, x2_ref, y_ref):
    mod = mod_ref[0]
    x = x_ref[...]
    h = _rms_mod(x, g_ref[...], mod[:, 4 * D_MODEL:5 * D_MODEL], mod[:, 3 * D_MODEL:4 * D_MODEL]).astype(BF16)
    ff = jnp.zeros((TM, D_MODEL), F32)
    for c in range(D_FF // D_MODEL):
        cols = slice(c * D_MODEL, (c + 1) * D_MODEL)
        a = jnp.maximum(jnp.dot(h, w1_ref[:, cols], preferred_element_type=F32), 0.0)
        ff = ff + _dot(a * a, w2_ref[cols, :])
    x2 = x + mod[:, 5 * D_MODEL:6 * D_MODEL] * ff
    x2_ref[...] = x2
    y_ref[...] = x2 * lax.rsqrt(jnp.mean(x2 * x2, axis=-1, keepdims=True) + EPS) * fg_ref[...]


def _mlp(x1, mod3, g, w1, w2, final_g, cond_base, rows_per_cond):
    t = x1.shape[0]
    row = lambda i: (i, 0)
    return pl.pallas_call(
        _mlp_kernel,
        grid=(t // TM,),
        in_specs=[pl.BlockSpec((TM, D_MODEL), row),
                  pl.BlockSpec((1, 1, 6 * D_MODEL), _cond_map(cond_base, rows_per_cond)),
                  _full((1, D_MODEL)), _full((D_MODEL, D_FF)), _full((D_FF, D_MODEL)), _full((1, D_MODEL))],
        out_specs=[pl.BlockSpec((TM, D_MODEL), row), pl.BlockSpec((TM, D_MODEL), row)],
        out_shape=[jax.ShapeDtypeStruct((t, D_MODEL), F32), jax.ShapeDtypeStruct((t, D_MODEL), F32)],
        compiler_params=_cparams(("parallel",)),
        name="mlp",
    )(x1, mod3, g, w1, w2, final_g)


def _pos_tables(n_tok, d):
    rows = n_tok // GRID_W
    assert rows == GRID_W
    quarter = d // 4
    omega = 1.0 / (POS_BASE ** (jnp.arange(quarter, dtype=F32) / quarter))
    ar = jnp.arange(rows, dtype=F32)[:, None] * omega
    ac = jnp.arange(GRID_W, dtype=F32)[:, None] * omega
    return (jnp.concatenate([jnp.sin(ar), jnp.cos(ar)], axis=-1),
            jnp.concatenate([jnp.sin(ac), jnp.cos(ac)], axis=-1))


def _pad_rows(m, start, total):
    return jnp.zeros((total, m.shape[1]), m.dtype).at[start:start + m.shape[0]].set(m)


def _layer_params(l, p):
    out = {
        "w_mix": p["w_mix"][l],
        "w_gate": p["w_gate"][l],
        "w_branch": p["w_branch"][l].astype(BF16),
        "w_out": p["w_out"][l].astype(BF16),
        "w1": p["mlp_w1"][l].astype(BF16),
        "w2": p["mlp_w2"][l].astype(BF16),
        "n1": p["norm1_g"][l].reshape(1, D_MODEL),
        "n2": p["norm2_g"][l].reshape(1, D_MODEL),
        "pool_w": jnp.einsum("gcd,gh->gchd", p["pool_w"][l], jnp.eye(4, dtype=F32)).reshape(MIX_W, MIX_W).astype(BF16),
        "pool_scale": p["pool_scale"][l].reshape(1, MIX_W),
        "gn": p["rwkv_gn"][l].reshape(1, MIX_W),
        "bg": _pad_rows(p["rwkv_bg"][l], 64, 128).astype(BF16),
        "gla_norm": p["gla_norm"][l].reshape(1, MIX_W),
    }
    for d in range(2):
        out["rwkv%d" % d] = {
            "mu": p["rwkv_mu"][l].reshape(1, 768),
            "kk": p["rwkv_kk"][l].reshape(1, MIX_W),
            "ka": p["rwkv_ka"][l].reshape(1, MIX_W),
            "rk": p["rwkv_rk"][l].reshape(1, MIX_W),
            "w0": p["rwkv_w0"][l, d].reshape(1, MIX_W),
            "a0": p["rwkv_a0"][l, d].reshape(1, MIX_W),
            "bw": _pad_rows(p["rwkv_bw"][l, d], 64 * d, 128).astype(BF16),
            "ba": _pad_rows(p["rwkv_ba"][l, d], 32 * d, 128).astype(BF16),
        }
        out["gla_ab%d" % d] = _pad_rows(p["gla_ab"][l, d], 16 * d, 128).astype(BF16)
        out["gla_abias%d" % d] = p["gla_abias"][l, d].reshape(1, 128)
    return out


def _run_layer(x, mod3, prm, final_g, s_rwkv0, s_gla0, nseq, seq_len, cond_base, rows_per_cond, pos_tables=None):
    if pos_tables is None:
        u_p, u_f, u_r, u_g = _inproj(x, mod3, prm["n1"], prm["w_mix"], cond_base, rows_per_cond)
    else:
        x, u_p, u_f, u_r, u_g = _inproj(x, mod3, prm["n1"], prm["w_mix"], cond_base, rows_per_cond,
                                        pos_tables, seq_len)
    y_a = _pool(u_p, prm["pool_w"], prm["pool_scale"], nseq, seq_len)
    if seq_len == FFT_N1 * FFT_N1:
        y_b = _four_large(u_f, nseq, seq_len)
    else:
        y_b = _four_small(u_f, nseq, seq_len)
    o_f, ex_f, sr_f = _rwkv_dir(u_r, prm["rwkv0"], s_rwkv0[0], nseq, seq_len, False)
    o_b, ex_b, sr_b = _rwkv_dir(u_r, prm["rwkv1"], s_rwkv0[1], nseq, seq_len, True)
    g_f, sg_f = _gla_dir(u_g, prm["gla_ab0"], prm["gla_abias0"], s_gla0[0], nseq, seq_len, False)
    g_b, sg_b = _gla_dir(u_g, prm["gla_ab1"], prm["gla_abias1"], s_gla0[1], nseq, seq_len, True)
    x1 = _merge(x, mod3, prm["n1"], y_a, y_b, o_f, o_b, ex_f, ex_b, u_r, g_f, g_b, u_g, prm,
                cond_base, rows_per_cond)
    x2, y = _mlp(x1, mod3, prm["n2"], prm["w1"], prm["w2"], final_g, cond_base, rows_per_cond)
    return x2, y, jnp.stack([sr_f, sr_b], axis=1), jnp.stack([sg_f, sg_b], axis=1)


def kernel(x_prompt, x_sample, state_rwkv, state_gla, c, c_ctx, ada_w, ada_b, norm1_g, norm2_g, w_in, pool_w, pool_scale, rwkv_mu, rwkv_w0, rwkv_bw, rwkv_a0, rwkv_ba, rwkv_kk, rwkv_ka, rwkv_bg, rwkv_rk, rwkv_gn, gla_ab, gla_abias, gla_norm, w_branch, w_out, mlp_w1, mlp_w2, final_g):
    w_mix, w_gate = _split_w_in(w_in)
    p = dict(w_mix=w_mix, w_gate=w_gate, pool_w=pool_w, pool_scale=pool_scale, rwkv_mu=rwkv_mu, rwkv_w0=rwkv_w0,
             rwkv_bw=rwkv_bw, rwkv_a0=rwkv_a0, rwkv_ba=rwkv_ba, rwkv_kk=rwkv_kk, rwkv_ka=rwkv_ka,
             rwkv_bg=rwkv_bg, rwkv_rk=rwkv_rk, rwkv_gn=rwkv_gn, gla_ab=gla_ab, gla_abias=gla_abias,
             gla_norm=gla_norm, w_branch=w_branch, w_out=w_out, mlp_w1=mlp_w1, mlp_w2=mlp_w2,
             norm1_g=norm1_g, norm2_g=norm2_g)
    bp, lp, _ = x_prompt.shape
    bs, ls, _ = x_sample.shape
    cond8 = jnp.zeros((8, D_MODEL), F32).at[0].set(c_ctx).at[1:1 + bs].set(c)
    mods = _ada(cond8, ada_w, ada_b)
    fg = final_g.reshape(1, D_MODEL)

    xp = x_prompt.reshape(bp * lp, D_MODEL)
    xs = x_sample.reshape(bs * ls, D_MODEL)
    pos_tables = _pos_tables(ls, D_MODEL)
    zr = jnp.zeros((2, bp) + state_rwkv.shape[3:], F32)
    zg = jnp.zeros((2, bp) + state_gla.shape[3:], F32)
    new_r, new_g = [], []
    for l in range(DEPTH):
        prm = _layer_params(l, p)
        mod3 = mods[l].reshape(8, 1, 6 * D_MODEL)
        xp, yp, s_r, s_g = _run_layer(xp, mod3, prm, fg, zr, zg, bp, lp, 0, bp * lp)
        new_r.append(s_r)
        new_g.append(s_g)
        xs, ys, _, _ = _run_layer(xs, mod3, prm, fg, jnp.swapaxes(state_rwkv[:, l], 0, 1),
                                  jnp.swapaxes(state_gla[:, l], 0, 1), bs, ls, 1, ls,
                                  pos_tables if l == 0 else None)
    y_prompt = yp.reshape(bp, lp, D_MODEL)
    y_sample = ys.reshape(bs, ls, D_MODEL)
    return (y_prompt, y_sample, jnp.stack(new_r, axis=1), jnp.stack(new_g, axis=1))
```

```python
import functools

import numpy as np
import jax
import jax.numpy as jnp
from jax import lax
from jax.experimental import pallas as pl
from jax.experimental.pallas import tpu as pltpu

F32 = jnp.float32
BF16 = jnp.bfloat16

D_MODEL = 1024
DEPTH = 2
GRID_W = 64
POS_BASE = 10000.0
MIX_W = 256
POOL_WINDOWS = (2, 4, 8, 16)
POOL_HALO = 8
RWKV_DECAY_SCALE = 0.606531
RWKV_GN_EPS = 64e-5
GLA_DK = 32
GLA_GATE_NORM = 16.0
D_FF = 4 * D_MODEL
EPS = 1e-6
P_MIX = 2336
P_MIX_PAD = 2432
CHUNK = 64
SUB = 16

TM = 512
TB = 256
SCAN_TB = 512
FFT_N1 = 64
VMEM_LIMIT = 56 * 1024 * 1024


def _cparams(sem):
    return pltpu.CompilerParams(dimension_semantics=sem, vmem_limit_bytes=VMEM_LIMIT)


def _full(shape):
    n = len(shape)
    return pl.BlockSpec(shape, lambda *_: (0,) * n)


def _dot(a, b):
    return jnp.dot(a.astype(BF16), b.astype(BF16), preferred_element_type=F32)


def _dot_nt(a, b):
    return lax.dot_general(a.astype(BF16), b.astype(BF16), (((1,), (1,)), ((), ())),
                           preferred_element_type=F32)


def _split2(a):
    hi = a.astype(BF16)
    lo = (a - hi.astype(F32)).astype(BF16)
    return hi, lo


def _dot_exact_rhs(a, w):
    hi, lo = _split2(a)
    return (jnp.dot(hi, w, preferred_element_type=F32) + jnp.dot(lo, w, preferred_element_type=F32))


def _dot_exact_lhs(w, a):
    hi, lo = _split2(a)
    return jnp.dot(w, hi, preferred_element_type=F32) + jnp.dot(w, lo, preferred_element_type=F32)


def _seg_ones(n, seg, scale=1.0):
    r = lax.broadcasted_iota(jnp.int32, (n, n), 0)
    c = lax.broadcasted_iota(jnp.int32, (n, n), 1)
    return jnp.where((r // seg) == (c // seg), scale, 0.0).astype(BF16)


def _log_sigmoid(x):
    return jnp.minimum(x, 0.0) - jnp.log(1.0 + jnp.exp(-jnp.abs(x)))


def _rms_mod(x, g, sc, sh):
    ms = jnp.mean(x * x, axis=-1, keepdims=True)
    return x * lax.rsqrt(ms + EPS) * g * (1.0 + sc) + sh


def _ada_kernel(c_ref, w_ref, b_ref, o_ref):
    c = c_ref[...]
    s = c * jax.nn.sigmoid(c)
    hi, lo = _split2(s)
    w = w_ref[0].astype(BF16)
    o_ref[0] = (jnp.dot(hi, w, preferred_element_type=F32) + jnp.dot(lo, w, preferred_element_type=F32)
                + b_ref[0])


def _ada(cond8, ada_w, ada_b):
    tn = 1536
    n = 6 * D_MODEL
    return pl.pallas_call(
        _ada_kernel,
        grid=(DEPTH, n // tn),
        in_specs=[_full((8, D_MODEL)),
                  pl.BlockSpec((1, D_MODEL, tn), lambda l, j: (l, 0, j)),
                  pl.BlockSpec((1, 1, tn), lambda l, j: (l, 0, j))],
        out_specs=pl.BlockSpec((1, 8, tn), lambda l, j: (l, 0, j)),
        out_shape=jax.ShapeDtypeStruct((DEPTH, 8, n), F32),
        compiler_params=_cparams(("parallel", "parallel")),
        name="ada_mod",
    )(cond8, ada_w, ada_b.reshape(DEPTH, 1, n))


def _inproj_kernel(x_ref, mod_ref, g_ref, w_ref, up_ref, uf_ref, ur_ref, ug_ref):
    _inproj_body(x_ref[...], mod_ref, g_ref, w_ref, up_ref, uf_ref, ur_ref, ug_ref)


def _inproj_pos_kernel(x_ref, rr_ref, cc_ref, mod_ref, g_ref, w_ref, xo_ref, up_ref, uf_ref, ur_ref, ug_ref,
                       *, seq_len):
    nrow = TM // GRID_W
    g0 = pl.multiple_of(((pl.program_id(0) * TM) % seq_len) // GRID_W, nrow)
    half = D_MODEL // 2
    rpart = jnp.broadcast_to(rr_ref[pl.ds(g0, nrow), :][:, None, :], (nrow, GRID_W, half)).reshape(TM, half)
    cpart = jnp.concatenate([cc_ref[...]] * nrow, axis=0)
    x = x_ref[...] + jnp.concatenate([rpart, cpart], axis=1)
    xo_ref[...] = x
    _inproj_body(x, mod_ref, g_ref, w_ref, up_ref, uf_ref, ur_ref, ug_ref)


def _inproj_body(x, mod_ref, g_ref, w_ref, up_ref, uf_ref, ur_ref, ug_ref):
    mod = mod_ref[0]
    h = _rms_mod(x, g_ref[...], mod[:, D_MODEL:2 * D_MODEL], mod[:, 0:D_MODEL]).astype(BF16)
    up_ref[...] = _dot_nt(h, w_ref[0:256, :])
    uf_ref[...] = _dot_nt(h, w_ref[256:512, :])
    ur_ref[...] = _dot_nt(h, w_ref[512:1536, :])
    ug_ref[...] = _dot_nt(h, w_ref[1536:P_MIX_PAD, :])


def _cond_map(cond_base, rows_per_cond):
    return lambda i: (cond_base + (i * TM) // rows_per_cond, 0, 0)


def _inproj(x, mod3, g, w_t, layer, cond_base, rows_per_cond, pos_tables=None, seq_len=None):
    t = x.shape[0]
    row = lambda i: (i, 0)
    in_specs = [pl.BlockSpec((1, 1, 6 * D_MODEL), _cond_map(cond_base, rows_per_cond)),
                _full((1, D_MODEL)),
                pl.BlockSpec((None, P_MIX_PAD, D_MODEL), lambda i: (layer, 0, 0))]
    out_specs = [pl.BlockSpec((TM, 256), row), pl.BlockSpec((TM, 256), row),
                 pl.BlockSpec((TM, 1024), row), pl.BlockSpec((TM, 896), row)]
    out_shape = [jax.ShapeDtypeStruct((t, 256), F32), jax.ShapeDtypeStruct((t, 256), F32),
                 jax.ShapeDtypeStruct((t, 1024), F32), jax.ShapeDtypeStruct((t, 896), F32)]
    x_spec = pl.BlockSpec((TM, D_MODEL), row)
    if pos_tables is None:
        return pl.pallas_call(
            _inproj_kernel, grid=(t // TM,), in_specs=[x_spec] + in_specs, out_specs=out_specs,
            out_shape=out_shape, compiler_params=_cparams(("parallel",)), name="in_proj",
        )(x, mod3, g, w_t)
    tab = _full((GRID_W, D_MODEL // 2))
    return pl.pallas_call(
        functools.partial(_inproj_pos_kernel, seq_len=seq_len), grid=(t // TM,),
        in_specs=[x_spec, tab, tab] + in_specs, out_specs=[x_spec] + out_specs,
        out_shape=[jax.ShapeDtypeStruct((t, D_MODEL), F32)] + out_shape,
        compiler_params=_cparams(("parallel",)), name="in_proj_pos",
    )(x, pos_tables[0], pos_tables[1], mod3, g, w_t)


def _pool_kernel(prev_ref, cur_ref, next_ref, wbd_ref, scale_ref, y_ref, *, seq_len):
    j = pl.program_id(1)
    nb = seq_len // TB
    cur = cur_ref[...]
    prev = jnp.where(j > 0, prev_ref[...], 0.0)
    nxt = jnp.where(j < nb - 1, next_ref[...], 0.0)
    ext = jnp.concatenate([prev, cur, nxt], axis=0)
    n_ext = TB + 2 * POOL_HALO

    def shift(x, k):
        return pltpu.roll(x, k % n_ext, 0)

    sums = [ext + shift(ext, 1)]
    for k in (1, 2, 4):
        sums.append(shift(sums[-1], -k) + shift(sums[-1], k))
    assert POOL_WINDOWS == (2, 4, 8, 16)
    pos = j * TB + lax.broadcasted_iota(jnp.int32, (TB, MIX_W), 0)
    group = lax.broadcasted_iota(jnp.int32, (TB, MIX_W), 1) // (MIX_W // len(POOL_WINDOWS))
    pooled = jnp.zeros((TB, MIX_W), F32)
    for gi, win in enumerate(POOL_WINDOWS):
        before, after = win // 2, win - win // 2 - 1
        s = sums[gi][POOL_HALO:POOL_HALO + TB]
        cnt = (jnp.minimum(pos + after, seq_len - 1) - jnp.maximum(pos - before, 0) + 1).astype(F32)
        pooled = jnp.where(group == gi, s / cnt - cur, pooled)
    y_ref[...] = _dot(pooled, wbd_ref[...]) * scale_ref[...]


def _pool(u_p, wbd, scale, nseq, seq_len):
    nb = seq_len // TB
    hb = TB // POOL_HALO
    nh = u_p.shape[0] // POOL_HALO
    return pl.pallas_call(
        functools.partial(_pool_kernel, seq_len=seq_len),
        grid=(nseq, nb),
        in_specs=[pl.BlockSpec((POOL_HALO, MIX_W), lambda b, j: (jnp.maximum((b * nb + j) * hb - 1, 0), 0)),
                  pl.BlockSpec((TB, MIX_W), lambda b, j: (b * nb + j, 0)),
                  pl.BlockSpec((POOL_HALO, MIX_W), lambda b, j: (jnp.minimum((b * nb + j + 1) * hb, nh - 1), 0)),
                  _full((MIX_W, MIX_W)), _full((1, MIX_W))],
        out_specs=pl.BlockSpec((TB, MIX_W), lambda b, j: (b * nb + j, 0)),
        out_shape=jax.ShapeDtypeStruct(u_p.shape, F32),
        compiler_params=_cparams(("parallel", "parallel")),
        name="pool_mixer",
    )(u_p, u_p, u_p, wbd, scale)


def _dft_tables(n):
    t = np.arange(n)
    ang = 2.0 * np.pi * ((np.outer(t, t)) % n) / n
    return np.cos(ang), np.sin(ang)


def _bd_np(m, copies):
    k = m.shape[0]
    out = np.zeros((k * copies, k * copies), m.dtype)
    for i in range(copies):
        out[i * k:(i + 1) * k, i * k:(i + 1) * k] = m
    return out


def _four_small_kernel(z_ref, cs_ref, cbd_ref, sbd_ref, y_ref, *, seq_len, norm):
    zf = _dot(cs_ref[...], z_ref[...])
    y = _dot(zf[:seq_len], cbd_ref[...]) + _dot(zf[seq_len:], sbd_ref[...])
    y_ref[...] = y * norm


def _four_small(u_f, nseq, seq_len):
    c, s = _dft_tables(seq_len)
    cs = jnp.asarray(np.concatenate([c, -s], axis=0), F32)
    c64, s64 = _dft_tables(MIX_W // 4)
    cbd = jnp.asarray(_bd_np(c64, 4), F32)
    sbd = jnp.asarray(_bd_np(s64, 4), F32)
    return pl.pallas_call(
        functools.partial(_four_small_kernel, seq_len=seq_len, norm=float((seq_len * 64) ** -0.5)),
        grid=(nseq,),
        in_specs=[pl.BlockSpec((seq_len, MIX_W), lambda b: (b, 0)),
                  _full((2 * seq_len, seq_len)), _full((MIX_W, MIX_W)), _full((MIX_W, MIX_W))],
        out_specs=pl.BlockSpec((seq_len, MIX_W), lambda b: (b, 0)),
        out_shape=jax.ShapeDtypeStruct(u_f.shape, F32),
        compiler_params=_cparams(("parallel",)),
        name="fourier_small",
    )(u_f, cs, cbd, sbd)


def _four_large_kernel(z_ref, fs_ref, g_ref, cbd_ref, sbd_ref, y_ref, a_scr, y_scr, *, norm):
    n1 = FFT_N1
    fs = fs_ref[...].astype(BF16)
    cbd = cbd_ref[...].astype(BF16)
    sbd = sbd_ref[...].astype(BF16)
    for t2 in range(n1):
        a_scr[t2 * 2 * n1:(t2 + 1) * 2 * n1, :] = _dot(fs, z_ref[pl.ds(t2, n1, stride=n1), :])
    for f1 in range(n1):
        a = jnp.concatenate([a_scr[pl.ds(f1, n1, stride=2 * n1), :],
                             a_scr[pl.ds(n1 + f1, n1, stride=2 * n1), :]], axis=0)
        x = _dot(g_ref[f1], a)
        y_scr[f1 * n1:(f1 + 1) * n1, :] = (_dot(x[:n1], cbd) + _dot(x[n1:], sbd)) * norm
    for f2 in range(n1):
        y_ref[f2 * n1:(f2 + 1) * n1, :] = y_scr[pl.ds(f2, n1, stride=n1), :]


def _four_large(u_f, nseq, seq_len):
    n1 = FFT_N1
    assert seq_len == n1 * n1
    c64, s64 = _dft_tables(n1)
    fs = jnp.asarray(np.concatenate([c64, -s64], axis=0), F32)
    f1 = np.arange(n1)[:, None, None]
    f2 = np.arange(n1)[None, :, None]
    t2 = np.arange(n1)[None, None, :]
    ang = 2.0 * np.pi * ((t2 * (f1 + n1 * f2)) % seq_len) / seq_len
    gr, gi = np.cos(ang), -np.sin(ang)
    g = jnp.asarray(np.concatenate([np.concatenate([gr, -gi], axis=2),
                                    np.concatenate([gi, gr], axis=2)], axis=1), F32)
    lanes = 128
    cbd = jnp.asarray(_bd_np(c64, lanes // 64), F32)
    sbd = jnp.asarray(_bd_np(s64, lanes // 64), F32)
    seq = pl.BlockSpec((seq_len, lanes), lambda b, h: (b, h))
    return pl.pallas_call(
        functools.partial(_four_large_kernel, norm=float((seq_len * 64) ** -0.5)),
        grid=(nseq, MIX_W // lanes),
        in_specs=[seq, _full((2 * n1, n1)), _full((n1, 2 * n1, 2 * n1)),
                  _full((lanes, lanes)), _full((lanes, lanes))],
        out_specs=seq,
        out_shape=jax.ShapeDtypeStruct(u_f.shape, F32),
        scratch_shapes=[pltpu.VMEM((2 * seq_len, lanes), F32), pltpu.VMEM((seq_len, lanes), F32)],
        compiler_params=_cparams(("parallel", "parallel")),
        name="fourier_large",
    )(u_f, fs, g, cbd, sbd)


def _bmm(x, y):
    return lax.dot_general(x.astype(BF16), y.astype(BF16), (((2,), (1,)), ((0,), (0,))),
                           preferred_element_type=F32)


def _bmm_nt(x, y):
    return lax.dot_general(x.astype(BF16), y.astype(BF16), (((2,), (2,)), ((0,), (0,))),
                           preferred_element_type=F32)


def _bmm_tn(x, y):
    return lax.dot_general(x.astype(BF16), y.astype(BF16), (((1,), (1,)), ((0,), (0,))),
                           preferred_element_type=F32)


def _slab_bd(y):
    lane = lax.broadcasted_iota(jnp.int32, y.shape, 2)
    z = jnp.zeros_like(y)
    return jnp.concatenate([jnp.where(lane < 64, y, z), jnp.where(lane >= 64, y, z)], axis=1)


def _mm(x, y):
    return _bmm(x, _slab_bd(y.astype(BF16)))


def _mm_pair(x, y1, y2):
    w = jnp.concatenate([_slab_bd(y1.astype(BF16)), _slab_bd(y2.astype(BF16))], axis=2)
    out = _bmm(x, w)
    return out[:, :, :128], out[:, :, 128:]


def _mm3(xs, y):
    yh, yl = _split2(y)
    bh, bl = _slab_bd(yh), _slab_bd(yl)
    parts = [_split2(x) for x in xs]
    xh = jnp.concatenate([p[0] for p in parts], axis=1)
    xl = jnp.concatenate([p[1] for p in parts], axis=1)
    hx = _bmm(xh, jnp.concatenate([bh, bl], axis=2))
    out = hx[:, :, :128] + (hx[:, :, 128:] + _bmm(xl, bh))
    return [out[:, i * CHUNK:(i + 1) * CHUNK] for i in range(len(xs))]


def _to_slabs(x):
    nch = x.shape[0] // CHUNK
    x4 = x.reshape(nch, CHUNK, MIX_W)
    return jnp.stack([x4[:, :, :128], x4[:, :, 128:]], axis=1).reshape(2 * nch, CHUNK, 128)


def _tri_inverse(n_mat, eye, blk):
    nd = jnp.where(blk, n_mat, 0.0)
    no = n_mat - nd
    n2, = _mm3([nd], nd)
    n4, y = _mm3([n2, nd], n2)
    x = n2 - nd - y
    n8, y = _mm3([n4, x], n4)
    x = x + n4 + y
    y, = _mm3([x], n8)
    x = x + n8 + y
    td = eye + x
    e = _mm(td, no)
    e2 = _mm(e, e)
    e3 = _mm(e, e2)
    return td + _mm(e2 - e - e3, td)


def _rwkv_kernel(cur_ref, prev_ref, next_ref, mu_ref, kkw_ref, ka_ref, rk_ref, w0_ref, a0_ref,
                 bw_ref, ba_ref, s0_ref, o_ref, ex_ref, sfin_ref, s_scr, *, rev, nb):
    j = pl.program_id(1)
    jj = (nb - 1 - j) if rev else j

    bd_r2 = lax.broadcasted_iota(jnp.int32, (128, 128), 0) // 64
    bd_c2 = lax.broadcasted_iota(jnp.int32, (128, 128), 1) // 64

    @pl.when(j == 0)
    def _():
        for hp in range(2):
            sp = s0_ref[0, 2 * hp:2 * hp + 2].reshape(128, 64)
            s_scr[hp] = jnp.where(bd_r2 == bd_c2, jnp.concatenate([sp, sp], axis=1), 0.0)

    tb = cur_ref.shape[0]
    z = cur_ref[:, 0:768]
    prow = jnp.where(jj > 0, prev_ref[POOL_HALO - 1:POOL_HALO, :], 0.0)
    nrow = jnp.where(jj < nb - 1, next_ref[0:1, :], 0.0)
    rowid = lax.broadcasted_iota(jnp.int32, (tb, 768), 0)
    up = jnp.where(rowid == 0, prow, pltpu.roll(z, 1, 0))
    dn = jnp.where(rowid == tb - 1, nrow, pltpu.roll(z, tb - 1, 0))
    zm = z + mu_ref[...] * (0.5 * (up + dn) - z)
    r = zm[:, 0:256]
    k = zm[:, 256:512]
    v = zm[:, 512:768]
    ones_bd = _seg_ones(MIX_W, 64)
    kk = k * kkw_ref[...]
    kk = kk * lax.rsqrt(_dot(kk * kk, ones_bd) + EPS)
    w_log = w0_ref[...] + _dot(jnp.tanh(cur_ref[:, 768:896]), bw_ref[...])
    lw = -RWKV_DECAY_SCALE * jax.nn.sigmoid(w_log)
    a = jax.nn.sigmoid(a0_ref[...] + _dot(cur_ref[:, 896:1024], ba_ref[...]))
    kd = k * (1.0 + (a - 1.0) * ka_ref[...])
    ex_ref[...] = _dot(r * kd * rk_ref[...], ones_bd) * v

    nch = tb // CHUNK
    tr = lax.broadcasted_iota(jnp.int32, (tb, tb), 0)
    tc = lax.broadcasted_iota(jnp.int32, (tb, tb), 1)
    same_chunk = (tr // CHUNK) == (tc // CHUNK)
    tri = jnp.where(same_chunk & ((tc >= tr) if rev else (tc <= tr)), 1.0, 0.0).astype(BF16)
    cum = _dot_exact_lhs(tri, lw)
    e_in = jnp.exp(cum)
    e_out = jnp.exp(-cum)
    last = 0 if rev else CHUNK - 1
    rt = _to_slabs(r * e_in)
    kh = _to_slabs(kd * e_out)
    bh = _to_slabs(kk * a * e_out)
    kt = _to_slabs(kk * jnp.exp(cum - lw))
    vv = _to_slabs(v)
    gam = _to_slabs(e_in)[:, last:last + 1, :]

    t_i = lax.broadcasted_iota(jnp.int32, (1, CHUNK, 128), 1)
    s_i = lax.broadcasted_iota(jnp.int32, (1, CHUNK, 128), 2) & (CHUNK - 1)
    if rev:
        strict, incl = s_i > t_i, s_i >= t_i
    else:
        strict, incl = s_i < t_i, s_i <= t_i
    blk = (s_i // SUB) == (t_i // SUB)
    eye = jnp.where(s_i == t_i, 1.0, 0.0)

    lhs = jnp.concatenate([kt, rt], axis=1)
    kb_bd = jnp.concatenate([_slab_bd(kh.astype(BF16)), _slab_bd(bh.astype(BF16))], axis=1)
    mkb = _bmm_nt(lhs, kb_bd)
    m_mat = jnp.where(strict, mkb[:, :CHUNK, :128], 0.0)
    p_mat = jnp.where(incl, mkb[:, CHUNK:, :128], 0.0)
    n_mat = jnp.where(strict, mkb[:, :CHUNK, 128:], 0.0)
    q_mat = jnp.where(incl, mkb[:, CHUNK:, 128:], 0.0)
    t_mat = _tri_inverse(n_mat, eye, blk)
    mpv = _mm(jnp.concatenate([m_mat, p_mat], axis=1), vv)
    mv, pv = mpv[:, :CHUNK], mpv[:, CHUNK:]
    tk, uv = _mm_pair(t_mat, kt, mv)
    qtk, quv = _mm_pair(q_mat, tk, uv)
    rc = rt - qtk
    oc = pv - quv
    bd_r = lax.broadcasted_iota(jnp.int32, (1, 128, 128), 1) // 64
    bd_c = lax.broadcasted_iota(jnp.int32, (1, 128, 128), 2) // 64
    bd_mask = bd_r == bd_c
    d_mat = jnp.where(bd_mask, _bmm_tn(tk, bh), 0.0)
    braw = jnp.where(bd_mask,
                     _bmm_tn(jnp.concatenate([vv, uv], axis=1), jnp.concatenate([kh, -bh], axis=1)), 0.0)

    s_cur = s_scr[...]
    for ci in range(nch):
        c = (nch - 1 - ci) if rev else ci
        g = slice(2 * c, 2 * c + 2)
        o = _bmm_nt(rc[g], s_cur) + oc[g]
        o_ref[c * CHUNK:(c + 1) * CHUNK, 0:128] = o[0]
        o_ref[c * CHUNK:(c + 1) * CHUNK, 128:256] = o[1]
        s_cur = (s_cur - _bmm(s_cur, d_mat[g]) + braw[g]) * gam[g]
    s_scr[...] = s_cur

    @pl.when(j == nb - 1)
    def _():
        for hp in range(2):
            s_hp = s_scr[hp]
            sfin_ref[0, 2 * hp] = s_hp[0:64, 0:64]
            sfin_ref[0, 2 * hp + 1] = s_hp[64:128, 64:128]


def _rwkv_dir(u_r, prm, s0, nseq, seq_len, rev):
    tb = min(seq_len, SCAN_TB)
    nb = seq_len // tb
    hb = tb // POOL_HALO
    nh = u_r.shape[0] // POOL_HALO
    t = u_r.shape[0]

    def jj(j):
        return (nb - 1 - j) if rev else j

    cur_map = lambda b, j: (b * nb + jj(j), 0)
    st_map = lambda b, j: (b, 0, 0, 0)
    vec = _full((1, MIX_W))
    return pl.pallas_call(
        functools.partial(_rwkv_kernel, rev=rev, nb=nb),
        grid=(nseq, nb),
        in_specs=[pl.BlockSpec((tb, 1024), cur_map),
                  pl.BlockSpec((POOL_HALO, 768), lambda b, j: (jnp.maximum((b * nb + jj(j)) * hb - 1, 0), 0)),
                  pl.BlockSpec((POOL_HALO, 768), lambda b, j: (jnp.minimum((b * nb + jj(j) + 1) * hb, nh - 1), 0)),
                  _full((1, 768)), vec, vec, vec, vec, vec,
                  _full((128, MIX_W)), _full((128, MIX_W)),
                  pl.BlockSpec((1, 4, 64, 64), st_map)],
        out_specs=[pl.BlockSpec((tb, MIX_W), cur_map), pl.BlockSpec((tb, MIX_W), cur_map),
                   pl.BlockSpec((1, 4, 64, 64), st_map)],
        out_shape=[jax.ShapeDtypeStruct((t, MIX_W), F32), jax.ShapeDtypeStruct((t, MIX_W), F32),
                   jax.ShapeDtypeStruct((nseq, 4, 64, 64), F32)],
        scratch_shapes=[pltpu.VMEM((2, 128, 128), F32)],
        compiler_params=_cparams(("parallel", "arbitrary")),
        name="rwkv_bwd" if rev else "rwkv_fwd",
    )(u_r, u_r, u_r, prm["mu"], prm["kk"], prm["ka"], prm["rk"], prm["w0"], prm["a0"],
      prm["bw"], prm["ba"], s0)


def _rows_bd4(y, width):
    lane = lax.broadcasted_iota(jnp.int32, y.shape, 2) // width
    z = jnp.zeros_like(y)
    return jnp.concatenate([jnp.where(lane == h, y, z) for h in range(4)], axis=1)


def _block_scans(x, levels):
    rows, width = x.shape
    row = lax.broadcasted_iota(jnp.int32, (rows, width), 0)
    pre, suf = [x], [x]
    for m in range(1, levels + 1):
        size, half = 1 << m, 1 << (m - 1)
        p, s = pre[-1], suf[-1]
        if half < 8:
            off = row & (size - 1)
            p3 = p.reshape(rows // 8, 8, width)
            s3 = s.reshape(rows // 8, 8, width)
            addp = jnp.zeros_like(x)
            adds = jnp.zeros_like(x)
            for o in range(half):
                addp = jnp.where(off == half + o, pltpu.roll(p3, o + 1, 1).reshape(rows, width), addp)
                adds = jnp.where(off == o, pltpu.roll(s3, 8 - (half - o), 1).reshape(rows, width), adds)
        else:
            p4 = p.reshape(rows // size, size, width)
            s4 = s.reshape(rows // size, size, width)
            second = (row & half) != 0
            addp = jnp.where(second, jnp.broadcast_to(p4[:, half - 1:half, :], p4.shape).reshape(rows, width), 0.0)
            adds = jnp.where(second, 0.0, jnp.broadcast_to(s4[:, half:half + 1, :], s4.shape).reshape(rows, width))
        pre.append(p + addp)
        suf.append(s + adds)
    return pre, suf


def _gla_kernel(u_ref, ab_ref, abias_ref, s0_ref, o_ref, sfin_ref, s_scr, *, rev, nb):
    j = pl.program_id(1)

    @pl.when(j == 0)
    def _():
        sp = s0_ref[0].reshape(4 * GLA_DK, 64)
        st = jnp.concatenate([sp, sp], axis=1).T[0:64, :]
        lane_h = lax.broadcasted_iota(jnp.int32, (64, 128), 1) // GLA_DK
        s_scr[...] = jnp.concatenate([jnp.where(lane_h == h, st, 0.0) for h in range(4)], axis=0)

    nch = u_ref.shape[0] // CHUNK
    la = _log_sigmoid(_dot(u_ref[:, 768:896], ab_ref[...]) + abias_ref[...]) * (1.0 / GLA_GATE_NORM)
    q = u_ref[:, 0:128] * float(GLA_DK ** -0.5)
    kg = u_ref[:, 128:256]
    vg3 = u_ref[:, 256:512].reshape(nch, CHUNK, MIX_W)

    pre, suf = _block_scans(la, 6)
    if rev:
        near, far, b = suf, pre, suf[6]
    else:
        near, far, b = pre, suf, pre[6]
    t_i = lax.broadcasted_iota(jnp.int32, (1, CHUNK, MIX_W), 1)
    s_i = lax.broadcasted_iota(jnp.int32, (1, CHUNK, MIX_W), 2) & (CHUNK - 1)

    def to3(x):
        return x.reshape(nch, CHUNK, 128)

    att = jnp.where(s_i == t_i, _bmm_nt(to3(q), _rows_bd4(to3(kg).astype(BF16), GLA_DK)), 0.0)
    for lev in range(1, 7):
        size, half = 1 << lev, 1 << (lev - 1)
        same = (t_i // size) == (s_i // size)
        if rev:
            mask = same & ((t_i & (size - 1)) < half) & ((s_i & (size - 1)) >= half)
        else:
            mask = same & ((t_i & (size - 1)) >= half) & ((s_i & (size - 1)) < half)
        qe = q * jnp.exp(near[lev - 1])
        ke = kg * jnp.exp(far[lev - 1] - la)
        att = att + jnp.where(mask, _bmm_nt(to3(qe), _rows_bd4(to3(ke).astype(BF16), GLA_DK)), 0.0)
    o_intra = _bmm(att, _rows_bd4(vg3.astype(BF16), 64))

    last = 0 if rev else CHUNK - 1
    b3 = to3(b)
    b_last = b3[:, last:last + 1, :]
    qb = to3(q) * jnp.exp(b3)
    bd_r = lax.broadcasted_iota(jnp.int32, (1, MIX_W, 128), 1) // 64
    bd_c = lax.broadcasted_iota(jnp.int32, (1, MIX_W, 128), 2) // GLA_DK
    inc = jnp.where(bd_r == bd_c, _bmm_tn(vg3, to3(kg) * jnp.exp(b_last - b3)), 0.0)
    decay = jnp.exp(b_last)

    s_cur = s_scr[...]
    for ci in range(nch):
        c = (nch - 1 - ci) if rev else ci
        o_ref[c * CHUNK:(c + 1) * CHUNK, :] = o_intra[c] + _dot_nt(qb[c], s_cur)
        s_cur = s_cur * decay[c] + inc[c]
    s_scr[...] = s_cur

    @pl.when(j == nb - 1)
    def _():
        s_t = s_scr[...]
        m = s_t[0:64] + s_t[64:128] + s_t[128:192] + s_t[192:256]
        nat = jnp.concatenate([m, jnp.zeros_like(m)], axis=0).T
        sfin_ref[0] = nat[:, 0:64].reshape(4, GLA_DK, 64)


def _gla_dir(u_g, ab, abias, s0, nseq, seq_len, rev):
    tb = min(seq_len, SCAN_TB)
    nb = seq_len // tb
    t = u_g.shape[0]
    cur_map = (lambda b, j: (b * nb + nb - 1 - j, 0)) if rev else (lambda b, j: (b * nb + j, 0))
    st_map = lambda b, j: (b, 0, 0, 0)
    return pl.pallas_call(
        functools.partial(_gla_kernel, rev=rev, nb=nb),
        grid=(nseq, nb),
        in_specs=[pl.BlockSpec((tb, 896), cur_map), _full((128, 128)), _full((1, 128)),
                  pl.BlockSpec((1, 4, GLA_DK, 64), st_map)],
        out_specs=[pl.BlockSpec((tb, MIX_W), cur_map), pl.BlockSpec((1, 4, GLA_DK, 64), st_map)],
        out_shape=[jax.ShapeDtypeStruct((t, MIX_W), F32), jax.ShapeDtypeStruct((nseq, 4, GLA_DK, 64), F32)],
        scratch_shapes=[pltpu.VMEM((256, 128), F32)],
        compiler_params=_cparams(("parallel", "arbitrary")),
        name="gla_bwd" if rev else "gla_fwd",
    )(u_g, ab, abias, s0)


def _merge_kernel(x_ref, mod_ref, g_ref, ya_ref, yb_ref, of_ref, ob_ref, exf_ref, exb_ref, cag_ref,
                  gf_ref, gb_ref, gout_ref, gn_ref, bg_ref, gnorm_ref, wg_ref, wb_ref, wo_ref, x1_ref):
    mod = mod_ref[0]
    x = x_ref[...]
    h = _rms_mod(x, g_ref[...], mod[:, D_MODEL:2 * D_MODEL], mod[:, 0:D_MODEL]).astype(BF16)
    mean64 = _seg_ones(MIX_W, 64, 1.0 / 64.0)
    o = of_ref[...] + ob_ref[...]
    mu = _dot_exact_rhs(o, mean64)
    oc = o - mu
    var = _dot_exact_rhs(oc * oc, mean64)
    gate_c = _dot(jax.nn.sigmoid(cag_ref[...]), bg_ref[...])
    y_c = (oc * lax.rsqrt(var + RWKV_GN_EPS) * gn_ref[...] + exf_ref[...] + exb_ref[...]) * gate_c
    og = gf_ref[...] + gb_ref[...]
    gout = gout_ref[...]
    y_d = (og * lax.rsqrt(_dot_exact_rhs(og * og, mean64) + EPS) * gnorm_ref[...]
           * (gout * jax.nn.sigmoid(gout)))
    ys = (ya_ref[...], yb_ref[...], y_c, y_d)
    merged = jnp.zeros((TM, D_MODEL), F32)
    for i in range(4):
        gate = jax.nn.sigmoid(_dot_nt(h, wg_ref[i * D_MODEL:(i + 1) * D_MODEL, :]))
        merged = merged + gate * _dot(ys[i], wb_ref[i])
    out = _dot(merged, wo_ref[...])
    x1_ref[...] = x + mod[:, 2 * D_MODEL:3 * D_MODEL] * out


def _merge(x, mod3, g, y_a, y_b, o_f, o_b, ex_f, ex_b, u_r, g_f, g_b, u_g, prm, big, layer,
           cond_base, rows_per_cond):
    t = x.shape[0]
    row = lambda i: (i, 0)
    mix = pl.BlockSpec((TM, MIX_W), row)
    vec = _full((1, MIX_W))
    lay3 = lambda i: (layer, 0, 0)
    return pl.pallas_call(
        _merge_kernel,
        grid=(t // TM,),
        in_specs=[pl.BlockSpec((TM, D_MODEL), row),
                  pl.BlockSpec((1, 1, 6 * D_MODEL), _cond_map(cond_base, rows_per_cond)),
                  _full((1, D_MODEL)),
                  mix, mix, mix, mix, mix, mix,
                  pl.BlockSpec((TM, 128), lambda i: (i, 7)),
                  mix, mix,
                  pl.BlockSpec((TM, MIX_W), lambda i: (i, 2)),
                  vec, _full((128, MIX_W)), vec,
                  pl.BlockSpec((None, 4 * D_MODEL, D_MODEL), lay3),
                  pl.BlockSpec((None, 4, MIX_W, D_MODEL), lambda i: (layer, 0, 0, 0)),
                  pl.BlockSpec((None, D_MODEL, D_MODEL), lay3)],
        out_specs=pl.BlockSpec((TM, D_MODEL), row),
        out_shape=jax.ShapeDtypeStruct((t, D_MODEL), F32),
        compiler_params=_cparams(("parallel",)),
        name="merge_out",
    )(x, mod3, g, y_a, y_b, o_f, o_b, ex_f, ex_b, u_r, g_f, g_b, u_g,
      prm["gn"], prm["bg"], prm["gla_norm"], big["w_gate_t"], big["w_branch"], big["w_out"])


def _mlp_kernel(x_ref, mod_ref, g_ref, w1_ref, w2_ref, fg_ref, x2_ref, y_ref):
    mod = mod_ref[0]
    x = x_ref[...]
    h = _rms_mod(x, g_ref[...], mod[:, 4 * D_MODEL:5 * D_MODEL], mod[:, 3 * D_MODEL:4 * D_MODEL]).astype(BF16)
    ff = jnp.zeros((TM, D_MODEL), F32)
    for c in range(D_FF // D_MODEL):
        cols = slice(c * D_MODEL, (c + 1) * D_MODEL)
        a = jnp.maximum(jnp.dot(h, w1_ref[:, cols], preferred_element_type=F32), 0.0)
        ff = ff + _dot(a * a, w2_ref[cols, :])
    x2 = x + mod[:, 5 * D_MODEL:6 * D_MODEL] * ff
    x2_ref[...] = x2
    y_ref[...] = x2 * lax.rsqrt(jnp.mean(x2 * x2, axis=-1, keepdims=True) + EPS) * fg_ref[...]


def _mlp(x1, mod3, g, w1, w2, layer, final_g, cond_base, rows_per_cond):
    t = x1.shape[0]
    row = lambda i: (i, 0)
    lay3 = lambda i: (layer, 0, 0)
    return pl.pallas_call(
        _mlp_kernel,
        grid=(t // TM,),
        in_specs=[pl.BlockSpec((TM, D_MODEL), row),
                  pl.BlockSpec((1, 1, 6 * D_MODEL), _cond_map(cond_base, rows_per_cond)),
                  _full((1, D_MODEL)), pl.BlockSpec((None, D_MODEL, D_FF), lay3),
                  pl.BlockSpec((None, D_FF, D_MODEL), lay3), _full((1, D_MODEL))],
        out_specs=[pl.BlockSpec((TM, D_MODEL), row), pl.BlockSpec((TM, D_MODEL), row)],
        out_shape=[jax.ShapeDtypeStruct((t, D_MODEL), F32), jax.ShapeDtypeStruct((t, D_MODEL), F32)],
        compiler_params=_cparams(("parallel",)),
        name="mlp",
    )(x1, mod3, g, w1, w2, final_g)


def _pos_tables(n_tok, d):
    rows = n_tok // GRID_W
    assert rows == GRID_W
    quarter = d // 4
    omega = 1.0 / (POS_BASE ** (jnp.arange(quarter, dtype=F32) / quarter))
    ar = jnp.arange(rows, dtype=F32)[:, None] * omega
    ac = jnp.arange(GRID_W, dtype=F32)[:, None] * omega
    return (jnp.concatenate([jnp.sin(ar), jnp.cos(ar)], axis=-1),
            jnp.concatenate([jnp.sin(ac), jnp.cos(ac)], axis=-1))


def _pad_rows(m, start, total):
    return jnp.zeros((total, m.shape[1]), m.dtype).at[start:start + m.shape[0]].set(m)


def _layer_params(l, p):
    out = {
        "n1": p["norm1_g"][l].reshape(1, D_MODEL),
        "n2": p["norm2_g"][l].reshape(1, D_MODEL),
        "pool_w": jnp.einsum("gcd,gh->gchd", p["pool_w"][l], jnp.eye(4, dtype=F32)).reshape(MIX_W, MIX_W).astype(BF16),
        "pool_scale": p["pool_scale"][l].reshape(1, MIX_W),
        "gn": p["rwkv_gn"][l].reshape(1, MIX_W),
        "bg": _pad_rows(p["rwkv_bg"][l], 64, 128).astype(BF16),
        "gla_norm": p["gla_norm"][l].reshape(1, MIX_W),
    }
    for d in range(2):
        out["rwkv%d" % d] = {
            "mu": p["rwkv_mu"][l].reshape(1, 768),
            "kk": p["rwkv_kk"][l].reshape(1, MIX_W),
            "ka": p["rwkv_ka"][l].reshape(1, MIX_W),
            "rk": p["rwkv_rk"][l].reshape(1, MIX_W),
            "w0": p["rwkv_w0"][l, d].reshape(1, MIX_W),
            "a0": p["rwkv_a0"][l, d].reshape(1, MIX_W),
            "bw": _pad_rows(p["rwkv_bw"][l, d], 64 * d, 128).astype(BF16),
            "ba": _pad_rows(p["rwkv_ba"][l, d], 32 * d, 128).astype(BF16),
        }
        out["gla_ab%d" % d] = _pad_rows(p["gla_ab"][l, d], 16 * d, 128).astype(BF16)
        out["gla_abias%d" % d] = p["gla_abias"][l, d].reshape(1, 128)
    return out


def _run_layer(x, mod3, prm, big, layer, final_g, s_rwkv0, s_gla0, nseq, seq_len, cond_base, rows_per_cond,
               pos_tables=None):
    if pos_tables is None:
        u_p, u_f, u_r, u_g = _inproj(x, mod3, prm["n1"], big["w_mix_t"], layer, cond_base, rows_per_cond)
    else:
        x, u_p, u_f, u_r, u_g = _inproj(x, mod3, prm["n1"], big["w_mix_t"], layer, cond_base, rows_per_cond,
                                        pos_tables, seq_len)
    y_a = _pool(u_p, prm["pool_w"], prm["pool_scale"], nseq, seq_len)
    if seq_len == FFT_N1 * FFT_N1:
        y_b = _four_large(u_f, nseq, seq_len)
    else:
        y_b = _four_small(u_f, nseq, seq_len)
    o_f, ex_f, sr_f = _rwkv_dir(u_r, prm["rwkv0"], s_rwkv0[0], nseq, seq_len, False)
    o_b, ex_b, sr_b = _rwkv_dir(u_r, prm["rwkv1"], s_rwkv0[1], nseq, seq_len, True)
    g_f, sg_f = _gla_dir(u_g, prm["gla_ab0"], prm["gla_abias0"], s_gla0[0], nseq, seq_len, False)
    g_b, sg_b = _gla_dir(u_g, prm["gla_ab1"], prm["gla_abias1"], s_gla0[1], nseq, seq_len, True)
    x1 = _merge(x, mod3, prm["n1"], y_a, y_b, o_f, o_b, ex_f, ex_b, u_r, g_f, g_b, u_g, prm, big, layer,
                cond_base, rows_per_cond)
    x2, y = _mlp(x1, mod3, prm["n2"], big["w1"], big["w2"], layer, final_g, cond_base, rows_per_cond)
    return x2, y, jnp.stack([sr_f, sr_b], axis=1), jnp.stack([sg_f, sg_b], axis=1)


def kernel(x_prompt, x_sample, state_rwkv, state_gla, c, c_ctx, ada_w, ada_b, norm1_g, norm2_g, w_in, pool_w, pool_scale, rwkv_mu, rwkv_w0, rwkv_bw, rwkv_a0, rwkv_ba, rwkv_kk, rwkv_ka, rwkv_bg, rwkv_rk, rwkv_gn, gla_ab, gla_abias, gla_norm, w_branch, w_out, mlp_w1, mlp_w2, final_g):
    p = dict(pool_w=pool_w, pool_scale=pool_scale, rwkv_mu=rwkv_mu, rwkv_w0=rwkv_w0,
             rwkv_bw=rwkv_bw, rwkv_a0=rwkv_a0, rwkv_ba=rwkv_ba, rwkv_kk=rwkv_kk, rwkv_ka=rwkv_ka,
             rwkv_bg=rwkv_bg, rwkv_rk=rwkv_rk, rwkv_gn=rwkv_gn, gla_ab=gla_ab, gla_abias=gla_abias,
             gla_norm=gla_norm, norm1_g=norm1_g, norm2_g=norm2_g)
    w_in_t = jnp.swapaxes(w_in, 1, 2)
    big = dict(w_mix_t=w_in_t[:, :P_MIX_PAD].astype(BF16), w_gate_t=w_in_t[:, P_MIX:].astype(BF16),
               w_branch=w_branch.astype(BF16), w_out=w_out.astype(BF16),
               w1=mlp_w1.astype(BF16), w2=mlp_w2.astype(BF16))
    bp, lp, _ = x_prompt.shape
    bs, ls, _ = x_sample.shape
    cond8 = jnp.zeros((8, D_MODEL), F32).at[0].set(c_ctx).at[1:1 + bs].set(c)
    mods = _ada(cond8, ada_w, ada_b)
    fg = final_g.reshape(1, D_MODEL)

    xp = x_prompt.reshape(bp * lp, D_MODEL)
    xs = x_sample.reshape(bs * ls, D_MODEL)
    pos_tables = _pos_tables(ls, D_MODEL)
    zr = jnp.zeros((2, bp) + state_rwkv.shape[3:], F32)
    zg = jnp.zeros((2, bp) + state_gla.shape[3:], F32)
    new_r, new_g = [], []
    for l in range(DEPTH):
        prm = _layer_params(l, p)
        mod3 = mods[l].reshape(8, 1, 6 * D_MODEL)
        xp, yp, s_r, s_g = _run_layer(xp, mod3, prm, big, l, fg, zr, zg, bp, lp, 0, bp * lp)
        new_r.append(s_r)
        new_g.append(s_g)
        xs, ys, _, _ = _run_layer(xs, mod3, prm, big, l, fg, jnp.swapaxes(state_rwkv[:, l], 0, 1),
                                  jnp.swapaxes(state_gla[:, l], 0, 1), bs, ls, 1, ls,
                                  pos_tables if l == 0 else None)
    y_prompt = yp.reshape(bp, lp, D_MODEL)
    y_sample = ys.reshape(bs, ls, D_MODEL)
    return (y_prompt, y_sample, jnp.stack(new_r, axis=1), jnp.stack(new_g, axis=1))
```

```python
import functools

import numpy as np
import jax
import jax.numpy as jnp
from jax import lax
from jax.experimental import pallas as pl
from jax.experimental.pallas import tpu as pltpu

F32 = jnp.float32
BF16 = jnp.bfloat16

D_MODEL = 1024
DEPTH = 2
GRID_W = 64
POS_BASE = 10000.0
MIX_W = 256
POOL_WINDOWS = (2, 4, 8, 16)
POOL_HALO = 8
RWKV_DECAY_SCALE = 0.606531
RWKV_GN_EPS = 64e-5
GLA_DK = 32
GLA_GATE_NORM = 16.0
D_FF = 4 * D_MODEL
EPS = 1e-6
P_MIX = 2336
P_MIX_PAD = 2432
CHUNK = 64
SUB = 16

TM = 512
TB = 256
SCAN_TB = 512
FFT_N1 = 64
VMEM_LIMIT = 56 * 1024 * 1024


def _cparams(sem):
    return pltpu.CompilerParams(dimension_semantics=sem, vmem_limit_bytes=VMEM_LIMIT)


def _full(shape):
    n = len(shape)
    return pl.BlockSpec(shape, lambda *_: (0,) * n)


def _dot(a, b):
    return jnp.dot(a.astype(BF16), b.astype(BF16), preferred_element_type=F32)


def _dot_nt(a, b):
    return lax.dot_general(a.astype(BF16), b.astype(BF16), (((1,), (1,)), ((), ())),
                           preferred_element_type=F32)


def _split2(a):
    hi = a.astype(BF16)
    lo = (a - hi.astype(F32)).astype(BF16)
    return hi, lo


def _dot_exact_rhs(a, w):
    hi, lo = _split2(a)
    return (jnp.dot(hi, w, preferred_element_type=F32) + jnp.dot(lo, w, preferred_element_type=F32))


def _dot_exact_lhs(w, a):
    hi, lo = _split2(a)
    return jnp.dot(w, hi, preferred_element_type=F32) + jnp.dot(w, lo, preferred_element_type=F32)


def _seg_ones(n, seg, scale=1.0):
    r = lax.broadcasted_iota(jnp.int32, (n, n), 0)
    c = lax.broadcasted_iota(jnp.int32, (n, n), 1)
    return jnp.where((r // seg) == (c // seg), scale, 0.0).astype(BF16)


def _log_sigmoid(x):
    return jnp.minimum(x, 0.0) - jnp.log(1.0 + jnp.exp(-jnp.abs(x)))


def _rms_mod(x, g, sc, sh):
    ms = jnp.mean(x * x, axis=-1, keepdims=True)
    return x * lax.rsqrt(ms + EPS) * g * (1.0 + sc) + sh


def _ada_kernel(c_ref, w_ref, b_ref, o_ref):
    c = c_ref[...]
    s = c * jax.nn.sigmoid(c)
    hi, lo = _split2(s)
    w = w_ref[0].astype(BF16)
    o_ref[0] = (jnp.dot(hi, w, preferred_element_type=F32) + jnp.dot(lo, w, preferred_element_type=F32)
                + b_ref[0])


def _ada(cond8, ada_w, ada_b):
    tn = 1536
    n = 6 * D_MODEL
    return pl.pallas_call(
        _ada_kernel,
        grid=(DEPTH, n // tn),
        in_specs=[_full((8, D_MODEL)),
                  pl.BlockSpec((1, D_MODEL, tn), lambda l, j: (l, 0, j)),
                  pl.BlockSpec((1, 1, tn), lambda l, j: (l, 0, j))],
        out_specs=pl.BlockSpec((1, 8, tn), lambda l, j: (l, 0, j)),
        out_shape=jax.ShapeDtypeStruct((DEPTH, 8, n), F32),
        compiler_params=_cparams(("parallel", "parallel")),
        name="ada_mod",
    )(cond8, ada_w, ada_b.reshape(DEPTH, 1, n))


def _inproj_kernel(x_ref, mod_ref, g_ref, w_ref, up_ref, uf_ref, ur_ref, ug_ref):
    _inproj_body(x_ref[...], mod_ref, g_ref, w_ref, up_ref, uf_ref, ur_ref, ug_ref)


def _inproj_pos_kernel(x_ref, rr_ref, cc_ref, mod_ref, g_ref, w_ref, xo_ref, up_ref, uf_ref, ur_ref, ug_ref,
                       *, seq_len):
    nrow = TM // GRID_W
    g0 = pl.multiple_of(((pl.program_id(0) * TM) % seq_len) // GRID_W, nrow)
    half = D_MODEL // 2
    rpart = jnp.broadcast_to(rr_ref[pl.ds(g0, nrow), :][:, None, :], (nrow, GRID_W, half)).reshape(TM, half)
    cpart = jnp.concatenate([cc_ref[...]] * nrow, axis=0)
    x = x_ref[...] + jnp.concatenate([rpart, cpart], axis=1)
    xo_ref[...] = x
    _inproj_body(x, mod_ref, g_ref, w_ref, up_ref, uf_ref, ur_ref, ug_ref)


def _inproj_body(x, mod_ref, g_ref, w_ref, up_ref, uf_ref, ur_ref, ug_ref):
    mod = mod_ref[0]
    h = _rms_mod(x, g_ref[...], mod[:, D_MODEL:2 * D_MODEL], mod[:, 0:D_MODEL]).astype(BF16)
    up_ref[...] = _dot_nt(h, w_ref[0:256, :])
    uf_ref[...] = _dot_nt(h, w_ref[256:512, :])
    ur_ref[...] = _dot_nt(h, w_ref[512:1536, :])
    ug_ref[...] = _dot_nt(h, w_ref[1536:P_MIX_PAD, :])


def _cond_map(cond_base, rows_per_cond):
    return lambda i: (cond_base + (i * TM) // rows_per_cond, 0, 0)


def _inproj(x, mod3, g, w_t, layer, cond_base, rows_per_cond, pos_tables=None, seq_len=None):
    t = x.shape[0]
    row = lambda i: (i, 0)
    in_specs = [pl.BlockSpec((1, 1, 6 * D_MODEL), _cond_map(cond_base, rows_per_cond)),
                _full((1, D_MODEL)),
                pl.BlockSpec((None, P_MIX_PAD, D_MODEL), lambda i: (layer, 0, 0))]
    out_specs = [pl.BlockSpec((TM, 256), row), pl.BlockSpec((TM, 256), row),
                 pl.BlockSpec((TM, 1024), row), pl.BlockSpec((TM, 896), row)]
    out_shape = [jax.ShapeDtypeStruct((t, 256), F32), jax.ShapeDtypeStruct((t, 256), F32),
                 jax.ShapeDtypeStruct((t, 1024), F32), jax.ShapeDtypeStruct((t, 896), F32)]
    x_spec = pl.BlockSpec((TM, D_MODEL), row)
    if pos_tables is None:
        return pl.pallas_call(
            _inproj_kernel, grid=(t // TM,), in_specs=[x_spec] + in_specs, out_specs=out_specs,
            out_shape=out_shape, compiler_params=_cparams(("parallel",)), name="in_proj",
        )(x, mod3, g, w_t)
    tab = _full((GRID_W, D_MODEL // 2))
    return pl.pallas_call(
        functools.partial(_inproj_pos_kernel, seq_len=seq_len), grid=(t // TM,),
        in_specs=[x_spec, tab, tab] + in_specs, out_specs=[x_spec] + out_specs,
        out_shape=[jax.ShapeDtypeStruct((t, D_MODEL), F32)] + out_shape,
        compiler_params=_cparams(("parallel",)), name="in_proj_pos",
    )(x, pos_tables[0], pos_tables[1], mod3, g, w_t)


def _pool_kernel(prev_ref, cur_ref, next_ref, wbd_ref, scale_ref, y_ref, *, seq_len):
    j = pl.program_id(1)
    nb = seq_len // TB
    cur = cur_ref[...]
    prev = jnp.where(j > 0, prev_ref[...], 0.0)
    nxt = jnp.where(j < nb - 1, next_ref[...], 0.0)
    ext = jnp.concatenate([prev, cur, nxt], axis=0)
    n_ext = TB + 2 * POOL_HALO

    def shift(x, k):
        return pltpu.roll(x, k % n_ext, 0)

    sums = [ext + shift(ext, 1)]
    for k in (1, 2, 4):
        sums.append(shift(sums[-1], -k) + shift(sums[-1], k))
    assert POOL_WINDOWS == (2, 4, 8, 16)
    pos = j * TB + lax.broadcasted_iota(jnp.int32, (TB, MIX_W), 0)
    group = lax.broadcasted_iota(jnp.int32, (TB, MIX_W), 1) // (MIX_W // len(POOL_WINDOWS))
    pooled = jnp.zeros((TB, MIX_W), F32)
    for gi, win in enumerate(POOL_WINDOWS):
        before, after = win // 2, win - win // 2 - 1
        s = sums[gi][POOL_HALO:POOL_HALO + TB]
        cnt = (jnp.minimum(pos + after, seq_len - 1) - jnp.maximum(pos - before, 0) + 1).astype(F32)
        pooled = jnp.where(group == gi, s / cnt - cur, pooled)
    y_ref[...] = _dot(pooled, wbd_ref[...]) * scale_ref[...]


def _pool(u_p, wbd, scale, nseq, seq_len):
    nb = seq_len // TB
    hb = TB // POOL_HALO
    nh = u_p.shape[0] // POOL_HALO
    return pl.pallas_call(
        functools.partial(_pool_kernel, seq_len=seq_len),
        grid=(nseq, nb),
        in_specs=[pl.BlockSpec((POOL_HALO, MIX_W), lambda b, j: (jnp.maximum((b * nb + j) * hb - 1, 0), 0)),
                  pl.BlockSpec((TB, MIX_W), lambda b, j: (b * nb + j, 0)),
                  pl.BlockSpec((POOL_HALO, MIX_W), lambda b, j: (jnp.minimum((b * nb + j + 1) * hb, nh - 1), 0)),
                  _full((MIX_W, MIX_W)), _full((1, MIX_W))],
        out_specs=pl.BlockSpec((TB, MIX_W), lambda b, j: (b * nb + j, 0)),
        out_shape=jax.ShapeDtypeStruct(u_p.shape, F32),
        compiler_params=_cparams(("parallel", "parallel")),
        name="pool_mixer",
    )(u_p, u_p, u_p, wbd, scale)


def _dft_tables(n):
    t = np.arange(n)
    ang = 2.0 * np.pi * ((np.outer(t, t)) % n) / n
    return np.cos(ang), np.sin(ang)


def _bd_np(m, copies):
    k = m.shape[0]
    out = np.zeros((k * copies, k * copies), m.dtype)
    for i in range(copies):
        out[i * k:(i + 1) * k, i * k:(i + 1) * k] = m
    return out


def _four_small_kernel(z_ref, cs_ref, cbd_ref, sbd_ref, y_ref, *, seq_len, norm):
    zf = _dot(cs_ref[...], z_ref[...])
    y = _dot(zf[:seq_len], cbd_ref[...]) + _dot(zf[seq_len:], sbd_ref[...])
    y_ref[...] = y * norm


def _four_small(u_f, nseq, seq_len):
    c, s = _dft_tables(seq_len)
    cs = jnp.asarray(np.concatenate([c, -s], axis=0), F32)
    c64, s64 = _dft_tables(MIX_W // 4)
    cbd = jnp.asarray(_bd_np(c64, 4), F32)
    sbd = jnp.asarray(_bd_np(s64, 4), F32)
    return pl.pallas_call(
        functools.partial(_four_small_kernel, seq_len=seq_len, norm=float((seq_len * 64) ** -0.5)),
        grid=(nseq,),
        in_specs=[pl.BlockSpec((seq_len, MIX_W), lambda b: (b, 0)),
                  _full((2 * seq_len, seq_len)), _full((MIX_W, MIX_W)), _full((MIX_W, MIX_W))],
        out_specs=pl.BlockSpec((seq_len, MIX_W), lambda b: (b, 0)),
        out_shape=jax.ShapeDtypeStruct(u_f.shape, F32),
        compiler_params=_cparams(("parallel",)),
        name="fourier_small",
    )(u_f, cs, cbd, sbd)


def _four_large_kernel(z_ref, fs_ref, g_ref, cbd_ref, sbd_ref, y_ref, a_scr, y_scr, *, norm):
    n1 = FFT_N1
    fs = fs_ref[...].astype(BF16)
    cbd = cbd_ref[...].astype(BF16)
    sbd = sbd_ref[...].astype(BF16)
    for t2 in range(n1):
        a_scr[t2 * 2 * n1:(t2 + 1) * 2 * n1, :] = _dot(fs, z_ref[pl.ds(t2, n1, stride=n1), :])
    for f1 in range(n1):
        a = jnp.concatenate([a_scr[pl.ds(f1, n1, stride=2 * n1), :],
                             a_scr[pl.ds(n1 + f1, n1, stride=2 * n1), :]], axis=0)
        x = _dot(g_ref[f1], a)
        y_scr[f1 * n1:(f1 + 1) * n1, :] = (_dot(x[:n1], cbd) + _dot(x[n1:], sbd)) * norm
    for f2 in range(n1):
        y_ref[f2 * n1:(f2 + 1) * n1, :] = y_scr[pl.ds(f2, n1, stride=n1), :]


def _four_large(u_f, nseq, seq_len):
    n1 = FFT_N1
    assert seq_len == n1 * n1
    c64, s64 = _dft_tables(n1)
    fs = jnp.asarray(np.concatenate([c64, -s64], axis=0), F32)
    f1 = np.arange(n1)[:, None, None]
    f2 = np.arange(n1)[None, :, None]
    t2 = np.arange(n1)[None, None, :]
    ang = 2.0 * np.pi * ((t2 * (f1 + n1 * f2)) % seq_len) / seq_len
    gr, gi = np.cos(ang), -np.sin(ang)
    g = jnp.asarray(np.concatenate([np.concatenate([gr, -gi], axis=2),
                                    np.concatenate([gi, gr], axis=2)], axis=1), F32)
    lanes = 128
    cbd = jnp.asarray(_bd_np(c64, lanes // 64), F32)
    sbd = jnp.asarray(_bd_np(s64, lanes // 64), F32)
    seq = pl.BlockSpec((seq_len, lanes), lambda b, h: (b, h))
    return pl.pallas_call(
        functools.partial(_four_large_kernel, norm=float((seq_len * 64) ** -0.5)),
        grid=(nseq, MIX_W // lanes),
        in_specs=[seq, _full((2 * n1, n1)), _full((n1, 2 * n1, 2 * n1)),
                  _full((lanes, lanes)), _full((lanes, lanes))],
        out_specs=seq,
        out_shape=jax.ShapeDtypeStruct(u_f.shape, F32),
        scratch_shapes=[pltpu.VMEM((2 * seq_len, lanes), F32), pltpu.VMEM((seq_len, lanes), F32)],
        compiler_params=_cparams(("parallel", "parallel")),
        name="fourier_large",
    )(u_f, fs, g, cbd, sbd)


def _bmm(x, y):
    return lax.dot_general(x.astype(BF16), y.astype(BF16), (((2,), (1,)), ((0,), (0,))),
                           preferred_element_type=F32)


def _bmm_nt(x, y):
    return lax.dot_general(x.astype(BF16), y.astype(BF16), (((2,), (2,)), ((0,), (0,))),
                           preferred_element_type=F32)


def _bmm_tn(x, y):
    return lax.dot_general(x.astype(BF16), y.astype(BF16), (((1,), (1,)), ((0,), (0,))),
                           preferred_element_type=F32)


def _slab_bd(y):
    lane = lax.broadcasted_iota(jnp.int32, y.shape, 2)
    z = jnp.zeros_like(y)
    return jnp.concatenate([jnp.where(lane < 64, y, z), jnp.where(lane >= 64, y, z)], axis=1)


def _mm(x, y):
    return _bmm(x, _slab_bd(y.astype(BF16)))


def _mm_pair(x, y1, y2):
    w = jnp.concatenate([_slab_bd(y1.astype(BF16)), _slab_bd(y2.astype(BF16))], axis=2)
    out = _bmm(x, w)
    return out[:, :, :128], out[:, :, 128:]


def _mm3(xs, y):
    yh, yl = _split2(y)
    bh, bl = _slab_bd(yh), _slab_bd(yl)
    parts = [_split2(x) for x in xs]
    xh = jnp.concatenate([p[0] for p in parts], axis=1)
    xl = jnp.concatenate([p[1] for p in parts], axis=1)
    hx = _bmm(xh, jnp.concatenate([bh, bl], axis=2))
    out = hx[:, :, :128] + (hx[:, :, 128:] + _bmm(xl, bh))
    return [out[:, i * CHUNK:(i + 1) * CHUNK] for i in range(len(xs))]


def _to_slabs(x):
    nch = x.shape[0] // CHUNK
    x4 = x.reshape(nch, CHUNK, MIX_W)
    return jnp.stack([x4[:, :, :128], x4[:, :, 128:]], axis=1).reshape(2 * nch, CHUNK, 128)


def _tri_inverse(n_mat, eye, blk):
    nd = jnp.where(blk, n_mat, 0.0)
    no = n_mat - nd
    n2, = _mm3([nd], nd)
    n4, y = _mm3([n2, nd], n2)
    x = n2 - nd - y
    n8, y = _mm3([n4, x], n4)
    x = x + n4 + y
    y, = _mm3([x], n8)
    x = x + n8 + y
    td = eye + x
    e = _mm(td, no)
    e2 = _mm(e, e)
    e3 = _mm(e, e2)
    return td + _mm(e2 - e - e3, td)


def _rwkv_state_to_bd(s4):
    bd_r = lax.broadcasted_iota(jnp.int32, (128, 128), 0) // 64
    bd_c = lax.broadcasted_iota(jnp.int32, (128, 128), 1) // 64
    pairs = []
    for hp in range(2):
        sp = s4[2 * hp:2 * hp + 2].reshape(128, 64)
        pairs.append(jnp.where(bd_r == bd_c, jnp.concatenate([sp, sp], axis=1), 0.0))
    return jnp.stack(pairs, axis=0)


def _rwkv_state_store(sfin_ref, q, s_bd):
    for hp in range(2):
        sfin_ref[q, 2 * hp] = s_bd[hp, 0:64, 0:64]
        sfin_ref[q, 2 * hp + 1] = s_bd[hp, 64:128, 64:128]


def _rwkv_kernel(cur_ref, prev_ref, next_ref, mu_ref, kkw_ref, ka_ref, rk_ref, w0_ref, a0_ref,
                 bw_ref, ba_ref, s0_ref, o_ref, ex_ref, sfin_ref, s_scr, *, rev, nb, spb):
    j = pl.program_id(1)
    jj = (nb - 1 - j) if rev else j

    if spb == 1:
        @pl.when(j == 0)
        def _():
            s_scr[...] = _rwkv_state_to_bd(s0_ref[0])

    tb = cur_ref.shape[0]
    seq_rows = tb // spb
    z = cur_ref[:, 0:768]
    rowid = lax.broadcasted_iota(jnp.int32, (tb, 768), 0)
    prow = jnp.where(rowid == 0, jnp.where(jj > 0, prev_ref[POOL_HALO - 1:POOL_HALO, :], 0.0), 0.0)
    nrow = jnp.where(rowid == tb - 1, jnp.where(jj < nb - 1, next_ref[0:1, :], 0.0), 0.0)
    in_seq = rowid & (seq_rows - 1)
    up = jnp.where(in_seq == 0, prow, pltpu.roll(z, 1, 0))
    dn = jnp.where(in_seq == seq_rows - 1, nrow, pltpu.roll(z, tb - 1, 0))
    zm = z + mu_ref[...] * (0.5 * (up + dn) - z)
    r = zm[:, 0:256]
    k = zm[:, 256:512]
    v = zm[:, 512:768]
    ones_bd = _seg_ones(MIX_W, 64)
    kk = k * kkw_ref[...]
    kk = kk * lax.rsqrt(_dot(kk * kk, ones_bd) + EPS)
    w_log = w0_ref[...] + _dot(jnp.tanh(cur_ref[:, 768:896]), bw_ref[...])
    lw = -RWKV_DECAY_SCALE * jax.nn.sigmoid(w_log)
    a = jax.nn.sigmoid(a0_ref[...] + _dot(cur_ref[:, 896:1024], ba_ref[...]))
    kd = k * (1.0 + (a - 1.0) * ka_ref[...])
    ex_ref[...] = _dot(r * kd * rk_ref[...], ones_bd) * v

    nch = tb // CHUNK
    tr = lax.broadcasted_iota(jnp.int32, (tb, tb), 0)
    tc = lax.broadcasted_iota(jnp.int32, (tb, tb), 1)
    same_chunk = (tr // CHUNK) == (tc // CHUNK)
    tri = jnp.where(same_chunk & ((tc >= tr) if rev else (tc <= tr)), 1.0, 0.0).astype(BF16)
    cum = _dot_exact_lhs(tri, lw)
    e_in = jnp.exp(cum)
    e_out = jnp.exp(-cum)
    last = 0 if rev else CHUNK - 1
    rt = _to_slabs(r * e_in)
    kh = _to_slabs(kd * e_out)
    bh = _to_slabs(kk * a * e_out)
    kt = _to_slabs(kk * jnp.exp(cum - lw))
    vv = _to_slabs(v)
    gam = _to_slabs(e_in)[:, last:last + 1, :]

    t_i = lax.broadcasted_iota(jnp.int32, (1, CHUNK, 128), 1)
    s_i = lax.broadcasted_iota(jnp.int32, (1, CHUNK, 128), 2) & (CHUNK - 1)
    if rev:
        strict, incl = s_i > t_i, s_i >= t_i
    else:
        strict, incl = s_i < t_i, s_i <= t_i
    blk = (s_i // SUB) == (t_i // SUB)
    eye = jnp.where(s_i == t_i, 1.0, 0.0)

    lhs = jnp.concatenate([kt, rt], axis=1)
    kb_bd = jnp.concatenate([_slab_bd(kh.astype(BF16)), _slab_bd(bh.astype(BF16))], axis=1)
    mkb = _bmm_nt(lhs, kb_bd)
    m_mat = jnp.where(strict, mkb[:, :CHUNK, :128], 0.0)
    p_mat = jnp.where(incl, mkb[:, CHUNK:, :128], 0.0)
    n_mat = jnp.where(strict, mkb[:, :CHUNK, 128:], 0.0)
    q_mat = jnp.where(incl, mkb[:, CHUNK:, 128:], 0.0)
    t_mat = _tri_inverse(n_mat, eye, blk)
    mpv = _mm(jnp.concatenate([m_mat, p_mat], axis=1), vv)
    mv, pv = mpv[:, :CHUNK], mpv[:, CHUNK:]
    tk, uv = _mm_pair(t_mat, kt, mv)
    qtk, quv = _mm_pair(q_mat, tk, uv)
    rc = rt - qtk
    oc = pv - quv
    bd_r = lax.broadcasted_iota(jnp.int32, (1, 128, 128), 1) // 64
    bd_c = lax.broadcasted_iota(jnp.int32, (1, 128, 128), 2) // 64
    bd_mask = bd_r == bd_c
    d_mat = jnp.where(bd_mask, _bmm_tn(tk, bh), 0.0)
    braw = jnp.where(bd_mask,
                     _bmm_tn(jnp.concatenate([vv, uv], axis=1), jnp.concatenate([kh, -bh], axis=1)), 0.0)

    cps = nch // spb
    s_cur = s_scr[...] if spb == 1 else None
    for ci in range(nch):
        c = (nch - 1 - ci) if rev else ci
        q, ci_seq = divmod(ci, cps)
        q = (spb - 1 - q) if rev else q
        if spb > 1 and ci_seq == 0:
            s_cur = _rwkv_state_to_bd(s0_ref[q])
        g = slice(2 * c, 2 * c + 2)
        o = _bmm_nt(rc[g], s_cur) + oc[g]
        o_ref[c * CHUNK:(c + 1) * CHUNK, 0:128] = o[0]
        o_ref[c * CHUNK:(c + 1) * CHUNK, 128:256] = o[1]
        s_cur = (s_cur - _bmm(s_cur, d_mat[g]) + braw[g]) * gam[g]
        if spb > 1 and ci_seq == cps - 1:
            _rwkv_state_store(sfin_ref, q, s_cur)
    if spb == 1:
        s_scr[...] = s_cur

        @pl.when(j == nb - 1)
        def _():
            _rwkv_state_store(sfin_ref, 0, s_scr[...])


def _scan_blocking(nseq, seq_len):
    spb = max(1, min(nseq, SCAN_TB // seq_len))
    assert nseq % spb == 0
    tb = min(seq_len, SCAN_TB) * spb
    return spb, tb, (seq_len * spb) // tb


def _rwkv_dir(u_r, prm, s0, nseq, seq_len, rev):
    spb, tb, nb = _scan_blocking(nseq, seq_len)
    hb = tb // POOL_HALO
    nh = u_r.shape[0] // POOL_HALO
    t = u_r.shape[0]

    def jj(j):
        return (nb - 1 - j) if rev else j

    cur_map = lambda b, j: (b * nb + jj(j), 0)
    st_map = lambda b, j: (b, 0, 0, 0)
    vec = _full((1, MIX_W))
    return pl.pallas_call(
        functools.partial(_rwkv_kernel, rev=rev, nb=nb, spb=spb),
        grid=(nseq // spb, nb),
        in_specs=[pl.BlockSpec((tb, 1024), cur_map),
                  pl.BlockSpec((POOL_HALO, 768), lambda b, j: (jnp.maximum((b * nb + jj(j)) * hb - 1, 0), 0)),
                  pl.BlockSpec((POOL_HALO, 768), lambda b, j: (jnp.minimum((b * nb + jj(j) + 1) * hb, nh - 1), 0)),
                  _full((1, 768)), vec, vec, vec, vec, vec,
                  _full((128, MIX_W)), _full((128, MIX_W)),
                  pl.BlockSpec((spb, 4, 64, 64), st_map)],
        out_specs=[pl.BlockSpec((tb, MIX_W), cur_map), pl.BlockSpec((tb, MIX_W), cur_map),
                   pl.BlockSpec((spb, 4, 64, 64), st_map)],
        out_shape=[jax.ShapeDtypeStruct((t, MIX_W), F32), jax.ShapeDtypeStruct((t, MIX_W), F32),
                   jax.ShapeDtypeStruct((nseq, 4, 64, 64), F32)],
        scratch_shapes=[pltpu.VMEM((2, 128, 128), F32)],
        compiler_params=_cparams(("parallel", "arbitrary")),
        name="rwkv_bwd" if rev else "rwkv_fwd",
    )(u_r, u_r, u_r, prm["mu"], prm["kk"], prm["ka"], prm["rk"], prm["w0"], prm["a0"],
      prm["bw"], prm["ba"], s0)


def _rows_bd4(y, width):
    lane = lax.broadcasted_iota(jnp.int32, y.shape, 2) // width
    z = jnp.zeros_like(y)
    return jnp.concatenate([jnp.where(lane == h, y, z) for h in range(4)], axis=1)


def _block_scans(x, levels):
    rows, width = x.shape
    row = lax.broadcasted_iota(jnp.int32, (rows, width), 0)
    pre, suf = [x], [x]
    for m in range(1, levels + 1):
        size, half = 1 << m, 1 << (m - 1)
        p, s = pre[-1], suf[-1]
        if half < 8:
            off = row & (size - 1)
            p3 = p.reshape(rows // 8, 8, width)
            s3 = s.reshape(rows // 8, 8, width)
            addp = jnp.zeros_like(x)
            adds = jnp.zeros_like(x)
            for o in range(half):
                addp = jnp.where(off == half + o, pltpu.roll(p3, o + 1, 1).reshape(rows, width), addp)
                adds = jnp.where(off == o, pltpu.roll(s3, 8 - (half - o), 1).reshape(rows, width), adds)
        else:
            p4 = p.reshape(rows // size, size, width)
            s4 = s.reshape(rows // size, size, width)
            second = (row & half) != 0
            addp = jnp.where(second, jnp.broadcast_to(p4[:, half - 1:half, :], p4.shape).reshape(rows, width), 0.0)
            adds = jnp.where(second, 0.0, jnp.broadcast_to(s4[:, half:half + 1, :], s4.shape).reshape(rows, width))
        pre.append(p + addp)
        suf.append(s + adds)
    return pre, suf


def _gla_kernel(u_ref, ab_ref, abias_ref, s0_ref, o_ref, sfin_ref, s_scr, *, rev, nb, spb):
    j = pl.program_id(1)

    def state_in(s4):
        sp = s4.reshape(4 * GLA_DK, 64)
        st = jnp.concatenate([sp, sp], axis=1).T[0:64, :]
        lane_h = lax.broadcasted_iota(jnp.int32, (64, 128), 1) // GLA_DK
        return jnp.concatenate([jnp.where(lane_h == h, st, 0.0) for h in range(4)], axis=0)

    def state_out(q, s_t):
        m = s_t[0:64] + s_t[64:128] + s_t[128:192] + s_t[192:256]
        nat = jnp.concatenate([m, jnp.zeros_like(m)], axis=0).T
        sfin_ref[q] = nat[:, 0:64].reshape(4, GLA_DK, 64)

    if spb == 1:
        @pl.when(j == 0)
        def _():
            s_scr[...] = state_in(s0_ref[0])

    nch = u_ref.shape[0] // CHUNK
    la = _log_sigmoid(_dot(u_ref[:, 768:896], ab_ref[...]) + abias_ref[...]) * (1.0 / GLA_GATE_NORM)
    q = u_ref[:, 0:128] * float(GLA_DK ** -0.5)
    kg = u_ref[:, 128:256]
    vg3 = u_ref[:, 256:512].reshape(nch, CHUNK, MIX_W)

    pre, suf = _block_scans(la, 6)
    if rev:
        near, far, b = suf, pre, suf[6]
    else:
        near, far, b = pre, suf, pre[6]
    t_i = lax.broadcasted_iota(jnp.int32, (1, CHUNK, MIX_W), 1)
    s_i = lax.broadcasted_iota(jnp.int32, (1, CHUNK, MIX_W), 2) & (CHUNK - 1)

    def to3(x):
        return x.reshape(nch, CHUNK, 128)

    att = jnp.where(s_i == t_i, _bmm_nt(to3(q), _rows_bd4(to3(kg).astype(BF16), GLA_DK)), 0.0)
    for lev in range(1, 7):
        size, half = 1 << lev, 1 << (lev - 1)
        same = (t_i // size) == (s_i // size)
        if rev:
            mask = same & ((t_i & (size - 1)) < half) & ((s_i & (size - 1)) >= half)
        else:
            mask = same & ((t_i & (size - 1)) >= half) & ((s_i & (size - 1)) < half)
        qe = q * jnp.exp(near[lev - 1])
        ke = kg * jnp.exp(far[lev - 1] - la)
        att = att + jnp.where(mask, _bmm_nt(to3(qe), _rows_bd4(to3(ke).astype(BF16), GLA_DK)), 0.0)
    o_intra = _bmm(att, _rows_bd4(vg3.astype(BF16), 64))

    last = 0 if rev else CHUNK - 1
    b3 = to3(b)
    b_last = b3[:, last:last + 1, :]
    qb = to3(q) * jnp.exp(b3)
    bd_r = lax.broadcasted_iota(jnp.int32, (1, MIX_W, 128), 1) // 64
    bd_c = lax.broadcasted_iota(jnp.int32, (1, MIX_W, 128), 2) // GLA_DK
    inc = jnp.where(bd_r == bd_c, _bmm_tn(vg3, to3(kg) * jnp.exp(b_last - b3)), 0.0)
    decay = jnp.exp(b_last)

    cps = nch // spb
    s_cur = s_scr[...] if spb == 1 else None
    for ci in range(nch):
        c = (nch - 1 - ci) if rev else ci
        qs, ci_seq = divmod(ci, cps)
        qs = (spb - 1 - qs) if rev else qs
        if spb > 1 and ci_seq == 0:
            s_cur = state_in(s0_ref[qs])
        o_ref[c * CHUNK:(c + 1) * CHUNK, :] = o_intra[c] + _dot_nt(qb[c], s_cur)
        s_cur = s_cur * decay[c] + inc[c]
        if spb > 1 and ci_seq == cps - 1:
            state_out(qs, s_cur)
    if spb == 1:
        s_scr[...] = s_cur

        @pl.when(j == nb - 1)
        def _():
            state_out(0, s_scr[...])


def _gla_dir(u_g, ab, abias, s0, nseq, seq_len, rev):
    spb, tb, nb = _scan_blocking(nseq, seq_len)
    t = u_g.shape[0]
    cur_map = (lambda b, j: (b * nb + nb - 1 - j, 0)) if rev else (lambda b, j: (b * nb + j, 0))
    st_map = lambda b, j: (b, 0, 0, 0)
    return pl.pallas_call(
        functools.partial(_gla_kernel, rev=rev, nb=nb, spb=spb),
        grid=(nseq // spb, nb),
        in_specs=[pl.BlockSpec((tb, 896), cur_map), _full((128, 128)), _full((1, 128)),
                  pl.BlockSpec((spb, 4, GLA_DK, 64), st_map)],
        out_specs=[pl.BlockSpec((tb, MIX_W), cur_map), pl.BlockSpec((spb, 4, GLA_DK, 64), st_map)],
        out_shape=[jax.ShapeDtypeStruct((t, MIX_W), F32), jax.ShapeDtypeStruct((nseq, 4, GLA_DK, 64), F32)],
        scratch_shapes=[pltpu.VMEM((256, 128), F32)],
        compiler_params=_cparams(("parallel", "arbitrary")),
        name="gla_bwd" if rev else "gla_fwd",
    )(u_g, ab, abias, s0)


def _merge_kernel(x_ref, mod_ref, g_ref, ya_ref, yb_ref, of_ref, ob_ref, exf_ref, exb_ref, cag_ref,
                  gf_ref, gb_ref, gout_ref, gn_ref, bg_ref, gnorm_ref, wg_ref, wb_ref, wo_ref, x1_ref):
    mod = mod_ref[0]
    x = x_ref[...]
    h = _rms_mod(x, g_ref[...], mod[:, D_MODEL:2 * D_MODEL], mod[:, 0:D_MODEL]).astype(BF16)
    mean64 = _seg_ones(MIX_W, 64, 1.0 / 64.0)
    o = of_ref[...] + ob_ref[...]
    mu = _dot_exact_rhs(o, mean64)
    oc = o - mu
    var = _dot_exact_rhs(oc * oc, mean64)
    gate_c = _dot(jax.nn.sigmoid(cag_ref[...]), bg_ref[...])
    y_c = (oc * lax.rsqrt(var + RWKV_GN_EPS) * gn_ref[...] + exf_ref[...] + exb_ref[...]) * gate_c
    og = gf_ref[...] + gb_ref[...]
    gout = gout_ref[...]
    y_d = (og * lax.rsqrt(_dot_exact_rhs(og * og, mean64) + EPS) * gnorm_ref[...]
           * (gout * jax.nn.sigmoid(gout)))
    ys = (ya_ref[...], yb_ref[...], y_c, y_d)
    merged = jnp.zeros((TM, D_MODEL), F32)
    for i in range(4):
        g0 = P_MIX + i * D_MODEL
        gate = jax.nn.sigmoid(_dot_nt(h, wg_ref[g0:g0 + D_MODEL, :]))
        merged = merged + gate * _dot(ys[i], wb_ref[i])
    out = _dot(merged, wo_ref[...])
    x1_ref[...] = x + mod[:, 2 * D_MODEL:3 * D_MODEL] * out


def _merge(x, mod3, g, y_a, y_b, o_f, o_b, ex_f, ex_b, u_r, g_f, g_b, u_g, prm, big, layer,
           cond_base, rows_per_cond):
    t = x.shape[0]
    row = lambda i: (i, 0)
    mix = pl.BlockSpec((TM, MIX_W), row)
    vec = _full((1, MIX_W))
    lay3 = lambda i: (layer, 0, 0)
    return pl.pallas_call(
        _merge_kernel,
        grid=(t // TM,),
        in_specs=[pl.BlockSpec((TM, D_MODEL), row),
                  pl.BlockSpec((1, 1, 6 * D_MODEL), _cond_map(cond_base, rows_per_cond)),
                  _full((1, D_MODEL)),
                  mix, mix, mix, mix, mix, mix,
                  pl.BlockSpec((TM, 128), lambda i: (i, 7)),
                  mix, mix,
                  pl.BlockSpec((TM, MIX_W), lambda i: (i, 2)),
                  vec, _full((128, MIX_W)), vec,
                  pl.BlockSpec((None, big["w_in_t"].shape[1], D_MODEL), lay3, pipeline_mode=pl.Buffered(1)),
                  pl.BlockSpec((None, 4, MIX_W, D_MODEL), lambda i: (layer, 0, 0, 0)),
                  pl.BlockSpec((None, D_MODEL, D_MODEL), lay3)],
        out_specs=pl.BlockSpec((TM, D_MODEL), row),
        out_shape=jax.ShapeDtypeStruct((t, D_MODEL), F32),
        compiler_params=_cparams(("parallel",)),
        name="merge_out",
    )(x, mod3, g, y_a, y_b, o_f, o_b, ex_f, ex_b, u_r, g_f, g_b, u_g,
      prm["gn"], prm["bg"], prm["gla_norm"], big["w_in_t"], big["w_branch"], big["w_out"])


def _mlp_kernel(x_ref, mod_ref, g_ref, w1_ref, w2_ref, fg_ref, x2_ref, y_ref):
    mod = mod_ref[0]
    x = x_ref[...]
    h = _rms_mod(x, g_ref[...], mod[:, 4 * D_MODEL:5 * D_MODEL], mod[:, 3 * D_MODEL:4 * D_MODEL]).astype(BF16)
    ff = jnp.zeros((TM, D_MODEL), F32)
    for c in range(D_FF // D_MODEL):
        cols = slice(c * D_MODEL, (c + 1) * D_MODEL)
        a = jnp.maximum(jnp.dot(h, w1_ref[:, cols], preferred_element_type=F32), 0.0)
        ff = ff + _dot(a * a, w2_ref[cols, :])
    x2 = x + mod[:, 5 * D_MODEL:6 * D_MODEL] * ff
    x2_ref[...] = x2
    y_ref[...] = x2 * lax.rsqrt(jnp.mean(x2 * x2, axis=-1, keepdims=True) + EPS) * fg_ref[...]


def _mlp(x1, mod3, g, w1, w2, layer, final_g, cond_base, rows_per_cond):
    t = x1.shape[0]
    row = lambda i: (i, 0)
    lay3 = lambda i: (layer, 0, 0)
    return pl.pallas_call(
        _mlp_kernel,
        grid=(t // TM,),
        in_specs=[pl.BlockSpec((TM, D_MODEL), row),
                  pl.BlockSpec((1, 1, 6 * D_MODEL), _cond_map(cond_base, rows_per_cond)),
                  _full((1, D_MODEL)), pl.BlockSpec((None, D_MODEL, D_FF), lay3),
                  pl.BlockSpec((None, D_FF, D_MODEL), lay3), _full((1, D_MODEL))],
        out_specs=[pl.BlockSpec((TM, D_MODEL), row), pl.BlockSpec((TM, D_MODEL), row)],
        out_shape=[jax.ShapeDtypeStruct((t, D_MODEL), F32), jax.ShapeDtypeStruct((t, D_MODEL), F32)],
        compiler_params=_cparams(("parallel",)),
        name="mlp",
    )(x1, mod3, g, w1, w2, final_g)


def _pos_tables(n_tok, d):
    rows = n_tok // GRID_W
    assert rows == GRID_W
    quarter = d // 4
    omega = 1.0 / (POS_BASE ** (jnp.arange(quarter, dtype=F32) / quarter))
    ar = jnp.arange(rows, dtype=F32)[:, None] * omega
    ac = jnp.arange(GRID_W, dtype=F32)[:, None] * omega
    return (jnp.concatenate([jnp.sin(ar), jnp.cos(ar)], axis=-1),
            jnp.concatenate([jnp.sin(ac), jnp.cos(ac)], axis=-1))


def _pad_rows(m, start, total):
    return jnp.zeros((total, m.shape[1]), m.dtype).at[start:start + m.shape[0]].set(m)


def _layer_params(l, p):
    out = {
        "n1": p["norm1_g"][l].reshape(1, D_MODEL),
        "n2": p["norm2_g"][l].reshape(1, D_MODEL),
        "pool_w": jnp.einsum("gcd,gh->gchd", p["pool_w"][l], jnp.eye(4, dtype=F32)).reshape(MIX_W, MIX_W).astype(BF16),
        "pool_scale": p["pool_scale"][l].reshape(1, MIX_W),
        "gn": p["rwkv_gn"][l].reshape(1, MIX_W),
        "bg": _pad_rows(p["rwkv_bg"][l], 64, 128).astype(BF16),
        "gla_norm": p["gla_norm"][l].reshape(1, MIX_W),
    }
    for d in range(2):
        out["rwkv%d" % d] = {
            "mu": p["rwkv_mu"][l].reshape(1, 768),
            "kk": p["rwkv_kk"][l].reshape(1, MIX_W),
            "ka": p["rwkv_ka"][l].reshape(1, MIX_W),
            "rk": p["rwkv_rk"][l].reshape(1, MIX_W),
            "w0": p["rwkv_w0"][l, d].reshape(1, MIX_W),
            "a0": p["rwkv_a0"][l, d].reshape(1, MIX_W),
            "bw": _pad_rows(p["rwkv_bw"][l, d], 64 * d, 128).astype(BF16),
            "ba": _pad_rows(p["rwkv_ba"][l, d], 32 * d, 128).astype(BF16),
        }
        out["gla_ab%d" % d] = _pad_rows(p["gla_ab"][l, d], 16 * d, 128).astype(BF16)
        out["gla_abias%d" % d] = p["gla_abias"][l, d].reshape(1, 128)
    return out


def _run_layer(x, mod3, prm, big, layer, final_g, s_rwkv0, s_gla0, nseq, seq_len, cond_base, rows_per_cond,
               pos_tables=None):
    if pos_tables is None:
        u_p, u_f, u_r, u_g = _inproj(x, mod3, prm["n1"], big["w_in_t"], layer, cond_base, rows_per_cond)
    else:
        x, u_p, u_f, u_r, u_g = _inproj(x, mod3, prm["n1"], big["w_in_t"], layer, cond_base, rows_per_cond,
                                        pos_tables, seq_len)
    y_a = _pool(u_p, prm["pool_w"], prm["pool_scale"], nseq, seq_len)
    if seq_len == FFT_N1 * FFT_N1:
        y_b = _four_large(u_f, nseq, seq_len)
    else:
        y_b = _four_small(u_f, nseq, seq_len)
    o_f, ex_f, sr_f = _rwkv_dir(u_r, prm["rwkv0"], s_rwkv0[0], nseq, seq_len, False)
    o_b, ex_b, sr_b = _rwkv_dir(u_r, prm["rwkv1"], s_rwkv0[1], nseq, seq_len, True)
    g_f, sg_f = _gla_dir(u_g, prm["gla_ab0"], prm["gla_abias0"], s_gla0[0], nseq, seq_len, False)
    g_b, sg_b = _gla_dir(u_g, prm["gla_ab1"], prm["gla_abias1"], s_gla0[1], nseq, seq_len, True)
    x1 = _merge(x, mod3, prm["n1"], y_a, y_b, o_f, o_b, ex_f, ex_b, u_r, g_f, g_b, u_g, prm, big, layer,
                cond_base, rows_per_cond)
    x2, y = _mlp(x1, mod3, prm["n2"], big["w1"], big["w2"], layer, final_g, cond_base, rows_per_cond)
    return x2, y, jnp.stack([sr_f, sr_b], axis=1), jnp.stack([sg_f, sg_b], axis=1)


def kernel(x_prompt, x_sample, state_rwkv, state_gla, c, c_ctx, ada_w, ada_b, norm1_g, norm2_g, w_in, pool_w, pool_scale, rwkv_mu, rwkv_w0, rwkv_bw, rwkv_a0, rwkv_ba, rwkv_kk, rwkv_ka, rwkv_bg, rwkv_rk, rwkv_gn, gla_ab, gla_abias, gla_norm, w_branch, w_out, mlp_w1, mlp_w2, final_g):
    p = dict(pool_w=pool_w, pool_scale=pool_scale, rwkv_mu=rwkv_mu, rwkv_w0=rwkv_w0,
             rwkv_bw=rwkv_bw, rwkv_a0=rwkv_a0, rwkv_ba=rwkv_ba, rwkv_kk=rwkv_kk, rwkv_ka=rwkv_ka,
             rwkv_bg=rwkv_bg, rwkv_rk=rwkv_rk, rwkv_gn=rwkv_gn, gla_ab=gla_ab, gla_abias=gla_abias,
             gla_norm=gla_norm, norm1_g=norm1_g, norm2_g=norm2_g)
    big = dict(w_in_t=jnp.swapaxes(w_in, 1, 2).astype(BF16),
               w_branch=w_branch.astype(BF16), w_out=w_out.astype(BF16),
               w1=mlp_w1.astype(BF16), w2=mlp_w2.astype(BF16))
    bp, lp, _ = x_prompt.shape
    bs, ls, _ = x_sample.shape
    cond8 = jnp.zeros((8, D_MODEL), F32).at[0].set(c_ctx).at[1:1 + bs].set(c)
    mods = _ada(cond8, ada_w, ada_b)
    fg = final_g.reshape(1, D_MODEL)

    xp = x_prompt.reshape(bp * lp, D_MODEL)
    xs = x_sample.reshape(bs * ls, D_MODEL)
    pos_tables = _pos_tables(ls, D_MODEL)
    zr = jnp.zeros((2, bp) + state_rwkv.shape[3:], F32)
    zg = jnp.zeros((2, bp) + state_gla.shape[3:], F32)
    new_r, new_g = [], []
    for l in range(DEPTH):
        prm = _layer_params(l, p)
        mod3 = mods[l].reshape(8, 1, 6 * D_MODEL)
        xp, yp, s_r, s_g = _run_layer(xp, mod3, prm, big, l, fg, zr, zg, bp, lp, 0, bp * lp)
        new_r.append(s_r)
        new_g.append(s_g)
        xs, ys, _, _ = _run_layer(xs, mod3, prm, big, l, fg, jnp.swapaxes(state_rwkv[:, l], 0, 1),
                                  jnp.swapaxes(state_gla[:, l], 0, 1), bs, ls, 1, ls,
                                  pos_tables if l == 0 else None)
    y_prompt = yp.reshape(bp, lp, D_MODEL)
    y_sample = ys.reshape(bs, ls, D_MODEL)
    return (y_prompt, y_sample, jnp.stack(new_r, axis=1), jnp.stack(new_g, axis=1))
```

```python
import functools

import numpy as np
import jax
import jax.numpy as jnp
from jax import lax
from jax.experimental import pallas as pl
from jax.experimental.pallas import tpu as pltpu

F32 = jnp.float32
BF16 = jnp.bfloat16

D_MODEL = 1024
DEPTH = 2
GRID_W = 64
POS_BASE = 10000.0
MIX_W = 256
POOL_WINDOWS = (2, 4, 8, 16)
POOL_HALO = 8
RWKV_DECAY_SCALE = 0.606531
RWKV_GN_EPS = 64e-5
GLA_DK = 32
GLA_GATE_NORM = 16.0
D_FF = 4 * D_MODEL
EPS = 1e-6
P_MIX = 2336
P_MIX_PAD = 2432
CHUNK = 64
SUB = 16

TM = 512
TB = 256
SCAN_TB = 512
FFT_N1 = 64
VMEM_LIMIT = 56 * 1024 * 1024


def _cparams(sem):
    return pltpu.CompilerParams(dimension_semantics=sem, vmem_limit_bytes=VMEM_LIMIT)


def _full(shape):
    n = len(shape)
    return pl.BlockSpec(shape, lambda *_: (0,) * n)


def _dot(a, b):
    return jnp.dot(a.astype(BF16), b.astype(BF16), preferred_element_type=F32)


def _dot_nt(a, b):
    return lax.dot_general(a.astype(BF16), b.astype(BF16), (((1,), (1,)), ((), ())),
                           preferred_element_type=F32)


def _split2(a):
    hi = a.astype(BF16)
    lo = (a - hi.astype(F32)).astype(BF16)
    return hi, lo


def _dot_exact_rhs(a, w):
    hi, lo = _split2(a)
    return (jnp.dot(hi, w, preferred_element_type=F32) + jnp.dot(lo, w, preferred_element_type=F32))


def _dot_exact_lhs(w, a):
    hi, lo = _split2(a)
    return jnp.dot(w, hi, preferred_element_type=F32) + jnp.dot(w, lo, preferred_element_type=F32)


def _seg_ones(n, seg, scale=1.0):
    r = lax.broadcasted_iota(jnp.int32, (n, n), 0)
    c = lax.broadcasted_iota(jnp.int32, (n, n), 1)
    return jnp.where((r // seg) == (c // seg), scale, 0.0).astype(BF16)


def _log_sigmoid(x):
    return jnp.minimum(x, 0.0) - jnp.log(1.0 + jnp.exp(-jnp.abs(x)))


def _rms_mod(x, g, sc, sh):
    ms = jnp.mean(x * x, axis=-1, keepdims=True)
    return x * lax.rsqrt(ms + EPS) * g * (1.0 + sc) + sh


def _ada_kernel(c_ref, w_ref, b_ref, o_ref):
    c = c_ref[...]
    s = c * jax.nn.sigmoid(c)
    hi, lo = _split2(s)
    w = w_ref[0].astype(BF16)
    o_ref[0] = (jnp.dot(hi, w, preferred_element_type=F32) + jnp.dot(lo, w, preferred_element_type=F32)
                + b_ref[0])


def _ada(cond8, ada_w, ada_b):
    tn = 1536
    n = 6 * D_MODEL
    return pl.pallas_call(
        _ada_kernel,
        grid=(DEPTH, n // tn),
        in_specs=[_full((8, D_MODEL)),
                  pl.BlockSpec((1, D_MODEL, tn), lambda l, j: (l, 0, j)),
                  pl.BlockSpec((1, 1, tn), lambda l, j: (l, 0, j))],
        out_specs=pl.BlockSpec((1, 8, tn), lambda l, j: (l, 0, j)),
        out_shape=jax.ShapeDtypeStruct((DEPTH, 8, n), F32),
        compiler_params=_cparams(("parallel", "parallel")),
        name="ada_mod",
    )(cond8, ada_w, ada_b.reshape(DEPTH, 1, n))


def _inproj_kernel(x_ref, mod_ref, g_ref, w_ref, up_ref, uf_ref, ur_ref, ug_ref):
    _inproj_body(x_ref[...], mod_ref, g_ref, w_ref, up_ref, uf_ref, ur_ref, ug_ref)


def _inproj_pos_kernel(x_ref, rr_ref, cc_ref, mod_ref, g_ref, w_ref, xo_ref, up_ref, uf_ref, ur_ref, ug_ref,
                       *, seq_len):
    nrow = TM // GRID_W
    g0 = pl.multiple_of(((pl.program_id(0) * TM) % seq_len) // GRID_W, nrow)
    half = D_MODEL // 2
    rpart = jnp.broadcast_to(rr_ref[pl.ds(g0, nrow), :][:, None, :], (nrow, GRID_W, half)).reshape(TM, half)
    cpart = jnp.concatenate([cc_ref[...]] * nrow, axis=0)
    x = x_ref[...] + jnp.concatenate([rpart, cpart], axis=1)
    xo_ref[...] = x
    _inproj_body(x, mod_ref, g_ref, w_ref, up_ref, uf_ref, ur_ref, ug_ref)


def _inproj_body(x, mod_ref, g_ref, w_ref, up_ref, uf_ref, ur_ref, ug_ref):
    mod = mod_ref[0]
    h = _rms_mod(x, g_ref[...], mod[:, D_MODEL:2 * D_MODEL], mod[:, 0:D_MODEL]).astype(BF16)
    up_ref[...] = _dot_nt(h, w_ref[0:256, :])
    uf_ref[...] = _dot_nt(h, w_ref[256:512, :])
    ur_ref[...] = _dot_nt(h, w_ref[512:1536, :])
    ug_ref[...] = _dot_nt(h, w_ref[1536:P_MIX_PAD, :])


def _cond_map(cond_base, rows_per_cond):
    return lambda i: (cond_base + (i * TM) // rows_per_cond, 0, 0)


def _inproj(x, mod3, g, w_t, layer, cond_base, rows_per_cond, pos_tables=None, seq_len=None):
    t = x.shape[0]
    row = lambda i: (i, 0)
    in_specs = [pl.BlockSpec((1, 1, 6 * D_MODEL), _cond_map(cond_base, rows_per_cond)),
                _full((1, D_MODEL)),
                pl.BlockSpec((None, P_MIX_PAD, D_MODEL), lambda i: (layer, 0, 0))]
    out_specs = [pl.BlockSpec((TM, 256), row), pl.BlockSpec((TM, 256), row),
                 pl.BlockSpec((TM, 1024), row), pl.BlockSpec((TM, 896), row)]
    out_shape = [jax.ShapeDtypeStruct((t, 256), F32), jax.ShapeDtypeStruct((t, 256), F32),
                 jax.ShapeDtypeStruct((t, 1024), F32), jax.ShapeDtypeStruct((t, 896), F32)]
    x_spec = pl.BlockSpec((TM, D_MODEL), row)
    if pos_tables is None:
        return pl.pallas_call(
            _inproj_kernel, grid=(t // TM,), in_specs=[x_spec] + in_specs, out_specs=out_specs,
            out_shape=out_shape, compiler_params=_cparams(("parallel",)), name="in_proj",
        )(x, mod3, g, w_t)
    tab = _full((GRID_W, D_MODEL // 2))
    return pl.pallas_call(
        functools.partial(_inproj_pos_kernel, seq_len=seq_len), grid=(t // TM,),
        in_specs=[x_spec, tab, tab] + in_specs, out_specs=[x_spec] + out_specs,
        out_shape=[jax.ShapeDtypeStruct((t, D_MODEL), F32)] + out_shape,
        compiler_params=_cparams(("parallel",)), name="in_proj_pos",
    )(x, pos_tables[0], pos_tables[1], mod3, g, w_t)


def _pool_kernel(prev_ref, cur_ref, next_ref, wbd_ref, scale_ref, y_ref, *, seq_len):
    j = pl.program_id(1)
    nb = seq_len // TB
    cur = cur_ref[...]
    prev = jnp.where(j > 0, prev_ref[...], 0.0)
    nxt = jnp.where(j < nb - 1, next_ref[...], 0.0)
    ext = jnp.concatenate([prev, cur, nxt], axis=0)
    n_ext = TB + 2 * POOL_HALO

    def shift(x, k):
        return pltpu.roll(x, k % n_ext, 0)

    sums = [ext + shift(ext, 1)]
    for k in (1, 2, 4):
        sums.append(shift(sums[-1], -k) + shift(sums[-1], k))
    assert POOL_WINDOWS == (2, 4, 8, 16)
    pos = j * TB + lax.broadcasted_iota(jnp.int32, (TB, MIX_W), 0)
    group = lax.broadcasted_iota(jnp.int32, (TB, MIX_W), 1) // (MIX_W // len(POOL_WINDOWS))
    pooled = jnp.zeros((TB, MIX_W), F32)
    for gi, win in enumerate(POOL_WINDOWS):
        before, after = win // 2, win - win // 2 - 1
        s = sums[gi][POOL_HALO:POOL_HALO + TB]
        cnt = (jnp.minimum(pos + after, seq_len - 1) - jnp.maximum(pos - before, 0) + 1).astype(F32)
        pooled = jnp.where(group == gi, s / cnt - cur, pooled)
    y_ref[...] = _dot(pooled, wbd_ref[...]) * scale_ref[...]


def _pool(u_p, wbd, scale, nseq, seq_len):
    nb = seq_len // TB
    hb = TB // POOL_HALO
    nh = u_p.shape[0] // POOL_HALO
    return pl.pallas_call(
        functools.partial(_pool_kernel, seq_len=seq_len),
        grid=(nseq, nb),
        in_specs=[pl.BlockSpec((POOL_HALO, MIX_W), lambda b, j: (jnp.maximum((b * nb + j) * hb - 1, 0), 0)),
                  pl.BlockSpec((TB, MIX_W), lambda b, j: (b * nb + j, 0)),
                  pl.BlockSpec((POOL_HALO, MIX_W), lambda b, j: (jnp.minimum((b * nb + j + 1) * hb, nh - 1), 0)),
                  _full((MIX_W, MIX_W)), _full((1, MIX_W))],
        out_specs=pl.BlockSpec((TB, MIX_W), lambda b, j: (b * nb + j, 0)),
        out_shape=jax.ShapeDtypeStruct(u_p.shape, F32),
        compiler_params=_cparams(("parallel", "parallel")),
        name="pool_mixer",
    )(u_p, u_p, u_p, wbd, scale)


def _dft_tables(n):
    t = np.arange(n)
    ang = 2.0 * np.pi * ((np.outer(t, t)) % n) / n
    return np.cos(ang), np.sin(ang)


def _bd_np(m, copies):
    k = m.shape[0]
    out = np.zeros((k * copies, k * copies), m.dtype)
    for i in range(copies):
        out[i * k:(i + 1) * k, i * k:(i + 1) * k] = m
    return out


def _four_small_kernel(z_ref, cs_ref, cbd_ref, sbd_ref, y_ref, *, seq_len, norm):
    zf = _dot(cs_ref[...], z_ref[...])
    y = _dot(zf[:seq_len], cbd_ref[...]) + _dot(zf[seq_len:], sbd_ref[...])
    y_ref[...] = y * norm


def _four_small(u_f, nseq, seq_len):
    c, s = _dft_tables(seq_len)
    cs = jnp.asarray(np.concatenate([c, -s], axis=0), F32)
    c64, s64 = _dft_tables(MIX_W // 4)
    cbd = jnp.asarray(_bd_np(c64, 4), F32)
    sbd = jnp.asarray(_bd_np(s64, 4), F32)
    return pl.pallas_call(
        functools.partial(_four_small_kernel, seq_len=seq_len, norm=float((seq_len * 64) ** -0.5)),
        grid=(nseq,),
        in_specs=[pl.BlockSpec((seq_len, MIX_W), lambda b: (b, 0)),
                  _full((2 * seq_len, seq_len)), _full((MIX_W, MIX_W)), _full((MIX_W, MIX_W))],
        out_specs=pl.BlockSpec((seq_len, MIX_W), lambda b: (b, 0)),
        out_shape=jax.ShapeDtypeStruct(u_f.shape, F32),
        compiler_params=_cparams(("parallel",)),
        name="fourier_small",
    )(u_f, cs, cbd, sbd)


def _four_large_kernel(z_ref, fs_ref, g_ref, cbd_ref, sbd_ref, y_ref, a_scr, y_scr, *, norm):
    n1 = FFT_N1
    fs = fs_ref[...].astype(BF16)
    cbd = cbd_ref[...].astype(BF16)
    sbd = sbd_ref[...].astype(BF16)
    for t2 in range(n1):
        a_scr[t2 * 2 * n1:(t2 + 1) * 2 * n1, :] = _dot(fs, z_ref[pl.ds(t2, n1, stride=n1), :])
    for f1 in range(n1):
        a = jnp.concatenate([a_scr[pl.ds(f1, n1, stride=2 * n1), :],
                             a_scr[pl.ds(n1 + f1, n1, stride=2 * n1), :]], axis=0)
        x = _dot(g_ref[f1], a)
        y_scr[f1 * n1:(f1 + 1) * n1, :] = (_dot(x[:n1], cbd) + _dot(x[n1:], sbd)) * norm
    for f2 in range(n1):
        y_ref[f2 * n1:(f2 + 1) * n1, :] = y_scr[pl.ds(f2, n1, stride=n1), :]


def _four_large(u_f, nseq, seq_len):
    n1 = FFT_N1
    assert seq_len == n1 * n1
    c64, s64 = _dft_tables(n1)
    fs = jnp.asarray(np.concatenate([c64, -s64], axis=0), F32)
    f1 = np.arange(n1)[:, None, None]
    f2 = np.arange(n1)[None, :, None]
    t2 = np.arange(n1)[None, None, :]
    ang = 2.0 * np.pi * ((t2 * (f1 + n1 * f2)) % seq_len) / seq_len
    gr, gi = np.cos(ang), -np.sin(ang)
    g = jnp.asarray(np.concatenate([np.concatenate([gr, -gi], axis=2),
                                    np.concatenate([gi, gr], axis=2)], axis=1), F32)
    lanes = 128
    cbd = jnp.asarray(_bd_np(c64, lanes // 64), F32)
    sbd = jnp.asarray(_bd_np(s64, lanes // 64), F32)
    seq = pl.BlockSpec((seq_len, lanes), lambda b, h: (b, h))
    return pl.pallas_call(
        functools.partial(_four_large_kernel, norm=float((seq_len * 64) ** -0.5)),
        grid=(nseq, MIX_W // lanes),
        in_specs=[seq, _full((2 * n1, n1)), _full((n1, 2 * n1, 2 * n1)),
                  _full((lanes, lanes)), _full((lanes, lanes))],
        out_specs=seq,
        out_shape=jax.ShapeDtypeStruct(u_f.shape, F32),
        scratch_shapes=[pltpu.VMEM((2 * seq_len, lanes), F32), pltpu.VMEM((seq_len, lanes), F32)],
        compiler_params=_cparams(("parallel", "parallel")),
        name="fourier_large",
    )(u_f, fs, g, cbd, sbd)


def _bmm(x, y):
    return lax.dot_general(x.astype(BF16), y.astype(BF16), (((2,), (1,)), ((0,), (0,))),
                           preferred_element_type=F32)


def _bmm_nt(x, y):
    return lax.dot_general(x.astype(BF16), y.astype(BF16), (((2,), (2,)), ((0,), (0,))),
                           preferred_element_type=F32)


def _bmm_tn(x, y):
    return lax.dot_general(x.astype(BF16), y.astype(BF16), (((1,), (1,)), ((0,), (0,))),
                           preferred_element_type=F32)


def _slab_bd(y):
    lane = lax.broadcasted_iota(jnp.int32, y.shape, 2)
    z = jnp.zeros_like(y)
    return jnp.concatenate([jnp.where(lane < 64, y, z), jnp.where(lane >= 64, y, z)], axis=1)


def _mm(x, y):
    return _bmm(x, _slab_bd(y.astype(BF16)))


def _mm_pair(x, y1, y2):
    w = jnp.concatenate([_slab_bd(y1.astype(BF16)), _slab_bd(y2.astype(BF16))], axis=2)
    out = _bmm(x, w)
    return out[:, :, :128], out[:, :, 128:]


def _mm3(xs, y):
    yh, yl = _split2(y)
    bh, bl = _slab_bd(yh), _slab_bd(yl)
    parts = [_split2(x) for x in xs]
    xh = jnp.concatenate([p[0] for p in parts], axis=1)
    xl = jnp.concatenate([p[1] for p in parts], axis=1)
    hx = _bmm(xh, jnp.concatenate([bh, bl], axis=2))
    out = hx[:, :, :128] + (hx[:, :, 128:] + _bmm(xl, bh))
    return [out[:, i * CHUNK:(i + 1) * CHUNK] for i in range(len(xs))]


def _to_slabs(x):
    nch = x.shape[0] // CHUNK
    x4 = x.reshape(nch, CHUNK, MIX_W)
    return jnp.stack([x4[:, :, :128], x4[:, :, 128:]], axis=1).reshape(2 * nch, CHUNK, 128)


def _tri_inverse(n_mat, eye, blk):
    nd = jnp.where(blk, n_mat, 0.0)
    no = n_mat - nd
    n2, = _mm3([nd], nd)
    n4, y = _mm3([n2, nd], n2)
    x = n2 - nd - y
    n8, y = _mm3([n4, x], n4)
    x = x + n4 + y
    y, = _mm3([x], n8)
    x = x + n8 + y
    td = eye + x
    e = _mm(td, no)
    e2 = _mm(e, e)
    e3 = _mm(e, e2)
    return td + _mm(e2 - e - e3, td)


def _rwkv_state_to_bd(s4):
    bd_r = lax.broadcasted_iota(jnp.int32, (128, 128), 0) // 64
    bd_c = lax.broadcasted_iota(jnp.int32, (128, 128), 1) // 64
    pairs = []
    for hp in range(2):
        sp = s4[2 * hp:2 * hp + 2].reshape(128, 64)
        pairs.append(jnp.where(bd_r == bd_c, jnp.concatenate([sp, sp], axis=1), 0.0))
    return jnp.stack(pairs, axis=0)


def _rwkv_state_store(sfin_ref, q, s_bd):
    for hp in range(2):
        sfin_ref[q, 2 * hp] = s_bd[hp, 0:64, 0:64]
        sfin_ref[q, 2 * hp + 1] = s_bd[hp, 64:128, 64:128]


def _rwkv_prep(cur_ref, prev_ref, next_ref, mu_ref, kkw_ref, jj, nb, spb):
    tb = cur_ref.shape[0]
    seq_rows = tb // spb
    z = cur_ref[:, 0:768]
    rowid = lax.broadcasted_iota(jnp.int32, (tb, 768), 0)
    prow = jnp.where(rowid == 0, jnp.where(jj > 0, prev_ref[POOL_HALO - 1:POOL_HALO, :], 0.0), 0.0)
    nrow = jnp.where(rowid == tb - 1, jnp.where(jj < nb - 1, next_ref[0:1, :], 0.0), 0.0)
    in_seq = rowid & (seq_rows - 1)
    up = jnp.where(in_seq == 0, prow, pltpu.roll(z, 1, 0))
    dn = jnp.where(in_seq == seq_rows - 1, nrow, pltpu.roll(z, tb - 1, 0))
    zm = z + mu_ref[...] * (0.5 * (up + dn) - z)
    r = zm[:, 0:256]
    k = zm[:, 256:512]
    v = zm[:, 512:768]
    kk = k * kkw_ref[...]
    kk = kk * lax.rsqrt(_dot(kk * kk, _seg_ones(MIX_W, 64)) + EPS)
    return r, k, v, kk


def _rwkv_dir_slabs(cur_ref, r, k, v, kk, ka_ref, rk_ref, w0_ref, a0_ref, bw_ref, ba_ref, ex_ref, rev):
    tb = cur_ref.shape[0]
    w_log = w0_ref[...] + _dot(jnp.tanh(cur_ref[:, 768:896]), bw_ref[...])
    lw = -RWKV_DECAY_SCALE * jax.nn.sigmoid(w_log)
    a = jax.nn.sigmoid(a0_ref[...] + _dot(cur_ref[:, 896:1024], ba_ref[...]))
    kd = k * (1.0 + (a - 1.0) * ka_ref[...])
    ex_ref[...] = _dot(r * kd * rk_ref[...], _seg_ones(MIX_W, 64)) * v
    tr = lax.broadcasted_iota(jnp.int32, (tb, tb), 0)
    tc = lax.broadcasted_iota(jnp.int32, (tb, tb), 1)
    same_chunk = (tr // CHUNK) == (tc // CHUNK)
    tri = jnp.where(same_chunk & ((tc >= tr) if rev else (tc <= tr)), 1.0, 0.0).astype(BF16)
    cum = _dot_exact_lhs(tri, lw)
    e_in = jnp.exp(cum)
    e_out = jnp.exp(-cum)
    last = 0 if rev else CHUNK - 1
    return dict(rt=_to_slabs(r * e_in), kh=_to_slabs(kd * e_out), bh=_to_slabs(kk * a * e_out),
                kt=_to_slabs(kk * jnp.exp(cum - lw)), vv=_to_slabs(v),
                gam=_to_slabs(e_in)[:, last:last + 1, :])


def _rwkv_kernel(*refs, nb, spb, shared):
    if shared:
        (cur_f, prev_f, next_f, mu_ref, kkw_ref, ka_ref, rk_ref, w0_f, a0_f, bw_f, ba_f, w0_b, a0_b, bw_b, ba_b,
         s0_f, s0_b, o_f, ex_f, sfin_f, o_b, ex_b, sfin_b, s_scr) = refs
        cur_b = cur_f
    else:
        (cur_f, prev_f, next_f, cur_b, prev_b, next_b, mu_ref, kkw_ref, ka_ref, rk_ref, w0_f, a0_f, bw_f, ba_f,
         w0_b, a0_b, bw_b, ba_b, s0_f, s0_b, o_f, ex_f, sfin_f, o_b, ex_b, sfin_b, s_scr) = refs
    j = pl.program_id(1)

    if spb == 1:
        @pl.when(j == 0)
        def _():
            s_scr[0] = _rwkv_state_to_bd(s0_f[0])
            s_scr[1] = _rwkv_state_to_bd(s0_b[0])

    pf = _rwkv_prep(cur_f, prev_f, next_f, mu_ref, kkw_ref, j, nb, spb)
    pb = pf if shared else _rwkv_prep(cur_b, prev_b, next_b, mu_ref, kkw_ref, nb - 1 - j, nb, spb)
    sl_f = _rwkv_dir_slabs(cur_f, *pf, ka_ref, rk_ref, w0_f, a0_f, bw_f, ba_f, ex_f, False)
    sl_b = _rwkv_dir_slabs(cur_b, *pb, ka_ref, rk_ref, w0_b, a0_b, bw_b, ba_b, ex_b, True)
    rt, kh, bh, kt, vv, gam = (jnp.concatenate([sl_f[n], sl_b[n]], axis=0)
                               for n in ("rt", "kh", "bh", "kt", "vv", "gam"))
    nch = cur_f.shape[0] // CHUNK
    gdir = 2 * nch
    n_slab = 2 * gdir

    t_i = lax.broadcasted_iota(jnp.int32, (n_slab, CHUNK, 128), 1)
    s_i = lax.broadcasted_iota(jnp.int32, (n_slab, CHUNK, 128), 2) & (CHUNK - 1)
    sign = jnp.where(lax.broadcasted_iota(jnp.int32, (n_slab, CHUNK, 128), 0) >= gdir, -1, 1)
    lag = (t_i - s_i) * sign
    strict, incl = lag > 0, lag >= 0
    blk = (s_i // SUB) == (t_i // SUB)
    eye = jnp.where(s_i == t_i, 1.0, 0.0)

    lhs = jnp.concatenate([kt, rt], axis=1)
    kb_bd = jnp.concatenate([_slab_bd(kh.astype(BF16)), _slab_bd(bh.astype(BF16))], axis=1)
    mkb = _bmm_nt(lhs, kb_bd)
    m_mat = jnp.where(strict, mkb[:, :CHUNK, :128], 0.0)
    p_mat = jnp.where(incl, mkb[:, CHUNK:, :128], 0.0)
    n_mat = jnp.where(strict, mkb[:, :CHUNK, 128:], 0.0)
    q_mat = jnp.where(incl, mkb[:, CHUNK:, 128:], 0.0)
    t_mat = _tri_inverse(n_mat, eye, blk)
    mpv = _mm(jnp.concatenate([m_mat, p_mat], axis=1), vv)
    mv, pv = mpv[:, :CHUNK], mpv[:, CHUNK:]
    tk, uv = _mm_pair(t_mat, kt, mv)
    qtk, quv = _mm_pair(q_mat, tk, uv)
    rc = rt - qtk
    oc = pv - quv
    bd_r = lax.broadcasted_iota(jnp.int32, (1, 128, 128), 1) // 64
    bd_c = lax.broadcasted_iota(jnp.int32, (1, 128, 128), 2) // 64
    bd_mask = bd_r == bd_c
    d_mat = jnp.where(bd_mask, _bmm_tn(tk, bh), 0.0)
    braw = jnp.where(bd_mask,
                     _bmm_tn(jnp.concatenate([vv, uv], axis=1), jnp.concatenate([kh, -bh], axis=1)), 0.0)

    cps = nch // spb
    dirs = ((0, False, s0_f, o_f, sfin_f), (1, True, s0_b, o_b, sfin_b))
    s_cur = [s_scr[0], s_scr[1]] if spb == 1 else [None, None]
    for ci in range(nch):
        for d, rev, s0_ref, o_ref, sfin_ref in dirs:
            c = (nch - 1 - ci) if rev else ci
            q, ci_seq = divmod(ci, cps)
            q = (spb - 1 - q) if rev else q
            if spb > 1 and ci_seq == 0:
                s_cur[d] = _rwkv_state_to_bd(s0_ref[q])
            g = slice(d * gdir + 2 * c, d * gdir + 2 * c + 2)
            o = _bmm_nt(rc[g], s_cur[d]) + oc[g]
            o_ref[c * CHUNK:(c + 1) * CHUNK, 0:128] = o[0]
            o_ref[c * CHUNK:(c + 1) * CHUNK, 128:256] = o[1]
            s_cur[d] = (s_cur[d] - _bmm(s_cur[d], d_mat[g]) + braw[g]) * gam[g]
            if spb > 1 and ci_seq == cps - 1:
                _rwkv_state_store(sfin_ref, q, s_cur[d])
    if spb == 1:
        s_scr[0] = s_cur[0]
        s_scr[1] = s_cur[1]

        @pl.when(j == nb - 1)
        def _():
            _rwkv_state_store(sfin_f, 0, s_scr[0])
            _rwkv_state_store(sfin_b, 0, s_scr[1])


def _scan_blocking(nseq, seq_len):
    spb = max(1, min(nseq, SCAN_TB // seq_len))
    assert nseq % spb == 0
    tb = min(seq_len, SCAN_TB) * spb
    return spb, tb, (seq_len * spb) // tb


def _rwkv_scan(u_r, prm, s0, nseq, seq_len):
    spb, tb, nb = _scan_blocking(nseq, seq_len)
    shared = nb == 1
    hb = tb // POOL_HALO
    nh = u_r.shape[0] // POOL_HALO
    t = u_r.shape[0]
    vec = _full((1, MIX_W))
    low = _full((128, MIX_W))
    st = pl.BlockSpec((spb, 4, 64, 64), lambda b, j: (b, 0, 0, 0))

    def block_specs(jmap):
        blk = lambda b, j: b * nb + jmap(j)
        return [pl.BlockSpec((tb, 1024), lambda b, j: (blk(b, j), 0)),
                pl.BlockSpec((POOL_HALO, 768), lambda b, j: (jnp.maximum(blk(b, j) * hb - 1, 0), 0)),
                pl.BlockSpec((POOL_HALO, 768), lambda b, j: (jnp.minimum((blk(b, j) + 1) * hb, nh - 1), 0))]

    fwd_map = lambda b, j: (b * nb + j, 0)
    bwd_map = lambda b, j: (b * nb + nb - 1 - j, 0)
    in_specs = block_specs(lambda j: j)
    operands = [u_r, u_r, u_r]
    if not shared:
        in_specs += block_specs(lambda j: nb - 1 - j)
        operands += [u_r, u_r, u_r]
    in_specs += [_full((1, 768)), vec, vec, vec] + [vec, vec, low, low] * 2 + [st, st]
    f, b = prm["rwkv0"], prm["rwkv1"]
    operands += [f["mu"], f["kk"], f["ka"], f["rk"], f["w0"], f["a0"], f["bw"], f["ba"],
                 b["w0"], b["a0"], b["bw"], b["ba"], s0[0], s0[1]]
    out_tok = jax.ShapeDtypeStruct((t, MIX_W), F32)
    out_st = jax.ShapeDtypeStruct((nseq, 4, 64, 64), F32)
    return pl.pallas_call(
        functools.partial(_rwkv_kernel, nb=nb, spb=spb, shared=shared),
        grid=(nseq // spb, nb),
        in_specs=in_specs,
        out_specs=[pl.BlockSpec((tb, MIX_W), fwd_map), pl.BlockSpec((tb, MIX_W), fwd_map), st,
                   pl.BlockSpec((tb, MIX_W), bwd_map), pl.BlockSpec((tb, MIX_W), bwd_map), st],
        out_shape=[out_tok, out_tok, out_st, out_tok, out_tok, out_st],
        scratch_shapes=[pltpu.VMEM((2, 2, 128, 128), F32)],
        compiler_params=_cparams(("parallel", "arbitrary")),
        name="rwkv_scan",
    )(*operands)


def _rows_bd4(y, width):
    lane = lax.broadcasted_iota(jnp.int32, y.shape, 2) // width
    z = jnp.zeros_like(y)
    return jnp.concatenate([jnp.where(lane == h, y, z) for h in range(4)], axis=1)


def _block_scans(x, levels):
    rows, width = x.shape
    row = lax.broadcasted_iota(jnp.int32, (rows, width), 0)
    pre, suf = [x], [x]
    for m in range(1, levels + 1):
        size, half = 1 << m, 1 << (m - 1)
        p, s = pre[-1], suf[-1]
        if half < 8:
            off = row & (size - 1)
            p3 = p.reshape(rows // 8, 8, width)
            s3 = s.reshape(rows // 8, 8, width)
            addp = jnp.zeros_like(x)
            adds = jnp.zeros_like(x)
            for o in range(half):
                addp = jnp.where(off == half + o, pltpu.roll(p3, o + 1, 1).reshape(rows, width), addp)
                adds = jnp.where(off == o, pltpu.roll(s3, 8 - (half - o), 1).reshape(rows, width), adds)
        else:
            p4 = p.reshape(rows // size, size, width)
            s4 = s.reshape(rows // size, size, width)
            second = (row & half) != 0
            addp = jnp.where(second, jnp.broadcast_to(p4[:, half - 1:half, :], p4.shape).reshape(rows, width), 0.0)
            adds = jnp.where(second, 0.0, jnp.broadcast_to(s4[:, half:half + 1, :], s4.shape).reshape(rows, width))
        pre.append(p + addp)
        suf.append(s + adds)
    return pre, suf


def _gla_state_in(s4):
    sp = s4.reshape(4 * GLA_DK, 64)
    st = jnp.concatenate([sp, sp], axis=1).T[0:64, :]
    lane_h = lax.broadcasted_iota(jnp.int32, (64, 128), 1) // GLA_DK
    return jnp.concatenate([jnp.where(lane_h == h, st, 0.0) for h in range(4)], axis=0)


def _gla_state_store(sfin_ref, q, s_t):
    m = s_t[0:64] + s_t[64:128] + s_t[128:192] + s_t[192:256]
    nat = jnp.concatenate([m, jnp.zeros_like(m)], axis=0).T
    sfin_ref[q] = nat[:, 0:64].reshape(4, GLA_DK, 64)


def _gla_dir_parts(u_ref, ab_ref, abias_ref, rev):
    la = _log_sigmoid(_dot(u_ref[:, 768:896], ab_ref[...]) + abias_ref[...]) * (1.0 / GLA_GATE_NORM)
    pre, suf = _block_scans(la, 6)
    near, far = (suf, pre) if rev else (pre, suf)
    return dict(la=la, near=near, far=far, b=near[6], q=u_ref[:, 0:128] * float(GLA_DK ** -0.5),
                kg=u_ref[:, 128:256], vg=u_ref[:, 256:512])


def _gla_kernel(*refs, nb, spb, shared):
    if shared:
        u_f, ab_f, abias_f, ab_b, abias_b, s0_f, s0_b, o_f, sfin_f, o_b, sfin_b, s_scr = refs
        u_b = u_f
    else:
        u_f, u_b, ab_f, abias_f, ab_b, abias_b, s0_f, s0_b, o_f, sfin_f, o_b, sfin_b, s_scr = refs
    j = pl.program_id(1)

    if spb == 1:
        @pl.when(j == 0)
        def _():
            s_scr[0] = _gla_state_in(s0_f[0])
            s_scr[1] = _gla_state_in(s0_b[0])

    parts = (_gla_dir_parts(u_f, ab_f, abias_f, False), _gla_dir_parts(u_b, ab_b, abias_b, True))
    nch = u_f.shape[0] // CHUNK
    n3 = 2 * nch

    def cat3(xs):
        return jnp.concatenate([x.reshape(nch, CHUNK, x.shape[1]) for x in xs], axis=0)

    t_i = lax.broadcasted_iota(jnp.int32, (n3, CHUNK, MIX_W), 1)
    s_i = lax.broadcasted_iota(jnp.int32, (n3, CHUNK, MIX_W), 2) & (CHUNK - 1)
    sign = jnp.where(lax.broadcasted_iota(jnp.int32, (n3, CHUNK, MIX_W), 0) >= nch, -1, 1)
    q3 = cat3([p["q"] for p in parts])
    k3 = cat3([p["kg"] for p in parts])
    v3 = cat3([p["vg"] for p in parts])
    att = jnp.where(s_i == t_i, _bmm_nt(q3, _rows_bd4(k3.astype(BF16), GLA_DK)), 0.0)
    for lev in range(1, 7):
        size, half = 1 << lev, 1 << (lev - 1)
        same = (t_i // size) == (s_i // size)
        t_late = jnp.where((t_i & (size - 1)) >= half, 1, 0)
        s_late = jnp.where((s_i & (size - 1)) >= half, 1, 0)
        mask = same & ((t_late - s_late) * sign == 1)
        qe = cat3([p["q"] * jnp.exp(p["near"][lev - 1]) for p in parts])
        ke = cat3([p["kg"] * jnp.exp(p["far"][lev - 1] - p["la"]) for p in parts])
        att = att + jnp.where(mask, _bmm_nt(qe, _rows_bd4(ke.astype(BF16), GLA_DK)), 0.0)
    o_intra = _bmm(att, _rows_bd4(v3.astype(BF16), 64))

    b3 = cat3([p["b"] for p in parts])
    b_last = jnp.concatenate([b3[:nch, CHUNK - 1:CHUNK, :], b3[nch:, 0:1, :]], axis=0)
    qb = q3 * jnp.exp(b3)
    bd_r = lax.broadcasted_iota(jnp.int32, (1, MIX_W, 128), 1) // 64
    bd_c = lax.broadcasted_iota(jnp.int32, (1, MIX_W, 128), 2) // GLA_DK
    inc = jnp.where(bd_r == bd_c, _bmm_tn(v3, k3 * jnp.exp(b_last - b3)), 0.0)
    decay = jnp.exp(b_last)

    cps = nch // spb
    dirs = ((0, False, s0_f, o_f, sfin_f), (1, True, s0_b, o_b, sfin_b))
    s_cur = [s_scr[0], s_scr[1]] if spb == 1 else [None, None]
    for ci in range(nch):
        for d, rev, s0_ref, o_ref, sfin_ref in dirs:
            c = (nch - 1 - ci) if rev else ci
            qs, ci_seq = divmod(ci, cps)
            qs = (spb - 1 - qs) if rev else qs
            if spb > 1 and ci_seq == 0:
                s_cur[d] = _gla_state_in(s0_ref[qs])
            g = d * nch + c
            o_ref[c * CHUNK:(c + 1) * CHUNK, :] = o_intra[g] + _dot_nt(qb[g], s_cur[d])
            s_cur[d] = s_cur[d] * decay[g] + inc[g]
            if spb > 1 and ci_seq == cps - 1:
                _gla_state_store(sfin_ref, qs, s_cur[d])
    if spb == 1:
        s_scr[0] = s_cur[0]
        s_scr[1] = s_cur[1]

        @pl.when(j == nb - 1)
        def _():
            _gla_state_store(sfin_f, 0, s_scr[0])
            _gla_state_store(sfin_b, 0, s_scr[1])


def _gla_scan(u_g, prm, s0, nseq, seq_len):
    spb, tb, nb = _scan_blocking(nseq, seq_len)
    shared = nb == 1
    t = u_g.shape[0]
    fwd_map = lambda b, j: (b * nb + j, 0)
    bwd_map = lambda b, j: (b * nb + nb - 1 - j, 0)
    st = pl.BlockSpec((spb, 4, GLA_DK, 64), lambda b, j: (b, 0, 0, 0))
    in_specs = [pl.BlockSpec((tb, 896), fwd_map)]
    operands = [u_g]
    if not shared:
        in_specs.append(pl.BlockSpec((tb, 896), bwd_map))
        operands.append(u_g)
    in_specs += [_full((128, 128)), _full((1, 128))] * 2 + [st, st]
    operands += [prm["gla_ab0"], prm["gla_abias0"], prm["gla_ab1"], prm["gla_abias1"], s0[0], s0[1]]
    out_tok = jax.ShapeDtypeStruct((t, MIX_W), F32)
    out_st = jax.ShapeDtypeStruct((nseq, 4, GLA_DK, 64), F32)
    return pl.pallas_call(
        functools.partial(_gla_kernel, nb=nb, spb=spb, shared=shared),
        grid=(nseq // spb, nb),
        in_specs=in_specs,
        out_specs=[pl.BlockSpec((tb, MIX_W), fwd_map), st, pl.BlockSpec((tb, MIX_W), bwd_map), st],
        out_shape=[out_tok, out_st, out_tok, out_st],
        scratch_shapes=[pltpu.VMEM((2, 256, 128), F32)],
        compiler_params=_cparams(("parallel", "arbitrary")),
        name="gla_scan",
    )(*operands)


def _merge_kernel(x_ref, mod_ref, g_ref, ya_ref, yb_ref, of_ref, ob_ref, exf_ref, exb_ref, cag_ref,
                  gf_ref, gb_ref, gout_ref, gn_ref, bg_ref, gnorm_ref, wg_ref, wb_ref, wo_ref, x1_ref):
    mod = mod_ref[0]
    x = x_ref[...]
    h = _rms_mod(x, g_ref[...], mod[:, D_MODEL:2 * D_MODEL], mod[:, 0:D_MODEL]).astype(BF16)
    mean64 = _seg_ones(MIX_W, 64, 1.0 / 64.0)
    o = of_ref[...] + ob_ref[...]
    mu = _dot_exact_rhs(o, mean64)
    oc = o - mu
    var = _dot_exact_rhs(oc * oc, mean64)
    gate_c = _dot(jax.nn.sigmoid(cag_ref[...]), bg_ref[...])
    y_c = (oc * lax.rsqrt(var + RWKV_GN_EPS) * gn_ref[...] + exf_ref[...] + exb_ref[...]) * gate_c
    og = gf_ref[...] + gb_ref[...]
    gout = gout_ref[...]
    y_d = (og * lax.rsqrt(_dot_exact_rhs(og * og, mean64) + EPS) * gnorm_ref[...]
           * (gout * jax.nn.sigmoid(gout)))
    ys = (ya_ref[...], yb_ref[...], y_c, y_d)
    merged = jnp.zeros((TM, D_MODEL), F32)
    for i in range(4):
        g0 = P_MIX + i * D_MODEL
        gate = jax.nn.sigmoid(_dot_nt(h, wg_ref[g0:g0 + D_MODEL, :]))
        merged = merged + gate * _dot(ys[i], wb_ref[i])
    out = _dot(merged, wo_ref[...])
    x1_ref[...] = x + mod[:, 2 * D_MODEL:3 * D_MODEL] * out


def _merge(x, mod3, g, y_a, y_b, o_f, o_b, ex_f, ex_b, u_r, g_f, g_b, u_g, prm, big, layer,
           cond_base, rows_per_cond):
    t = x.shape[0]
    row = lambda i: (i, 0)
    mix = pl.BlockSpec((TM, MIX_W), row)
    vec = _full((1, MIX_W))
    lay3 = lambda i: (layer, 0, 0)
    return pl.pallas_call(
        _merge_kernel,
        grid=(t // TM,),
        in_specs=[pl.BlockSpec((TM, D_MODEL), row),
                  pl.BlockSpec((1, 1, 6 * D_MODEL), _cond_map(cond_base, rows_per_cond)),
                  _full((1, D_MODEL)),
                  mix, mix, mix, mix, mix, mix,
                  pl.BlockSpec((TM, 128), lambda i: (i, 7)),
                  mix, mix,
                  pl.BlockSpec((TM, MIX_W), lambda i: (i, 2)),
                  vec, _full((128, MIX_W)), vec,
                  pl.BlockSpec((None, big["w_in_t"].shape[1], D_MODEL), lay3, pipeline_mode=pl.Buffered(1)),
                  pl.BlockSpec((None, 4, MIX_W, D_MODEL), lambda i: (layer, 0, 0, 0)),
                  pl.BlockSpec((None, D_MODEL, D_MODEL), lay3)],
        out_specs=pl.BlockSpec((TM, D_MODEL), row),
        out_shape=jax.ShapeDtypeStruct((t, D_MODEL), F32),
        compiler_params=_cparams(("parallel",)),
        name="merge_out",
    )(x, mod3, g, y_a, y_b, o_f, o_b, ex_f, ex_b, u_r, g_f, g_b, u_g,
      prm["gn"], prm["bg"], prm["gla_norm"], big["w_in_t"], big["w_branch"], big["w_out"])


def _mlp_kernel(x_ref, mod_ref, g_ref, w1_ref, w2_ref, fg_ref, x2_ref, y_ref):
    mod = mod_ref[0]
    x = x_ref[...]
    h = _rms_mod(x, g_ref[...], mod[:, 4 * D_MODEL:5 * D_MODEL], mod[:, 3 * D_MODEL:4 * D_MODEL]).astype(BF16)
    ff = jnp.zeros((TM, D_MODEL), F32)
    for c in range(D_FF // D_MODEL):
        cols = slice(c * D_MODEL, (c + 1) * D_MODEL)
        a = jnp.maximum(jnp.dot(h, w1_ref[:, cols], preferred_element_type=F32), 0.0)
        ff = ff + _dot(a * a, w2_ref[cols, :])
    x2 = x + mod[:, 5 * D_MODEL:6 * D_MODEL] * ff
    x2_ref[...] = x2
    y_ref[...] = x2 * lax.rsqrt(jnp.mean(x2 * x2, axis=-1, keepdims=True) + EPS) * fg_ref[...]


def _mlp(x1, mod3, g, w1, w2, layer, final_g, cond_base, rows_per_cond):
    t = x1.shape[0]
    row = lambda i: (i, 0)
    lay3 = lambda i: (layer, 0, 0)
    return pl.pallas_call(
        _mlp_kernel,
        grid=(t // TM,),
        in_specs=[pl.BlockSpec((TM, D_MODEL), row),
                  pl.BlockSpec((1, 1, 6 * D_MODEL), _cond_map(cond_base, rows_per_cond)),
                  _full((1, D_MODEL)), pl.BlockSpec((None, D_MODEL, D_FF), lay3),
                  pl.BlockSpec((None, D_FF, D_MODEL), lay3), _full((1, D_MODEL))],
        out_specs=[pl.BlockSpec((TM, D_MODEL), row), pl.BlockSpec((TM, D_MODEL), row)],
        out_shape=[jax.ShapeDtypeStruct((t, D_MODEL), F32), jax.ShapeDtypeStruct((t, D_MODEL), F32)],
        compiler_params=_cparams(("parallel",)),
        name="mlp",
    )(x1, mod3, g, w1, w2, final_g)


def _pos_tables(n_tok, d):
    rows = n_tok // GRID_W
    assert rows == GRID_W
    quarter = d // 4
    omega = 1.0 / (POS_BASE ** (jnp.arange(quarter, dtype=F32) / quarter))
    ar = jnp.arange(rows, dtype=F32)[:, None] * omega
    ac = jnp.arange(GRID_W, dtype=F32)[:, None] * omega
    return (jnp.concatenate([jnp.sin(ar), jnp.cos(ar)], axis=-1),
            jnp.concatenate([jnp.sin(ac), jnp.cos(ac)], axis=-1))


def _pad_rows(m, start, total):
    return jnp.zeros((total, m.shape[1]), m.dtype).at[start:start + m.shape[0]].set(m)


def _layer_params(l, p):
    out = {
        "n1": p["norm1_g"][l].reshape(1, D_MODEL),
        "n2": p["norm2_g"][l].reshape(1, D_MODEL),
        "pool_w": jnp.einsum("gcd,gh->gchd", p["pool_w"][l], jnp.eye(4, dtype=F32)).reshape(MIX_W, MIX_W).astype(BF16),
        "pool_scale": p["pool_scale"][l].reshape(1, MIX_W),
        "gn": p["rwkv_gn"][l].reshape(1, MIX_W),
        "bg": _pad_rows(p["rwkv_bg"][l], 64, 128).astype(BF16),
        "gla_norm": p["gla_norm"][l].reshape(1, MIX_W),
    }
    for d in range(2):
        out["rwkv%d" % d] = {
            "mu": p["rwkv_mu"][l].reshape(1, 768),
            "kk": p["rwkv_kk"][l].reshape(1, MIX_W),
            "ka": p["rwkv_ka"][l].reshape(1, MIX_W),
            "rk": p["rwkv_rk"][l].reshape(1, MIX_W),
            "w0": p["rwkv_w0"][l, d].reshape(1, MIX_W),
            "a0": p["rwkv_a0"][l, d].reshape(1, MIX_W),
            "bw": _pad_rows(p["rwkv_bw"][l, d], 64 * d, 128).astype(BF16),
            "ba": _pad_rows(p["rwkv_ba"][l, d], 32 * d, 128).astype(BF16),
        }
        out["gla_ab%d" % d] = _pad_rows(p["gla_ab"][l, d], 16 * d, 128).astype(BF16)
        out["gla_abias%d" % d] = p["gla_abias"][l, d].reshape(1, 128)
    return out


def _run_layer(x, mod3, prm, big, layer, final_g, s_rwkv0, s_gla0, nseq, seq_len, cond_base, rows_per_cond,
               pos_tables=None):
    if pos_tables is None:
        u_p, u_f, u_r, u_g = _inproj(x, mod3, prm["n1"], big["w_in_t"], layer, cond_base, rows_per_cond)
    else:
        x, u_p, u_f, u_r, u_g = _inproj(x, mod3, prm["n1"], big["w_in_t"], layer, cond_base, rows_per_cond,
                                        pos_tables, seq_len)
    y_a = _pool(u_p, prm["pool_w"], prm["pool_scale"], nseq, seq_len)
    if seq_len == FFT_N1 * FFT_N1:
        y_b = _four_large(u_f, nseq, seq_len)
    else:
        y_b = _four_small(u_f, nseq, seq_len)
    o_f, ex_f, sr_f, o_b, ex_b, sr_b = _rwkv_scan(u_r, prm, s_rwkv0, nseq, seq_len)
    g_f, sg_f, g_b, sg_b = _gla_scan(u_g, prm, s_gla0, nseq, seq_len)
    x1 = _merge(x, mod3, prm["n1"], y_a, y_b, o_f, o_b, ex_f, ex_b, u_r, g_f, g_b, u_g, prm, big, layer,
                cond_base, rows_per_cond)
    x2, y = _mlp(x1, mod3, prm["n2"], big["w1"], big["w2"], layer, final_g, cond_base, rows_per_cond)
    return x2, y, jnp.stack([sr_f, sr_b], axis=1), jnp.stack([sg_f, sg_b], axis=1)


def kernel(x_prompt, x_sample, state_rwkv, state_gla, c, c_ctx, ada_w, ada_b, norm1_g, norm2_g, w_in, pool_w, pool_scale, rwkv_mu, rwkv_w0, rwkv_bw, rwkv_a0, rwkv_ba, rwkv_kk, rwkv_ka, rwkv_bg, rwkv_rk, rwkv_gn, gla_ab, gla_abias, gla_norm, w_branch, w_out, mlp_w1, mlp_w2, final_g):
    p = dict(pool_w=pool_w, pool_scale=pool_scale, rwkv_mu=rwkv_mu, rwkv_w0=rwkv_w0,
             rwkv_bw=rwkv_bw, rwkv_a0=rwkv_a0, rwkv_ba=rwkv_ba, rwkv_kk=rwkv_kk, rwkv_ka=rwkv_ka,
             rwkv_bg=rwkv_bg, rwkv_rk=rwkv_rk, rwkv_gn=rwkv_gn, gla_ab=gla_ab, gla_abias=gla_abias,
             gla_norm=gla_norm, norm1_g=norm1_g, norm2_g=norm2_g)
    big = dict(w_in_t=jnp.swapaxes(w_in, 1, 2).astype(BF16),
               w_branch=w_branch.astype(BF16), w_out=w_out.astype(BF16),
               w1=mlp_w1.astype(BF16), w2=mlp_w2.astype(BF16))
    bp, lp, _ = x_prompt.shape
    bs, ls, _ = x_sample.shape
    cond8 = jnp.zeros((8, D_MODEL), F32).at[0].set(c_ctx).at[1:1 + bs].set(c)
    mods = _ada(cond8, ada_w, ada_b)
    fg = final_g.reshape(1, D_MODEL)

    xp = x_prompt.reshape(bp * lp, D_MODEL)
    xs = x_sample.reshape(bs * ls, D_MODEL)
    pos_tables = _pos_tables(ls, D_MODEL)
    zr = jnp.zeros((2, bp) + state_rwkv.shape[3:], F32)
    zg = jnp.zeros((2, bp) + state_gla.shape[3:], F32)
    new_r, new_g = [], []
    for l in range(DEPTH):
        prm = _layer_params(l, p)
        mod3 = mods[l].reshape(8, 1, 6 * D_MODEL)
        xp, yp, s_r, s_g = _run_layer(xp, mod3, prm, big, l, fg, zr, zg, bp, lp, 0, bp * lp)
        new_r.append(s_r)
        new_g.append(s_g)
        xs, ys, _, _ = _run_layer(xs, mod3, prm, big, l, fg, jnp.swapaxes(state_rwkv[:, l], 0, 1),
                                  jnp.swapaxes(state_gla[:, l], 0, 1), bs, ls, 1, ls,
                                  pos_tables if l == 0 else None)
    y_prompt = yp.reshape(bp, lp, D_MODEL)
    y_sample = ys.reshape(bs, ls, D_MODEL)
    return (y_prompt, y_sample, jnp.stack(new_r, axis=1), jnp.stack(new_g, axis=1))
```

```python
import functools

import numpy as np
import jax
import jax.numpy as jnp
from jax import lax
from jax.experimental import pallas as pl
from jax.experimental.pallas import tpu as pltpu

F32 = jnp.float32
BF16 = jnp.bfloat16

D_MODEL = 1024
DEPTH = 2
GRID_W = 64
POS_BASE = 10000.0
MIX_W = 256
POOL_WINDOWS = (2, 4, 8, 16)
POOL_HALO = 8
RWKV_DECAY_SCALE = 0.606531
RWKV_GN_EPS = 64e-5
GLA_DK = 32
GLA_GATE_NORM = 16.0
D_FF = 4 * D_MODEL
EPS = 1e-6
P_MIX = 2336
P_MIX_PAD = 2432
CHUNK = 64
SUB = 16

TM = 512
POOL_TB = 1024
SCAN_TB = 512
FFT_N1 = 64
VMEM_LIMIT = 56 * 1024 * 1024


def _cparams(sem):
    return pltpu.CompilerParams(dimension_semantics=sem, vmem_limit_bytes=VMEM_LIMIT)


def _full(shape):
    n = len(shape)
    return pl.BlockSpec(shape, lambda *_: (0,) * n)


def _dot(a, b):
    return jnp.dot(a.astype(BF16), b.astype(BF16), preferred_element_type=F32)


def _dot_nt(a, b):
    return lax.dot_general(a.astype(BF16), b.astype(BF16), (((1,), (1,)), ((), ())),
                           preferred_element_type=F32)


def _split2(a):
    hi = a.astype(BF16)
    lo = (a - hi.astype(F32)).astype(BF16)
    return hi, lo


def _dot_exact_rhs(a, w):
    hi, lo = _split2(a)
    return (jnp.dot(hi, w, preferred_element_type=F32) + jnp.dot(lo, w, preferred_element_type=F32))


def _dot_exact_lhs(w, a):
    hi, lo = _split2(a)
    return jnp.dot(w, hi, preferred_element_type=F32) + jnp.dot(w, lo, preferred_element_type=F32)


def _seg_ones(n, seg, scale=1.0):
    r = lax.broadcasted_iota(jnp.int32, (n, n), 0)
    c = lax.broadcasted_iota(jnp.int32, (n, n), 1)
    return jnp.where((r // seg) == (c // seg), scale, 0.0).astype(BF16)


def _log_sigmoid(x):
    return jnp.minimum(x, 0.0) - jnp.log(1.0 + jnp.exp(-jnp.abs(x)))


def _rms_mod(x, g, sc, sh):
    ms = jnp.mean(x * x, axis=-1, keepdims=True)
    return x * lax.rsqrt(ms + EPS) * g * (1.0 + sc) + sh


def _ada_kernel(c_ref, w_ref, b_ref, o_ref):
    c = c_ref[...]
    s = c * jax.nn.sigmoid(c)
    hi, lo = _split2(s)
    w = w_ref[0].astype(BF16)
    o_ref[0] = (jnp.dot(hi, w, preferred_element_type=F32) + jnp.dot(lo, w, preferred_element_type=F32)
                + b_ref[0])


def _ada(cond8, ada_w, ada_b):
    tn = 1536
    n = 6 * D_MODEL
    return pl.pallas_call(
        _ada_kernel,
        grid=(DEPTH, n // tn),
        in_specs=[_full((8, D_MODEL)),
                  pl.BlockSpec((1, D_MODEL, tn), lambda l, j: (l, 0, j)),
                  pl.BlockSpec((1, 1, tn), lambda l, j: (l, 0, j))],
        out_specs=pl.BlockSpec((1, 8, tn), lambda l, j: (l, 0, j)),
        out_shape=jax.ShapeDtypeStruct((DEPTH, 8, n), F32),
        compiler_params=_cparams(("parallel", "parallel")),
        name="ada_mod",
    )(cond8, ada_w, ada_b.reshape(DEPTH, 1, n))


def _inproj_kernel(x_ref, mod_ref, g_ref, w_ref, up_ref, uf_ref, ur_ref, ug_ref):
    _inproj_body(x_ref[...], mod_ref, g_ref, w_ref, up_ref, uf_ref, ur_ref, ug_ref)


def _inproj_pos_kernel(x_ref, rr_ref, cc_ref, mod_ref, g_ref, w_ref, xo_ref, up_ref, uf_ref, ur_ref, ug_ref,
                       *, seq_len):
    nrow = TM // GRID_W
    g0 = pl.multiple_of(((pl.program_id(0) * TM) % seq_len) // GRID_W, nrow)
    half = D_MODEL // 2
    rpart = jnp.broadcast_to(rr_ref[pl.ds(g0, nrow), :][:, None, :], (nrow, GRID_W, half)).reshape(TM, half)
    cpart = jnp.concatenate([cc_ref[...]] * nrow, axis=0)
    x = x_ref[...] + jnp.concatenate([rpart, cpart], axis=1)
    xo_ref[...] = x
    _inproj_body(x, mod_ref, g_ref, w_ref, up_ref, uf_ref, ur_ref, ug_ref)


def _inproj_body(x, mod_ref, g_ref, w_ref, up_ref, uf_ref, ur_ref, ug_ref):
    mod = mod_ref[0]
    h = _rms_mod(x, g_ref[...], mod[:, D_MODEL:2 * D_MODEL], mod[:, 0:D_MODEL]).astype(BF16)
    up_ref[...] = _dot_nt(h, w_ref[0:256, :])
    uf_ref[...] = _dot_nt(h, w_ref[256:512, :])
    ur_ref[...] = _dot_nt(h, w_ref[512:1536, :])
    ug_ref[...] = _dot_nt(h, w_ref[1536:P_MIX_PAD, :])


def _cond_map(cond_base, rows_per_cond):
    return lambda i: (cond_base + (i * TM) // rows_per_cond, 0, 0)


def _inproj(x, mod3, g, w_t, layer, cond_base, rows_per_cond, pos_tables=None, seq_len=None):
    t = x.shape[0]
    row = lambda i: (i, 0)
    in_specs = [pl.BlockSpec((1, 1, 6 * D_MODEL), _cond_map(cond_base, rows_per_cond)),
                _full((1, D_MODEL)),
                pl.BlockSpec((None, P_MIX_PAD, D_MODEL), lambda i: (layer, 0, 0))]
    out_specs = [pl.BlockSpec((TM, 256), row), pl.BlockSpec((TM, 256), row),
                 pl.BlockSpec((TM, 1024), row), pl.BlockSpec((TM, 896), row)]
    out_shape = [jax.ShapeDtypeStruct((t, 256), F32), jax.ShapeDtypeStruct((t, 256), F32),
                 jax.ShapeDtypeStruct((t, 1024), F32), jax.ShapeDtypeStruct((t, 896), F32)]
    x_spec = pl.BlockSpec((TM, D_MODEL), row)
    if pos_tables is None:
        return pl.pallas_call(
            _inproj_kernel, grid=(t // TM,), in_specs=[x_spec] + in_specs, out_specs=out_specs,
            out_shape=out_shape, compiler_params=_cparams(("parallel",)), name="in_proj",
        )(x, mod3, g, w_t)
    tab = _full((GRID_W, D_MODEL // 2))
    return pl.pallas_call(
        functools.partial(_inproj_pos_kernel, seq_len=seq_len), grid=(t // TM,),
        in_specs=[x_spec, tab, tab] + in_specs, out_specs=[x_spec] + out_specs,
        out_shape=[jax.ShapeDtypeStruct((t, D_MODEL), F32)] + out_shape,
        compiler_params=_cparams(("parallel",)), name="in_proj_pos",
    )(x, pos_tables[0], pos_tables[1], mod3, g, w_t)


def _pool_kernel(prev_ref, cur_ref, next_ref, wbd_ref, scale_ref, y_ref, *, seq_len):
    j = pl.program_id(1)
    tb = cur_ref.shape[0]
    nb = seq_len // tb
    cur = cur_ref[...]
    prev = jnp.where(j > 0, prev_ref[...], 0.0)
    nxt = jnp.where(j < nb - 1, next_ref[...], 0.0)
    ext = jnp.concatenate([prev, cur, nxt], axis=0)
    n_ext = tb + 2 * POOL_HALO

    def shift(x, k):
        return pltpu.roll(x, k % n_ext, 0)

    sums = [ext + shift(ext, 1)]
    for k in (1, 2, 4):
        sums.append(shift(sums[-1], -k) + shift(sums[-1], k))
    assert POOL_WINDOWS == (2, 4, 8, 16)
    pos = j * tb + lax.broadcasted_iota(jnp.int32, (tb, MIX_W), 0)
    group = lax.broadcasted_iota(jnp.int32, (tb, MIX_W), 1) // (MIX_W // len(POOL_WINDOWS))
    pooled = jnp.zeros((tb, MIX_W), F32)
    for gi, win in enumerate(POOL_WINDOWS):
        before, after = win // 2, win - win // 2 - 1
        s = sums[gi][POOL_HALO:POOL_HALO + tb]
        cnt = (jnp.minimum(pos + after, seq_len - 1) - jnp.maximum(pos - before, 0) + 1).astype(F32)
        pooled = jnp.where(group == gi, s / cnt - cur, pooled)
    y_ref[...] = _dot(pooled, wbd_ref[...]) * scale_ref[...]


def _pool(u_p, wbd, scale, nseq, seq_len):
    tb = min(seq_len, POOL_TB)
    nb = seq_len // tb
    hb = tb // POOL_HALO
    nh = u_p.shape[0] // POOL_HALO
    return pl.pallas_call(
        functools.partial(_pool_kernel, seq_len=seq_len),
        grid=(nseq, nb),
        in_specs=[pl.BlockSpec((POOL_HALO, MIX_W), lambda b, j: (jnp.maximum((b * nb + j) * hb - 1, 0), 0)),
                  pl.BlockSpec((tb, MIX_W), lambda b, j: (b * nb + j, 0)),
                  pl.BlockSpec((POOL_HALO, MIX_W), lambda b, j: (jnp.minimum((b * nb + j + 1) * hb, nh - 1), 0)),
                  _full((MIX_W, MIX_W)), _full((1, MIX_W))],
        out_specs=pl.BlockSpec((tb, MIX_W), lambda b, j: (b * nb + j, 0)),
        out_shape=jax.ShapeDtypeStruct(u_p.shape, F32),
        compiler_params=_cparams(("parallel", "parallel")),
        name="pool_mixer",
    )(u_p, u_p, u_p, wbd, scale)


def _dft_tables(n):
    t = np.arange(n)
    ang = 2.0 * np.pi * ((np.outer(t, t)) % n) / n
    return np.cos(ang), np.sin(ang)


def _bd_np(m, copies):
    k = m.shape[0]
    out = np.zeros((k * copies, k * copies), m.dtype)
    for i in range(copies):
        out[i * k:(i + 1) * k, i * k:(i + 1) * k] = m
    return out


def _four_small_kernel(z_ref, cs_ref, cbd_ref, sbd_ref, y_ref, *, seq_len, norm):
    zf = _dot(cs_ref[...], z_ref[...])
    y = _dot(zf[:seq_len], cbd_ref[...]) + _dot(zf[seq_len:], sbd_ref[...])
    y_ref[...] = y * norm


def _four_small(u_f, nseq, seq_len):
    c, s = _dft_tables(seq_len)
    cs = jnp.asarray(np.concatenate([c, -s], axis=0), F32)
    c64, s64 = _dft_tables(MIX_W // 4)
    cbd = jnp.asarray(_bd_np(c64, 4), F32)
    sbd = jnp.asarray(_bd_np(s64, 4), F32)
    return pl.pallas_call(
        functools.partial(_four_small_kernel, seq_len=seq_len, norm=float((seq_len * 64) ** -0.5)),
        grid=(nseq,),
        in_specs=[pl.BlockSpec((seq_len, MIX_W), lambda b: (b, 0)),
                  _full((2 * seq_len, seq_len)), _full((MIX_W, MIX_W)), _full((MIX_W, MIX_W))],
        out_specs=pl.BlockSpec((seq_len, MIX_W), lambda b: (b, 0)),
        out_shape=jax.ShapeDtypeStruct(u_f.shape, F32),
        compiler_params=_cparams(("parallel",)),
        name="fourier_small",
    )(u_f, cs, cbd, sbd)


def _four_large_kernel(z0_ref, z1_ref, fs_ref, g_ref, cbd_ref, sbd_ref, y_ref, a0_scr, a1_scr, xr_scr, xi_scr,
                       y0_scr, y1_scr, *, norm):
    n1 = FFT_N1
    fs = fs_ref[...].astype(BF16)

    def rows(ref0, ref1, start, stride):
        sl = pl.ds(start, n1, stride=stride)
        return jnp.concatenate([ref0[sl, :], ref1[sl, :]], axis=1)

    for t2 in range(n1):
        a = _dot(fs, rows(z0_ref, z1_ref, t2, n1))
        a0_scr[t2 * 2 * n1:(t2 + 1) * 2 * n1, :] = a[:, :128]
        a1_scr[t2 * 2 * n1:(t2 + 1) * 2 * n1, :] = a[:, 128:]
    for f1 in range(n1):
        a = jnp.concatenate([rows(a0_scr, a1_scr, f1, 2 * n1), rows(a0_scr, a1_scr, n1 + f1, 2 * n1)],
                            axis=0)
        x = _dot(g_ref[f1], a)
        xr_scr[f1 * n1:(f1 + 1) * n1, :] = x[:n1]
        xi_scr[f1 * n1:(f1 + 1) * n1, :] = x[n1:]
    y = (_dot(xr_scr[...], cbd_ref[...]) + _dot(xi_scr[...], sbd_ref[...])) * norm
    y0_scr[...] = y[:, :128]
    y1_scr[...] = y[:, 128:]
    for f2 in range(n1):
        y_ref[f2 * n1:(f2 + 1) * n1, :] = rows(y0_scr, y1_scr, f2, n1)


def _four_large(u_f, nseq, seq_len):
    n1 = FFT_N1
    assert seq_len == n1 * n1
    c64, s64 = _dft_tables(n1)
    fs = jnp.asarray(np.concatenate([c64, -s64], axis=0), F32)
    f1 = np.arange(n1)[:, None, None]
    f2 = np.arange(n1)[None, :, None]
    t2 = np.arange(n1)[None, None, :]
    ang = 2.0 * np.pi * ((t2 * (f1 + n1 * f2)) % seq_len) / seq_len
    gr, gi = np.cos(ang), -np.sin(ang)
    g = jnp.asarray(np.concatenate([np.concatenate([gr, -gi], axis=2),
                                    np.concatenate([gi, gr], axis=2)], axis=1), F32)
    cbd = jnp.asarray(_bd_np(c64, 4), F32)
    sbd = jnp.asarray(_bd_np(s64, 4), F32)
    half = MIX_W // 2
    return pl.pallas_call(
        functools.partial(_four_large_kernel, norm=float((seq_len * 64) ** -0.5)),
        grid=(nseq,),
        in_specs=[pl.BlockSpec((seq_len, half), lambda b: (b, 0)), pl.BlockSpec((seq_len, half), lambda b: (b, 1)),
                  _full((2 * n1, n1)), _full((n1, 2 * n1, 2 * n1)), _full((MIX_W, MIX_W)), _full((MIX_W, MIX_W))],
        out_specs=pl.BlockSpec((seq_len, MIX_W), lambda b: (b, 0)),
        out_shape=jax.ShapeDtypeStruct(u_f.shape, F32),
        scratch_shapes=[pltpu.VMEM((2 * seq_len, half), F32), pltpu.VMEM((2 * seq_len, half), F32),
                        pltpu.VMEM((seq_len, MIX_W), F32), pltpu.VMEM((seq_len, MIX_W), F32),
                        pltpu.VMEM((seq_len, half), F32), pltpu.VMEM((seq_len, half), F32)],
        compiler_params=_cparams(("parallel",)),
        name="fourier_large",
    )(u_f, u_f, fs, g, cbd, sbd)


def _bmm(x, y):
    return lax.dot_general(x.astype(BF16), y.astype(BF16), (((2,), (1,)), ((0,), (0,))),
                           preferred_element_type=F32)


def _bmm_nt(x, y):
    return lax.dot_general(x.astype(BF16), y.astype(BF16), (((2,), (2,)), ((0,), (0,))),
                           preferred_element_type=F32)


def _bmm_tn(x, y):
    return lax.dot_general(x.astype(BF16), y.astype(BF16), (((1,), (1,)), ((0,), (0,))),
                           preferred_element_type=F32)


def _slab_bd(y):
    lane = lax.broadcasted_iota(jnp.int32, y.shape, 2)
    z = jnp.zeros_like(y)
    return jnp.concatenate([jnp.where(lane < 64, y, z), jnp.where(lane >= 64, y, z)], axis=1)


def _mm(x, y):
    return _bmm(x, _slab_bd(y.astype(BF16)))


def _mm_pair(x, y1, y2):
    w = jnp.concatenate([_slab_bd(y1.astype(BF16)), _slab_bd(y2.astype(BF16))], axis=2)
    out = _bmm(x, w)
    return out[:, :, :128], out[:, :, 128:]


def _mm3(xs, y):
    yh, yl = _split2(y)
    bh, bl = _slab_bd(yh), _slab_bd(yl)
    parts = [_split2(x) for x in xs]
    xh = jnp.concatenate([p[0] for p in parts], axis=1)
    xl = jnp.concatenate([p[1] for p in parts], axis=1)
    hx = _bmm(xh, jnp.concatenate([bh, bl], axis=2))
    out = hx[:, :, :128] + (hx[:, :, 128:] + _bmm(xl, bh))
    return [out[:, i * CHUNK:(i + 1) * CHUNK] for i in range(len(xs))]


def _to_slabs(x):
    nch = x.shape[0] // CHUNK
    x4 = x.reshape(nch, CHUNK, MIX_W)
    return jnp.stack([x4[:, :, :128], x4[:, :, 128:]], axis=1).reshape(2 * nch, CHUNK, 128)


def _tri_inverse(n_mat, eye, blk):
    nd = jnp.where(blk, n_mat, 0.0)
    no = n_mat - nd
    n2, = _mm3([nd], nd)
    n4, y = _mm3([n2, nd], n2)
    x = n2 - nd - y
    n8, y = _mm3([n4, x], n4)
    x = x + n4 + y
    y, = _mm3([x], n8)
    x = x + n8 + y
    td = eye + x
    e = _mm(td, no)
    e2 = _mm(e, e)
    e3 = _mm(e, e2)
    return td + _mm(e2 - e - e3, td)


def _rwkv_state_to_bd(s4):
    bd_r = lax.broadcasted_iota(jnp.int32, (128, 128), 0) // 64
    bd_c = lax.broadcasted_iota(jnp.int32, (128, 128), 1) // 64
    pairs = []
    for hp in range(2):
        sp = s4[2 * hp:2 * hp + 2].reshape(128, 64)
        pairs.append(jnp.where(bd_r == bd_c, jnp.concatenate([sp, sp], axis=1), 0.0))
    return jnp.stack(pairs, axis=0)


def _rwkv_state_store(sfin_ref, q, s_bd):
    for hp in range(2):
        sfin_ref[q, 2 * hp] = s_bd[hp, 0:64, 0:64]
        sfin_ref[q, 2 * hp + 1] = s_bd[hp, 64:128, 64:128]


def _rwkv_prep(cur_ref, prev_ref, next_ref, mu_ref, kkw_ref, jj, nb, spb):
    tb = cur_ref.shape[0]
    seq_rows = tb // spb
    z = cur_ref[:, 0:768]
    rowid = lax.broadcasted_iota(jnp.int32, (tb, 768), 0)
    prow = jnp.where(rowid == 0, jnp.where(jj > 0, prev_ref[POOL_HALO - 1:POOL_HALO, :], 0.0), 0.0)
    nrow = jnp.where(rowid == tb - 1, jnp.where(jj < nb - 1, next_ref[0:1, :], 0.0), 0.0)
    in_seq = rowid & (seq_rows - 1)
    up = jnp.where(in_seq == 0, prow, pltpu.roll(z, 1, 0))
    dn = jnp.where(in_seq == seq_rows - 1, nrow, pltpu.roll(z, tb - 1, 0))
    zm = z + mu_ref[...] * (0.5 * (up + dn) - z)
    r = zm[:, 0:256]
    k = zm[:, 256:512]
    v = zm[:, 512:768]
    kk = k * kkw_ref[...]
    kk = kk * lax.rsqrt(_dot(kk * kk, _seg_ones(MIX_W, 64)) + EPS)
    return r, k, v, kk


def _rwkv_dir_slabs(cur_ref, r, k, v, kk, ka_ref, rk_ref, w0_ref, a0_ref, bw_ref, ba_ref, ex_ref, rev):
    tb = cur_ref.shape[0]
    w_log = w0_ref[...] + _dot(jnp.tanh(cur_ref[:, 768:896]), bw_ref[...])
    lw = -RWKV_DECAY_SCALE * jax.nn.sigmoid(w_log)
    a = jax.nn.sigmoid(a0_ref[...] + _dot(cur_ref[:, 896:1024], ba_ref[...]))
    kd = k * (1.0 + (a - 1.0) * ka_ref[...])
    ex_ref[...] = _dot(r * kd * rk_ref[...], _seg_ones(MIX_W, 64)) * v
    tr = lax.broadcasted_iota(jnp.int32, (tb, tb), 0)
    tc = lax.broadcasted_iota(jnp.int32, (tb, tb), 1)
    same_chunk = (tr // CHUNK) == (tc // CHUNK)
    tri = jnp.where(same_chunk & ((tc >= tr) if rev else (tc <= tr)), 1.0, 0.0).astype(BF16)
    cum = _dot_exact_lhs(tri, lw)
    e_in = jnp.exp(cum)
    e_out = jnp.exp(-cum)
    last = 0 if rev else CHUNK - 1
    return dict(rt=_to_slabs(r * e_in), kh=_to_slabs((kd * e_out).astype(BF16)),
                bh=_to_slabs((kk * a * e_out).astype(BF16)),
                kt=_to_slabs((kk * jnp.exp(cum - lw)).astype(BF16)), vv=_to_slabs(v.astype(BF16)),
                gam=_to_slabs(e_in)[:, last:last + 1, :])


def _rwkv_kernel(*refs, nb, spb, shared):
    if shared:
        (cur_f, prev_f, next_f, mu_ref, kkw_ref, ka_ref, rk_ref, w0_f, a0_f, bw_f, ba_f, w0_b, a0_b, bw_b, ba_b,
         s0_f, s0_b, o_f, ex_f, sfin_f, o_b, ex_b, sfin_b, s_scr) = refs
        cur_b = cur_f
    else:
        (cur_f, prev_f, next_f, cur_b, prev_b, next_b, mu_ref, kkw_ref, ka_ref, rk_ref, w0_f, a0_f, bw_f, ba_f,
         w0_b, a0_b, bw_b, ba_b, s0_f, s0_b, o_f, ex_f, sfin_f, o_b, ex_b, sfin_b, s_scr) = refs
    j = pl.program_id(1)

    if spb == 1:
        @pl.when(j == 0)
        def _():
            s_scr[0] = _rwkv_state_to_bd(s0_f[0])
            s_scr[1] = _rwkv_state_to_bd(s0_b[0])

    pf = _rwkv_prep(cur_f, prev_f, next_f, mu_ref, kkw_ref, j, nb, spb)
    pb = pf if shared else _rwkv_prep(cur_b, prev_b, next_b, mu_ref, kkw_ref, nb - 1 - j, nb, spb)
    sl_f = _rwkv_dir_slabs(cur_f, *pf, ka_ref, rk_ref, w0_f, a0_f, bw_f, ba_f, ex_f, False)
    sl_b = _rwkv_dir_slabs(cur_b, *pb, ka_ref, rk_ref, w0_b, a0_b, bw_b, ba_b, ex_b, True)
    rt, kh, bh, kt, vv, gam = (jnp.concatenate([sl_f[n], sl_b[n]], axis=0)
                               for n in ("rt", "kh", "bh", "kt", "vv", "gam"))
    nch = cur_f.shape[0] // CHUNK
    gdir = 2 * nch
    n_slab = 2 * gdir

    mshape = (2, 1, CHUNK, 128)
    lag = ((lax.broadcasted_iota(jnp.int32, mshape, 2) - (lax.broadcasted_iota(jnp.int32, mshape, 3) & (CHUNK - 1)))
           * jnp.where(lax.broadcasted_iota(jnp.int32, mshape, 0) >= 1, -1, 1))
    strict, incl = lag > 0, lag >= 0

    def masked(mask, x):
        return jnp.where(mask, x.reshape(2, gdir, CHUNK, 128), 0.0).reshape(n_slab, CHUNK, 128)

    t_i = lax.broadcasted_iota(jnp.int32, (1, CHUNK, 128), 1)
    s_i = lax.broadcasted_iota(jnp.int32, (1, CHUNK, 128), 2) & (CHUNK - 1)
    blk = (s_i // SUB) == (t_i // SUB)
    eye = jnp.where(s_i == t_i, 1.0, 0.0)

    lhs = jnp.concatenate([kt, rt.astype(BF16)], axis=1)
    kb_bd = jnp.concatenate([_slab_bd(kh.astype(BF16)), _slab_bd(bh.astype(BF16))], axis=1)
    mkb = _bmm_nt(lhs, kb_bd)
    m_mat = masked(strict, mkb[:, :CHUNK, :128])
    p_mat = masked(incl, mkb[:, CHUNK:, :128])
    n_mat = masked(strict, mkb[:, :CHUNK, 128:])
    q_mat = masked(incl, mkb[:, CHUNK:, 128:])
    t_mat = _tri_inverse(n_mat, eye, blk)
    mpv = _mm(jnp.concatenate([m_mat, p_mat], axis=1), vv)
    mv, pv = mpv[:, :CHUNK], mpv[:, CHUNK:]
    tk, uv = _mm_pair(t_mat, kt, mv)
    qtk, quv = _mm_pair(q_mat, tk, uv)
    rc = rt - qtk
    oc = pv - quv
    bd_r = lax.broadcasted_iota(jnp.int32, (1, 128, 128), 1) // 64
    bd_c = lax.broadcasted_iota(jnp.int32, (1, 128, 128), 2) // 64
    bd_mask = bd_r == bd_c
    d_mat = jnp.where(bd_mask, _bmm_tn(tk, bh), 0.0)
    braw = jnp.where(bd_mask,
                     _bmm_tn(jnp.concatenate([vv, uv.astype(BF16)], axis=1),
                             jnp.concatenate([kh, -bh], axis=1)), 0.0)

    cps = nch // spb
    dirs = ((0, False, s0_f, o_f, sfin_f), (1, True, s0_b, o_b, sfin_b))
    s_cur = [s_scr[0], s_scr[1]] if spb == 1 else [None, None]
    for ci in range(nch):
        for d, rev, s0_ref, o_ref, sfin_ref in dirs:
            c = (nch - 1 - ci) if rev else ci
            q, ci_seq = divmod(ci, cps)
            q = (spb - 1 - q) if rev else q
            if spb > 1 and ci_seq == 0:
                s_cur[d] = _rwkv_state_to_bd(s0_ref[q])
            g = slice(d * gdir + 2 * c, d * gdir + 2 * c + 2)
            o = _bmm_nt(rc[g], s_cur[d]) + oc[g]
            o_ref[c * CHUNK:(c + 1) * CHUNK, 0:128] = o[0]
            o_ref[c * CHUNK:(c + 1) * CHUNK, 128:256] = o[1]
            s_cur[d] = (s_cur[d] - _bmm(s_cur[d], d_mat[g]) + braw[g]) * gam[g]
            if spb > 1 and ci_seq == cps - 1:
                _rwkv_state_store(sfin_ref, q, s_cur[d])
    if spb == 1:
        s_scr[0] = s_cur[0]
        s_scr[1] = s_cur[1]

        @pl.when(j == nb - 1)
        def _():
            _rwkv_state_store(sfin_f, 0, s_scr[0])
            _rwkv_state_store(sfin_b, 0, s_scr[1])


def _scan_blocking(nseq, seq_len):
    spb = max(1, min(nseq, SCAN_TB // seq_len))
    assert nseq % spb == 0
    tb = min(seq_len, SCAN_TB) * spb
    return spb, tb, (seq_len * spb) // tb


def _rwkv_scan(u_r, prm, s0, nseq, seq_len):
    spb, tb, nb = _scan_blocking(nseq, seq_len)
    shared = nb == 1
    hb = tb // POOL_HALO
    nh = u_r.shape[0] // POOL_HALO
    t = u_r.shape[0]
    vec = _full((1, MIX_W))
    low = _full((128, MIX_W))
    st = pl.BlockSpec((spb, 4, 64, 64), lambda b, j: (b, 0, 0, 0))

    def block_specs(jmap):
        blk = lambda b, j: b * nb + jmap(j)
        return [pl.BlockSpec((tb, 1024), lambda b, j: (blk(b, j), 0)),
                pl.BlockSpec((POOL_HALO, 768), lambda b, j: (jnp.maximum(blk(b, j) * hb - 1, 0), 0)),
                pl.BlockSpec((POOL_HALO, 768), lambda b, j: (jnp.minimum((blk(b, j) + 1) * hb, nh - 1), 0))]

    fwd_map = lambda b, j: (b * nb + j, 0)
    bwd_map = lambda b, j: (b * nb + nb - 1 - j, 0)
    in_specs = block_specs(lambda j: j)
    operands = [u_r, u_r, u_r]
    if not shared:
        in_specs += block_specs(lambda j: nb - 1 - j)
        operands += [u_r, u_r, u_r]
    in_specs += [_full((1, 768)), vec, vec, vec] + [vec, vec, low, low] * 2 + [st, st]
    f, b = prm["rwkv0"], prm["rwkv1"]
    operands += [f["mu"], f["kk"], f["ka"], f["rk"], f["w0"], f["a0"], f["bw"], f["ba"],
                 b["w0"], b["a0"], b["bw"], b["ba"], s0[0], s0[1]]
    out_tok = jax.ShapeDtypeStruct((t, MIX_W), F32)
    out_st = jax.ShapeDtypeStruct((nseq, 4, 64, 64), F32)
    return pl.pallas_call(
        functools.partial(_rwkv_kernel, nb=nb, spb=spb, shared=shared),
        grid=(nseq // spb, nb),
        in_specs=in_specs,
        out_specs=[pl.BlockSpec((tb, MIX_W), fwd_map), pl.BlockSpec((tb, MIX_W), fwd_map), st,
                   pl.BlockSpec((tb, MIX_W), bwd_map), pl.BlockSpec((tb, MIX_W), bwd_map), st],
        out_shape=[out_tok, out_tok, out_st, out_tok, out_tok, out_st],
        scratch_shapes=[pltpu.VMEM((2, 2, 128, 128), F32)],
        compiler_params=_cparams(("parallel", "arbitrary")),
        name="rwkv_scan",
    )(*operands)


def _rows_bd4(y, width):
    lane = lax.broadcasted_iota(jnp.int32, y.shape, 2) // width
    z = jnp.zeros_like(y)
    return jnp.concatenate([jnp.where(lane == h, y, z) for h in range(4)], axis=1)


def _block_scans(x, levels):
    rows, width = x.shape
    row = lax.broadcasted_iota(jnp.int32, (rows, width), 0)
    pre, suf = [x], [x]
    for m in range(1, levels + 1):
        size, half = 1 << m, 1 << (m - 1)
        p, s = pre[-1], suf[-1]
        if half < 8:
            off = row & (size - 1)
            p3 = p.reshape(rows // 8, 8, width)
            s3 = s.reshape(rows // 8, 8, width)
            addp = jnp.zeros_like(x)
            adds = jnp.zeros_like(x)
            for o in range(half):
                addp = jnp.where(off == half + o, pltpu.roll(p3, o + 1, 1).reshape(rows, width), addp)
                adds = jnp.where(off == o, pltpu.roll(s3, 8 - (half - o), 1).reshape(rows, width), adds)
        else:
            p4 = p.reshape(rows // size, size, width)
            s4 = s.reshape(rows // size, size, width)
            second = (row & half) != 0
            addp = jnp.where(second, jnp.broadcast_to(p4[:, half - 1:half, :], p4.shape).reshape(rows, width), 0.0)
            adds = jnp.where(second, 0.0, jnp.broadcast_to(s4[:, half:half + 1, :], s4.shape).reshape(rows, width))
        pre.append(p + addp)
        suf.append(s + adds)
    return pre, suf


def _gla_state_in(s4):
    sp = s4.reshape(4 * GLA_DK, 64)
    st = jnp.concatenate([sp, sp], axis=1).T[0:64, :]
    lane_h = lax.broadcasted_iota(jnp.int32, (64, 128), 1) // GLA_DK
    return jnp.concatenate([jnp.where(lane_h == h, st, 0.0) for h in range(4)], axis=0)


def _gla_state_store(sfin_ref, q, s_t):
    m = s_t[0:64] + s_t[64:128] + s_t[128:192] + s_t[192:256]
    nat = jnp.concatenate([m, jnp.zeros_like(m)], axis=0).T
    sfin_ref[q] = nat[:, 0:64].reshape(4, GLA_DK, 64)


def _gla_dir_parts(u_ref, ab_ref, abias_ref, rev):
    la = _log_sigmoid(_dot(u_ref[:, 768:896], ab_ref[...]) + abias_ref[...]) * (1.0 / GLA_GATE_NORM)
    pre, suf = _block_scans(la, 6)
    near, far = (suf, pre) if rev else (pre, suf)
    return dict(la=la, near=near, far=far, b=near[6], q=u_ref[:, 0:128] * float(GLA_DK ** -0.5),
                kg=u_ref[:, 128:256], vg=u_ref[:, 256:512])


def _gla_kernel(*refs, nb, spb, shared):
    if shared:
        u_f, ab_f, abias_f, ab_b, abias_b, s0_f, s0_b, o_f, sfin_f, o_b, sfin_b, s_scr = refs
        u_b = u_f
    else:
        u_f, u_b, ab_f, abias_f, ab_b, abias_b, s0_f, s0_b, o_f, sfin_f, o_b, sfin_b, s_scr = refs
    j = pl.program_id(1)

    if spb == 1:
        @pl.when(j == 0)
        def _():
            s_scr[0] = _gla_state_in(s0_f[0])
            s_scr[1] = _gla_state_in(s0_b[0])

    parts = (_gla_dir_parts(u_f, ab_f, abias_f, False), _gla_dir_parts(u_b, ab_b, abias_b, True))
    nch = u_f.shape[0] // CHUNK
    n3 = 2 * nch

    def cat3(xs):
        return jnp.concatenate([x.reshape(nch, CHUNK, x.shape[1]) for x in xs], axis=0)

    mshape = (2, 1, CHUNK, MIX_W)
    t_i = lax.broadcasted_iota(jnp.int32, mshape, 2)
    s_i = lax.broadcasted_iota(jnp.int32, mshape, 3) & (CHUNK - 1)
    sign = jnp.where(lax.broadcasted_iota(jnp.int32, mshape, 0) >= 1, -1, 1)

    def select(mask, x, other):
        return jnp.where(mask, x.reshape(2, nch, CHUNK, MIX_W),
                         other.reshape(2, nch, CHUNK, MIX_W)).reshape(n3, CHUNK, MIX_W)

    q3 = cat3([p["q"] for p in parts])
    k3 = cat3([p["kg"] for p in parts])
    v3 = cat3([p["vg"] for p in parts])
    att = select(s_i == t_i, _bmm_nt(q3, _rows_bd4(k3.astype(BF16), GLA_DK)), jnp.zeros((n3, CHUNK, MIX_W), F32))
    for lev in range(1, 7):
        size, half = 1 << lev, 1 << (lev - 1)
        same = (t_i // size) == (s_i // size)
        t_late = jnp.where((t_i & (size - 1)) >= half, 1, 0)
        s_late = jnp.where((s_i & (size - 1)) >= half, 1, 0)
        mask = same & ((t_late - s_late) * sign == 1)
        qe = cat3([p["q"] * jnp.exp(p["near"][lev - 1]) for p in parts])
        ke = cat3([p["kg"] * jnp.exp(p["far"][lev - 1] - p["la"]) for p in parts])
        att = select(mask, _bmm_nt(qe, _rows_bd4(ke.astype(BF16), GLA_DK)), att)
    o_intra = _bmm(att, _rows_bd4(v3.astype(BF16), 64))

    b3 = cat3([p["b"] for p in parts])
    b_last = jnp.concatenate([b3[:nch, CHUNK - 1:CHUNK, :], b3[nch:, 0:1, :]], axis=0)
    qb = q3 * jnp.exp(b3)
    bd_r = lax.broadcasted_iota(jnp.int32, (1, MIX_W, 128), 1) // 64
    bd_c = lax.broadcasted_iota(jnp.int32, (1, MIX_W, 128), 2) // GLA_DK
    inc = jnp.where(bd_r == bd_c, _bmm_tn(v3, k3 * jnp.exp(b_last - b3)), 0.0)
    decay = jnp.exp(b_last)

    cps = nch // spb
    dirs = ((0, False, s0_f, o_f, sfin_f), (1, True, s0_b, o_b, sfin_b))
    s_cur = [s_scr[0], s_scr[1]] if spb == 1 else [None, None]
    for ci in range(nch):
        for d, rev, s0_ref, o_ref, sfin_ref in dirs:
            c = (nch - 1 - ci) if rev else ci
            qs, ci_seq = divmod(ci, cps)
            qs = (spb - 1 - qs) if rev else qs
            if spb > 1 and ci_seq == 0:
                s_cur[d] = _gla_state_in(s0_ref[qs])
            g = d * nch + c
            o_ref[c * CHUNK:(c + 1) * CHUNK, :] = o_intra[g] + _dot_nt(qb[g], s_cur[d])
            s_cur[d] = s_cur[d] * decay[g] + inc[g]
            if spb > 1 and ci_seq == cps - 1:
                _gla_state_store(sfin_ref, qs, s_cur[d])
    if spb == 1:
        s_scr[0] = s_cur[0]
        s_scr[1] = s_cur[1]

        @pl.when(j == nb - 1)
        def _():
            _gla_state_store(sfin_f, 0, s_scr[0])
            _gla_state_store(sfin_b, 0, s_scr[1])


def _gla_scan(u_g, prm, s0, nseq, seq_len):
    spb, tb, nb = _scan_blocking(nseq, seq_len)
    shared = nb == 1
    t = u_g.shape[0]
    fwd_map = lambda b, j: (b * nb + j, 0)
    bwd_map = lambda b, j: (b * nb + nb - 1 - j, 0)
    st = pl.BlockSpec((spb, 4, GLA_DK, 64), lambda b, j: (b, 0, 0, 0))
    in_specs = [pl.BlockSpec((tb, 896), fwd_map)]
    operands = [u_g]
    if not shared:
        in_specs.append(pl.BlockSpec((tb, 896), bwd_map))
        operands.append(u_g)
    in_specs += [_full((128, 128)), _full((1, 128))] * 2 + [st, st]
    operands += [prm["gla_ab0"], prm["gla_abias0"], prm["gla_ab1"], prm["gla_abias1"], s0[0], s0[1]]
    out_tok = jax.ShapeDtypeStruct((t, MIX_W), F32)
    out_st = jax.ShapeDtypeStruct((nseq, 4, GLA_DK, 64), F32)
    return pl.pallas_call(
        functools.partial(_gla_kernel, nb=nb, spb=spb, shared=shared),
        grid=(nseq // spb, nb),
        in_specs=in_specs,
        out_specs=[pl.BlockSpec((tb, MIX_W), fwd_map), st, pl.BlockSpec((tb, MIX_W), bwd_map), st],
        out_shape=[out_tok, out_st, out_tok, out_st],
        scratch_shapes=[pltpu.VMEM((2, 256, 128), F32)],
        compiler_params=_cparams(("parallel", "arbitrary")),
        name="gla_scan",
    )(*operands)


def _merge_kernel(x_ref, mod_ref, g_ref, ya_ref, yb_ref, of_ref, ob_ref, exf_ref, exb_ref, cag_ref,
                  gf_ref, gb_ref, gout_ref, gn_ref, bg_ref, gnorm_ref, wg_ref, wb_ref, wo_ref, x1_ref):
    mod = mod_ref[0]
    x = x_ref[...]
    h = _rms_mod(x, g_ref[...], mod[:, D_MODEL:2 * D_MODEL], mod[:, 0:D_MODEL]).astype(BF16)
    mean64 = _seg_ones(MIX_W, 64, 1.0 / 64.0)
    o = of_ref[...] + ob_ref[...]
    mu = _dot_exact_rhs(o, mean64)
    oc = o - mu
    var = _dot_exact_rhs(oc * oc, mean64)
    gate_c = _dot(jax.nn.sigmoid(cag_ref[...]), bg_ref[...])
    y_c = (oc * lax.rsqrt(var + RWKV_GN_EPS) * gn_ref[...] + exf_ref[...] + exb_ref[...]) * gate_c
    og = gf_ref[...] + gb_ref[...]
    gout = gout_ref[...]
    y_d = (og * lax.rsqrt(_dot_exact_rhs(og * og, mean64) + EPS) * gnorm_ref[...]
           * (gout * jax.nn.sigmoid(gout)))
    ys = (ya_ref[...], yb_ref[...], y_c, y_d)
    merged = jnp.zeros((TM, D_MODEL), F32)
    for i in range(4):
        g0 = P_MIX + i * D_MODEL
        gate = jax.nn.sigmoid(_dot_nt(h, wg_ref[g0:g0 + D_MODEL, :]))
        merged = merged + gate * _dot(ys[i], wb_ref[i])
    out = _dot(merged, wo_ref[...])
    x1_ref[...] = x + mod[:, 2 * D_MODEL:3 * D_MODEL] * out


def _merge(x, mod3, g, y_a, y_b, o_f, o_b, ex_f, ex_b, u_r, g_f, g_b, u_g, prm, big, layer,
           cond_base, rows_per_cond):
    t = x.shape[0]
    row = lambda i: (i, 0)
    mix = pl.BlockSpec((TM, MIX_W), row)
    vec = _full((1, MIX_W))
    lay3 = lambda i: (layer, 0, 0)
    return pl.pallas_call(
        _merge_kernel,
        grid=(t // TM,),
        in_specs=[pl.BlockSpec((TM, D_MODEL), row),
                  pl.BlockSpec((1, 1, 6 * D_MODEL), _cond_map(cond_base, rows_per_cond)),
                  _full((1, D_MODEL)),
                  mix, mix, mix, mix, mix, mix,
                  pl.BlockSpec((TM, 128), lambda i: (i, 7)),
                  mix, mix,
                  pl.BlockSpec((TM, MIX_W), lambda i: (i, 2)),
                  vec, _full((128, MIX_W)), vec,
                  pl.BlockSpec((None, big["w_in_t"].shape[1], D_MODEL), lay3, pipeline_mode=pl.Buffered(1)),
                  pl.BlockSpec((None, 4, MIX_W, D_MODEL), lambda i: (layer, 0, 0, 0)),
                  pl.BlockSpec((None, D_MODEL, D_MODEL), lay3)],
        out_specs=pl.BlockSpec((TM, D_MODEL), row),
        out_shape=jax.ShapeDtypeStruct((t, D_MODEL), F32),
        compiler_params=_cparams(("parallel",)),
        name="merge_out",
    )(x, mod3, g, y_a, y_b, o_f, o_b, ex_f, ex_b, u_r, g_f, g_b, u_g,
      prm["gn"], prm["bg"], prm["gla_norm"], big["w_in_t"], big["w_branch"], big["w_out"])


def _mlp_kernel(x_ref, mod_ref, g_ref, w1_ref, w2_ref, fg_ref, x2_ref, y_ref):
    mod = mod_ref[0]
    x = x_ref[...]
    h = _rms_mod(x, g_ref[...], mod[:, 4 * D_MODEL:5 * D_MODEL], mod[:, 3 * D_MODEL:4 * D_MODEL]).astype(BF16)
    ff = jnp.zeros((TM, D_MODEL), F32)
    for c in range(D_FF // D_MODEL):
        cols = slice(c * D_MODEL, (c + 1) * D_MODEL)
        a = jnp.maximum(jnp.dot(h, w1_ref[:, cols], preferred_element_type=F32), 0.0)
        ff = ff + _dot(a * a, w2_ref[cols, :])
    x2 = x + mod[:, 5 * D_MODEL:6 * D_MODEL] * ff
    x2_ref[...] = x2
    y_ref[...] = x2 * lax.rsqrt(jnp.mean(x2 * x2, axis=-1, keepdims=True) + EPS) * fg_ref[...]


def _mlp(x1, mod3, g, w1, w2, layer, final_g, cond_base, rows_per_cond):
    t = x1.shape[0]
    row = lambda i: (i, 0)
    lay3 = lambda i: (layer, 0, 0)
    return pl.pallas_call(
        _mlp_kernel,
        grid=(t // TM,),
        in_specs=[pl.BlockSpec((TM, D_MODEL), row),
                  pl.BlockSpec((1, 1, 6 * D_MODEL), _cond_map(cond_base, rows_per_cond)),
                  _full((1, D_MODEL)), pl.BlockSpec((None, D_MODEL, D_FF), lay3),
                  pl.BlockSpec((None, D_FF, D_MODEL), lay3), _full((1, D_MODEL))],
        out_specs=[pl.BlockSpec((TM, D_MODEL), row), pl.BlockSpec((TM, D_MODEL), row)],
        out_shape=[jax.ShapeDtypeStruct((t, D_MODEL), F32), jax.ShapeDtypeStruct((t, D_MODEL), F32)],
        compiler_params=_cparams(("parallel",)),
        name="mlp",
    )(x1, mod3, g, w1, w2, final_g)


def _pos_tables(n_tok, d):
    rows = n_tok // GRID_W
    assert rows == GRID_W
    quarter = d // 4
    omega = 1.0 / (POS_BASE ** (jnp.arange(quarter, dtype=F32) / quarter))
    ar = jnp.arange(rows, dtype=F32)[:, None] * omega
    ac = jnp.arange(GRID_W, dtype=F32)[:, None] * omega
    return (jnp.concatenate([jnp.sin(ar), jnp.cos(ar)], axis=-1),
            jnp.concatenate([jnp.sin(ac), jnp.cos(ac)], axis=-1))


def _pad_rows(m, start, total):
    return jnp.zeros((total, m.shape[1]), m.dtype).at[start:start + m.shape[0]].set(m)


def _layer_params(l, p):
    out = {
        "n1": p["norm1_g"][l].reshape(1, D_MODEL),
        "n2": p["norm2_g"][l].reshape(1, D_MODEL),
        "pool_w": jnp.einsum("gcd,gh->gchd", p["pool_w"][l], jnp.eye(4, dtype=F32)).reshape(MIX_W, MIX_W).astype(BF16),
        "pool_scale": p["pool_scale"][l].reshape(1, MIX_W),
        "gn": p["rwkv_gn"][l].reshape(1, MIX_W),
        "bg": _pad_rows(p["rwkv_bg"][l], 64, 128).astype(BF16),
        "gla_norm": p["gla_norm"][l].reshape(1, MIX_W),
    }
    for d in range(2):
        out["rwkv%d" % d] = {
            "mu": p["rwkv_mu"][l].reshape(1, 768),
            "kk": p["rwkv_kk"][l].reshape(1, MIX_W),
            "ka": p["rwkv_ka"][l].reshape(1, MIX_W),
            "rk": p["rwkv_rk"][l].reshape(1, MIX_W),
            "w0": p["rwkv_w0"][l, d].reshape(1, MIX_W),
            "a0": p["rwkv_a0"][l, d].reshape(1, MIX_W),
            "bw": _pad_rows(p["rwkv_bw"][l, d], 64 * d, 128).astype(BF16),
            "ba": _pad_rows(p["rwkv_ba"][l, d], 32 * d, 128).astype(BF16),
        }
        out["gla_ab%d" % d] = _pad_rows(p["gla_ab"][l, d], 16 * d, 128).astype(BF16)
        out["gla_abias%d" % d] = p["gla_abias"][l, d].reshape(1, 128)
    return out


def _run_layer(x, mod3, prm, big, layer, final_g, s_rwkv0, s_gla0, nseq, seq_len, cond_base, rows_per_cond,
               pos_tables=None):
    if pos_tables is None:
        u_p, u_f, u_r, u_g = _inproj(x, mod3, prm["n1"], big["w_in_t"], layer, cond_base, rows_per_cond)
    else:
        x, u_p, u_f, u_r, u_g = _inproj(x, mod3, prm["n1"], big["w_in_t"], layer, cond_base, rows_per_cond,
                                        pos_tables, seq_len)
    y_a = _pool(u_p, prm["pool_w"], prm["pool_scale"], nseq, seq_len)
    if seq_len == FFT_N1 * FFT_N1:
        y_b = _four_large(u_f, nseq, seq_len)
    else:
        y_b = _four_small(u_f, nseq, seq_len)
    o_f, ex_f, sr_f, o_b, ex_b, sr_b = _rwkv_scan(u_r, prm, s_rwkv0, nseq, seq_len)
    g_f, sg_f, g_b, sg_b = _gla_scan(u_g, prm, s_gla0, nseq, seq_len)
    x1 = _merge(x, mod3, prm["n1"], y_a, y_b, o_f, o_b, ex_f, ex_b, u_r, g_f, g_b, u_g, prm, big, layer,
                cond_base, rows_per_cond)
    x2, y = _mlp(x1, mod3, prm["n2"], big["w1"], big["w2"], layer, final_g, cond_base, rows_per_cond)
    return x2, y, jnp.stack([sr_f, sr_b], axis=1), jnp.stack([sg_f, sg_b], axis=1)


def kernel(x_prompt, x_sample, state_rwkv, state_gla, c, c_ctx, ada_w, ada_b, norm1_g, norm2_g, w_in, pool_w, pool_scale, rwkv_mu, rwkv_w0, rwkv_bw, rwkv_a0, rwkv_ba, rwkv_kk, rwkv_ka, rwkv_bg, rwkv_rk, rwkv_gn, gla_ab, gla_abias, gla_norm, w_branch, w_out, mlp_w1, mlp_w2, final_g):
    p = dict(pool_w=pool_w, pool_scale=pool_scale, rwkv_mu=rwkv_mu, rwkv_w0=rwkv_w0,
             rwkv_bw=rwkv_bw, rwkv_a0=rwkv_a0, rwkv_ba=rwkv_ba, rwkv_kk=rwkv_kk, rwkv_ka=rwkv_ka,
             rwkv_bg=rwkv_bg, rwkv_rk=rwkv_rk, rwkv_gn=rwkv_gn, gla_ab=gla_ab, gla_abias=gla_abias,
             gla_norm=gla_norm, norm1_g=norm1_g, norm2_g=norm2_g)
    big = dict(w_in_t=jnp.swapaxes(w_in, 1, 2).astype(BF16),
               w_branch=w_branch.astype(BF16), w_out=w_out.astype(BF16),
               w1=mlp_w1.astype(BF16), w2=mlp_w2.astype(BF16))
    bp, lp, _ = x_prompt.shape
    bs, ls, _ = x_sample.shape
    cond8 = jnp.zeros((8, D_MODEL), F32).at[0].set(c_ctx).at[1:1 + bs].set(c)
    mods = _ada(cond8, ada_w, ada_b)
    fg = final_g.reshape(1, D_MODEL)

    xp = x_prompt.reshape(bp * lp, D_MODEL)
    xs = x_sample.reshape(bs * ls, D_MODEL)
    pos_tables = _pos_tables(ls, D_MODEL)
    zr = jnp.zeros((2, bp) + state_rwkv.shape[3:], F32)
    zg = jnp.zeros((2, bp) + state_gla.shape[3:], F32)
    new_r, new_g = [], []
    for l in range(DEPTH):
        prm = _layer_params(l, p)
        mod3 = mods[l].reshape(8, 1, 6 * D_MODEL)
        xp, yp, s_r, s_g = _run_layer(xp, mod3, prm, big, l, fg, zr, zg, bp, lp, 0, bp * lp)
        new_r.append(s_r)
        new_g.append(s_g)
        xs, ys, _, _ = _run_layer(xs, mod3, prm, big, l, fg, jnp.swapaxes(state_rwkv[:, l], 0, 1),
                                  jnp.swapaxes(state_gla[:, l], 0, 1), bs, ls, 1, ls,
                                  pos_tables if l == 0 else None)
    y_prompt = yp.reshape(bp, lp, D_MODEL)
    y_sample = ys.reshape(bs, ls, D_MODEL)
    return (y_prompt, y_sample, jnp.stack(new_r, axis=1), jnp.stack(new_g, axis=1))
```

```python
import functools

import numpy as np
import jax
import jax.numpy as jnp
from jax import lax
from jax.experimental import pallas as pl
from jax.experimental.pallas import tpu as pltpu

F32 = jnp.float32
BF16 = jnp.bfloat16

D_MODEL = 1024
DEPTH = 2
GRID_W = 64
POS_BASE = 10000.0
MIX_W = 256
POOL_WINDOWS = (2, 4, 8, 16)
POOL_HALO = 8
RWKV_DECAY_SCALE = 0.606531
RWKV_GN_EPS = 64e-5
GLA_DK = 32
GLA_GATE_NORM = 16.0
D_FF = 4 * D_MODEL
EPS = 1e-6
P_MIX = 2336
P_MIX_PAD = 2432
CHUNK = 64
SUB = 16

TM = 512
POOL_TB = 1024
RWKV_TB = 512
GLA_TB = 1024
FFT_N1 = 64
VMEM_LIMIT = 56 * 1024 * 1024


def _cparams(sem):
    return pltpu.CompilerParams(dimension_semantics=sem, vmem_limit_bytes=VMEM_LIMIT)


def _full(shape):
    n = len(shape)
    return pl.BlockSpec(shape, lambda *_: (0,) * n)


def _dot(a, b):
    return jnp.dot(a.astype(BF16), b.astype(BF16), preferred_element_type=F32)


def _dot_nt(a, b):
    return lax.dot_general(a.astype(BF16), b.astype(BF16), (((1,), (1,)), ((), ())),
                           preferred_element_type=F32)


def _split2(a):
    hi = a.astype(BF16)
    lo = (a - hi.astype(F32)).astype(BF16)
    return hi, lo


def _dot_exact_rhs(a, w):
    hi, lo = _split2(a)
    return (jnp.dot(hi, w, preferred_element_type=F32) + jnp.dot(lo, w, preferred_element_type=F32))


def _dot_exact_lhs(w, a):
    hi, lo = _split2(a)
    return jnp.dot(w, hi, preferred_element_type=F32) + jnp.dot(w, lo, preferred_element_type=F32)


def _seg_ones(n, seg, scale=1.0):
    r = lax.broadcasted_iota(jnp.int32, (n, n), 0)
    c = lax.broadcasted_iota(jnp.int32, (n, n), 1)
    return jnp.where((r // seg) == (c // seg), scale, 0.0).astype(BF16)


def _log_sigmoid(x):
    return jnp.minimum(x, 0.0) - jnp.log(1.0 + jnp.exp(-jnp.abs(x)))


def _rms_mod(x, g, sc, sh):
    ms = jnp.mean(x * x, axis=-1, keepdims=True)
    return x * lax.rsqrt(ms + EPS) * g * (1.0 + sc) + sh


def _ada_kernel(c_ref, w_ref, b_ref, o_ref):
    c = c_ref[...]
    s = c * jax.nn.sigmoid(c)
    hi, lo = _split2(s)
    w = w_ref[0].astype(BF16)
    o_ref[0] = (jnp.dot(hi, w, preferred_element_type=F32) + jnp.dot(lo, w, preferred_element_type=F32)
                + b_ref[0])


def _ada(cond8, ada_w, ada_b):
    tn = 1536
    n = 6 * D_MODEL
    return pl.pallas_call(
        _ada_kernel,
        grid=(DEPTH, n // tn),
        in_specs=[_full((8, D_MODEL)),
                  pl.BlockSpec((1, D_MODEL, tn), lambda l, j: (l, 0, j)),
                  pl.BlockSpec((1, 1, tn), lambda l, j: (l, 0, j))],
        out_specs=pl.BlockSpec((1, 8, tn), lambda l, j: (l, 0, j)),
        out_shape=jax.ShapeDtypeStruct((DEPTH, 8, n), F32),
        compiler_params=_cparams(("parallel", "parallel")),
        name="ada_mod",
    )(cond8, ada_w, ada_b.reshape(DEPTH, 1, n))


def _inproj_kernel(x_ref, mod_ref, g_ref, w_ref, up_ref, uf_ref, ur_ref, ug_ref):
    _inproj_body(x_ref[...], mod_ref, g_ref, w_ref, up_ref, uf_ref, ur_ref, ug_ref)


def _inproj_pos_kernel(x_ref, rr_ref, cc_ref, mod_ref, g_ref, w_ref, xo_ref, up_ref, uf_ref, ur_ref, ug_ref,
                       *, seq_len):
    nrow = TM // GRID_W
    g0 = pl.multiple_of(((pl.program_id(0) * TM) % seq_len) // GRID_W, nrow)
    half = D_MODEL // 2
    rpart = jnp.broadcast_to(rr_ref[pl.ds(g0, nrow), :][:, None, :], (nrow, GRID_W, half)).reshape(TM, half)
    cpart = jnp.concatenate([cc_ref[...]] * nrow, axis=0)
    x = x_ref[...] + jnp.concatenate([rpart, cpart], axis=1)
    xo_ref[...] = x
    _inproj_body(x, mod_ref, g_ref, w_ref, up_ref, uf_ref, ur_ref, ug_ref)


def _inproj_body(x, mod_ref, g_ref, w_ref, up_ref, uf_ref, ur_ref, ug_ref):
    mod = mod_ref[0]
    h = _rms_mod(x, g_ref[...], mod[:, D_MODEL:2 * D_MODEL], mod[:, 0:D_MODEL]).astype(BF16)
    up_ref[...] = _dot_nt(h, w_ref[0:256, :])
    uf_ref[...] = _dot_nt(h, w_ref[256:512, :])
    ur_ref[...] = _dot_nt(h, w_ref[512:1536, :])
    ug_ref[...] = _dot_nt(h, w_ref[1536:P_MIX_PAD, :])


def _cond_map(cond_base, rows_per_cond):
    return lambda i: (cond_base + (i * TM) // rows_per_cond, 0, 0)


def _inproj(x, mod3, g, w_t, layer, cond_base, rows_per_cond, pos_tables=None, seq_len=None):
    t = x.shape[0]
    row = lambda i: (i, 0)
    in_specs = [pl.BlockSpec((1, 1, 6 * D_MODEL), _cond_map(cond_base, rows_per_cond)),
                _full((1, D_MODEL)),
                pl.BlockSpec((None, P_MIX_PAD, D_MODEL), lambda i: (layer, 0, 0))]
    out_specs = [pl.BlockSpec((TM, 256), row), pl.BlockSpec((TM, 256), row),
                 pl.BlockSpec((TM, 1024), row), pl.BlockSpec((TM, 896), row)]
    out_shape = [jax.ShapeDtypeStruct((t, 256), F32), jax.ShapeDtypeStruct((t, 256), F32),
                 jax.ShapeDtypeStruct((t, 1024), F32), jax.ShapeDtypeStruct((t, 896), F32)]
    x_spec = pl.BlockSpec((TM, D_MODEL), row)
    if pos_tables is None:
        return pl.pallas_call(
            _inproj_kernel, grid=(t // TM,), in_specs=[x_spec] + in_specs, out_specs=out_specs,
            out_shape=out_shape, compiler_params=_cparams(("parallel",)), name="in_proj",
        )(x, mod3, g, w_t)
    tab = _full((GRID_W, D_MODEL // 2))
    return pl.pallas_call(
        functools.partial(_inproj_pos_kernel, seq_len=seq_len), grid=(t // TM,),
        in_specs=[x_spec, tab, tab] + in_specs, out_specs=[x_spec] + out_specs,
        out_shape=[jax.ShapeDtypeStruct((t, D_MODEL), F32)] + out_shape,
        compiler_params=_cparams(("parallel",)), name="in_proj_pos",
    )(x, pos_tables[0], pos_tables[1], mod3, g, w_t)


def _pool_kernel(prev_ref, cur_ref, next_ref, wbd_ref, scale_ref, y_ref, *, seq_len):
    j = pl.program_id(1)
    tb = cur_ref.shape[0]
    nb = seq_len // tb
    cur = cur_ref[...]
    prev = jnp.where(j > 0, prev_ref[...], 0.0)
    nxt = jnp.where(j < nb - 1, next_ref[...], 0.0)
    ext = jnp.concatenate([prev, cur, nxt], axis=0)
    n_ext = tb + 2 * POOL_HALO

    def shift(x, k):
        return pltpu.roll(x, k % n_ext, 0)

    sums = [ext + shift(ext, 1)]
    for k in (1, 2, 4):
        sums.append(shift(sums[-1], -k) + shift(sums[-1], k))
    assert POOL_WINDOWS == (2, 4, 8, 16)
    pos = j * tb + lax.broadcasted_iota(jnp.int32, (tb, MIX_W), 0)
    group = lax.broadcasted_iota(jnp.int32, (tb, MIX_W), 1) // (MIX_W // len(POOL_WINDOWS))
    pooled = jnp.zeros((tb, MIX_W), F32)
    for gi, win in enumerate(POOL_WINDOWS):
        before, after = win // 2, win - win // 2 - 1
        s = sums[gi][POOL_HALO:POOL_HALO + tb]
        cnt = (jnp.minimum(pos + after, seq_len - 1) - jnp.maximum(pos - before, 0) + 1).astype(F32)
        pooled = jnp.where(group == gi, s / cnt - cur, pooled)
    y_ref[...] = _dot(pooled, wbd_ref[...]) * scale_ref[...]


def _pool(u_p, wbd, scale, nseq, seq_len):
    tb = min(seq_len, POOL_TB)
    nb = seq_len // tb
    hb = tb // POOL_HALO
    nh = u_p.shape[0] // POOL_HALO
    return pl.pallas_call(
        functools.partial(_pool_kernel, seq_len=seq_len),
        grid=(nseq, nb),
        in_specs=[pl.BlockSpec((POOL_HALO, MIX_W), lambda b, j: (jnp.maximum((b * nb + j) * hb - 1, 0), 0)),
                  pl.BlockSpec((tb, MIX_W), lambda b, j: (b * nb + j, 0)),
                  pl.BlockSpec((POOL_HALO, MIX_W), lambda b, j: (jnp.minimum((b * nb + j + 1) * hb, nh - 1), 0)),
                  _full((MIX_W, MIX_W)), _full((1, MIX_W))],
        out_specs=pl.BlockSpec((tb, MIX_W), lambda b, j: (b * nb + j, 0)),
        out_shape=jax.ShapeDtypeStruct(u_p.shape, F32),
        compiler_params=_cparams(("parallel", "parallel")),
        name="pool_mixer",
    )(u_p, u_p, u_p, wbd, scale)


def _dft_tables(n):
    t = np.arange(n)
    ang = 2.0 * np.pi * ((np.outer(t, t)) % n) / n
    return np.cos(ang), np.sin(ang)


def _bd_np(m, copies):
    k = m.shape[0]
    out = np.zeros((k * copies, k * copies), m.dtype)
    for i in range(copies):
        out[i * k:(i + 1) * k, i * k:(i + 1) * k] = m
    return out


def _four_small_kernel(z_ref, cs_ref, cbd_ref, sbd_ref, y_ref, *, seq_len, norm):
    zf = _dot(cs_ref[...], z_ref[...])
    y = _dot(zf[:seq_len], cbd_ref[...]) + _dot(zf[seq_len:], sbd_ref[...])
    y_ref[...] = y * norm


def _four_small(u_f, nseq, seq_len):
    c, s = _dft_tables(seq_len)
    cs = jnp.asarray(np.concatenate([c, -s], axis=0), F32)
    c64, s64 = _dft_tables(MIX_W // 4)
    cbd = jnp.asarray(_bd_np(c64, 4), F32)
    sbd = jnp.asarray(_bd_np(s64, 4), F32)
    return pl.pallas_call(
        functools.partial(_four_small_kernel, seq_len=seq_len, norm=float((seq_len * 64) ** -0.5)),
        grid=(nseq,),
        in_specs=[pl.BlockSpec((seq_len, MIX_W), lambda b: (b, 0)),
                  _full((2 * seq_len, seq_len)), _full((MIX_W, MIX_W)), _full((MIX_W, MIX_W))],
        out_specs=pl.BlockSpec((seq_len, MIX_W), lambda b: (b, 0)),
        out_shape=jax.ShapeDtypeStruct(u_f.shape, F32),
        compiler_params=_cparams(("parallel",)),
        name="fourier_small",
    )(u_f, cs, cbd, sbd)


def _four_large_kernel(z0_ref, z1_ref, fs_ref, g_ref, cbd_ref, sbd_ref, y_ref, a0_scr, a1_scr, xr_scr, xi_scr,
                       y0_scr, y1_scr, *, norm):
    n1 = FFT_N1
    fs = fs_ref[...].astype(BF16)

    def rows(ref0, ref1, start, stride):
        sl = pl.ds(start, n1, stride=stride)
        return jnp.concatenate([ref0[sl, :], ref1[sl, :]], axis=1)

    for t2 in range(n1):
        a = _dot(fs, rows(z0_ref, z1_ref, t2, n1))
        a0_scr[t2 * 2 * n1:(t2 + 1) * 2 * n1, :] = a[:, :128]
        a1_scr[t2 * 2 * n1:(t2 + 1) * 2 * n1, :] = a[:, 128:]
    for f1 in range(n1):
        a = jnp.concatenate([rows(a0_scr, a1_scr, f1, 2 * n1), rows(a0_scr, a1_scr, n1 + f1, 2 * n1)],
                            axis=0)
        x = _dot(g_ref[f1], a)
        xr_scr[f1 * n1:(f1 + 1) * n1, :] = x[:n1]
        xi_scr[f1 * n1:(f1 + 1) * n1, :] = x[n1:]
    y = (_dot(xr_scr[...], cbd_ref[...]) + _dot(xi_scr[...], sbd_ref[...])) * norm
    y0_scr[...] = y[:, :128]
    y1_scr[...] = y[:, 128:]
    for f2 in range(n1):
        y_ref[f2 * n1:(f2 + 1) * n1, :] = rows(y0_scr, y1_scr, f2, n1)


def _four_large(u_f, nseq, seq_len):
    n1 = FFT_N1
    assert seq_len == n1 * n1
    c64, s64 = _dft_tables(n1)
    fs = jnp.asarray(np.concatenate([c64, -s64], axis=0), F32)
    f1 = np.arange(n1)[:, None, None]
    f2 = np.arange(n1)[None, :, None]
    t2 = np.arange(n1)[None, None, :]
    ang = 2.0 * np.pi * ((t2 * (f1 + n1 * f2)) % seq_len) / seq_len
    gr, gi = np.cos(ang), -np.sin(ang)
    g = jnp.asarray(np.concatenate([np.concatenate([gr, -gi], axis=2),
                                    np.concatenate([gi, gr], axis=2)], axis=1), F32)
    cbd = jnp.asarray(_bd_np(c64, 4), F32)
    sbd = jnp.asarray(_bd_np(s64, 4), F32)
    half = MIX_W // 2
    return pl.pallas_call(
        functools.partial(_four_large_kernel, norm=float((seq_len * 64) ** -0.5)),
        grid=(nseq,),
        in_specs=[pl.BlockSpec((seq_len, half), lambda b: (b, 0)), pl.BlockSpec((seq_len, half), lambda b: (b, 1)),
                  _full((2 * n1, n1)), _full((n1, 2 * n1, 2 * n1)), _full((MIX_W, MIX_W)), _full((MIX_W, MIX_W))],
        out_specs=pl.BlockSpec((seq_len, MIX_W), lambda b: (b, 0)),
        out_shape=jax.ShapeDtypeStruct(u_f.shape, F32),
        scratch_shapes=[pltpu.VMEM((2 * seq_len, half), F32), pltpu.VMEM((2 * seq_len, half), F32),
                        pltpu.VMEM((seq_len, MIX_W), F32), pltpu.VMEM((seq_len, MIX_W), F32),
                        pltpu.VMEM((seq_len, half), F32), pltpu.VMEM((seq_len, half), F32)],
        compiler_params=_cparams(("parallel",)),
        name="fourier_large",
    )(u_f, u_f, fs, g, cbd, sbd)


def _bmm(x, y):
    return lax.dot_general(x.astype(BF16), y.astype(BF16), (((2,), (1,)), ((0,), (0,))),
                           preferred_element_type=F32)


def _bmm_nt(x, y):
    return lax.dot_general(x.astype(BF16), y.astype(BF16), (((2,), (2,)), ((0,), (0,))),
                           preferred_element_type=F32)


def _bmm_tn(x, y):
    return lax.dot_general(x.astype(BF16), y.astype(BF16), (((1,), (1,)), ((0,), (0,))),
                           preferred_element_type=F32)


def _slab_bd(y):
    lane = lax.broadcasted_iota(jnp.int32, y.shape, 2)
    z = jnp.zeros_like(y)
    return jnp.concatenate([jnp.where(lane < 64, y, z), jnp.where(lane >= 64, y, z)], axis=1)


def _mm(x, y):
    return _bmm(x, _slab_bd(y.astype(BF16)))


def _mm_pair(x, y1, y2):
    w = jnp.concatenate([_slab_bd(y1.astype(BF16)), _slab_bd(y2.astype(BF16))], axis=2)
    out = _bmm(x, w)
    return out[:, :, :128], out[:, :, 128:]


def _mm3(xs, y):
    yh, yl = _split2(y)
    bh, bl = _slab_bd(yh), _slab_bd(yl)
    parts = [_split2(x) for x in xs]
    xh = jnp.concatenate([p[0] for p in parts], axis=1)
    xl = jnp.concatenate([p[1] for p in parts], axis=1)
    hx = _bmm(xh, jnp.concatenate([bh, bl], axis=2))
    out = hx[:, :, :128] + (hx[:, :, 128:] + _bmm(xl, bh))
    return [out[:, i * CHUNK:(i + 1) * CHUNK] for i in range(len(xs))]


def _to_slabs(x):
    nch = x.shape[0] // CHUNK
    x4 = x.reshape(nch, CHUNK, MIX_W)
    return jnp.stack([x4[:, :, :128], x4[:, :, 128:]], axis=1).reshape(2 * nch, CHUNK, 128)


def _tri_inverse(n_mat, eye, blk):
    nd = jnp.where(blk, n_mat, 0.0)
    no = n_mat - nd
    n2, = _mm3([nd], nd)
    n4, y = _mm3([n2, nd], n2)
    x = n2 - nd - y
    n8, y = _mm3([n4, x], n4)
    x = x + n4 + y
    y, = _mm3([x], n8)
    x = x + n8 + y
    td = eye + x
    e = _mm(td, no)
    e2 = _mm(e, e)
    e3 = _mm(e, e2)
    return td + _mm(e2 - e - e3, td)


def _rwkv_state_to_bd(s4):
    bd_r = lax.broadcasted_iota(jnp.int32, (128, 128), 0) // 64
    bd_c = lax.broadcasted_iota(jnp.int32, (128, 128), 1) // 64
    pairs = []
    for hp in range(2):
        sp = s4[2 * hp:2 * hp + 2].reshape(128, 64)
        pairs.append(jnp.where(bd_r == bd_c, jnp.concatenate([sp, sp], axis=1), 0.0))
    return jnp.stack(pairs, axis=0)


def _rwkv_state_store(sfin_ref, q, s_bd):
    for hp in range(2):
        sfin_ref[q, 2 * hp] = s_bd[hp, 0:64, 0:64]
        sfin_ref[q, 2 * hp + 1] = s_bd[hp, 64:128, 64:128]


def _rwkv_prep(cur_ref, prev_ref, next_ref, mu_ref, kkw_ref, jj, nb, spb):
    tb = cur_ref.shape[0]
    seq_rows = tb // spb
    z = cur_ref[:, 0:768]
    rowid = lax.broadcasted_iota(jnp.int32, (tb, 768), 0)
    prow = jnp.where(rowid == 0, jnp.where(jj > 0, prev_ref[POOL_HALO - 1:POOL_HALO, :], 0.0), 0.0)
    nrow = jnp.where(rowid == tb - 1, jnp.where(jj < nb - 1, next_ref[0:1, :], 0.0), 0.0)
    in_seq = rowid & (seq_rows - 1)
    up = jnp.where(in_seq == 0, prow, pltpu.roll(z, 1, 0))
    dn = jnp.where(in_seq == seq_rows - 1, nrow, pltpu.roll(z, tb - 1, 0))
    zm = z + mu_ref[...] * (0.5 * (up + dn) - z)
    r = zm[:, 0:256]
    k = zm[:, 256:512]
    v = zm[:, 512:768]
    kk = k * kkw_ref[...]
    kk = kk * lax.rsqrt(_dot(kk * kk, _seg_ones(MIX_W, 64)) + EPS)
    return r, k, v, kk


def _rwkv_dir_slabs(cur_ref, r, k, v, kk, ka_ref, rk_ref, w0_ref, a0_ref, bw_ref, ba_ref, ex_ref, rev):
    tb = cur_ref.shape[0]
    w_log = w0_ref[...] + _dot(jnp.tanh(cur_ref[:, 768:896]), bw_ref[...])
    lw = -RWKV_DECAY_SCALE * jax.nn.sigmoid(w_log)
    a = jax.nn.sigmoid(a0_ref[...] + _dot(cur_ref[:, 896:1024], ba_ref[...]))
    kd = k * (1.0 + (a - 1.0) * ka_ref[...])
    ex_ref[...] = _dot(r * kd * rk_ref[...], _seg_ones(MIX_W, 64)) * v
    tr = lax.broadcasted_iota(jnp.int32, (tb, tb), 0)
    tc = lax.broadcasted_iota(jnp.int32, (tb, tb), 1)
    same_chunk = (tr // CHUNK) == (tc // CHUNK)
    tri = jnp.where(same_chunk & ((tc >= tr) if rev else (tc <= tr)), 1.0, 0.0).astype(BF16)
    cum = _dot_exact_lhs(tri, lw)
    e_in = jnp.exp(cum)
    e_out = jnp.exp(-cum)
    last = 0 if rev else CHUNK - 1
    return dict(rt=_to_slabs(r * e_in), kh=_to_slabs((kd * e_out).astype(BF16)),
                bh=_to_slabs((kk * a * e_out).astype(BF16)),
                kt=_to_slabs((kk * jnp.exp(cum - lw)).astype(BF16)), vv=_to_slabs(v.astype(BF16)),
                gam=_to_slabs(e_in)[:, last:last + 1, :])


def _rwkv_kernel(*refs, nb, spb, shared):
    if shared:
        (cur_f, prev_f, next_f, mu_ref, kkw_ref, ka_ref, rk_ref, w0_f, a0_f, bw_f, ba_f, w0_b, a0_b, bw_b, ba_b,
         s0_f, s0_b, o_f, ex_f, sfin_f, o_b, ex_b, sfin_b, s_scr) = refs
        cur_b = cur_f
    else:
        (cur_f, prev_f, next_f, cur_b, prev_b, next_b, mu_ref, kkw_ref, ka_ref, rk_ref, w0_f, a0_f, bw_f, ba_f,
         w0_b, a0_b, bw_b, ba_b, s0_f, s0_b, o_f, ex_f, sfin_f, o_b, ex_b, sfin_b, s_scr) = refs
    j = pl.program_id(1)

    if spb == 1:
        @pl.when(j == 0)
        def _():
            s_scr[0] = _rwkv_state_to_bd(s0_f[0])
            s_scr[1] = _rwkv_state_to_bd(s0_b[0])

    pf = _rwkv_prep(cur_f, prev_f, next_f, mu_ref, kkw_ref, j, nb, spb)
    pb = pf if shared else _rwkv_prep(cur_b, prev_b, next_b, mu_ref, kkw_ref, nb - 1 - j, nb, spb)
    sl_f = _rwkv_dir_slabs(cur_f, *pf, ka_ref, rk_ref, w0_f, a0_f, bw_f, ba_f, ex_f, False)
    sl_b = _rwkv_dir_slabs(cur_b, *pb, ka_ref, rk_ref, w0_b, a0_b, bw_b, ba_b, ex_b, True)
    rt, kh, bh, kt, vv, gam = (jnp.concatenate([sl_f[n], sl_b[n]], axis=0)
                               for n in ("rt", "kh", "bh", "kt", "vv", "gam"))
    nch = cur_f.shape[0] // CHUNK
    gdir = 2 * nch
    n_slab = 2 * gdir

    mshape = (2, 1, CHUNK, 128)
    lag = ((lax.broadcasted_iota(jnp.int32, mshape, 2) - (lax.broadcasted_iota(jnp.int32, mshape, 3) & (CHUNK - 1)))
           * jnp.where(lax.broadcasted_iota(jnp.int32, mshape, 0) >= 1, -1, 1))
    strict, incl = lag > 0, lag >= 0

    def masked(mask, x):
        return jnp.where(mask, x.reshape(2, gdir, CHUNK, 128), 0.0).reshape(n_slab, CHUNK, 128)

    t_i = lax.broadcasted_iota(jnp.int32, (1, CHUNK, 128), 1)
    s_i = lax.broadcasted_iota(jnp.int32, (1, CHUNK, 128), 2) & (CHUNK - 1)
    blk = (s_i // SUB) == (t_i // SUB)
    eye = jnp.where(s_i == t_i, 1.0, 0.0)

    lhs = jnp.concatenate([kt, rt.astype(BF16)], axis=1)
    kb_bd = jnp.concatenate([_slab_bd(kh.astype(BF16)), _slab_bd(bh.astype(BF16))], axis=1)
    mkb = _bmm_nt(lhs, kb_bd)
    m_mat = masked(strict, mkb[:, :CHUNK, :128])
    p_mat = masked(incl, mkb[:, CHUNK:, :128])
    n_mat = masked(strict, mkb[:, :CHUNK, 128:])
    q_mat = masked(incl, mkb[:, CHUNK:, 128:])
    t_mat = _tri_inverse(n_mat, eye, blk)
    mpv = _mm(jnp.concatenate([m_mat, p_mat], axis=1), vv)
    mv, pv = mpv[:, :CHUNK], mpv[:, CHUNK:]
    tk, uv = _mm_pair(t_mat, kt, mv)
    qtk, quv = _mm_pair(q_mat, tk, uv)
    rc = rt - qtk
    oc = pv - quv
    bd_r = lax.broadcasted_iota(jnp.int32, (1, 128, 128), 1) // 64
    bd_c = lax.broadcasted_iota(jnp.int32, (1, 128, 128), 2) // 64
    bd_mask = bd_r == bd_c
    d_mat = jnp.where(bd_mask, _bmm_tn(tk, bh), 0.0)
    braw = jnp.where(bd_mask,
                     _bmm_tn(jnp.concatenate([vv, uv.astype(BF16)], axis=1),
                             jnp.concatenate([kh, -bh], axis=1)), 0.0)

    cps = nch // spb
    dirs = ((0, False, s0_f, o_f, sfin_f), (1, True, s0_b, o_b, sfin_b))
    s_cur = [s_scr[0], s_scr[1]] if spb == 1 else [None, None]
    for ci in range(nch):
        for d, rev, s0_ref, o_ref, sfin_ref in dirs:
            c = (nch - 1 - ci) if rev else ci
            q, ci_seq = divmod(ci, cps)
            q = (spb - 1 - q) if rev else q
            if spb > 1 and ci_seq == 0:
                s_cur[d] = _rwkv_state_to_bd(s0_ref[q])
            g = slice(d * gdir + 2 * c, d * gdir + 2 * c + 2)
            o = _bmm_nt(rc[g], s_cur[d]) + oc[g]
            o_ref[c * CHUNK:(c + 1) * CHUNK, 0:128] = o[0]
            o_ref[c * CHUNK:(c + 1) * CHUNK, 128:256] = o[1]
            s_cur[d] = (s_cur[d] - _bmm(s_cur[d], d_mat[g]) + braw[g]) * gam[g]
            if spb > 1 and ci_seq == cps - 1:
                _rwkv_state_store(sfin_ref, q, s_cur[d])
    if spb == 1:
        s_scr[0] = s_cur[0]
        s_scr[1] = s_cur[1]

        @pl.when(j == nb - 1)
        def _():
            _rwkv_state_store(sfin_f, 0, s_scr[0])
            _rwkv_state_store(sfin_b, 0, s_scr[1])


def _scan_blocking(nseq, seq_len, max_rows):
    spb = max(1, min(nseq, max_rows // seq_len))
    assert nseq % spb == 0
    tb = min(seq_len, max_rows) * spb
    return spb, tb, (seq_len * spb) // tb


def _rwkv_scan(u_r, prm, s0, nseq, seq_len):
    spb, tb, nb = _scan_blocking(nseq, seq_len, RWKV_TB)
    shared = nb == 1
    hb = tb // POOL_HALO
    nh = u_r.shape[0] // POOL_HALO
    t = u_r.shape[0]
    vec = _full((1, MIX_W))
    low = _full((128, MIX_W))
    st = pl.BlockSpec((spb, 4, 64, 64), lambda b, j: (b, 0, 0, 0))

    def block_specs(jmap):
        blk = lambda b, j: b * nb + jmap(j)
        return [pl.BlockSpec((tb, 1024), lambda b, j: (blk(b, j), 0)),
                pl.BlockSpec((POOL_HALO, 768), lambda b, j: (jnp.maximum(blk(b, j) * hb - 1, 0), 0)),
                pl.BlockSpec((POOL_HALO, 768), lambda b, j: (jnp.minimum((blk(b, j) + 1) * hb, nh - 1), 0))]

    fwd_map = lambda b, j: (b * nb + j, 0)
    bwd_map = lambda b, j: (b * nb + nb - 1 - j, 0)
    in_specs = block_specs(lambda j: j)
    operands = [u_r, u_r, u_r]
    if not shared:
        in_specs += block_specs(lambda j: nb - 1 - j)
        operands += [u_r, u_r, u_r]
    in_specs += [_full((1, 768)), vec, vec, vec] + [vec, vec, low, low] * 2 + [st, st]
    f, b = prm["rwkv0"], prm["rwkv1"]
    operands += [f["mu"], f["kk"], f["ka"], f["rk"], f["w0"], f["a0"], f["bw"], f["ba"],
                 b["w0"], b["a0"], b["bw"], b["ba"], s0[0], s0[1]]
    out_tok = jax.ShapeDtypeStruct((t, MIX_W), F32)
    out_st = jax.ShapeDtypeStruct((nseq, 4, 64, 64), F32)
    return pl.pallas_call(
        functools.partial(_rwkv_kernel, nb=nb, spb=spb, shared=shared),
        grid=(nseq // spb, nb),
        in_specs=in_specs,
        out_specs=[pl.BlockSpec((tb, MIX_W), fwd_map), pl.BlockSpec((tb, MIX_W), fwd_map), st,
                   pl.BlockSpec((tb, MIX_W), bwd_map), pl.BlockSpec((tb, MIX_W), bwd_map), st],
        out_shape=[out_tok, out_tok, out_st, out_tok, out_tok, out_st],
        scratch_shapes=[pltpu.VMEM((2, 2, 128, 128), F32)],
        compiler_params=_cparams(("parallel", "arbitrary")),
        name="rwkv_scan",
    )(*operands)


def _rows_bd4(y, width):
    lane = lax.broadcasted_iota(jnp.int32, y.shape, 2) // width
    z = jnp.zeros_like(y)
    return jnp.concatenate([jnp.where(lane == h, y, z) for h in range(4)], axis=1)


def _block_scans(x, levels):
    rows, width = x.shape
    row = lax.broadcasted_iota(jnp.int32, (rows, width), 0)
    pre, suf = [x], [x]
    for m in range(1, levels + 1):
        size, half = 1 << m, 1 << (m - 1)
        p, s = pre[-1], suf[-1]
        if half < 8:
            off = row & (size - 1)
            p3 = p.reshape(rows // 8, 8, width)
            s3 = s.reshape(rows // 8, 8, width)
            addp = jnp.zeros_like(x)
            adds = jnp.zeros_like(x)
            for o in range(half):
                addp = jnp.where(off == half + o, pltpu.roll(p3, o + 1, 1).reshape(rows, width), addp)
                adds = jnp.where(off == o, pltpu.roll(s3, 8 - (half - o), 1).reshape(rows, width), adds)
        else:
            p4 = p.reshape(rows // size, size, width)
            s4 = s.reshape(rows // size, size, width)
            second = (row & half) != 0
            addp = jnp.where(second, jnp.broadcast_to(p4[:, half - 1:half, :], p4.shape).reshape(rows, width), 0.0)
            adds = jnp.where(second, 0.0, jnp.broadcast_to(s4[:, half:half + 1, :], s4.shape).reshape(rows, width))
        pre.append(p + addp)
        suf.append(s + adds)
    return pre, suf


def _gla_state_in(s4):
    sp = s4.reshape(4 * GLA_DK, 64)
    st = jnp.concatenate([sp, sp], axis=1).T[0:64, :]
    lane_h = lax.broadcasted_iota(jnp.int32, (64, 128), 1) // GLA_DK
    return jnp.concatenate([jnp.where(lane_h == h, st, 0.0) for h in range(4)], axis=0)


def _gla_state_store(sfin_ref, q, s_t):
    m = s_t[0:64] + s_t[64:128] + s_t[128:192] + s_t[192:256]
    nat = jnp.concatenate([m, jnp.zeros_like(m)], axis=0).T
    sfin_ref[q] = nat[:, 0:64].reshape(4, GLA_DK, 64)


def _gla_dir_parts(u_ref, ab_ref, abias_ref, rev):
    la = _log_sigmoid(_dot(u_ref[:, 768:896], ab_ref[...]) + abias_ref[...]) * (1.0 / GLA_GATE_NORM)
    pre, suf = _block_scans(la, 6)
    near, far = (suf, pre) if rev else (pre, suf)
    return dict(la=la, near=near, far=far, b=near[6], q=u_ref[:, 0:128] * float(GLA_DK ** -0.5),
                kg=u_ref[:, 128:256], vg=u_ref[:, 256:512])


def _gla_kernel(*refs, nb, spb, shared):
    if shared:
        u_f, ab_f, abias_f, ab_b, abias_b, s0_f, s0_b, o_f, sfin_f, o_b, sfin_b, s_scr = refs
        u_b = u_f
    else:
        u_f, u_b, ab_f, abias_f, ab_b, abias_b, s0_f, s0_b, o_f, sfin_f, o_b, sfin_b, s_scr = refs
    j = pl.program_id(1)

    if spb == 1:
        @pl.when(j == 0)
        def _():
            s_scr[0] = _gla_state_in(s0_f[0])
            s_scr[1] = _gla_state_in(s0_b[0])

    parts = (_gla_dir_parts(u_f, ab_f, abias_f, False), _gla_dir_parts(u_b, ab_b, abias_b, True))
    nch = u_f.shape[0] // CHUNK
    n3 = 2 * nch

    def cat3(xs):
        return jnp.concatenate([x.reshape(nch, CHUNK, x.shape[1]) for x in xs], axis=0)

    mshape = (2, 1, CHUNK, MIX_W)
    t_i = lax.broadcasted_iota(jnp.int32, mshape, 2)
    s_i = lax.broadcasted_iota(jnp.int32, mshape, 3) & (CHUNK - 1)
    sign = jnp.where(lax.broadcasted_iota(jnp.int32, mshape, 0) >= 1, -1, 1)

    def select(mask, x, other):
        return jnp.where(mask, x.reshape(2, nch, CHUNK, MIX_W),
                         other.reshape(2, nch, CHUNK, MIX_W)).reshape(n3, CHUNK, MIX_W)

    q3 = cat3([p["q"] for p in parts])
    k3 = cat3([p["kg"] for p in parts])
    v3 = cat3([p["vg"] for p in parts])
    att = select(s_i == t_i, _bmm_nt(q3, _rows_bd4(k3.astype(BF16), GLA_DK)), jnp.zeros((n3, CHUNK, MIX_W), F32))
    for lev in range(1, 7):
        size, half = 1 << lev, 1 << (lev - 1)
        same = (t_i // size) == (s_i // size)
        t_late = jnp.where((t_i & (size - 1)) >= half, 1, 0)
        s_late = jnp.where((s_i & (size - 1)) >= half, 1, 0)
        mask = same & ((t_late - s_late) * sign == 1)
        qe = cat3([p["q"] * jnp.exp(p["near"][lev - 1]) for p in parts])
        ke = cat3([p["kg"] * jnp.exp(p["far"][lev - 1] - p["la"]) for p in parts])
        att = select(mask, _bmm_nt(qe, _rows_bd4(ke.astype(BF16), GLA_DK)), att)
    o_intra = _bmm(att, _rows_bd4(v3.astype(BF16), 64))

    b3 = cat3([p["b"] for p in parts])
    b_last = jnp.concatenate([b3[:nch, CHUNK - 1:CHUNK, :], b3[nch:, 0:1, :]], axis=0)
    qb = q3 * jnp.exp(b3)
    bd_r = lax.broadcasted_iota(jnp.int32, (1, MIX_W, 128), 1) // 64
    bd_c = lax.broadcasted_iota(jnp.int32, (1, MIX_W, 128), 2) // GLA_DK
    inc = jnp.where(bd_r == bd_c, _bmm_tn(v3, k3 * jnp.exp(b_last - b3)), 0.0)
    decay = jnp.exp(b_last)

    cps = nch // spb
    dirs = ((0, False, s0_f, o_f, sfin_f), (1, True, s0_b, o_b, sfin_b))
    s_cur = [s_scr[0], s_scr[1]] if spb == 1 else [None, None]
    for ci in range(nch):
        for d, rev, s0_ref, o_ref, sfin_ref in dirs:
            c = (nch - 1 - ci) if rev else ci
            qs, ci_seq = divmod(ci, cps)
            qs = (spb - 1 - qs) if rev else qs
            if spb > 1 and ci_seq == 0:
                s_cur[d] = _gla_state_in(s0_ref[qs])
            g = d * nch + c
            o_ref[c * CHUNK:(c + 1) * CHUNK, :] = o_intra[g] + _dot_nt(qb[g], s_cur[d])
            s_cur[d] = s_cur[d] * decay[g] + inc[g]
            if spb > 1 and ci_seq == cps - 1:
                _gla_state_store(sfin_ref, qs, s_cur[d])
    if spb == 1:
        s_scr[0] = s_cur[0]
        s_scr[1] = s_cur[1]

        @pl.when(j == nb - 1)
        def _():
            _gla_state_store(sfin_f, 0, s_scr[0])
            _gla_state_store(sfin_b, 0, s_scr[1])


def _gla_scan(u_g, prm, s0, nseq, seq_len):
    spb, tb, nb = _scan_blocking(nseq, seq_len, GLA_TB)
    shared = nb == 1
    t = u_g.shape[0]
    fwd_map = lambda b, j: (b * nb + j, 0)
    bwd_map = lambda b, j: (b * nb + nb - 1 - j, 0)
    st = pl.BlockSpec((spb, 4, GLA_DK, 64), lambda b, j: (b, 0, 0, 0))
    in_specs = [pl.BlockSpec((tb, 896), fwd_map)]
    operands = [u_g]
    if not shared:
        in_specs.append(pl.BlockSpec((tb, 896), bwd_map))
        operands.append(u_g)
    in_specs += [_full((128, 128)), _full((1, 128))] * 2 + [st, st]
    operands += [prm["gla_ab0"], prm["gla_abias0"], prm["gla_ab1"], prm["gla_abias1"], s0[0], s0[1]]
    out_tok = jax.ShapeDtypeStruct((t, MIX_W), F32)
    out_st = jax.ShapeDtypeStruct((nseq, 4, GLA_DK, 64), F32)
    return pl.pallas_call(
        functools.partial(_gla_kernel, nb=nb, spb=spb, shared=shared),
        grid=(nseq // spb, nb),
        in_specs=in_specs,
        out_specs=[pl.BlockSpec((tb, MIX_W), fwd_map), st, pl.BlockSpec((tb, MIX_W), bwd_map), st],
        out_shape=[out_tok, out_st, out_tok, out_st],
        scratch_shapes=[pltpu.VMEM((2, 256, 128), F32)],
        compiler_params=_cparams(("parallel", "arbitrary")),
        name="gla_scan",
    )(*operands)


def _merge_kernel(x_ref, mod_ref, g_ref, ya_ref, yb_ref, of_ref, ob_ref, exf_ref, exb_ref, cag_ref,
                  gf_ref, gb_ref, gout_ref, gn_ref, bg_ref, gnorm_ref, wg_ref, wb_ref, wo_ref, x1_ref):
    mod = mod_ref[0]
    x = x_ref[...]
    h = _rms_mod(x, g_ref[...], mod[:, D_MODEL:2 * D_MODEL], mod[:, 0:D_MODEL]).astype(BF16)
    mean64 = _seg_ones(MIX_W, 64, 1.0 / 64.0)
    o = of_ref[...] + ob_ref[...]
    mu = _dot_exact_rhs(o, mean64)
    oc = o - mu
    var = _dot_exact_rhs(oc * oc, mean64)
    gate_c = _dot(jax.nn.sigmoid(cag_ref[...]), bg_ref[...])
    y_c = (oc * lax.rsqrt(var + RWKV_GN_EPS) * gn_ref[...] + exf_ref[...] + exb_ref[...]) * gate_c
    og = gf_ref[...] + gb_ref[...]
    gout = gout_ref[...]
    y_d = (og * lax.rsqrt(_dot_exact_rhs(og * og, mean64) + EPS) * gnorm_ref[...]
           * (gout * jax.nn.sigmoid(gout)))
    ys = (ya_ref[...], yb_ref[...], y_c, y_d)
    merged = jnp.zeros((TM, D_MODEL), F32)
    for i in range(4):
        g0 = P_MIX + i * D_MODEL
        gate = jax.nn.sigmoid(_dot_nt(h, wg_ref[g0:g0 + D_MODEL, :]))
        merged = merged + gate * _dot(ys[i], wb_ref[i])
    out = _dot(merged, wo_ref[...])
    x1_ref[...] = x + mod[:, 2 * D_MODEL:3 * D_MODEL] * out


def _merge(x, mod3, g, y_a, y_b, o_f, o_b, ex_f, ex_b, u_r, g_f, g_b, u_g, prm, big, layer,
           cond_base, rows_per_cond):
    t = x.shape[0]
    row = lambda i: (i, 0)
    mix = pl.BlockSpec((TM, MIX_W), row)
    vec = _full((1, MIX_W))
    lay3 = lambda i: (layer, 0, 0)
    return pl.pallas_call(
        _merge_kernel,
        grid=(t // TM,),
        in_specs=[pl.BlockSpec((TM, D_MODEL), row),
                  pl.BlockSpec((1, 1, 6 * D_MODEL), _cond_map(cond_base, rows_per_cond)),
                  _full((1, D_MODEL)),
                  mix, mix, mix, mix, mix, mix,
                  pl.BlockSpec((TM, 128), lambda i: (i, 7)),
                  mix, mix,
                  pl.BlockSpec((TM, MIX_W), lambda i: (i, 2)),
                  vec, _full((128, MIX_W)), vec,
                  pl.BlockSpec((None, big["w_in_t"].shape[1], D_MODEL), lay3, pipeline_mode=pl.Buffered(1)),
                  pl.BlockSpec((None, 4, MIX_W, D_MODEL), lambda i: (layer, 0, 0, 0)),
                  pl.BlockSpec((None, D_MODEL, D_MODEL), lay3)],
        out_specs=pl.BlockSpec((TM, D_MODEL), row),
        out_shape=jax.ShapeDtypeStruct((t, D_MODEL), F32),
        compiler_params=_cparams(("parallel",)),
        name="merge_out",
    )(x, mod3, g, y_a, y_b, o_f, o_b, ex_f, ex_b, u_r, g_f, g_b, u_g,
      prm["gn"], prm["bg"], prm["gla_norm"], big["w_in_t"], big["w_branch"], big["w_out"])


def _mlp_kernel(x_ref, mod_ref, g_ref, w1_ref, w2_ref, fg_ref, x2_ref, y_ref):
    mod = mod_ref[0]
    x = x_ref[...]
    h = _rms_mod(x, g_ref[...], mod[:, 4 * D_MODEL:5 * D_MODEL], mod[:, 3 * D_MODEL:4 * D_MODEL]).astype(BF16)
    ff = jnp.zeros((TM, D_MODEL), F32)
    for c in range(D_FF // D_MODEL):
        cols = slice(c * D_MODEL, (c + 1) * D_MODEL)
        a = jnp.maximum(jnp.dot(h, w1_ref[:, cols], preferred_element_type=F32), 0.0)
        ff = ff + _dot(a * a, w2_ref[cols, :])
    x2 = x + mod[:, 5 * D_MODEL:6 * D_MODEL] * ff
    x2_ref[...] = x2
    y_ref[...] = x2 * lax.rsqrt(jnp.mean(x2 * x2, axis=-1, keepdims=True) + EPS) * fg_ref[...]


def _mlp(x1, mod3, g, w1, w2, layer, final_g, cond_base, rows_per_cond):
    t = x1.shape[0]
    row = lambda i: (i, 0)
    lay3 = lambda i: (layer, 0, 0)
    return pl.pallas_call(
        _mlp_kernel,
        grid=(t // TM,),
        in_specs=[pl.BlockSpec((TM, D_MODEL), row),
                  pl.BlockSpec((1, 1, 6 * D_MODEL), _cond_map(cond_base, rows_per_cond)),
                  _full((1, D_MODEL)), pl.BlockSpec((None, D_MODEL, D_FF), lay3),
                  pl.BlockSpec((None, D_FF, D_MODEL), lay3), _full((1, D_MODEL))],
        out_specs=[pl.BlockSpec((TM, D_MODEL), row), pl.BlockSpec((TM, D_MODEL), row)],
        out_shape=[jax.ShapeDtypeStruct((t, D_MODEL), F32), jax.ShapeDtypeStruct((t, D_MODEL), F32)],
        compiler_params=_cparams(("parallel",)),
        name="mlp",
    )(x1, mod3, g, w1, w2, final_g)


def _pos_tables(n_tok, d):
    rows = n_tok // GRID_W
    assert rows == GRID_W
    quarter = d // 4
    omega = 1.0 / (POS_BASE ** (jnp.arange(quarter, dtype=F32) / quarter))
    ar = jnp.arange(rows, dtype=F32)[:, None] * omega
    ac = jnp.arange(GRID_W, dtype=F32)[:, None] * omega
    return (jnp.concatenate([jnp.sin(ar), jnp.cos(ar)], axis=-1),
            jnp.concatenate([jnp.sin(ac), jnp.cos(ac)], axis=-1))


def _pad_rows(m, start, total):
    return jnp.zeros((total, m.shape[1]), m.dtype).at[start:start + m.shape[0]].set(m)


def _layer_params(l, p):
    out = {
        "n1": p["norm1_g"][l].reshape(1, D_MODEL),
        "n2": p["norm2_g"][l].reshape(1, D_MODEL),
        "pool_w": jnp.einsum("gcd,gh->gchd", p["pool_w"][l], jnp.eye(4, dtype=F32)).reshape(MIX_W, MIX_W).astype(BF16),
        "pool_scale": p["pool_scale"][l].reshape(1, MIX_W),
        "gn": p["rwkv_gn"][l].reshape(1, MIX_W),
        "bg": _pad_rows(p["rwkv_bg"][l], 64, 128).astype(BF16),
        "gla_norm": p["gla_norm"][l].reshape(1, MIX_W),
    }
    for d in range(2):
        out["rwkv%d" % d] = {
            "mu": p["rwkv_mu"][l].reshape(1, 768),
            "kk": p["rwkv_kk"][l].reshape(1, MIX_W),
            "ka": p["rwkv_ka"][l].reshape(1, MIX_W),
            "rk": p["rwkv_rk"][l].reshape(1, MIX_W),
            "w0": p["rwkv_w0"][l, d].reshape(1, MIX_W),
            "a0": p["rwkv_a0"][l, d].reshape(1, MIX_W),
            "bw": _pad_rows(p["rwkv_bw"][l, d], 64 * d, 128).astype(BF16),
            "ba": _pad_rows(p["rwkv_ba"][l, d], 32 * d, 128).astype(BF16),
        }
        out["gla_ab%d" % d] = _pad_rows(p["gla_ab"][l, d], 16 * d, 128).astype(BF16)
        out["gla_abias%d" % d] = p["gla_abias"][l, d].reshape(1, 128)
    return out


def _run_layer(x, mod3, prm, big, layer, final_g, s_rwkv0, s_gla0, nseq, seq_len, cond_base, rows_per_cond,
               pos_tables=None):
    if pos_tables is None:
        u_p, u_f, u_r, u_g = _inproj(x, mod3, prm["n1"], big["w_in_t"], layer, cond_base, rows_per_cond)
    else:
        x, u_p, u_f, u_r, u_g = _inproj(x, mod3, prm["n1"], big["w_in_t"], layer, cond_base, rows_per_cond,
                                        pos_tables, seq_len)
    y_a = _pool(u_p, prm["pool_w"], prm["pool_scale"], nseq, seq_len)
    if seq_len == FFT_N1 * FFT_N1:
        y_b = _four_large(u_f, nseq, seq_len)
    else:
        y_b = _four_small(u_f, nseq, seq_len)
    o_f, ex_f, sr_f, o_b, ex_b, sr_b = _rwkv_scan(u_r, prm, s_rwkv0, nseq, seq_len)
    g_f, sg_f, g_b, sg_b = _gla_scan(u_g, prm, s_gla0, nseq, seq_len)
    x1 = _merge(x, mod3, prm["n1"], y_a, y_b, o_f, o_b, ex_f, ex_b, u_r, g_f, g_b, u_g, prm, big, layer,
                cond_base, rows_per_cond)
    x2, y = _mlp(x1, mod3, prm["n2"], big["w1"], big["w2"], layer, final_g, cond_base, rows_per_cond)
    return x2, y, jnp.stack([sr_f, sr_b], axis=1), jnp.stack([sg_f, sg_b], axis=1)


def kernel(x_prompt, x_sample, state_rwkv, state_gla, c, c_ctx, ada_w, ada_b, norm1_g, norm2_g, w_in, pool_w, pool_scale, rwkv_mu, rwkv_w0, rwkv_bw, rwkv_a0, rwkv_ba, rwkv_kk, rwkv_ka, rwkv_bg, rwkv_rk, rwkv_gn, gla_ab, gla_abias, gla_norm, w_branch, w_out, mlp_w1, mlp_w2, final_g):
    p = dict(pool_w=pool_w, pool_scale=pool_scale, rwkv_mu=rwkv_mu, rwkv_w0=rwkv_w0,
             rwkv_bw=rwkv_bw, rwkv_a0=rwkv_a0, rwkv_ba=rwkv_ba, rwkv_kk=rwkv_kk, rwkv_ka=rwkv_ka,
             rwkv_bg=rwkv_bg, rwkv_rk=rwkv_rk, rwkv_gn=rwkv_gn, gla_ab=gla_ab, gla_abias=gla_abias,
             gla_norm=gla_norm, norm1_g=norm1_g, norm2_g=norm2_g)
    big = dict(w_in_t=jnp.swapaxes(w_in, 1, 2).astype(BF16),
               w_branch=w_branch.astype(BF16), w_out=w_out.astype(BF16),
               w1=mlp_w1.astype(BF16), w2=mlp_w2.astype(BF16))
    bp, lp, _ = x_prompt.shape
    bs, ls, _ = x_sample.shape
    cond8 = jnp.zeros((8, D_MODEL), F32).at[0].set(c_ctx).at[1:1 + bs].set(c)
    mods = _ada(cond8, ada_w, ada_b)
    fg = final_g.reshape(1, D_MODEL)

    xp = x_prompt.reshape(bp * lp, D_MODEL)
    xs = x_sample.reshape(bs * ls, D_MODEL)
    pos_tables = _pos_tables(ls, D_MODEL)
    zr = jnp.zeros((2, bp) + state_rwkv.shape[3:], F32)
    zg = jnp.zeros((2, bp) + state_gla.shape[3:], F32)
    new_r, new_g = [], []
    for l in range(DEPTH):
        prm = _layer_params(l, p)
        mod3 = mods[l].reshape(8, 1, 6 * D_MODEL)
        xp, yp, s_r, s_g = _run_layer(xp, mod3, prm, big, l, fg, zr, zg, bp, lp, 0, bp * lp)
        new_r.append(s_r)
        new_g.append(s_g)
        xs, ys, _, _ = _run_layer(xs, mod3, prm, big, l, fg, jnp.swapaxes(state_rwkv[:, l], 0, 1),
                                  jnp.swapaxes(state_gla[:, l], 0, 1), bs, ls, 1, ls,
                                  pos_tables if l == 0 else None)
    y_prompt = yp.reshape(bp, lp, D_MODEL)
    y_sample = ys.reshape(bs, ls, D_MODEL)
    return (y_prompt, y_sample, jnp.stack(new_r, axis=1), jnp.stack(new_g, axis=1))
```

```python
import functools

import numpy as np
import jax
import jax.numpy as jnp
from jax import lax
from jax.experimental import pallas as pl
from jax.experimental.pallas import tpu as pltpu

F32 = jnp.float32
BF16 = jnp.bfloat16

D_MODEL = 1024
DEPTH = 2
GRID_W = 64
POS_BASE = 10000.0
MIX_W = 256
POOL_WINDOWS = (2, 4, 8, 16)
POOL_HALO = 8
RWKV_DECAY_SCALE = 0.606531
RWKV_GN_EPS = 64e-5
GLA_DK = 32
GLA_GATE_NORM = 16.0
D_FF = 4 * D_MODEL
EPS = 1e-6
P_MIX = 2336
P_MIX_PAD = 2432
CHUNK = 64
SUB = 16

TM = 512
TM_WIDE = 1024
POOL_TB = 1024
RWKV_TB = 512
GLA_TB = 1024
FFT_N1 = 64
VMEM_LIMIT = 56 * 1024 * 1024


def _cparams(sem):
    return pltpu.CompilerParams(dimension_semantics=sem, vmem_limit_bytes=VMEM_LIMIT)


def _full(shape):
    n = len(shape)
    return pl.BlockSpec(shape, lambda *_: (0,) * n)


def _dot(a, b):
    return jnp.dot(a.astype(BF16), b.astype(BF16), preferred_element_type=F32)


def _dot_nt(a, b):
    return lax.dot_general(a.astype(BF16), b.astype(BF16), (((1,), (1,)), ((), ())),
                           preferred_element_type=F32)


def _split2(a):
    hi = a.astype(BF16)
    lo = (a - hi.astype(F32)).astype(BF16)
    return hi, lo


def _dot_exact_rhs(a, w):
    hi, lo = _split2(a)
    return (jnp.dot(hi, w, preferred_element_type=F32) + jnp.dot(lo, w, preferred_element_type=F32))


def _dot_exact_lhs(w, a):
    hi, lo = _split2(a)
    return jnp.dot(w, hi, preferred_element_type=F32) + jnp.dot(w, lo, preferred_element_type=F32)


def _seg_ones(n, seg, scale=1.0):
    r = lax.broadcasted_iota(jnp.int32, (n, n), 0)
    c = lax.broadcasted_iota(jnp.int32, (n, n), 1)
    return jnp.where((r // seg) == (c // seg), scale, 0.0).astype(BF16)


def _log_sigmoid(x):
    return jnp.minimum(x, 0.0) - jnp.log(1.0 + jnp.exp(-jnp.abs(x)))


def _rms_mod(x, g, sc, sh):
    ms = jnp.mean(x * x, axis=-1, keepdims=True)
    return x * lax.rsqrt(ms + EPS) * g * (1.0 + sc) + sh


def _ada_kernel(c_ref, w_ref, b_ref, o_ref):
    c = c_ref[...]
    s = c * jax.nn.sigmoid(c)
    hi, lo = _split2(s)
    w = w_ref[0].astype(BF16)
    o_ref[0] = (jnp.dot(hi, w, preferred_element_type=F32) + jnp.dot(lo, w, preferred_element_type=F32)
                + b_ref[0])


def _ada(cond8, ada_w, ada_b):
    tn = 1536
    n = 6 * D_MODEL
    return pl.pallas_call(
        _ada_kernel,
        grid=(DEPTH, n // tn),
        in_specs=[_full((8, D_MODEL)),
                  pl.BlockSpec((1, D_MODEL, tn), lambda l, j: (l, 0, j)),
                  pl.BlockSpec((1, 1, tn), lambda l, j: (l, 0, j))],
        out_specs=pl.BlockSpec((1, 8, tn), lambda l, j: (l, 0, j)),
        out_shape=jax.ShapeDtypeStruct((DEPTH, 8, n), F32),
        compiler_params=_cparams(("parallel", "parallel")),
        name="ada_mod",
    )(cond8, ada_w, ada_b.reshape(DEPTH, 1, n))


def _inproj_kernel(x_ref, mod_ref, g_ref, w_ref, up_ref, uf_ref, ur_ref, ug_ref):
    _inproj_body(x_ref[...], mod_ref, g_ref, w_ref, up_ref, uf_ref, ur_ref, ug_ref)


def _inproj_pos_kernel(x_ref, rr_ref, cc_ref, mod_ref, g_ref, w_ref, xo_ref, up_ref, uf_ref, ur_ref, ug_ref,
                       *, seq_len):
    tile = x_ref.shape[0]
    nrow = tile // GRID_W
    g0 = pl.multiple_of(((pl.program_id(0) * tile) % seq_len) // GRID_W, nrow)
    half = D_MODEL // 2
    rpart = jnp.broadcast_to(rr_ref[pl.ds(g0, nrow), :][:, None, :], (nrow, GRID_W, half)).reshape(tile, half)
    cpart = jnp.concatenate([cc_ref[...]] * nrow, axis=0)
    x = x_ref[...] + jnp.concatenate([rpart, cpart], axis=1)
    xo_ref[...] = x
    _inproj_body(x, mod_ref, g_ref, w_ref, up_ref, uf_ref, ur_ref, ug_ref)


def _inproj_body(x, mod_ref, g_ref, w_ref, up_ref, uf_ref, ur_ref, ug_ref):
    mod = mod_ref[0]
    h = _rms_mod(x, g_ref[...], mod[:, D_MODEL:2 * D_MODEL], mod[:, 0:D_MODEL]).astype(BF16)
    up_ref[...] = _dot_nt(h, w_ref[0:256, :])
    uf_ref[...] = _dot_nt(h, w_ref[256:512, :])
    ur_ref[...] = _dot_nt(h, w_ref[512:1536, :])
    ug_ref[...] = _dot_nt(h, w_ref[1536:P_MIX_PAD, :])


def _cond_map(cond_base, rows_per_cond, tile=TM):
    return lambda i: (cond_base + (i * tile) // rows_per_cond, 0, 0)


def _inproj(x, mod3, g, w_t, layer, cond_base, rows_per_cond, pos_tables=None, seq_len=None):
    t = x.shape[0]
    row = lambda i: (i, 0)
    tm = TM_WIDE
    in_specs = [pl.BlockSpec((1, 1, 6 * D_MODEL), _cond_map(cond_base, rows_per_cond, tm)),
                _full((1, D_MODEL)),
                pl.BlockSpec((None, P_MIX_PAD, D_MODEL), lambda i: (layer, 0, 0), pipeline_mode=pl.Buffered(1))]
    out_specs = [pl.BlockSpec((tm, 256), row), pl.BlockSpec((tm, 256), row),
                 pl.BlockSpec((tm, 1024), row), pl.BlockSpec((tm, 896), row)]
    out_shape = [jax.ShapeDtypeStruct((t, 256), F32), jax.ShapeDtypeStruct((t, 256), F32),
                 jax.ShapeDtypeStruct((t, 1024), F32), jax.ShapeDtypeStruct((t, 896), F32)]
    x_spec = pl.BlockSpec((tm, D_MODEL), row)
    if pos_tables is None:
        return pl.pallas_call(
            _inproj_kernel, grid=(t // tm,), in_specs=[x_spec] + in_specs, out_specs=out_specs,
            out_shape=out_shape, compiler_params=_cparams(("parallel",)), name="in_proj",
        )(x, mod3, g, w_t)
    tab = _full((GRID_W, D_MODEL // 2))
    return pl.pallas_call(
        functools.partial(_inproj_pos_kernel, seq_len=seq_len), grid=(t // tm,),
        in_specs=[x_spec, tab, tab] + in_specs, out_specs=[x_spec] + out_specs,
        out_shape=[jax.ShapeDtypeStruct((t, D_MODEL), F32)] + out_shape,
        compiler_params=_cparams(("parallel",)), name="in_proj_pos",
    )(x, pos_tables[0], pos_tables[1], mod3, g, w_t)


def _pool_kernel(prev_ref, cur_ref, next_ref, wbd_ref, scale_ref, y_ref, *, seq_len):
    j = pl.program_id(1)
    tb = cur_ref.shape[0]
    nb = seq_len // tb
    cur = cur_ref[...]
    prev = jnp.where(j > 0, prev_ref[...], 0.0)
    nxt = jnp.where(j < nb - 1, next_ref[...], 0.0)
    ext = jnp.concatenate([prev, cur, nxt], axis=0)
    n_ext = tb + 2 * POOL_HALO

    def shift(x, k):
        return pltpu.roll(x, k % n_ext, 0)

    sums = [ext + shift(ext, 1)]
    for k in (1, 2, 4):
        sums.append(shift(sums[-1], -k) + shift(sums[-1], k))
    assert POOL_WINDOWS == (2, 4, 8, 16)
    pos = j * tb + lax.broadcasted_iota(jnp.int32, (tb, MIX_W), 0)
    group = lax.broadcasted_iota(jnp.int32, (tb, MIX_W), 1) // (MIX_W // len(POOL_WINDOWS))
    pooled = jnp.zeros((tb, MIX_W), F32)
    for gi, win in enumerate(POOL_WINDOWS):
        before, after = win // 2, win - win // 2 - 1
        s = sums[gi][POOL_HALO:POOL_HALO + tb]
        cnt = (jnp.minimum(pos + after, seq_len - 1) - jnp.maximum(pos - before, 0) + 1).astype(F32)
        pooled = jnp.where(group == gi, s / cnt - cur, pooled)
    y_ref[...] = _dot(pooled, wbd_ref[...]) * scale_ref[...]


def _pool(u_p, wbd, scale, nseq, seq_len):
    tb = min(seq_len, POOL_TB)
    nb = seq_len // tb
    hb = tb // POOL_HALO
    nh = u_p.shape[0] // POOL_HALO
    return pl.pallas_call(
        functools.partial(_pool_kernel, seq_len=seq_len),
        grid=(nseq, nb),
        in_specs=[pl.BlockSpec((POOL_HALO, MIX_W), lambda b, j: (jnp.maximum((b * nb + j) * hb - 1, 0), 0)),
                  pl.BlockSpec((tb, MIX_W), lambda b, j: (b * nb + j, 0)),
                  pl.BlockSpec((POOL_HALO, MIX_W), lambda b, j: (jnp.minimum((b * nb + j + 1) * hb, nh - 1), 0)),
                  _full((MIX_W, MIX_W)), _full((1, MIX_W))],
        out_specs=pl.BlockSpec((tb, MIX_W), lambda b, j: (b * nb + j, 0)),
        out_shape=jax.ShapeDtypeStruct(u_p.shape, F32),
        compiler_params=_cparams(("parallel", "parallel")),
        name="pool_mixer",
    )(u_p, u_p, u_p, wbd, scale)


def _dft_tables(n):
    t = np.arange(n)
    ang = 2.0 * np.pi * ((np.outer(t, t)) % n) / n
    return np.cos(ang), np.sin(ang)


def _bd_np(m, copies):
    k = m.shape[0]
    out = np.zeros((k * copies, k * copies), m.dtype)
    for i in range(copies):
        out[i * k:(i + 1) * k, i * k:(i + 1) * k] = m
    return out


def _four_small_kernel(z_ref, cs_ref, cbd_ref, sbd_ref, y_ref, *, seq_len, norm):
    zf = _dot(cs_ref[...], z_ref[...])
    y = _dot(zf[:seq_len], cbd_ref[...]) + _dot(zf[seq_len:], sbd_ref[...])
    y_ref[...] = y * norm


def _four_small(u_f, nseq, seq_len):
    c, s = _dft_tables(seq_len)
    cs = jnp.asarray(np.concatenate([c, -s], axis=0), F32)
    c64, s64 = _dft_tables(MIX_W // 4)
    cbd = jnp.asarray(_bd_np(c64, 4), F32)
    sbd = jnp.asarray(_bd_np(s64, 4), F32)
    return pl.pallas_call(
        functools.partial(_four_small_kernel, seq_len=seq_len, norm=float((seq_len * 64) ** -0.5)),
        grid=(nseq,),
        in_specs=[pl.BlockSpec((seq_len, MIX_W), lambda b: (b, 0)),
                  _full((2 * seq_len, seq_len)), _full((MIX_W, MIX_W)), _full((MIX_W, MIX_W))],
        out_specs=pl.BlockSpec((seq_len, MIX_W), lambda b: (b, 0)),
        out_shape=jax.ShapeDtypeStruct(u_f.shape, F32),
        compiler_params=_cparams(("parallel",)),
        name="fourier_small",
    )(u_f, cs, cbd, sbd)


def _four_large_kernel(z0_ref, z1_ref, fs_ref, g_ref, cbd_ref, sbd_ref, y_ref, a0_scr, a1_scr, xr_scr, xi_scr,
                       y0_scr, y1_scr, *, norm):
    n1 = FFT_N1
    fs = fs_ref[...].astype(BF16)

    def rows(ref0, ref1, start, stride):
        sl = pl.ds(start, n1, stride=stride)
        return jnp.concatenate([ref0[sl, :], ref1[sl, :]], axis=1)

    for t2 in range(n1):
        a = _dot(fs, rows(z0_ref, z1_ref, t2, n1))
        a0_scr[t2 * 2 * n1:(t2 + 1) * 2 * n1, :] = a[:, :128]
        a1_scr[t2 * 2 * n1:(t2 + 1) * 2 * n1, :] = a[:, 128:]
    for f1 in range(n1):
        a = jnp.concatenate([rows(a0_scr, a1_scr, f1, 2 * n1), rows(a0_scr, a1_scr, n1 + f1, 2 * n1)],
                            axis=0)
        x = _dot(g_ref[f1], a)
        xr_scr[f1 * n1:(f1 + 1) * n1, :] = x[:n1]
        xi_scr[f1 * n1:(f1 + 1) * n1, :] = x[n1:]
    y = (_dot(xr_scr[...], cbd_ref[...]) + _dot(xi_scr[...], sbd_ref[...])) * norm
    y0_scr[...] = y[:, :128]
    y1_scr[...] = y[:, 128:]
    for f2 in range(n1):
        y_ref[f2 * n1:(f2 + 1) * n1, :] = rows(y0_scr, y1_scr, f2, n1)


def _four_large(u_f, nseq, seq_len):
    n1 = FFT_N1
    assert seq_len == n1 * n1
    c64, s64 = _dft_tables(n1)
    fs = jnp.asarray(np.concatenate([c64, -s64], axis=0), F32)
    f1 = np.arange(n1)[:, None, None]
    f2 = np.arange(n1)[None, :, None]
    t2 = np.arange(n1)[None, None, :]
    ang = 2.0 * np.pi * ((t2 * (f1 + n1 * f2)) % seq_len) / seq_len
    gr, gi = np.cos(ang), -np.sin(ang)
    g = jnp.asarray(np.concatenate([np.concatenate([gr, -gi], axis=2),
                                    np.concatenate([gi, gr], axis=2)], axis=1), F32)
    cbd = jnp.asarray(_bd_np(c64, 4), F32)
    sbd = jnp.asarray(_bd_np(s64, 4), F32)
    half = MIX_W // 2
    return pl.pallas_call(
        functools.partial(_four_large_kernel, norm=float((seq_len * 64) ** -0.5)),
        grid=(nseq,),
        in_specs=[pl.BlockSpec((seq_len, half), lambda b: (b, 0)), pl.BlockSpec((seq_len, half), lambda b: (b, 1)),
                  _full((2 * n1, n1)), _full((n1, 2 * n1, 2 * n1)), _full((MIX_W, MIX_W)), _full((MIX_W, MIX_W))],
        out_specs=pl.BlockSpec((seq_len, MIX_W), lambda b: (b, 0)),
        out_shape=jax.ShapeDtypeStruct(u_f.shape, F32),
        scratch_shapes=[pltpu.VMEM((2 * seq_len, half), F32), pltpu.VMEM((2 * seq_len, half), F32),
                        pltpu.VMEM((seq_len, MIX_W), F32), pltpu.VMEM((seq_len, MIX_W), F32),
                        pltpu.VMEM((seq_len, half), F32), pltpu.VMEM((seq_len, half), F32)],
        compiler_params=_cparams(("parallel",)),
        name="fourier_large",
    )(u_f, u_f, fs, g, cbd, sbd)


def _bmm(x, y):
    return lax.dot_general(x.astype(BF16), y.astype(BF16), (((2,), (1,)), ((0,), (0,))),
                           preferred_element_type=F32)


def _bmm_nt(x, y):
    return lax.dot_general(x.astype(BF16), y.astype(BF16), (((2,), (2,)), ((0,), (0,))),
                           preferred_element_type=F32)


def _bmm_tn(x, y):
    return lax.dot_general(x.astype(BF16), y.astype(BF16), (((1,), (1,)), ((0,), (0,))),
                           preferred_element_type=F32)


def _slab_bd(y):
    lane = lax.broadcasted_iota(jnp.int32, y.shape, 2)
    z = jnp.zeros_like(y)
    return jnp.concatenate([jnp.where(lane < 64, y, z), jnp.where(lane >= 64, y, z)], axis=1)


def _mm(x, y):
    return _bmm(x, _slab_bd(y.astype(BF16)))


def _mm_pair(x, y1, y2):
    w = jnp.concatenate([_slab_bd(y1.astype(BF16)), _slab_bd(y2.astype(BF16))], axis=2)
    out = _bmm(x, w)
    return out[:, :, :128], out[:, :, 128:]


def _mm3(xs, y):
    yh, yl = _split2(y)
    bh, bl = _slab_bd(yh), _slab_bd(yl)
    parts = [_split2(x) for x in xs]
    xh = jnp.concatenate([p[0] for p in parts], axis=1)
    xl = jnp.concatenate([p[1] for p in parts], axis=1)
    hx = _bmm(xh, jnp.concatenate([bh, bl], axis=2))
    out = hx[:, :, :128] + (hx[:, :, 128:] + _bmm(xl, bh))
    return [out[:, i * CHUNK:(i + 1) * CHUNK] for i in range(len(xs))]


def _to_slabs(x):
    nch = x.shape[0] // CHUNK
    x4 = x.reshape(nch, CHUNK, MIX_W)
    return jnp.stack([x4[:, :, :128], x4[:, :, 128:]], axis=1).reshape(2 * nch, CHUNK, 128)


def _tri_inverse(n_mat, eye, blk):
    nd = jnp.where(blk, n_mat, 0.0)
    no = n_mat - nd
    n2, = _mm3([nd], nd)
    n4, y = _mm3([n2, nd], n2)
    x = n2 - nd - y
    n8, y = _mm3([n4, x], n4)
    x = x + n4 + y
    y, = _mm3([x], n8)
    x = x + n8 + y
    td = eye + x
    e = _mm(td, no)
    e2 = _mm(e, e)
    e3 = _mm(e, e2)
    return td + _mm(e2 - e - e3, td)


def _rwkv_state_to_bd(s4):
    bd_r = lax.broadcasted_iota(jnp.int32, (128, 128), 0) // 64
    bd_c = lax.broadcasted_iota(jnp.int32, (128, 128), 1) // 64
    pairs = []
    for hp in range(2):
        sp = s4[2 * hp:2 * hp + 2].reshape(128, 64)
        pairs.append(jnp.where(bd_r == bd_c, jnp.concatenate([sp, sp], axis=1), 0.0))
    return jnp.stack(pairs, axis=0)


def _rwkv_state_store(sfin_ref, q, s_bd):
    for hp in range(2):
        sfin_ref[q, 2 * hp] = s_bd[hp, 0:64, 0:64]
        sfin_ref[q, 2 * hp + 1] = s_bd[hp, 64:128, 64:128]


def _rwkv_prep(cur_ref, prev_ref, next_ref, mu_ref, kkw_ref, jj, nb, spb):
    tb = cur_ref.shape[0]
    seq_rows = tb // spb
    z = cur_ref[:, 0:768]
    rowid = lax.broadcasted_iota(jnp.int32, (tb, 768), 0)
    prow = jnp.where(rowid == 0, jnp.where(jj > 0, prev_ref[POOL_HALO - 1:POOL_HALO, :], 0.0), 0.0)
    nrow = jnp.where(rowid == tb - 1, jnp.where(jj < nb - 1, next_ref[0:1, :], 0.0), 0.0)
    in_seq = rowid & (seq_rows - 1)
    up = jnp.where(in_seq == 0, prow, pltpu.roll(z, 1, 0))
    dn = jnp.where(in_seq == seq_rows - 1, nrow, pltpu.roll(z, tb - 1, 0))
    zm = z + mu_ref[...] * (0.5 * (up + dn) - z)
    r = zm[:, 0:256]
    k = zm[:, 256:512]
    v = zm[:, 512:768]
    kk = k * kkw_ref[...]
    kk = kk * lax.rsqrt(_dot(kk * kk, _seg_ones(MIX_W, 64)) + EPS)
    return r, k, v, kk


def _rwkv_dir_slabs(cur_ref, r, k, v, kk, ka_ref, rk_ref, w0_ref, a0_ref, bw_ref, ba_ref, ex_ref, rev):
    tb = cur_ref.shape[0]
    w_log = w0_ref[...] + _dot(jnp.tanh(cur_ref[:, 768:896]), bw_ref[...])
    lw = -RWKV_DECAY_SCALE * jax.nn.sigmoid(w_log)
    a = jax.nn.sigmoid(a0_ref[...] + _dot(cur_ref[:, 896:1024], ba_ref[...]))
    kd = k * (1.0 + (a - 1.0) * ka_ref[...])
    ex_ref[...] = _dot(r * kd * rk_ref[...], _seg_ones(MIX_W, 64)) * v
    tr = lax.broadcasted_iota(jnp.int32, (tb, tb), 0)
    tc = lax.broadcasted_iota(jnp.int32, (tb, tb), 1)
    same_chunk = (tr // CHUNK) == (tc // CHUNK)
    tri = jnp.where(same_chunk & ((tc >= tr) if rev else (tc <= tr)), 1.0, 0.0).astype(BF16)
    cum = _dot_exact_lhs(tri, lw)
    e_in = jnp.exp(cum)
    e_out = jnp.exp(-cum)
    last = 0 if rev else CHUNK - 1
    return dict(rt=_to_slabs(r * e_in), kh=_to_slabs((kd * e_out).astype(BF16)),
                bh=_to_slabs((kk * a * e_out).astype(BF16)),
                kt=_to_slabs((kk * jnp.exp(cum - lw)).astype(BF16)), vv=_to_slabs(v.astype(BF16)),
                gam=_to_slabs(e_in)[:, last:last + 1, :])


def _rwkv_kernel(*refs, nb, spb, shared):
    if shared:
        (cur_f, prev_f, next_f, mu_ref, kkw_ref, ka_ref, rk_ref, w0_f, a0_f, bw_f, ba_f, w0_b, a0_b, bw_b, ba_b,
         s0_f, s0_b, o_f, ex_f, sfin_f, o_b, ex_b, sfin_b, s_scr) = refs
        cur_b = cur_f
    else:
        (cur_f, prev_f, next_f, cur_b, prev_b, next_b, mu_ref, kkw_ref, ka_ref, rk_ref, w0_f, a0_f, bw_f, ba_f,
         w0_b, a0_b, bw_b, ba_b, s0_f, s0_b, o_f, ex_f, sfin_f, o_b, ex_b, sfin_b, s_scr) = refs
    j = pl.program_id(1)

    if spb == 1:
        @pl.when(j == 0)
        def _():
            s_scr[0] = _rwkv_state_to_bd(s0_f[0])
            s_scr[1] = _rwkv_state_to_bd(s0_b[0])

    pf = _rwkv_prep(cur_f, prev_f, next_f, mu_ref, kkw_ref, j, nb, spb)
    pb = pf if shared else _rwkv_prep(cur_b, prev_b, next_b, mu_ref, kkw_ref, nb - 1 - j, nb, spb)
    sl_f = _rwkv_dir_slabs(cur_f, *pf, ka_ref, rk_ref, w0_f, a0_f, bw_f, ba_f, ex_f, False)
    sl_b = _rwkv_dir_slabs(cur_b, *pb, ka_ref, rk_ref, w0_b, a0_b, bw_b, ba_b, ex_b, True)
    rt, kh, bh, kt, vv, gam = (jnp.concatenate([sl_f[n], sl_b[n]], axis=0)
                               for n in ("rt", "kh", "bh", "kt", "vv", "gam"))
    nch = cur_f.shape[0] // CHUNK
    gdir = 2 * nch
    n_slab = 2 * gdir

    mshape = (2, 1, CHUNK, 128)
    lag = ((lax.broadcasted_iota(jnp.int32, mshape, 2) - (lax.broadcasted_iota(jnp.int32, mshape, 3) & (CHUNK - 1)))
           * jnp.where(lax.broadcasted_iota(jnp.int32, mshape, 0) >= 1, -1, 1))
    strict, incl = lag > 0, lag >= 0

    def masked(mask, x):
        return jnp.where(mask, x.reshape(2, gdir, CHUNK, 128), 0.0).reshape(n_slab, CHUNK, 128)

    t_i = lax.broadcasted_iota(jnp.int32, (1, CHUNK, 128), 1)
    s_i = lax.broadcasted_iota(jnp.int32, (1, CHUNK, 128), 2) & (CHUNK - 1)
    blk = (s_i // SUB) == (t_i // SUB)
    eye = jnp.where(s_i == t_i, 1.0, 0.0)

    lhs = jnp.concatenate([kt, rt.astype(BF16)], axis=1)
    kb_bd = jnp.concatenate([_slab_bd(kh.astype(BF16)), _slab_bd(bh.astype(BF16))], axis=1)
    mkb = _bmm_nt(lhs, kb_bd)
    m_mat = masked(strict, mkb[:, :CHUNK, :128])
    p_mat = masked(incl, mkb[:, CHUNK:, :128])
    n_mat = masked(strict, mkb[:, :CHUNK, 128:])
    q_mat = masked(incl, mkb[:, CHUNK:, 128:])
    t_mat = _tri_inverse(n_mat, eye, blk)
    mpv = _mm(jnp.concatenate([m_mat, p_mat], axis=1), vv)
    mv, pv = mpv[:, :CHUNK], mpv[:, CHUNK:]
    tk, uv = _mm_pair(t_mat, kt, mv)
    qtk, quv = _mm_pair(q_mat, tk, uv)
    rc = rt - qtk
    oc = pv - quv
    bd_r = lax.broadcasted_iota(jnp.int32, (1, 128, 128), 1) // 64
    bd_c = lax.broadcasted_iota(jnp.int32, (1, 128, 128), 2) // 64
    bd_mask = bd_r == bd_c
    d_mat = jnp.where(bd_mask, _bmm_tn(tk, bh), 0.0)
    braw = jnp.where(bd_mask,
                     _bmm_tn(jnp.concatenate([vv, uv.astype(BF16)], axis=1),
                             jnp.concatenate([kh, -bh], axis=1)), 0.0)

    cps = nch // spb
    dirs = ((0, False, s0_f, o_f, sfin_f), (1, True, s0_b, o_b, sfin_b))
    s_cur = [s_scr[0], s_scr[1]] if spb == 1 else [None, None]
    for ci in range(nch):
        for d, rev, s0_ref, o_ref, sfin_ref in dirs:
            c = (nch - 1 - ci) if rev else ci
            q, ci_seq = divmod(ci, cps)
            q = (spb - 1 - q) if rev else q
            if spb > 1 and ci_seq == 0:
                s_cur[d] = _rwkv_state_to_bd(s0_ref[q])
            g = slice(d * gdir + 2 * c, d * gdir + 2 * c + 2)
            o = _bmm_nt(rc[g], s_cur[d]) + oc[g]
            o_ref[c * CHUNK:(c + 1) * CHUNK, 0:128] = o[0]
            o_ref[c * CHUNK:(c + 1) * CHUNK, 128:256] = o[1]
            s_cur[d] = (s_cur[d] - _bmm(s_cur[d], d_mat[g]) + braw[g]) * gam[g]
            if spb > 1 and ci_seq == cps - 1:
                _rwkv_state_store(sfin_ref, q, s_cur[d])
    if spb == 1:
        s_scr[0] = s_cur[0]
        s_scr[1] = s_cur[1]

        @pl.when(j == nb - 1)
        def _():
            _rwkv_state_store(sfin_f, 0, s_scr[0])
            _rwkv_state_store(sfin_b, 0, s_scr[1])


def _scan_blocking(nseq, seq_len, max_rows):
    spb = max(1, min(nseq, max_rows // seq_len))
    assert nseq % spb == 0
    tb = min(seq_len, max_rows) * spb
    return spb, tb, (seq_len * spb) // tb


def _rwkv_scan(u_r, prm, s0, nseq, seq_len):
    spb, tb, nb = _scan_blocking(nseq, seq_len, RWKV_TB)
    shared = nb == 1
    hb = tb // POOL_HALO
    nh = u_r.shape[0] // POOL_HALO
    t = u_r.shape[0]
    vec = _full((1, MIX_W))
    low = _full((128, MIX_W))
    st = pl.BlockSpec((spb, 4, 64, 64), lambda b, j: (b, 0, 0, 0))

    def block_specs(jmap):
        blk = lambda b, j: b * nb + jmap(j)
        return [pl.BlockSpec((tb, 1024), lambda b, j: (blk(b, j), 0)),
                pl.BlockSpec((POOL_HALO, 768), lambda b, j: (jnp.maximum(blk(b, j) * hb - 1, 0), 0)),
                pl.BlockSpec((POOL_HALO, 768), lambda b, j: (jnp.minimum((blk(b, j) + 1) * hb, nh - 1), 0))]

    fwd_map = lambda b, j: (b * nb + j, 0)
    bwd_map = lambda b, j: (b * nb + nb - 1 - j, 0)
    in_specs = block_specs(lambda j: j)
    operands = [u_r, u_r, u_r]
    if not shared:
        in_specs += block_specs(lambda j: nb - 1 - j)
        operands += [u_r, u_r, u_r]
    in_specs += [_full((1, 768)), vec, vec, vec] + [vec, vec, low, low] * 2 + [st, st]
    f, b = prm["rwkv0"], prm["rwkv1"]
    operands += [f["mu"], f["kk"], f["ka"], f["rk"], f["w0"], f["a0"], f["bw"], f["ba"],
                 b["w0"], b["a0"], b["bw"], b["ba"], s0[0], s0[1]]
    out_tok = jax.ShapeDtypeStruct((t, MIX_W), F32)
    out_st = jax.ShapeDtypeStruct((nseq, 4, 64, 64), F32)
    return pl.pallas_call(
        functools.partial(_rwkv_kernel, nb=nb, spb=spb, shared=shared),
        grid=(nseq // spb, nb),
        in_specs=in_specs,
        out_specs=[pl.BlockSpec((tb, MIX_W), fwd_map), pl.BlockSpec((tb, MIX_W), fwd_map), st,
                   pl.BlockSpec((tb, MIX_W), bwd_map), pl.BlockSpec((tb, MIX_W), bwd_map), st],
        out_shape=[out_tok, out_tok, out_st, out_tok, out_tok, out_st],
        scratch_shapes=[pltpu.VMEM((2, 2, 128, 128), F32)],
        compiler_params=_cparams(("parallel", "arbitrary")),
        name="rwkv_scan",
    )(*operands)


def _rows_bd4(y, width):
    lane = lax.broadcasted_iota(jnp.int32, y.shape, 2) // width
    z = jnp.zeros_like(y)
    return jnp.concatenate([jnp.where(lane == h, y, z) for h in range(4)], axis=1)


def _block_scans(x, levels):
    rows, width = x.shape
    row = lax.broadcasted_iota(jnp.int32, (rows, width), 0)
    pre, suf = [x], [x]
    for m in range(1, levels + 1):
        size, half = 1 << m, 1 << (m - 1)
        p, s = pre[-1], suf[-1]
        if half < 8:
            off = row & (size - 1)
            p3 = p.reshape(rows // 8, 8, width)
            s3 = s.reshape(rows // 8, 8, width)
            addp = jnp.zeros_like(x)
            adds = jnp.zeros_like(x)
            for o in range(half):
                addp = jnp.where(off == half + o, pltpu.roll(p3, o + 1, 1).reshape(rows, width), addp)
                adds = jnp.where(off == o, pltpu.roll(s3, 8 - (half - o), 1).reshape(rows, width), adds)
        else:
            p4 = p.reshape(rows // size, size, width)
            s4 = s.reshape(rows // size, size, width)
            second = (row & half) != 0
            addp = jnp.where(second, jnp.broadcast_to(p4[:, half - 1:half, :], p4.shape).reshape(rows, width), 0.0)
            adds = jnp.where(second, 0.0, jnp.broadcast_to(s4[:, half:half + 1, :], s4.shape).reshape(rows, width))
        pre.append(p + addp)
        suf.append(s + adds)
    return pre, suf


def _gla_state_in(s4):
    sp = s4.reshape(4 * GLA_DK, 64)
    st = jnp.concatenate([sp, sp], axis=1).T[0:64, :]
    lane_h = lax.broadcasted_iota(jnp.int32, (64, 128), 1) // GLA_DK
    return jnp.concatenate([jnp.where(lane_h == h, st, 0.0) for h in range(4)], axis=0)


def _gla_state_store(sfin_ref, q, s_t):
    m = s_t[0:64] + s_t[64:128] + s_t[128:192] + s_t[192:256]
    nat = jnp.concatenate([m, jnp.zeros_like(m)], axis=0).T
    sfin_ref[q] = nat[:, 0:64].reshape(4, GLA_DK, 64)


def _gla_dir_parts(u_ref, ab_ref, abias_ref, rev):
    la = _log_sigmoid(_dot(u_ref[:, 768:896], ab_ref[...]) + abias_ref[...]) * (1.0 / GLA_GATE_NORM)
    pre, suf = _block_scans(la, 6)
    near, far = (suf, pre) if rev else (pre, suf)
    return dict(la=la, near=near, far=far, b=near[6], q=u_ref[:, 0:128] * float(GLA_DK ** -0.5),
                kg=u_ref[:, 128:256], vg=u_ref[:, 256:512])


def _gla_kernel(*refs, nb, spb, shared):
    if shared:
        u_f, ab_f, abias_f, ab_b, abias_b, s0_f, s0_b, o_f, sfin_f, o_b, sfin_b, s_scr = refs
        u_b = u_f
    else:
        u_f, u_b, ab_f, abias_f, ab_b, abias_b, s0_f, s0_b, o_f, sfin_f, o_b, sfin_b, s_scr = refs
    j = pl.program_id(1)

    if spb == 1:
        @pl.when(j == 0)
        def _():
            s_scr[0] = _gla_state_in(s0_f[0])
            s_scr[1] = _gla_state_in(s0_b[0])

    parts = (_gla_dir_parts(u_f, ab_f, abias_f, False), _gla_dir_parts(u_b, ab_b, abias_b, True))
    nch = u_f.shape[0] // CHUNK
    n3 = 2 * nch

    def cat3(xs):
        return jnp.concatenate([x.reshape(nch, CHUNK, x.shape[1]) for x in xs], axis=0)

    mshape = (2, 1, CHUNK, MIX_W)
    t_i = lax.broadcasted_iota(jnp.int32, mshape, 2)
    s_i = lax.broadcasted_iota(jnp.int32, mshape, 3) & (CHUNK - 1)
    sign = jnp.where(lax.broadcasted_iota(jnp.int32, mshape, 0) >= 1, -1, 1)

    def select(mask, x, other):
        return jnp.where(mask, x.reshape(2, nch, CHUNK, MIX_W),
                         other.reshape(2, nch, CHUNK, MIX_W)).reshape(n3, CHUNK, MIX_W)

    q3 = cat3([p["q"] for p in parts])
    k3 = cat3([p["kg"] for p in parts])
    v3 = cat3([p["vg"] for p in parts])
    att = select(s_i == t_i, _bmm_nt(q3, _rows_bd4(k3.astype(BF16), GLA_DK)), jnp.zeros((n3, CHUNK, MIX_W), F32))
    for lev in range(1, 7):
        size, half = 1 << lev, 1 << (lev - 1)
        same = (t_i // size) == (s_i // size)
        t_late = jnp.where((t_i & (size - 1)) >= half, 1, 0)
        s_late = jnp.where((s_i & (size - 1)) >= half, 1, 0)
        mask = same & ((t_late - s_late) * sign == 1)
        qe = cat3([p["q"] * jnp.exp(p["near"][lev - 1]) for p in parts])
        ke = cat3([p["kg"] * jnp.exp(p["far"][lev - 1] - p["la"]) for p in parts])
        att = select(mask, _bmm_nt(qe, _rows_bd4(ke.astype(BF16), GLA_DK)), att)
    o_intra = _bmm(att, _rows_bd4(v3.astype(BF16), 64))

    b3 = cat3([p["b"] for p in parts])
    b_last = jnp.concatenate([b3[:nch, CHUNK - 1:CHUNK, :], b3[nch:, 0:1, :]], axis=0)
    qb = q3 * jnp.exp(b3)
    bd_r = lax.broadcasted_iota(jnp.int32, (1, MIX_W, 128), 1) // 64
    bd_c = lax.broadcasted_iota(jnp.int32, (1, MIX_W, 128), 2) // GLA_DK
    inc = jnp.where(bd_r == bd_c, _bmm_tn(v3, k3 * jnp.exp(b_last - b3)), 0.0)
    decay = jnp.exp(b_last)

    cps = nch // spb
    dirs = ((0, False, s0_f, o_f, sfin_f), (1, True, s0_b, o_b, sfin_b))
    s_cur = [s_scr[0], s_scr[1]] if spb == 1 else [None, None]
    for ci in range(nch):
        for d, rev, s0_ref, o_ref, sfin_ref in dirs:
            c = (nch - 1 - ci) if rev else ci
            qs, ci_seq = divmod(ci, cps)
            qs = (spb - 1 - qs) if rev else qs
            if spb > 1 and ci_seq == 0:
                s_cur[d] = _gla_state_in(s0_ref[qs])
            g = d * nch + c
            o_ref[c * CHUNK:(c + 1) * CHUNK, :] = o_intra[g] + _dot_nt(qb[g], s_cur[d])
            s_cur[d] = s_cur[d] * decay[g] + inc[g]
            if spb > 1 and ci_seq == cps - 1:
                _gla_state_store(sfin_ref, qs, s_cur[d])
    if spb == 1:
        s_scr[0] = s_cur[0]
        s_scr[1] = s_cur[1]

        @pl.when(j == nb - 1)
        def _():
            _gla_state_store(sfin_f, 0, s_scr[0])
            _gla_state_store(sfin_b, 0, s_scr[1])


def _gla_scan(u_g, prm, s0, nseq, seq_len):
    spb, tb, nb = _scan_blocking(nseq, seq_len, GLA_TB)
    shared = nb == 1
    t = u_g.shape[0]
    fwd_map = lambda b, j: (b * nb + j, 0)
    bwd_map = lambda b, j: (b * nb + nb - 1 - j, 0)
    st = pl.BlockSpec((spb, 4, GLA_DK, 64), lambda b, j: (b, 0, 0, 0))
    in_specs = [pl.BlockSpec((tb, 896), fwd_map)]
    operands = [u_g]
    if not shared:
        in_specs.append(pl.BlockSpec((tb, 896), bwd_map))
        operands.append(u_g)
    in_specs += [_full((128, 128)), _full((1, 128))] * 2 + [st, st]
    operands += [prm["gla_ab0"], prm["gla_abias0"], prm["gla_ab1"], prm["gla_abias1"], s0[0], s0[1]]
    out_tok = jax.ShapeDtypeStruct((t, MIX_W), F32)
    out_st = jax.ShapeDtypeStruct((nseq, 4, GLA_DK, 64), F32)
    return pl.pallas_call(
        functools.partial(_gla_kernel, nb=nb, spb=spb, shared=shared),
        grid=(nseq // spb, nb),
        in_specs=in_specs,
        out_specs=[pl.BlockSpec((tb, MIX_W), fwd_map), st, pl.BlockSpec((tb, MIX_W), bwd_map), st],
        out_shape=[out_tok, out_st, out_tok, out_st],
        scratch_shapes=[pltpu.VMEM((2, 256, 128), F32)],
        compiler_params=_cparams(("parallel", "arbitrary")),
        name="gla_scan",
    )(*operands)


def _merge_kernel(x_ref, mod_ref, g_ref, ya_ref, yb_ref, of_ref, ob_ref, exf_ref, exb_ref, cag_ref,
                  gf_ref, gb_ref, gout_ref, gn_ref, bg_ref, gnorm_ref, wg_ref, wb_ref, wo_ref, x1_ref):
    mod = mod_ref[0]
    x = x_ref[...]
    h = _rms_mod(x, g_ref[...], mod[:, D_MODEL:2 * D_MODEL], mod[:, 0:D_MODEL]).astype(BF16)
    mean64 = _seg_ones(MIX_W, 64, 1.0 / 64.0)
    o = of_ref[...] + ob_ref[...]
    mu = _dot_exact_rhs(o, mean64)
    oc = o - mu
    var = _dot_exact_rhs(oc * oc, mean64)
    gate_c = _dot(jax.nn.sigmoid(cag_ref[...]), bg_ref[...])
    y_c = (oc * lax.rsqrt(var + RWKV_GN_EPS) * gn_ref[...] + exf_ref[...] + exb_ref[...]) * gate_c
    og = gf_ref[...] + gb_ref[...]
    gout = gout_ref[...]
    y_d = (og * lax.rsqrt(_dot_exact_rhs(og * og, mean64) + EPS) * gnorm_ref[...]
           * (gout * jax.nn.sigmoid(gout)))
    ys = (ya_ref[...], yb_ref[...], y_c, y_d)
    merged = jnp.zeros((TM, D_MODEL), F32)
    for i in range(4):
        g0 = P_MIX + i * D_MODEL
        gate = jax.nn.sigmoid(_dot_nt(h, wg_ref[g0:g0 + D_MODEL, :]))
        merged = merged + gate * _dot(ys[i], wb_ref[i])
    out = _dot(merged, wo_ref[...])
    x1_ref[...] = x + mod[:, 2 * D_MODEL:3 * D_MODEL] * out


def _merge(x, mod3, g, y_a, y_b, o_f, o_b, ex_f, ex_b, u_r, g_f, g_b, u_g, prm, big, layer,
           cond_base, rows_per_cond):
    t = x.shape[0]
    row = lambda i: (i, 0)
    mix = pl.BlockSpec((TM, MIX_W), row)
    vec = _full((1, MIX_W))
    lay3 = lambda i: (layer, 0, 0)
    return pl.pallas_call(
        _merge_kernel,
        grid=(t // TM,),
        in_specs=[pl.BlockSpec((TM, D_MODEL), row),
                  pl.BlockSpec((1, 1, 6 * D_MODEL), _cond_map(cond_base, rows_per_cond)),
                  _full((1, D_MODEL)),
                  mix, mix, mix, mix, mix, mix,
                  pl.BlockSpec((TM, 128), lambda i: (i, 7)),
                  mix, mix,
                  pl.BlockSpec((TM, MIX_W), lambda i: (i, 2)),
                  vec, _full((128, MIX_W)), vec,
                  pl.BlockSpec((None, big["w_in_t"].shape[1], D_MODEL), lay3, pipeline_mode=pl.Buffered(1)),
                  pl.BlockSpec((None, 4, MIX_W, D_MODEL), lambda i: (layer, 0, 0, 0)),
                  pl.BlockSpec((None, D_MODEL, D_MODEL), lay3)],
        out_specs=pl.BlockSpec((TM, D_MODEL), row),
        out_shape=jax.ShapeDtypeStruct((t, D_MODEL), F32),
        compiler_params=_cparams(("parallel",)),
        name="merge_out",
    )(x, mod3, g, y_a, y_b, o_f, o_b, ex_f, ex_b, u_r, g_f, g_b, u_g,
      prm["gn"], prm["bg"], prm["gla_norm"], big["w_in_t"], big["w_branch"], big["w_out"])


def _mlp_kernel(x_ref, mod_ref, g_ref, w1_ref, w2_ref, fg_ref, out_ref, *, final):
    mod = mod_ref[0]
    x = x_ref[...]
    h = _rms_mod(x, g_ref[...], mod[:, 4 * D_MODEL:5 * D_MODEL], mod[:, 3 * D_MODEL:4 * D_MODEL]).astype(BF16)
    ff = jnp.zeros(x.shape, F32)
    for c in range(D_FF // D_MODEL):
        cols = slice(c * D_MODEL, (c + 1) * D_MODEL)
        a = jnp.maximum(jnp.dot(h, w1_ref[:, cols], preferred_element_type=F32), 0.0)
        ff = ff + _dot(a * a, w2_ref[cols, :])
    x2 = x + mod[:, 5 * D_MODEL:6 * D_MODEL] * ff
    if final:
        x2 = x2 * lax.rsqrt(jnp.mean(x2 * x2, axis=-1, keepdims=True) + EPS) * fg_ref[...]
    out_ref[...] = x2


def _mlp(x1, mod3, g, w1, w2, layer, final_g, cond_base, rows_per_cond):
    t = x1.shape[0]
    row = lambda i: (i, 0)
    lay3 = lambda i: (layer, 0, 0)
    return pl.pallas_call(
        functools.partial(_mlp_kernel, final=layer == DEPTH - 1),
        grid=(t // TM_WIDE,),
        in_specs=[pl.BlockSpec((TM_WIDE, D_MODEL), row),
                  pl.BlockSpec((1, 1, 6 * D_MODEL), _cond_map(cond_base, rows_per_cond, TM_WIDE)),
                  _full((1, D_MODEL)),
                  pl.BlockSpec((None, D_MODEL, D_FF), lay3, pipeline_mode=pl.Buffered(1)),
                  pl.BlockSpec((None, D_FF, D_MODEL), lay3, pipeline_mode=pl.Buffered(1)),
                  _full((1, D_MODEL))],
        out_specs=pl.BlockSpec((TM_WIDE, D_MODEL), row),
        out_shape=jax.ShapeDtypeStruct((t, D_MODEL), F32),
        compiler_params=_cparams(("parallel",)),
        name="mlp",
    )(x1, mod3, g, w1, w2, final_g)


def _pos_tables(n_tok, d):
    rows = n_tok // GRID_W
    assert rows == GRID_W
    quarter = d // 4
    omega = 1.0 / (POS_BASE ** (jnp.arange(quarter, dtype=F32) / quarter))
    ar = jnp.arange(rows, dtype=F32)[:, None] * omega
    ac = jnp.arange(GRID_W, dtype=F32)[:, None] * omega
    return (jnp.concatenate([jnp.sin(ar), jnp.cos(ar)], axis=-1),
            jnp.concatenate([jnp.sin(ac), jnp.cos(ac)], axis=-1))


def _pad_rows(m, start, total):
    return jnp.zeros((total, m.shape[1]), m.dtype).at[start:start + m.shape[0]].set(m)


def _layer_params(l, p):
    out = {
        "n1": p["norm1_g"][l].reshape(1, D_MODEL),
        "n2": p["norm2_g"][l].reshape(1, D_MODEL),
        "pool_w": jnp.einsum("gcd,gh->gchd", p["pool_w"][l], jnp.eye(4, dtype=F32)).reshape(MIX_W, MIX_W).astype(BF16),
        "pool_scale": p["pool_scale"][l].reshape(1, MIX_W),
        "gn": p["rwkv_gn"][l].reshape(1, MIX_W),
        "bg": _pad_rows(p["rwkv_bg"][l], 64, 128).astype(BF16),
        "gla_norm": p["gla_norm"][l].reshape(1, MIX_W),
    }
    for d in range(2):
        out["rwkv%d" % d] = {
            "mu": p["rwkv_mu"][l].reshape(1, 768),
            "kk": p["rwkv_kk"][l].reshape(1, MIX_W),
            "ka": p["rwkv_ka"][l].reshape(1, MIX_W),
            "rk": p["rwkv_rk"][l].reshape(1, MIX_W),
            "w0": p["rwkv_w0"][l, d].reshape(1, MIX_W),
            "a0": p["rwkv_a0"][l, d].reshape(1, MIX_W),
            "bw": _pad_rows(p["rwkv_bw"][l, d], 64 * d, 128).astype(BF16),
            "ba": _pad_rows(p["rwkv_ba"][l, d], 32 * d, 128).astype(BF16),
        }
        out["gla_ab%d" % d] = _pad_rows(p["gla_ab"][l, d], 16 * d, 128).astype(BF16)
        out["gla_abias%d" % d] = p["gla_abias"][l, d].reshape(1, 128)
    return out


def _run_layer(x, mod3, prm, big, layer, final_g, s_rwkv0, s_gla0, nseq, seq_len, cond_base, rows_per_cond,
               pos_tables=None):
    if pos_tables is None:
        u_p, u_f, u_r, u_g = _inproj(x, mod3, prm["n1"], big["w_in_t"], layer, cond_base, rows_per_cond)
    else:
        x, u_p, u_f, u_r, u_g = _inproj(x, mod3, prm["n1"], big["w_in_t"], layer, cond_base, rows_per_cond,
                                        pos_tables, seq_len)
    y_a = _pool(u_p, prm["pool_w"], prm["pool_scale"], nseq, seq_len)
    if seq_len == FFT_N1 * FFT_N1:
        y_b = _four_large(u_f, nseq, seq_len)
    else:
        y_b = _four_small(u_f, nseq, seq_len)
    o_f, ex_f, sr_f, o_b, ex_b, sr_b = _rwkv_scan(u_r, prm, s_rwkv0, nseq, seq_len)
    g_f, sg_f, g_b, sg_b = _gla_scan(u_g, prm, s_gla0, nseq, seq_len)
    x1 = _merge(x, mod3, prm["n1"], y_a, y_b, o_f, o_b, ex_f, ex_b, u_r, g_f, g_b, u_g, prm, big, layer,
                cond_base, rows_per_cond)
    x2 = _mlp(x1, mod3, prm["n2"], big["w1"], big["w2"], layer, final_g, cond_base, rows_per_cond)
    return x2, jnp.stack([sr_f, sr_b], axis=1), jnp.stack([sg_f, sg_b], axis=1)


def kernel(x_prompt, x_sample, state_rwkv, state_gla, c, c_ctx, ada_w, ada_b, norm1_g, norm2_g, w_in, pool_w, pool_scale, rwkv_mu, rwkv_w0, rwkv_bw, rwkv_a0, rwkv_ba, rwkv_kk, rwkv_ka, rwkv_bg, rwkv_rk, rwkv_gn, gla_ab, gla_abias, gla_norm, w_branch, w_out, mlp_w1, mlp_w2, final_g):
    p = dict(pool_w=pool_w, pool_scale=pool_scale, rwkv_mu=rwkv_mu, rwkv_w0=rwkv_w0,
             rwkv_bw=rwkv_bw, rwkv_a0=rwkv_a0, rwkv_ba=rwkv_ba, rwkv_kk=rwkv_kk, rwkv_ka=rwkv_ka,
             rwkv_bg=rwkv_bg, rwkv_rk=rwkv_rk, rwkv_gn=rwkv_gn, gla_ab=gla_ab, gla_abias=gla_abias,
             gla_norm=gla_norm, norm1_g=norm1_g, norm2_g=norm2_g)
    big = dict(w_in_t=jnp.swapaxes(w_in, 1, 2).astype(BF16),
               w_branch=w_branch.astype(BF16), w_out=w_out.astype(BF16),
               w1=mlp_w1.astype(BF16), w2=mlp_w2.astype(BF16))
    bp, lp, _ = x_prompt.shape
    bs, ls, _ = x_sample.shape
    cond8 = jnp.zeros((8, D_MODEL), F32).at[0].set(c_ctx).at[1:1 + bs].set(c)
    mods = _ada(cond8, ada_w, ada_b)
    fg = final_g.reshape(1, D_MODEL)

    xp = x_prompt.reshape(bp * lp, D_MODEL)
    xs = x_sample.reshape(bs * ls, D_MODEL)
    pos_tables = _pos_tables(ls, D_MODEL)
    zr = jnp.zeros((2, bp) + state_rwkv.shape[3:], F32)
    zg = jnp.zeros((2, bp) + state_gla.shape[3:], F32)
    new_r, new_g = [], []
    for l in range(DEPTH):
        prm = _layer_params(l, p)
        mod3 = mods[l].reshape(8, 1, 6 * D_MODEL)
        xp, s_r, s_g = _run_layer(xp, mod3, prm, big, l, fg, zr, zg, bp, lp, 0, bp * lp)
        new_r.append(s_r)
        new_g.append(s_g)
        xs, _, _ = _run_layer(xs, mod3, prm, big, l, fg, jnp.swapaxes(state_rwkv[:, l], 0, 1),
                              jnp.swapaxes(state_gla[:, l], 0, 1), bs, ls, 1, ls,
                              pos_tables if l == 0 else None)
    y_prompt = xp.reshape(bp, lp, D_MODEL)
    y_sample = xs.reshape(bs, ls, D_MODEL)
    return (y_prompt, y_sample, jnp.stack(new_r, axis=1), jnp.stack(new_g, axis=1))
```

```python
import functools

import numpy as np
import jax
import jax.numpy as jnp
from jax import lax
from jax.experimental import pallas as pl
from jax.experimental.pallas import tpu as pltpu

F32 = jnp.float32
BF16 = jnp.bfloat16

D_MODEL = 1024
DEPTH = 2
GRID_W = 64
POS_BASE = 10000.0
MIX_W = 256
POOL_WINDOWS = (2, 4, 8, 16)
POOL_HALO = 8
RWKV_DECAY_SCALE = 0.606531
RWKV_GN_EPS = 64e-5
GLA_DK = 32
GLA_GATE_NORM = 16.0
D_FF = 4 * D_MODEL
EPS = 1e-6
P_MIX = 2336
P_MIX_PAD = 2432
CHUNK = 64
SUB = 16

TM = 512
TM_WIDE = 1024
POOL_TB = 1024
RWKV_TB = 512
GLA_TB = 1024
FFT_N1 = 64
VMEM_LIMIT = 56 * 1024 * 1024


def _cparams(sem):
    return pltpu.CompilerParams(dimension_semantics=sem, vmem_limit_bytes=VMEM_LIMIT)


def _full(shape):
    n = len(shape)
    return pl.BlockSpec(shape, lambda *_: (0,) * n)


def _dot(a, b):
    return jnp.dot(a.astype(BF16), b.astype(BF16), preferred_element_type=F32)


def _dot_nt(a, b):
    return lax.dot_general(a.astype(BF16), b.astype(BF16), (((1,), (1,)), ((), ())),
                           preferred_element_type=F32)


def _split2(a):
    hi = a.astype(BF16)
    lo = (a - hi.astype(F32)).astype(BF16)
    return hi, lo


def _dot_exact_rhs(a, w):
    hi, lo = _split2(a)
    return (jnp.dot(hi, w, preferred_element_type=F32) + jnp.dot(lo, w, preferred_element_type=F32))


def _dot_exact_lhs(w, a):
    hi, lo = _split2(a)
    return jnp.dot(w, hi, preferred_element_type=F32) + jnp.dot(w, lo, preferred_element_type=F32)


def _seg_ones(n, seg, scale=1.0):
    r = lax.broadcasted_iota(jnp.int32, (n, n), 0)
    c = lax.broadcasted_iota(jnp.int32, (n, n), 1)
    return jnp.where((r // seg) == (c // seg), scale, 0.0).astype(BF16)


def _log_sigmoid(x):
    return jnp.minimum(x, 0.0) - jnp.log(1.0 + jnp.exp(-jnp.abs(x)))


def _rms_mod(x, g, sc, sh):
    ms = jnp.mean(x * x, axis=-1, keepdims=True)
    return x * lax.rsqrt(ms + EPS) * g * (1.0 + sc) + sh


def _ada_kernel(c_ref, w_ref, b_ref, o_ref):
    c = c_ref[...]
    s = c * jax.nn.sigmoid(c)
    hi, lo = _split2(s)
    w = w_ref[0].astype(BF16)
    o_ref[0] = (jnp.dot(hi, w, preferred_element_type=F32) + jnp.dot(lo, w, preferred_element_type=F32)
                + b_ref[0])


def _ada(cond8, ada_w, ada_b):
    tn = 3072
    n = 6 * D_MODEL
    return pl.pallas_call(
        _ada_kernel,
        grid=(DEPTH, n // tn),
        in_specs=[_full((8, D_MODEL)),
                  pl.BlockSpec((1, D_MODEL, tn), lambda l, j: (l, 0, j)),
                  pl.BlockSpec((1, 1, tn), lambda l, j: (l, 0, j))],
        out_specs=pl.BlockSpec((1, 8, tn), lambda l, j: (l, 0, j)),
        out_shape=jax.ShapeDtypeStruct((DEPTH, 8, n), F32),
        compiler_params=_cparams(("parallel", "parallel")),
        name="ada_mod",
    )(cond8, ada_w, ada_b.reshape(DEPTH, 1, n))


def _inproj_kernel(x_ref, mod_ref, g_ref, w_ref, up_ref, uf_ref, ur_ref, ug_ref):
    _inproj_body(x_ref[...], mod_ref, g_ref, w_ref, up_ref, uf_ref, ur_ref, ug_ref)


def _inproj_pos_kernel(x_ref, rr_ref, cc_ref, mod_ref, g_ref, w_ref, xo_ref, up_ref, uf_ref, ur_ref, ug_ref,
                       *, seq_len):
    tile = x_ref.shape[0]
    nrow = tile // GRID_W
    g0 = pl.multiple_of(((pl.program_id(0) * tile) % seq_len) // GRID_W, nrow)
    half = D_MODEL // 2
    rpart = jnp.broadcast_to(rr_ref[pl.ds(g0, nrow), :][:, None, :], (nrow, GRID_W, half)).reshape(tile, half)
    cpart = jnp.concatenate([cc_ref[...]] * nrow, axis=0)
    x = x_ref[...] + jnp.concatenate([rpart, cpart], axis=1)
    xo_ref[...] = x
    _inproj_body(x, mod_ref, g_ref, w_ref, up_ref, uf_ref, ur_ref, ug_ref)


def _inproj_body(x, mod_ref, g_ref, w_ref, up_ref, uf_ref, ur_ref, ug_ref):
    mod = mod_ref[0]
    h = _rms_mod(x, g_ref[...], mod[:, D_MODEL:2 * D_MODEL], mod[:, 0:D_MODEL]).astype(BF16)
    up_ref[...] = _dot_nt(h, w_ref[0:256, :])
    uf_ref[...] = _dot_nt(h, w_ref[256:512, :])
    ur_ref[...] = _dot_nt(h, w_ref[512:1536, :])
    ug_ref[...] = _dot_nt(h, w_ref[1536:P_MIX_PAD, :])


def _cond_map(cond_base, rows_per_cond, tile=TM):
    return lambda i: (cond_base + (i * tile) // rows_per_cond, 0, 0)


def _inproj(x, mod3, g, w_t, layer, cond_base, rows_per_cond, pos_tables=None, seq_len=None):
    t = x.shape[0]
    row = lambda i: (i, 0)
    tm = TM_WIDE
    in_specs = [pl.BlockSpec((1, 1, 6 * D_MODEL), _cond_map(cond_base, rows_per_cond, tm)),
                _full((1, D_MODEL)),
                pl.BlockSpec((None, P_MIX_PAD, D_MODEL), lambda i: (layer, 0, 0), pipeline_mode=pl.Buffered(1))]
    out_specs = [pl.BlockSpec((tm, 256), row), pl.BlockSpec((tm, 256), row),
                 pl.BlockSpec((tm, 1024), row), pl.BlockSpec((tm, 896), row)]
    out_shape = [jax.ShapeDtypeStruct((t, 256), F32), jax.ShapeDtypeStruct((t, 256), F32),
                 jax.ShapeDtypeStruct((t, 1024), F32), jax.ShapeDtypeStruct((t, 896), F32)]
    x_spec = pl.BlockSpec((tm, D_MODEL), row)
    if pos_tables is None:
        return pl.pallas_call(
            _inproj_kernel, grid=(t // tm,), in_specs=[x_spec] + in_specs, out_specs=out_specs,
            out_shape=out_shape, compiler_params=_cparams(("parallel",)), name="in_proj",
        )(x, mod3, g, w_t)
    tab = _full((GRID_W, D_MODEL // 2))
    return pl.pallas_call(
        functools.partial(_inproj_pos_kernel, seq_len=seq_len), grid=(t // tm,),
        in_specs=[x_spec, tab, tab] + in_specs, out_specs=[x_spec] + out_specs,
        out_shape=[jax.ShapeDtypeStruct((t, D_MODEL), F32)] + out_shape,
        compiler_params=_cparams(("parallel",)), name="in_proj_pos",
    )(x, pos_tables[0], pos_tables[1], mod3, g, w_t)


def _pool_kernel(prev_ref, cur_ref, next_ref, wbd_ref, scale_ref, y_ref, *, seq_len):
    j = pl.program_id(1)
    tb = cur_ref.shape[0]
    nb = seq_len // tb
    cur = cur_ref[...]
    prev = jnp.where(j > 0, prev_ref[...], 0.0)
    nxt = jnp.where(j < nb - 1, next_ref[...], 0.0)
    ext = jnp.concatenate([prev, cur, nxt], axis=0)
    n_ext = tb + 2 * POOL_HALO

    def shift(x, k):
        return pltpu.roll(x, k % n_ext, 0)

    sums = [ext + shift(ext, 1)]
    for k in (1, 2, 4):
        sums.append(shift(sums[-1], -k) + shift(sums[-1], k))
    assert POOL_WINDOWS == (2, 4, 8, 16)
    pos = j * tb + lax.broadcasted_iota(jnp.int32, (tb, MIX_W), 0)
    group = lax.broadcasted_iota(jnp.int32, (tb, MIX_W), 1) // (MIX_W // len(POOL_WINDOWS))
    pooled = jnp.zeros((tb, MIX_W), F32)
    for gi, win in enumerate(POOL_WINDOWS):
        before, after = win // 2, win - win // 2 - 1
        s = sums[gi][POOL_HALO:POOL_HALO + tb]
        cnt = (jnp.minimum(pos + after, seq_len - 1) - jnp.maximum(pos - before, 0) + 1).astype(F32)
        pooled = jnp.where(group == gi, s / cnt - cur, pooled)
    y_ref[...] = _dot(pooled, wbd_ref[...]) * scale_ref[...]


def _pool(u_p, wbd, scale, nseq, seq_len):
    tb = min(seq_len, POOL_TB)
    nb = seq_len // tb
    hb = tb // POOL_HALO
    nh = u_p.shape[0] // POOL_HALO
    return pl.pallas_call(
        functools.partial(_pool_kernel, seq_len=seq_len),
        grid=(nseq, nb),
        in_specs=[pl.BlockSpec((POOL_HALO, MIX_W), lambda b, j: (jnp.maximum((b * nb + j) * hb - 1, 0), 0)),
                  pl.BlockSpec((tb, MIX_W), lambda b, j: (b * nb + j, 0)),
                  pl.BlockSpec((POOL_HALO, MIX_W), lambda b, j: (jnp.minimum((b * nb + j + 1) * hb, nh - 1), 0)),
                  _full((MIX_W, MIX_W)), _full((1, MIX_W))],
        out_specs=pl.BlockSpec((tb, MIX_W), lambda b, j: (b * nb + j, 0)),
        out_shape=jax.ShapeDtypeStruct(u_p.shape, F32),
        compiler_params=_cparams(("parallel", "parallel")),
        name="pool_mixer",
    )(u_p, u_p, u_p, wbd, scale)


def _dft_tables(n):
    t = np.arange(n)
    ang = 2.0 * np.pi * ((np.outer(t, t)) % n) / n
    return np.cos(ang), np.sin(ang)


def _bd_np(m, copies):
    k = m.shape[0]
    out = np.zeros((k * copies, k * copies), m.dtype)
    for i in range(copies):
        out[i * k:(i + 1) * k, i * k:(i + 1) * k] = m
    return out


def _four_small_kernel(z_ref, cs_ref, cbd_ref, sbd_ref, y_ref, *, seq_len, norm):
    cs = cs_ref[...].astype(BF16)
    zr, zi = [], []
    for g in range(z_ref.shape[0] // seq_len):
        zf = _dot(cs, z_ref[g * seq_len:(g + 1) * seq_len, :])
        zr.append(zf[:seq_len])
        zi.append(zf[seq_len:])
    y = _dot(jnp.concatenate(zr, axis=0), cbd_ref[...]) + _dot(jnp.concatenate(zi, axis=0), sbd_ref[...])
    y_ref[...] = y * norm


def _four_small(u_f, nseq, seq_len):
    c, s = _dft_tables(seq_len)
    cs = jnp.asarray(np.concatenate([c, -s], axis=0), F32)
    c64, s64 = _dft_tables(MIX_W // 4)
    cbd = jnp.asarray(_bd_np(c64, 4), F32)
    sbd = jnp.asarray(_bd_np(s64, 4), F32)
    per_step = 4 if nseq % 4 == 0 else 1
    rows = per_step * seq_len
    return pl.pallas_call(
        functools.partial(_four_small_kernel, seq_len=seq_len, norm=float((seq_len * 64) ** -0.5)),
        grid=(nseq // per_step,),
        in_specs=[pl.BlockSpec((rows, MIX_W), lambda b: (b, 0)),
                  _full((2 * seq_len, seq_len)), _full((MIX_W, MIX_W)), _full((MIX_W, MIX_W))],
        out_specs=pl.BlockSpec((rows, MIX_W), lambda b: (b, 0)),
        out_shape=jax.ShapeDtypeStruct(u_f.shape, F32),
        compiler_params=_cparams(("parallel",)),
        name="fourier_small",
    )(u_f, cs, cbd, sbd)


def _four_large_kernel(z0_ref, z1_ref, fs_ref, g_ref, cbd_ref, sbd_ref, y_ref, a0_scr, a1_scr, xr_scr, xi_scr,
                       y0_scr, y1_scr, *, norm):
    n1 = FFT_N1
    fs = fs_ref[...].astype(BF16)

    def rows(ref0, ref1, start, stride):
        sl = pl.ds(start, n1, stride=stride)
        return jnp.concatenate([ref0[sl, :], ref1[sl, :]], axis=1)

    for t2 in range(n1):
        a = _dot(fs, rows(z0_ref, z1_ref, t2, n1))
        a0_scr[t2 * 2 * n1:(t2 + 1) * 2 * n1, :] = a[:, :128]
        a1_scr[t2 * 2 * n1:(t2 + 1) * 2 * n1, :] = a[:, 128:]
    for f1 in range(n1):
        a = jnp.concatenate([rows(a0_scr, a1_scr, f1, 2 * n1), rows(a0_scr, a1_scr, n1 + f1, 2 * n1)],
                            axis=0)
        x = _dot(g_ref[f1], a)
        xr_scr[f1 * n1:(f1 + 1) * n1, :] = x[:n1]
        xi_scr[f1 * n1:(f1 + 1) * n1, :] = x[n1:]
    y = (_dot(xr_scr[...], cbd_ref[...]) + _dot(xi_scr[...], sbd_ref[...])) * norm
    y0_scr[...] = y[:, :128]
    y1_scr[...] = y[:, 128:]
    for f2 in range(n1):
        y_ref[f2 * n1:(f2 + 1) * n1, :] = rows(y0_scr, y1_scr, f2, n1)


def _four_large(u_f, nseq, seq_len):
    n1 = FFT_N1
    assert seq_len == n1 * n1
    c64, s64 = _dft_tables(n1)
    fs = jnp.asarray(np.concatenate([c64, -s64], axis=0), F32)
    f1 = np.arange(n1)[:, None, None]
    f2 = np.arange(n1)[None, :, None]
    t2 = np.arange(n1)[None, None, :]
    ang = 2.0 * np.pi * ((t2 * (f1 + n1 * f2)) % seq_len) / seq_len
    gr, gi = np.cos(ang), -np.sin(ang)
    g = jnp.asarray(np.concatenate([np.concatenate([gr, -gi], axis=2),
                                    np.concatenate([gi, gr], axis=2)], axis=1), F32)
    cbd = jnp.asarray(_bd_np(c64, 4), F32)
    sbd = jnp.asarray(_bd_np(s64, 4), F32)
    half = MIX_W // 2
    return pl.pallas_call(
        functools.partial(_four_large_kernel, norm=float((seq_len * 64) ** -0.5)),
        grid=(nseq,),
        in_specs=[pl.BlockSpec((seq_len, half), lambda b: (b, 0)), pl.BlockSpec((seq_len, half), lambda b: (b, 1)),
                  _full((2 * n1, n1)), _full((n1, 2 * n1, 2 * n1)), _full((MIX_W, MIX_W)), _full((MIX_W, MIX_W))],
        out_specs=pl.BlockSpec((seq_len, MIX_W), lambda b: (b, 0)),
        out_shape=jax.ShapeDtypeStruct(u_f.shape, F32),
        scratch_shapes=[pltpu.VMEM((2 * seq_len, half), F32), pltpu.VMEM((2 * seq_len, half), F32),
                        pltpu.VMEM((seq_len, MIX_W), F32), pltpu.VMEM((seq_len, MIX_W), F32),
                        pltpu.VMEM((seq_len, half), F32), pltpu.VMEM((seq_len, half), F32)],
        compiler_params=_cparams(("parallel",)),
        name="fourier_large",
    )(u_f, u_f, fs, g, cbd, sbd)


def _bmm(x, y):
    return lax.dot_general(x.astype(BF16), y.astype(BF16), (((2,), (1,)), ((0,), (0,))),
                           preferred_element_type=F32)


def _bmm_nt(x, y):
    return lax.dot_general(x.astype(BF16), y.astype(BF16), (((2,), (2,)), ((0,), (0,))),
                           preferred_element_type=F32)


def _bmm_tn(x, y):
    return lax.dot_general(x.astype(BF16), y.astype(BF16), (((1,), (1,)), ((0,), (0,))),
                           preferred_element_type=F32)


def _slab_bd(y):
    lane = lax.broadcasted_iota(jnp.int32, y.shape, 2)
    z = jnp.zeros_like(y)
    return jnp.concatenate([jnp.where(lane < 64, y, z), jnp.where(lane >= 64, y, z)], axis=1)


def _mm(x, y):
    return _bmm(x, _slab_bd(y.astype(BF16)))


def _mm_pair(x, y1, y2):
    w = jnp.concatenate([_slab_bd(y1.astype(BF16)), _slab_bd(y2.astype(BF16))], axis=2)
    out = _bmm(x, w)
    return out[:, :, :128], out[:, :, 128:]


def _mm3(xs, y):
    yh, yl = _split2(y)
    bh, bl = _slab_bd(yh), _slab_bd(yl)
    parts = [_split2(x) for x in xs]
    xh = jnp.concatenate([p[0] for p in parts], axis=1)
    xl = jnp.concatenate([p[1] for p in parts], axis=1)
    hx = _bmm(xh, jnp.concatenate([bh, bl], axis=2))
    out = hx[:, :, :128] + (hx[:, :, 128:] + _bmm(xl, bh))
    return [out[:, i * CHUNK:(i + 1) * CHUNK] for i in range(len(xs))]


def _to_slabs(x):
    nch = x.shape[0] // CHUNK
    x4 = x.reshape(nch, CHUNK, MIX_W)
    return jnp.stack([x4[:, :, :128], x4[:, :, 128:]], axis=1).reshape(2 * nch, CHUNK, 128)


def _tri_inverse(n_mat, eye, blk):
    nd = jnp.where(blk, n_mat, 0.0)
    no = n_mat - nd
    n2, = _mm3([nd], nd)
    n4, y = _mm3([n2, nd], n2)
    x = n2 - nd - y
    n8, y = _mm3([n4, x], n4)
    x = x + n4 + y
    y, = _mm3([x], n8)
    x = x + n8 + y
    td = eye + x
    e = _mm(td, no)
    e2 = _mm(e, e)
    e3 = _mm(e, e2)
    return td + _mm(e2 - e - e3, td)


def _rwkv_state_to_bd(s4):
    bd_r = lax.broadcasted_iota(jnp.int32, (128, 128), 0) // 64
    bd_c = lax.broadcasted_iota(jnp.int32, (128, 128), 1) // 64
    pairs = []
    for hp in range(2):
        sp = s4[2 * hp:2 * hp + 2].reshape(128, 64)
        pairs.append(jnp.where(bd_r == bd_c, jnp.concatenate([sp, sp], axis=1), 0.0))
    return jnp.stack(pairs, axis=0)


def _rwkv_state_store(sfin_ref, q, s_bd):
    for hp in range(2):
        sfin_ref[q, 2 * hp] = s_bd[hp, 0:64, 0:64]
        sfin_ref[q, 2 * hp + 1] = s_bd[hp, 64:128, 64:128]


def _rwkv_prep(cur_ref, prev_ref, next_ref, mu_ref, kkw_ref, jj, nb, spb):
    tb = cur_ref.shape[0]
    seq_rows = tb // spb
    z = cur_ref[:, 0:768]
    rowid = lax.broadcasted_iota(jnp.int32, (tb, 768), 0)
    prow = jnp.where(rowid == 0, jnp.where(jj > 0, prev_ref[POOL_HALO - 1:POOL_HALO, :], 0.0), 0.0)
    nrow = jnp.where(rowid == tb - 1, jnp.where(jj < nb - 1, next_ref[0:1, :], 0.0), 0.0)
    in_seq = rowid & (seq_rows - 1)
    up = jnp.where(in_seq == 0, prow, pltpu.roll(z, 1, 0))
    dn = jnp.where(in_seq == seq_rows - 1, nrow, pltpu.roll(z, tb - 1, 0))
    zm = z + mu_ref[...] * (0.5 * (up + dn) - z)
    r = zm[:, 0:256]
    k = zm[:, 256:512]
    v = zm[:, 512:768]
    kk = k * kkw_ref[...]
    kk = kk * lax.rsqrt(_dot(kk * kk, _seg_ones(MIX_W, 64)) + EPS)
    return r, k, v, kk


def _rwkv_dir_slabs(cur_ref, r, k, v, kk, ka_ref, rk_ref, w0_ref, a0_ref, bw_ref, ba_ref, ex_ref, rev):
    tb = cur_ref.shape[0]
    w_log = w0_ref[...] + _dot(jnp.tanh(cur_ref[:, 768:896]), bw_ref[...])
    lw = -RWKV_DECAY_SCALE * jax.nn.sigmoid(w_log)
    a = jax.nn.sigmoid(a0_ref[...] + _dot(cur_ref[:, 896:1024], ba_ref[...]))
    kd = k * (1.0 + (a - 1.0) * ka_ref[...])
    ex_ref[...] = _dot(r * kd * rk_ref[...], _seg_ones(MIX_W, 64)) * v
    tr = lax.broadcasted_iota(jnp.int32, (tb, tb), 0)
    tc = lax.broadcasted_iota(jnp.int32, (tb, tb), 1)
    same_chunk = (tr // CHUNK) == (tc // CHUNK)
    tri = jnp.where(same_chunk & ((tc >= tr) if rev else (tc <= tr)), 1.0, 0.0).astype(BF16)
    cum = _dot_exact_lhs(tri, lw)
    e_in = jnp.exp(cum)
    e_out = jnp.exp(-cum)
    last = 0 if rev else CHUNK - 1
    return dict(rt=_to_slabs(r * e_in), kh=_to_slabs((kd * e_out).astype(BF16)),
                bh=_to_slabs((kk * a * e_out).astype(BF16)),
                kt=_to_slabs((kk * jnp.exp(cum - lw)).astype(BF16)), vv=_to_slabs(v.astype(BF16)),
                gam=_to_slabs(e_in)[:, last:last + 1, :])


def _rwkv_kernel(*refs, nb, spb, shared):
    if shared:
        (cur_f, prev_f, next_f, mu_ref, kkw_ref, ka_ref, rk_ref, w0_f, a0_f, bw_f, ba_f, w0_b, a0_b, bw_b, ba_b,
         s0_f, s0_b, o_f, ex_f, sfin_f, o_b, ex_b, sfin_b, s_scr) = refs
        cur_b = cur_f
    else:
        (cur_f, prev_f, next_f, cur_b, prev_b, next_b, mu_ref, kkw_ref, ka_ref, rk_ref, w0_f, a0_f, bw_f, ba_f,
         w0_b, a0_b, bw_b, ba_b, s0_f, s0_b, o_f, ex_f, sfin_f, o_b, ex_b, sfin_b, s_scr) = refs
    j = pl.program_id(1)

    if spb == 1:
        @pl.when(j == 0)
        def _():
            s_scr[0] = _rwkv_state_to_bd(s0_f[0])
            s_scr[1] = _rwkv_state_to_bd(s0_b[0])

    pf = _rwkv_prep(cur_f, prev_f, next_f, mu_ref, kkw_ref, j, nb, spb)
    pb = pf if shared else _rwkv_prep(cur_b, prev_b, next_b, mu_ref, kkw_ref, nb - 1 - j, nb, spb)
    sl_f = _rwkv_dir_slabs(cur_f, *pf, ka_ref, rk_ref, w0_f, a0_f, bw_f, ba_f, ex_f, False)
    sl_b = _rwkv_dir_slabs(cur_b, *pb, ka_ref, rk_ref, w0_b, a0_b, bw_b, ba_b, ex_b, True)
    rt, kh, bh, kt, vv, gam = (jnp.concatenate([sl_f[n], sl_b[n]], axis=0)
                               for n in ("rt", "kh", "bh", "kt", "vv", "gam"))
    nch = cur_f.shape[0] // CHUNK
    gdir = 2 * nch
    n_slab = 2 * gdir

    mshape = (2, 1, CHUNK, 128)
    lag = ((lax.broadcasted_iota(jnp.int32, mshape, 2) - (lax.broadcasted_iota(jnp.int32, mshape, 3) & (CHUNK - 1)))
           * jnp.where(lax.broadcasted_iota(jnp.int32, mshape, 0) >= 1, -1, 1))
    strict, incl = lag > 0, lag >= 0

    def masked(mask, x):
        return jnp.where(mask, x.reshape(2, gdir, CHUNK, 128), 0.0).reshape(n_slab, CHUNK, 128)

    t_i = lax.broadcasted_iota(jnp.int32, (1, CHUNK, 128), 1)
    s_i = lax.broadcasted_iota(jnp.int32, (1, CHUNK, 128), 2) & (CHUNK - 1)
    blk = (s_i // SUB) == (t_i // SUB)
    eye = jnp.where(s_i == t_i, 1.0, 0.0)

    lhs = jnp.concatenate([kt, rt.astype(BF16)], axis=1)
    kb_bd = jnp.concatenate([_slab_bd(kh.astype(BF16)), _slab_bd(bh.astype(BF16))], axis=1)
    mkb = _bmm_nt(lhs, kb_bd)
    m_mat = masked(strict, mkb[:, :CHUNK, :128])
    p_mat = masked(incl, mkb[:, CHUNK:, :128])
    n_mat = masked(strict, mkb[:, :CHUNK, 128:])
    q_mat = masked(incl, mkb[:, CHUNK:, 128:])
    t_mat = _tri_inverse(n_mat, eye, blk)
    mpv = _mm(jnp.concatenate([m_mat, p_mat], axis=1), vv)
    mv, pv = mpv[:, :CHUNK], mpv[:, CHUNK:]
    tk, uv = _mm_pair(t_mat, kt, mv)
    qtk, quv = _mm_pair(q_mat, tk, uv)
    rc = rt - qtk
    oc = pv - quv
    bd_r = lax.broadcasted_iota(jnp.int32, (1, 128, 128), 1) // 64
    bd_c = lax.broadcasted_iota(jnp.int32, (1, 128, 128), 2) // 64
    bd_mask = bd_r == bd_c
    d_mat = jnp.where(bd_mask, _bmm_tn(tk, bh), 0.0)
    braw = jnp.where(bd_mask,
                     _bmm_tn(jnp.concatenate([vv, uv.astype(BF16)], axis=1),
                             jnp.concatenate([kh, -bh], axis=1)), 0.0)

    cps = nch // spb
    dirs = ((0, False, s0_f, o_f, sfin_f), (1, True, s0_b, o_b, sfin_b))
    s_cur = [s_scr[0], s_scr[1]] if spb == 1 else [None, None]
    for ci in range(nch):
        for d, rev, s0_ref, o_ref, sfin_ref in dirs:
            c = (nch - 1 - ci) if rev else ci
            q, ci_seq = divmod(ci, cps)
            q = (spb - 1 - q) if rev else q
            if spb > 1 and ci_seq == 0:
                s_cur[d] = _rwkv_state_to_bd(s0_ref[q])
            g = slice(d * gdir + 2 * c, d * gdir + 2 * c + 2)
            o = _bmm_nt(rc[g], s_cur[d]) + oc[g]
            o_ref[c * CHUNK:(c + 1) * CHUNK, 0:128] = o[0]
            o_ref[c * CHUNK:(c + 1) * CHUNK, 128:256] = o[1]
            s_cur[d] = (s_cur[d] - _bmm(s_cur[d], d_mat[g]) + braw[g]) * gam[g]
            if spb > 1 and ci_seq == cps - 1:
                _rwkv_state_store(sfin_ref, q, s_cur[d])
    if spb == 1:
        s_scr[0] = s_cur[0]
        s_scr[1] = s_cur[1]

        @pl.when(j == nb - 1)
        def _():
            _rwkv_state_store(sfin_f, 0, s_scr[0])
            _rwkv_state_store(sfin_b, 0, s_scr[1])


def _scan_blocking(nseq, seq_len, max_rows):
    spb = max(1, min(nseq, max_rows // seq_len))
    assert nseq % spb == 0
    tb = min(seq_len, max_rows) * spb
    return spb, tb, (seq_len * spb) // tb


def _rwkv_scan(u_r, prm, s0, nseq, seq_len):
    spb, tb, nb = _scan_blocking(nseq, seq_len, RWKV_TB)
    shared = nb == 1
    hb = tb // POOL_HALO
    nh = u_r.shape[0] // POOL_HALO
    t = u_r.shape[0]
    vec = _full((1, MIX_W))
    low = _full((128, MIX_W))
    st = pl.BlockSpec((spb, 4, 64, 64), lambda b, j: (b, 0, 0, 0))

    def block_specs(jmap):
        blk = lambda b, j: b * nb + jmap(j)
        return [pl.BlockSpec((tb, 1024), lambda b, j: (blk(b, j), 0)),
                pl.BlockSpec((POOL_HALO, 768), lambda b, j: (jnp.maximum(blk(b, j) * hb - 1, 0), 0)),
                pl.BlockSpec((POOL_HALO, 768), lambda b, j: (jnp.minimum((blk(b, j) + 1) * hb, nh - 1), 0))]

    fwd_map = lambda b, j: (b * nb + j, 0)
    bwd_map = lambda b, j: (b * nb + nb - 1 - j, 0)
    in_specs = block_specs(lambda j: j)
    operands = [u_r, u_r, u_r]
    if not shared:
        in_specs += block_specs(lambda j: nb - 1 - j)
        operands += [u_r, u_r, u_r]
    in_specs += [_full((1, 768)), vec, vec, vec] + [vec, vec, low, low] * 2 + [st, st]
    f, b = prm["rwkv0"], prm["rwkv1"]
    operands += [f["mu"], f["kk"], f["ka"], f["rk"], f["w0"], f["a0"], f["bw"], f["ba"],
                 b["w0"], b["a0"], b["bw"], b["ba"], s0[0], s0[1]]
    out_tok = jax.ShapeDtypeStruct((t, MIX_W), F32)
    out_st = jax.ShapeDtypeStruct((nseq, 4, 64, 64), F32)
    return pl.pallas_call(
        functools.partial(_rwkv_kernel, nb=nb, spb=spb, shared=shared),
        grid=(nseq // spb, nb),
        in_specs=in_specs,
        out_specs=[pl.BlockSpec((tb, MIX_W), fwd_map), pl.BlockSpec((tb, MIX_W), fwd_map), st,
                   pl.BlockSpec((tb, MIX_W), bwd_map), pl.BlockSpec((tb, MIX_W), bwd_map), st],
        out_shape=[out_tok, out_tok, out_st, out_tok, out_tok, out_st],
        scratch_shapes=[pltpu.VMEM((2, 2, 128, 128), F32)],
        compiler_params=_cparams(("parallel", "arbitrary")),
        name="rwkv_scan",
    )(*operands)


def _rows_bd4(y, width):
    lane = lax.broadcasted_iota(jnp.int32, y.shape, 2) // width
    z = jnp.zeros_like(y)
    return jnp.concatenate([jnp.where(lane == h, y, z) for h in range(4)], axis=1)


def _block_scans(x, levels):
    rows, width = x.shape
    row = lax.broadcasted_iota(jnp.int32, (rows, width), 0)
    pre, suf = [x], [x]
    for m in range(1, levels + 1):
        size, half = 1 << m, 1 << (m - 1)
        p, s = pre[-1], suf[-1]
        if half < 8:
            off = row & (size - 1)
            p3 = p.reshape(rows // 8, 8, width)
            s3 = s.reshape(rows // 8, 8, width)
            addp = jnp.zeros_like(x)
            adds = jnp.zeros_like(x)
            for o in range(half):
                addp = jnp.where(off == half + o, pltpu.roll(p3, o + 1, 1).reshape(rows, width), addp)
                adds = jnp.where(off == o, pltpu.roll(s3, 8 - (half - o), 1).reshape(rows, width), adds)
        else:
            p4 = p.reshape(rows // size, size, width)
            s4 = s.reshape(rows // size, size, width)
            second = (row & half) != 0
            addp = jnp.where(second, jnp.broadcast_to(p4[:, half - 1:half, :], p4.shape).reshape(rows, width), 0.0)
            adds = jnp.where(second, 0.0, jnp.broadcast_to(s4[:, half:half + 1, :], s4.shape).reshape(rows, width))
        pre.append(p + addp)
        suf.append(s + adds)
    return pre, suf


def _gla_state_in(s4):
    sp = s4.reshape(4 * GLA_DK, 64)
    st = jnp.concatenate([sp, sp], axis=1).T[0:64, :]
    lane_h = lax.broadcasted_iota(jnp.int32, (64, 128), 1) // GLA_DK
    return jnp.concatenate([jnp.where(lane_h == h, st, 0.0) for h in range(4)], axis=0)


def _gla_state_store(sfin_ref, q, s_t):
    m = s_t[0:64] + s_t[64:128] + s_t[128:192] + s_t[192:256]
    nat = jnp.concatenate([m, jnp.zeros_like(m)], axis=0).T
    sfin_ref[q] = nat[:, 0:64].reshape(4, GLA_DK, 64)


def _gla_dir_parts(u_ref, ab_ref, abias_ref, rev):
    la = _log_sigmoid(_dot(u_ref[:, 768:896], ab_ref[...]) + abias_ref[...]) * (1.0 / GLA_GATE_NORM)
    pre, suf = _block_scans(la, 6)
    near, far = (suf, pre) if rev else (pre, suf)
    return dict(la=la, near=near, far=far, b=near[6], q=u_ref[:, 0:128] * float(GLA_DK ** -0.5),
                kg=u_ref[:, 128:256], vg=u_ref[:, 256:512])


def _gla_kernel(*refs, nb, spb, shared):
    if shared:
        u_f, ab_f, abias_f, ab_b, abias_b, s0_f, s0_b, o_f, sfin_f, o_b, sfin_b, s_scr = refs
        u_b = u_f
    else:
        u_f, u_b, ab_f, abias_f, ab_b, abias_b, s0_f, s0_b, o_f, sfin_f, o_b, sfin_b, s_scr = refs
    j = pl.program_id(1)

    if spb == 1:
        @pl.when(j == 0)
        def _():
            s_scr[0] = _gla_state_in(s0_f[0])
            s_scr[1] = _gla_state_in(s0_b[0])

    parts = (_gla_dir_parts(u_f, ab_f, abias_f, False), _gla_dir_parts(u_b, ab_b, abias_b, True))
    nch = u_f.shape[0] // CHUNK
    n3 = 2 * nch

    def cat3(xs):
        return jnp.concatenate([x.reshape(nch, CHUNK, x.shape[1]) for x in xs], axis=0)

    mshape = (2, 1, CHUNK, MIX_W)
    t_i = lax.broadcasted_iota(jnp.int32, mshape, 2)
    s_i = lax.broadcasted_iota(jnp.int32, mshape, 3) & (CHUNK - 1)
    sign = jnp.where(lax.broadcasted_iota(jnp.int32, mshape, 0) >= 1, -1, 1)

    def select(mask, x, other):
        return jnp.where(mask, x.reshape(2, nch, CHUNK, MIX_W),
                         other.reshape(2, nch, CHUNK, MIX_W)).reshape(n3, CHUNK, MIX_W)

    q3 = cat3([p["q"] for p in parts])
    k3 = cat3([p["kg"] for p in parts])
    v3 = cat3([p["vg"] for p in parts])
    att = select(s_i == t_i, _bmm_nt(q3, _rows_bd4(k3.astype(BF16), GLA_DK)), jnp.zeros((n3, CHUNK, MIX_W), F32))
    for lev in range(1, 7):
        size, half = 1 << lev, 1 << (lev - 1)
        same = (t_i // size) == (s_i // size)
        t_late = jnp.where((t_i & (size - 1)) >= half, 1, 0)
        s_late = jnp.where((s_i & (size - 1)) >= half, 1, 0)
        mask = same & ((t_late - s_late) * sign == 1)
        qe = cat3([p["q"] * jnp.exp(p["near"][lev - 1]) for p in parts])
        ke = cat3([p["kg"] * jnp.exp(p["far"][lev - 1] - p["la"]) for p in parts])
        att = select(mask, _bmm_nt(qe, _rows_bd4(ke.astype(BF16), GLA_DK)), att)
    o_intra = _bmm(att, _rows_bd4(v3.astype(BF16), 64))

    b3 = cat3([p["b"] for p in parts])
    b_last = jnp.concatenate([b3[:nch, CHUNK - 1:CHUNK, :], b3[nch:, 0:1, :]], axis=0)
    qb = q3 * jnp.exp(b3)
    bd_r = lax.broadcasted_iota(jnp.int32, (1, MIX_W, 128), 1) // 64
    bd_c = lax.broadcasted_iota(jnp.int32, (1, MIX_W, 128), 2) // GLA_DK
    inc = jnp.where(bd_r == bd_c, _bmm_tn(v3, k3 * jnp.exp(b_last - b3)), 0.0)
    decay = jnp.exp(b_last)

    cps = nch // spb
    dirs = ((0, False, s0_f, o_f, sfin_f), (1, True, s0_b, o_b, sfin_b))
    s_cur = [s_scr[0], s_scr[1]] if spb == 1 else [None, None]
    for ci in range(nch):
        for d, rev, s0_ref, o_ref, sfin_ref in dirs:
            c = (nch - 1 - ci) if rev else ci
            qs, ci_seq = divmod(ci, cps)
            qs = (spb - 1 - qs) if rev else qs
            if spb > 1 and ci_seq == 0:
                s_cur[d] = _gla_state_in(s0_ref[qs])
            g = d * nch + c
            o_ref[c * CHUNK:(c + 1) * CHUNK, :] = o_intra[g] + _dot_nt(qb[g], s_cur[d])
            s_cur[d] = s_cur[d] * decay[g] + inc[g]
            if spb > 1 and ci_seq == cps - 1:
                _gla_state_store(sfin_ref, qs, s_cur[d])
    if spb == 1:
        s_scr[0] = s_cur[0]
        s_scr[1] = s_cur[1]

        @pl.when(j == nb - 1)
        def _():
            _gla_state_store(sfin_f, 0, s_scr[0])
            _gla_state_store(sfin_b, 0, s_scr[1])


def _gla_scan(u_g, prm, s0, nseq, seq_len):
    spb, tb, nb = _scan_blocking(nseq, seq_len, GLA_TB)
    shared = nb == 1
    t = u_g.shape[0]
    fwd_map = lambda b, j: (b * nb + j, 0)
    bwd_map = lambda b, j: (b * nb + nb - 1 - j, 0)
    st = pl.BlockSpec((spb, 4, GLA_DK, 64), lambda b, j: (b, 0, 0, 0))
    in_specs = [pl.BlockSpec((tb, 896), fwd_map)]
    operands = [u_g]
    if not shared:
        in_specs.append(pl.BlockSpec((tb, 896), bwd_map))
        operands.append(u_g)
    in_specs += [_full((128, 128)), _full((1, 128))] * 2 + [st, st]
    operands += [prm["gla_ab0"], prm["gla_abias0"], prm["gla_ab1"], prm["gla_abias1"], s0[0], s0[1]]
    out_tok = jax.ShapeDtypeStruct((t, MIX_W), F32)
    out_st = jax.ShapeDtypeStruct((nseq, 4, GLA_DK, 64), F32)
    return pl.pallas_call(
        functools.partial(_gla_kernel, nb=nb, spb=spb, shared=shared),
        grid=(nseq // spb, nb),
        in_specs=in_specs,
        out_specs=[pl.BlockSpec((tb, MIX_W), fwd_map), st, pl.BlockSpec((tb, MIX_W), bwd_map), st],
        out_shape=[out_tok, out_st, out_tok, out_st],
        scratch_shapes=[pltpu.VMEM((2, 256, 128), F32)],
        compiler_params=_cparams(("parallel", "arbitrary")),
        name="gla_scan",
    )(*operands)


def _merge_kernel(x_ref, mod_ref, g_ref, ya_ref, yb_ref, of_ref, ob_ref, exf_ref, exb_ref, cag_ref,
                  gf_ref, gb_ref, gout_ref, gn_ref, bg_ref, gnorm_ref, wg_ref, wb_ref, wo_ref, x1_ref):
    mod = mod_ref[0]
    x = x_ref[...]
    h = _rms_mod(x, g_ref[...], mod[:, D_MODEL:2 * D_MODEL], mod[:, 0:D_MODEL]).astype(BF16)
    mean64 = _seg_ones(MIX_W, 64, 1.0 / 64.0)
    o = of_ref[...] + ob_ref[...]
    mu = _dot_exact_rhs(o, mean64)
    oc = o - mu
    var = _dot_exact_rhs(oc * oc, mean64)
    gate_c = _dot(jax.nn.sigmoid(cag_ref[...]), bg_ref[...])
    y_c = (oc * lax.rsqrt(var + RWKV_GN_EPS) * gn_ref[...] + exf_ref[...] + exb_ref[...]) * gate_c
    og = gf_ref[...] + gb_ref[...]
    gout = gout_ref[...]
    y_d = (og * lax.rsqrt(_dot_exact_rhs(og * og, mean64) + EPS) * gnorm_ref[...]
           * (gout * jax.nn.sigmoid(gout)))
    ys = (ya_ref[...], yb_ref[...], y_c, y_d)
    merged = jnp.zeros((TM, D_MODEL), F32)
    for i in range(4):
        g0 = P_MIX + i * D_MODEL
        gate = jax.nn.sigmoid(_dot_nt(h, wg_ref[g0:g0 + D_MODEL, :]))
        merged = merged + gate * _dot(ys[i], wb_ref[i])
    out = _dot(merged, wo_ref[...])
    x1_ref[...] = x + mod[:, 2 * D_MODEL:3 * D_MODEL] * out


def _merge(x, mod3, g, y_a, y_b, o_f, o_b, ex_f, ex_b, u_r, g_f, g_b, u_g, prm, big, layer,
           cond_base, rows_per_cond):
    t = x.shape[0]
    row = lambda i: (i, 0)
    mix = pl.BlockSpec((TM, MIX_W), row)
    vec = _full((1, MIX_W))
    lay3 = lambda i: (layer, 0, 0)
    return pl.pallas_call(
        _merge_kernel,
        grid=(t // TM,),
        in_specs=[pl.BlockSpec((TM, D_MODEL), row),
                  pl.BlockSpec((1, 1, 6 * D_MODEL), _cond_map(cond_base, rows_per_cond)),
                  _full((1, D_MODEL)),
                  mix, mix, mix, mix, mix, mix,
                  pl.BlockSpec((TM, 128), lambda i: (i, 7)),
                  mix, mix,
                  pl.BlockSpec((TM, MIX_W), lambda i: (i, 2)),
                  vec, _full((128, MIX_W)), vec,
                  pl.BlockSpec((None, big["w_in_t"].shape[1], D_MODEL), lay3, pipeline_mode=pl.Buffered(1)),
                  pl.BlockSpec((None, 4, MIX_W, D_MODEL), lambda i: (layer, 0, 0, 0)),
                  pl.BlockSpec((None, D_MODEL, D_MODEL), lay3)],
        out_specs=pl.BlockSpec((TM, D_MODEL), row),
        out_shape=jax.ShapeDtypeStruct((t, D_MODEL), F32),
        compiler_params=_cparams(("parallel",)),
        name="merge_out",
    )(x, mod3, g, y_a, y_b, o_f, o_b, ex_f, ex_b, u_r, g_f, g_b, u_g,
      prm["gn"], prm["bg"], prm["gla_norm"], big["w_in_t"], big["w_branch"], big["w_out"])


def _mlp_kernel(x_ref, mod_ref, g_ref, w1_ref, w2_ref, fg_ref, out_ref, *, final):
    mod = mod_ref[0]
    x = x_ref[...]
    h = _rms_mod(x, g_ref[...], mod[:, 4 * D_MODEL:5 * D_MODEL], mod[:, 3 * D_MODEL:4 * D_MODEL]).astype(BF16)
    ff = jnp.zeros(x.shape, F32)
    for c in range(D_FF // D_MODEL):
        cols = slice(c * D_MODEL, (c + 1) * D_MODEL)
        a = jnp.maximum(jnp.dot(h, w1_ref[:, cols], preferred_element_type=F32), 0.0)
        ff = ff + _dot(a * a, w2_ref[cols, :])
    x2 = x + mod[:, 5 * D_MODEL:6 * D_MODEL] * ff
    if final:
        x2 = x2 * lax.rsqrt(jnp.mean(x2 * x2, axis=-1, keepdims=True) + EPS) * fg_ref[...]
    out_ref[...] = x2


def _mlp(x1, mod3, g, w1, w2, layer, final_g, cond_base, rows_per_cond):
    t = x1.shape[0]
    row = lambda i: (i, 0)
    lay3 = lambda i: (layer, 0, 0)
    return pl.pallas_call(
        functools.partial(_mlp_kernel, final=layer == DEPTH - 1),
        grid=(t // TM_WIDE,),
        in_specs=[pl.BlockSpec((TM_WIDE, D_MODEL), row),
                  pl.BlockSpec((1, 1, 6 * D_MODEL), _cond_map(cond_base, rows_per_cond, TM_WIDE)),
                  _full((1, D_MODEL)),
                  pl.BlockSpec((None, D_MODEL, D_FF), lay3, pipeline_mode=pl.Buffered(1)),
                  pl.BlockSpec((None, D_FF, D_MODEL), lay3, pipeline_mode=pl.Buffered(1)),
                  _full((1, D_MODEL))],
        out_specs=pl.BlockSpec((TM_WIDE, D_MODEL), row),
        out_shape=jax.ShapeDtypeStruct((t, D_MODEL), F32),
        compiler_params=_cparams(("parallel",)),
        name="mlp",
    )(x1, mod3, g, w1, w2, final_g)


def _pos_tables(n_tok, d):
    rows = n_tok // GRID_W
    assert rows == GRID_W
    quarter = d // 4
    omega = 1.0 / (POS_BASE ** (jnp.arange(quarter, dtype=F32) / quarter))
    ar = jnp.arange(rows, dtype=F32)[:, None] * omega
    ac = jnp.arange(GRID_W, dtype=F32)[:, None] * omega
    return (jnp.concatenate([jnp.sin(ar), jnp.cos(ar)], axis=-1),
            jnp.concatenate([jnp.sin(ac), jnp.cos(ac)], axis=-1))


def _pad_rows(m, start, total):
    return jnp.zeros((total, m.shape[1]), m.dtype).at[start:start + m.shape[0]].set(m)


def _layer_params(l, p):
    out = {
        "n1": p["norm1_g"][l].reshape(1, D_MODEL),
        "n2": p["norm2_g"][l].reshape(1, D_MODEL),
        "pool_w": jnp.einsum("gcd,gh->gchd", p["pool_w"][l], jnp.eye(4, dtype=F32)).reshape(MIX_W, MIX_W).astype(BF16),
        "pool_scale": p["pool_scale"][l].reshape(1, MIX_W),
        "gn": p["rwkv_gn"][l].reshape(1, MIX_W),
        "bg": _pad_rows(p["rwkv_bg"][l], 64, 128).astype(BF16),
        "gla_norm": p["gla_norm"][l].reshape(1, MIX_W),
    }
    for d in range(2):
        out["rwkv%d" % d] = {
            "mu": p["rwkv_mu"][l].reshape(1, 768),
            "kk": p["rwkv_kk"][l].reshape(1, MIX_W),
            "ka": p["rwkv_ka"][l].reshape(1, MIX_W),
            "rk": p["rwkv_rk"][l].reshape(1, MIX_W),
            "w0": p["rwkv_w0"][l, d].reshape(1, MIX_W),
            "a0": p["rwkv_a0"][l, d].reshape(1, MIX_W),
            "bw": _pad_rows(p["rwkv_bw"][l, d], 64 * d, 128).astype(BF16),
            "ba": _pad_rows(p["rwkv_ba"][l, d], 32 * d, 128).astype(BF16),
        }
        out["gla_ab%d" % d] = _pad_rows(p["gla_ab"][l, d], 16 * d, 128).astype(BF16)
        out["gla_abias%d" % d] = p["gla_abias"][l, d].reshape(1, 128)
    return out


def _run_layer(x, mod3, prm, big, layer, final_g, s_rwkv0, s_gla0, nseq, seq_len, cond_base, rows_per_cond,
               pos_tables=None):
    if pos_tables is None:
        u_p, u_f, u_r, u_g = _inproj(x, mod3, prm["n1"], big["w_in_t"], layer, cond_base, rows_per_cond)
    else:
        x, u_p, u_f, u_r, u_g = _inproj(x, mod3, prm["n1"], big["w_in_t"], layer, cond_base, rows_per_cond,
                                        pos_tables, seq_len)
    y_a = _pool(u_p, prm["pool_w"], prm["pool_scale"], nseq, seq_len)
    if seq_len == FFT_N1 * FFT_N1:
        y_b = _four_large(u_f, nseq, seq_len)
    else:
        y_b = _four_small(u_f, nseq, seq_len)
    o_f, ex_f, sr_f, o_b, ex_b, sr_b = _rwkv_scan(u_r, prm, s_rwkv0, nseq, seq_len)
    g_f, sg_f, g_b, sg_b = _gla_scan(u_g, prm, s_gla0, nseq, seq_len)
    x1 = _merge(x, mod3, prm["n1"], y_a, y_b, o_f, o_b, ex_f, ex_b, u_r, g_f, g_b, u_g, prm, big, layer,
                cond_base, rows_per_cond)
    x2 = _mlp(x1, mod3, prm["n2"], big["w1"], big["w2"], layer, final_g, cond_base, rows_per_cond)
    return x2, jnp.stack([sr_f, sr_b], axis=1), jnp.stack([sg_f, sg_b], axis=1)


def kernel(x_prompt, x_sample, state_rwkv, state_gla, c, c_ctx, ada_w, ada_b, norm1_g, norm2_g, w_in, pool_w, pool_scale, rwkv_mu, rwkv_w0, rwkv_bw, rwkv_a0, rwkv_ba, rwkv_kk, rwkv_ka, rwkv_bg, rwkv_rk, rwkv_gn, gla_ab, gla_abias, gla_norm, w_branch, w_out, mlp_w1, mlp_w2, final_g):
    p = dict(pool_w=pool_w, pool_scale=pool_scale, rwkv_mu=rwkv_mu, rwkv_w0=rwkv_w0,
             rwkv_bw=rwkv_bw, rwkv_a0=rwkv_a0, rwkv_ba=rwkv_ba, rwkv_kk=rwkv_kk, rwkv_ka=rwkv_ka,
             rwkv_bg=rwkv_bg, rwkv_rk=rwkv_rk, rwkv_gn=rwkv_gn, gla_ab=gla_ab, gla_abias=gla_abias,
             gla_norm=gla_norm, norm1_g=norm1_g, norm2_g=norm2_g)
    big = dict(w_in_t=jnp.swapaxes(w_in, 1, 2).astype(BF16),
               w_branch=w_branch.astype(BF16), w_out=w_out.astype(BF16),
               w1=mlp_w1.astype(BF16), w2=mlp_w2.astype(BF16))
    bp, lp, _ = x_prompt.shape
    bs, ls, _ = x_sample.shape
    cond8 = jnp.zeros((8, D_MODEL), F32).at[0].set(c_ctx).at[1:1 + bs].set(c)
    mods = _ada(cond8, ada_w, ada_b)
    fg = final_g.reshape(1, D_MODEL)

    xp = x_prompt.reshape(bp * lp, D_MODEL)
    xs = x_sample.reshape(bs * ls, D_MODEL)
    pos_tables = _pos_tables(ls, D_MODEL)
    zr = jnp.zeros((2, bp) + state_rwkv.shape[3:], F32)
    zg = jnp.zeros((2, bp) + state_gla.shape[3:], F32)
    new_r, new_g = [], []
    for l in range(DEPTH):
        prm = _layer_params(l, p)
        mod3 = mods[l].reshape(8, 1, 6 * D_MODEL)
        xp, s_r, s_g = _run_layer(xp, mod3, prm, big, l, fg, zr, zg, bp, lp, 0, bp * lp)
        new_r.append(s_r)
        new_g.append(s_g)
        xs, _, _ = _run_layer(xs, mod3, prm, big, l, fg, jnp.swapaxes(state_rwkv[:, l], 0, 1),
                              jnp.swapaxes(state_gla[:, l], 0, 1), bs, ls, 1, ls,
                              pos_tables if l == 0 else None)
    y_prompt = xp.reshape(bp, lp, D_MODEL)
    y_sample = xs.reshape(bs, ls, D_MODEL)
    return (y_prompt, y_sample, jnp.stack(new_r, axis=1), jnp.stack(new_g, axis=1))
```

```python
import functools

import numpy as np
import jax
import jax.numpy as jnp
from jax import lax
from jax.experimental import pallas as pl
from jax.experimental.pallas import tpu as pltpu

F32 = jnp.float32
BF16 = jnp.bfloat16

D_MODEL = 1024
DEPTH = 2
GRID_W = 64
POS_BASE = 10000.0
MIX_W = 256
POOL_WINDOWS = (2, 4, 8, 16)
POOL_HALO = 8
RWKV_DECAY_SCALE = 0.606531
RWKV_GN_EPS = 64e-5
GLA_DK = 32
GLA_GATE_NORM = 16.0
D_FF = 4 * D_MODEL
EPS = 1e-6
P_MIX = 2336
P_MIX_PAD = 2432
CHUNK = 64
SUB = 16

TM = 512
TM_WIDE = 1024
POOL_TB = 1024
RWKV_TB = 512
GLA_TB = 1024
FFT_N1 = 64
VMEM_LIMIT = 56 * 1024 * 1024


def _cparams(sem):
    return pltpu.CompilerParams(dimension_semantics=sem, vmem_limit_bytes=VMEM_LIMIT)


def _full(shape):
    n = len(shape)
    return pl.BlockSpec(shape, lambda *_: (0,) * n)


def _dot(a, b):
    return jnp.dot(a.astype(BF16), b.astype(BF16), preferred_element_type=F32)


def _dot_nt(a, b):
    return lax.dot_general(a.astype(BF16), b.astype(BF16), (((1,), (1,)), ((), ())),
                           preferred_element_type=F32)


def _split2(a):
    hi = a.astype(BF16)
    lo = (a - hi.astype(F32)).astype(BF16)
    return hi, lo


def _dot_exact_rhs(a, w):
    hi, lo = _split2(a)
    return (jnp.dot(hi, w, preferred_element_type=F32) + jnp.dot(lo, w, preferred_element_type=F32))


def _dot_exact_lhs(w, a):
    hi, lo = _split2(a)
    return jnp.dot(w, hi, preferred_element_type=F32) + jnp.dot(w, lo, preferred_element_type=F32)


def _seg_ones(n, seg, scale=1.0):
    r = lax.broadcasted_iota(jnp.int32, (n, n), 0)
    c = lax.broadcasted_iota(jnp.int32, (n, n), 1)
    return jnp.where((r // seg) == (c // seg), scale, 0.0).astype(BF16)


def _log_sigmoid(x):
    return jnp.minimum(x, 0.0) - jnp.log(1.0 + jnp.exp(-jnp.abs(x)))


def _rms_mod(x, g, sc, sh):
    ms = jnp.mean(x * x, axis=-1, keepdims=True)
    return x * lax.rsqrt(ms + EPS) * g * (1.0 + sc) + sh


def _ada_kernel(c_ref, w_ref, b_ref, o_ref):
    c = c_ref[...]
    s = c * jax.nn.sigmoid(c)
    hi, lo = _split2(s)
    w = w_ref[0].astype(BF16)
    o_ref[0] = (jnp.dot(hi, w, preferred_element_type=F32) + jnp.dot(lo, w, preferred_element_type=F32)
                + b_ref[0])


def _ada(cond8, ada_w, ada_b):
    tn = 3072
    n = 6 * D_MODEL
    return pl.pallas_call(
        _ada_kernel,
        grid=(DEPTH, n // tn),
        in_specs=[_full((8, D_MODEL)),
                  pl.BlockSpec((1, D_MODEL, tn), lambda l, j: (l, 0, j)),
                  pl.BlockSpec((1, 1, tn), lambda l, j: (l, 0, j))],
        out_specs=pl.BlockSpec((1, 8, tn), lambda l, j: (l, 0, j)),
        out_shape=jax.ShapeDtypeStruct((DEPTH, 8, n), F32),
        compiler_params=_cparams(("parallel", "parallel")),
        name="ada_mod",
    )(cond8, ada_w, ada_b.reshape(DEPTH, 1, n))


def _inproj_kernel(x_ref, mod_ref, g_ref, w_ref, up_ref, uf_ref, ur_ref, ug_ref):
    _inproj_body(x_ref[...], mod_ref, g_ref, w_ref, up_ref, uf_ref, ur_ref, ug_ref)


def _inproj_pos_kernel(x_ref, rr_ref, cc_ref, mod_ref, g_ref, w_ref, xo_ref, up_ref, uf_ref, ur_ref, ug_ref,
                       *, seq_len):
    tile = x_ref.shape[0]
    nrow = tile // GRID_W
    g0 = pl.multiple_of(((pl.program_id(0) * tile) % seq_len) // GRID_W, nrow)
    half = D_MODEL // 2
    rpart = jnp.broadcast_to(rr_ref[pl.ds(g0, nrow), :][:, None, :], (nrow, GRID_W, half)).reshape(tile, half)
    cpart = jnp.concatenate([cc_ref[...]] * nrow, axis=0)
    x = x_ref[...] + jnp.concatenate([rpart, cpart], axis=1)
    xo_ref[...] = x
    _inproj_body(x, mod_ref, g_ref, w_ref, up_ref, uf_ref, ur_ref, ug_ref)


def _inproj_body(x, mod_ref, g_ref, w_ref, up_ref, uf_ref, ur_ref, ug_ref):
    mod = mod_ref[0]
    h = _rms_mod(x, g_ref[...], mod[:, D_MODEL:2 * D_MODEL], mod[:, 0:D_MODEL]).astype(BF16)
    up_ref[...] = _dot_nt(h, w_ref[0:256, :])
    uf_ref[...] = _dot_nt(h, w_ref[256:512, :])
    ur_ref[...] = _dot_nt(h, w_ref[512:1536, :])
    ug_ref[...] = _dot_nt(h, w_ref[1536:P_MIX_PAD, :])


def _cond_map(cond_base, rows_per_cond, tile=TM):
    return lambda i: (cond_base + (i * tile) // rows_per_cond, 0, 0)


def _inproj(x, mod3, g, w_t, layer, cond_base, rows_per_cond, pos_tables=None, seq_len=None):
    t = x.shape[0]
    row = lambda i: (i, 0)
    tm = TM_WIDE
    in_specs = [pl.BlockSpec((1, 1, 6 * D_MODEL), _cond_map(cond_base, rows_per_cond, tm)),
                _full((1, D_MODEL)),
                pl.BlockSpec((None, P_MIX_PAD, D_MODEL), lambda i: (layer, 0, 0), pipeline_mode=pl.Buffered(1))]
    out_specs = [pl.BlockSpec((tm, 256), row), pl.BlockSpec((tm, 256), row),
                 pl.BlockSpec((tm, 1024), row), pl.BlockSpec((tm, 896), row)]
    out_shape = [jax.ShapeDtypeStruct((t, 256), F32), jax.ShapeDtypeStruct((t, 256), F32),
                 jax.ShapeDtypeStruct((t, 1024), F32), jax.ShapeDtypeStruct((t, 896), F32)]
    x_spec = pl.BlockSpec((tm, D_MODEL), row)
    if pos_tables is None:
        return pl.pallas_call(
            _inproj_kernel, grid=(t // tm,), in_specs=[x_spec] + in_specs, out_specs=out_specs,
            out_shape=out_shape, compiler_params=_cparams(("parallel",)), name="in_proj",
        )(x, mod3, g, w_t)
    tab = _full((GRID_W, D_MODEL // 2))
    return pl.pallas_call(
        functools.partial(_inproj_pos_kernel, seq_len=seq_len), grid=(t // tm,),
        in_specs=[x_spec, tab, tab] + in_specs, out_specs=[x_spec] + out_specs,
        out_shape=[jax.ShapeDtypeStruct((t, D_MODEL), F32)] + out_shape,
        compiler_params=_cparams(("parallel",)), name="in_proj_pos",
    )(x, pos_tables[0], pos_tables[1], mod3, g, w_t)


def _pool_kernel(prev_ref, cur_ref, next_ref, wbd_ref, scale_ref, y_ref, *, seq_len, spb):
    j = pl.program_id(1)
    tb = cur_ref.shape[0]
    rows = tb // spb
    nb = seq_len // rows
    cur = cur_ref[...]
    if spb == 1:
        prev = jnp.where(j > 0, prev_ref[...], 0.0)[None]
        nxt = jnp.where(j < nb - 1, next_ref[...], 0.0)[None]
    else:
        prev = nxt = jnp.zeros((spb, POOL_HALO, MIX_W), F32)
    ext = jnp.concatenate([prev, cur.reshape(spb, rows, MIX_W), nxt], axis=1)
    n_ext = rows + 2 * POOL_HALO

    def shift(x, k):
        return pltpu.roll(x, k % n_ext, 1)

    sums = [ext + shift(ext, 1)]
    for k in (1, 2, 4):
        sums.append(shift(sums[-1], -k) + shift(sums[-1], k))
    assert POOL_WINDOWS == (2, 4, 8, 16)
    pos = (j * tb + lax.broadcasted_iota(jnp.int32, (tb, MIX_W), 0)) % seq_len
    group = lax.broadcasted_iota(jnp.int32, (tb, MIX_W), 1) // (MIX_W // len(POOL_WINDOWS))
    pooled = jnp.zeros((tb, MIX_W), F32)
    for gi, win in enumerate(POOL_WINDOWS):
        before, after = win // 2, win - win // 2 - 1
        s = sums[gi][:, POOL_HALO:POOL_HALO + rows].reshape(tb, MIX_W)
        cnt = (jnp.minimum(pos + after, seq_len - 1) - jnp.maximum(pos - before, 0) + 1).astype(F32)
        pooled = jnp.where(group == gi, s / cnt - cur, pooled)
    y_ref[...] = _dot(pooled, wbd_ref[...]) * scale_ref[...]


def _pool(u_p, wbd, scale, nseq, seq_len):
    spb, tb, nb = _scan_blocking(nseq, seq_len, POOL_TB)
    hb = tb // POOL_HALO
    nh = u_p.shape[0] // POOL_HALO
    return pl.pallas_call(
        functools.partial(_pool_kernel, seq_len=seq_len, spb=spb),
        grid=(nseq // spb, nb),
        in_specs=[pl.BlockSpec((POOL_HALO, MIX_W), lambda b, j: (jnp.maximum((b * nb + j) * hb - 1, 0), 0)),
                  pl.BlockSpec((tb, MIX_W), lambda b, j: (b * nb + j, 0)),
                  pl.BlockSpec((POOL_HALO, MIX_W), lambda b, j: (jnp.minimum((b * nb + j + 1) * hb, nh - 1), 0)),
                  _full((MIX_W, MIX_W)), _full((1, MIX_W))],
        out_specs=pl.BlockSpec((tb, MIX_W), lambda b, j: (b * nb + j, 0)),
        out_shape=jax.ShapeDtypeStruct(u_p.shape, F32),
        compiler_params=_cparams(("parallel", "parallel")),
        name="pool_mixer",
    )(u_p, u_p, u_p, wbd, scale)


def _dft_tables(n):
    t = np.arange(n)
    ang = 2.0 * np.pi * ((np.outer(t, t)) % n) / n
    return np.cos(ang), np.sin(ang)


def _bd_np(m, copies):
    k = m.shape[0]
    out = np.zeros((k * copies, k * copies), m.dtype)
    for i in range(copies):
        out[i * k:(i + 1) * k, i * k:(i + 1) * k] = m
    return out


def _four_small_kernel(z_ref, cs_ref, cbd_ref, sbd_ref, y_ref, *, seq_len, norm):
    cs = cs_ref[...].astype(BF16)
    zr, zi = [], []
    for g in range(z_ref.shape[0] // seq_len):
        zf = _dot(cs, z_ref[g * seq_len:(g + 1) * seq_len, :])
        zr.append(zf[:seq_len])
        zi.append(zf[seq_len:])
    y = _dot(jnp.concatenate(zr, axis=0), cbd_ref[...]) + _dot(jnp.concatenate(zi, axis=0), sbd_ref[...])
    y_ref[...] = y * norm


def _four_small(u_f, nseq, seq_len):
    c, s = _dft_tables(seq_len)
    cs = jnp.asarray(np.concatenate([c, -s], axis=0), F32)
    c64, s64 = _dft_tables(MIX_W // 4)
    cbd = jnp.asarray(_bd_np(c64, 4), F32)
    sbd = jnp.asarray(_bd_np(s64, 4), F32)
    per_step = 4 if nseq % 4 == 0 else 1
    rows = per_step * seq_len
    return pl.pallas_call(
        functools.partial(_four_small_kernel, seq_len=seq_len, norm=float((seq_len * 64) ** -0.5)),
        grid=(nseq // per_step,),
        in_specs=[pl.BlockSpec((rows, MIX_W), lambda b: (b, 0)),
                  _full((2 * seq_len, seq_len)), _full((MIX_W, MIX_W)), _full((MIX_W, MIX_W))],
        out_specs=pl.BlockSpec((rows, MIX_W), lambda b: (b, 0)),
        out_shape=jax.ShapeDtypeStruct(u_f.shape, F32),
        compiler_params=_cparams(("parallel",)),
        name="fourier_small",
    )(u_f, cs, cbd, sbd)


def _four_large_kernel(z0_ref, z1_ref, fs_ref, g_ref, cbd_ref, sbd_ref, y_ref, a0_scr, a1_scr, xr_scr, xi_scr,
                       y0_scr, y1_scr, *, norm):
    n1 = FFT_N1
    fs = fs_ref[...].astype(BF16)

    def rows(ref0, ref1, start, stride):
        sl = pl.ds(start, n1, stride=stride)
        return jnp.concatenate([ref0[sl, :], ref1[sl, :]], axis=1)

    for t2 in range(n1):
        a = _dot(fs, rows(z0_ref, z1_ref, t2, n1))
        a0_scr[t2 * 2 * n1:(t2 + 1) * 2 * n1, :] = a[:, :128]
        a1_scr[t2 * 2 * n1:(t2 + 1) * 2 * n1, :] = a[:, 128:]
    for f1 in range(n1):
        a = jnp.concatenate([rows(a0_scr, a1_scr, f1, 2 * n1), rows(a0_scr, a1_scr, n1 + f1, 2 * n1)],
                            axis=0)
        x = _dot(g_ref[f1], a)
        xr_scr[f1 * n1:(f1 + 1) * n1, :] = x[:n1]
        xi_scr[f1 * n1:(f1 + 1) * n1, :] = x[n1:]
    y = (_dot(xr_scr[...], cbd_ref[...]) + _dot(xi_scr[...], sbd_ref[...])) * norm
    y0_scr[...] = y[:, :128]
    y1_scr[...] = y[:, 128:]
    for f2 in range(n1):
        y_ref[f2 * n1:(f2 + 1) * n1, :] = rows(y0_scr, y1_scr, f2, n1)


def _four_large(u_f, nseq, seq_len):
    n1 = FFT_N1
    assert seq_len == n1 * n1
    c64, s64 = _dft_tables(n1)
    fs = jnp.asarray(np.concatenate([c64, -s64], axis=0), F32)
    f1 = np.arange(n1)[:, None, None]
    f2 = np.arange(n1)[None, :, None]
    t2 = np.arange(n1)[None, None, :]
    ang = 2.0 * np.pi * ((t2 * (f1 + n1 * f2)) % seq_len) / seq_len
    gr, gi = np.cos(ang), -np.sin(ang)
    g = jnp.asarray(np.concatenate([np.concatenate([gr, -gi], axis=2),
                                    np.concatenate([gi, gr], axis=2)], axis=1), F32)
    cbd = jnp.asarray(_bd_np(c64, 4), F32)
    sbd = jnp.asarray(_bd_np(s64, 4), F32)
    half = MIX_W // 2
    return pl.pallas_call(
        functools.partial(_four_large_kernel, norm=float((seq_len * 64) ** -0.5)),
        grid=(nseq,),
        in_specs=[pl.BlockSpec((seq_len, half), lambda b: (b, 0)), pl.BlockSpec((seq_len, half), lambda b: (b, 1)),
                  _full((2 * n1, n1)), _full((n1, 2 * n1, 2 * n1)), _full((MIX_W, MIX_W)), _full((MIX_W, MIX_W))],
        out_specs=pl.BlockSpec((seq_len, MIX_W), lambda b: (b, 0)),
        out_shape=jax.ShapeDtypeStruct(u_f.shape, F32),
        scratch_shapes=[pltpu.VMEM((2 * seq_len, half), F32), pltpu.VMEM((2 * seq_len, half), F32),
                        pltpu.VMEM((seq_len, MIX_W), F32), pltpu.VMEM((seq_len, MIX_W), F32),
                        pltpu.VMEM((seq_len, half), F32), pltpu.VMEM((seq_len, half), F32)],
        compiler_params=_cparams(("parallel",)),
        name="fourier_large",
    )(u_f, u_f, fs, g, cbd, sbd)


def _bmm(x, y):
    return lax.dot_general(x.astype(BF16), y.astype(BF16), (((2,), (1,)), ((0,), (0,))),
                           preferred_element_type=F32)


def _bmm_nt(x, y):
    return lax.dot_general(x.astype(BF16), y.astype(BF16), (((2,), (2,)), ((0,), (0,))),
                           preferred_element_type=F32)


def _bmm_tn(x, y):
    return lax.dot_general(x.astype(BF16), y.astype(BF16), (((1,), (1,)), ((0,), (0,))),
                           preferred_element_type=F32)


def _slab_bd(y):
    lane = lax.broadcasted_iota(jnp.int32, y.shape, 2)
    z = jnp.zeros_like(y)
    return jnp.concatenate([jnp.where(lane < 64, y, z), jnp.where(lane >= 64, y, z)], axis=1)


def _mm(x, y):
    return _bmm(x, _slab_bd(y.astype(BF16)))


def _mm_pair(x, y1, y2):
    w = jnp.concatenate([_slab_bd(y1.astype(BF16)), _slab_bd(y2.astype(BF16))], axis=2)
    out = _bmm(x, w)
    return out[:, :, :128], out[:, :, 128:]


def _mm3(xs, y):
    yh, yl = _split2(y)
    bh, bl = _slab_bd(yh), _slab_bd(yl)
    parts = [_split2(x) for x in xs]
    xh = jnp.concatenate([p[0] for p in parts], axis=1)
    xl = jnp.concatenate([p[1] for p in parts], axis=1)
    hx = _bmm(xh, jnp.concatenate([bh, bl], axis=2))
    out = hx[:, :, :128] + (hx[:, :, 128:] + _bmm(xl, bh))
    return [out[:, i * CHUNK:(i + 1) * CHUNK] for i in range(len(xs))]


def _to_slabs(x):
    nch = x.shape[0] // CHUNK
    x4 = x.reshape(nch, CHUNK, MIX_W)
    return jnp.stack([x4[:, :, :128], x4[:, :, 128:]], axis=1).reshape(2 * nch, CHUNK, 128)


def _tri_inverse(n_mat, eye, blk):
    nd = jnp.where(blk, n_mat, 0.0)
    no = n_mat - nd
    n2, = _mm3([nd], nd)
    n4, y = _mm3([n2, nd], n2)
    x = n2 - nd - y
    n8, y = _mm3([n4, x], n4)
    x = x + n4 + y
    y, = _mm3([x], n8)
    x = x + n8 + y
    td = eye + x
    e = _mm(td, no)
    e2 = _mm(e, e)
    e3 = _mm(e, e2)
    return td + _mm(e2 - e - e3, td)


def _rwkv_state_to_bd(s4):
    bd_r = lax.broadcasted_iota(jnp.int32, (128, 128), 0) // 64
    bd_c = lax.broadcasted_iota(jnp.int32, (128, 128), 1) // 64
    pairs = []
    for hp in range(2):
        sp = s4[2 * hp:2 * hp + 2].reshape(128, 64)
        pairs.append(jnp.where(bd_r == bd_c, jnp.concatenate([sp, sp], axis=1), 0.0))
    return jnp.stack(pairs, axis=0)


def _rwkv_state_store(sfin_ref, q, s_bd):
    for hp in range(2):
        sfin_ref[q, 2 * hp] = s_bd[hp, 0:64, 0:64]
        sfin_ref[q, 2 * hp + 1] = s_bd[hp, 64:128, 64:128]


def _rwkv_prep(cur_ref, prev_ref, next_ref, mu_ref, kkw_ref, jj, nb, spb):
    tb = cur_ref.shape[0]
    seq_rows = tb // spb
    z = cur_ref[:, 0:768]
    rowid = lax.broadcasted_iota(jnp.int32, (tb, 768), 0)
    prow = jnp.where(rowid == 0, jnp.where(jj > 0, prev_ref[POOL_HALO - 1:POOL_HALO, :], 0.0), 0.0)
    nrow = jnp.where(rowid == tb - 1, jnp.where(jj < nb - 1, next_ref[0:1, :], 0.0), 0.0)
    in_seq = rowid & (seq_rows - 1)
    up = jnp.where(in_seq == 0, prow, pltpu.roll(z, 1, 0))
    dn = jnp.where(in_seq == seq_rows - 1, nrow, pltpu.roll(z, tb - 1, 0))
    zm = z + mu_ref[...] * (0.5 * (up + dn) - z)
    r = zm[:, 0:256]
    k = zm[:, 256:512]
    v = zm[:, 512:768]
    kk = k * kkw_ref[...]
    kk = kk * lax.rsqrt(_dot(kk * kk, _seg_ones(MIX_W, 64)) + EPS)
    return r, k, v, kk


def _rwkv_dir_slabs(cur_ref, r, k, v, kk, ka_ref, rk_ref, w0_ref, a0_ref, bw_ref, ba_ref, ex_ref, rev):
    tb = cur_ref.shape[0]
    w_log = w0_ref[...] + _dot(jnp.tanh(cur_ref[:, 768:896]), bw_ref[...])
    lw = -RWKV_DECAY_SCALE * jax.nn.sigmoid(w_log)
    a = jax.nn.sigmoid(a0_ref[...] + _dot(cur_ref[:, 896:1024], ba_ref[...]))
    kd = k * (1.0 + (a - 1.0) * ka_ref[...])
    ex_ref[...] = _dot(r * kd * rk_ref[...], _seg_ones(MIX_W, 64)) * v
    tr = lax.broadcasted_iota(jnp.int32, (tb, tb), 0)
    tc = lax.broadcasted_iota(jnp.int32, (tb, tb), 1)
    same_chunk = (tr // CHUNK) == (tc // CHUNK)
    tri = jnp.where(same_chunk & ((tc >= tr) if rev else (tc <= tr)), 1.0, 0.0).astype(BF16)
    cum = _dot_exact_lhs(tri, lw)
    e_in = jnp.exp(cum)
    e_out = jnp.exp(-cum)
    last = 0 if rev else CHUNK - 1
    return dict(rt=_to_slabs(r * e_in), kh=_to_slabs((kd * e_out).astype(BF16)),
                bh=_to_slabs((kk * a * e_out).astype(BF16)),
                kt=_to_slabs((kk * jnp.exp(cum - lw)).astype(BF16)), vv=_to_slabs(v.astype(BF16)),
                gam=_to_slabs(e_in)[:, last:last + 1, :])


def _rwkv_kernel(*refs, nb, spb, shared):
    if shared:
        (cur_f, prev_f, next_f, mu_ref, kkw_ref, ka_ref, rk_ref, w0_f, a0_f, bw_f, ba_f, w0_b, a0_b, bw_b, ba_b,
         s0_f, s0_b, o_f, ex_f, sfin_f, o_b, ex_b, sfin_b, s_scr) = refs
        cur_b = cur_f
    else:
        (cur_f, prev_f, next_f, cur_b, prev_b, next_b, mu_ref, kkw_ref, ka_ref, rk_ref, w0_f, a0_f, bw_f, ba_f,
         w0_b, a0_b, bw_b, ba_b, s0_f, s0_b, o_f, ex_f, sfin_f, o_b, ex_b, sfin_b, s_scr) = refs
    j = pl.program_id(1)

    if spb == 1:
        @pl.when(j == 0)
        def _():
            s_scr[0] = _rwkv_state_to_bd(s0_f[0])
            s_scr[1] = _rwkv_state_to_bd(s0_b[0])

    pf = _rwkv_prep(cur_f, prev_f, next_f, mu_ref, kkw_ref, j, nb, spb)
    pb = pf if shared else _rwkv_prep(cur_b, prev_b, next_b, mu_ref, kkw_ref, nb - 1 - j, nb, spb)
    sl_f = _rwkv_dir_slabs(cur_f, *pf, ka_ref, rk_ref, w0_f, a0_f, bw_f, ba_f, ex_f, False)
    sl_b = _rwkv_dir_slabs(cur_b, *pb, ka_ref, rk_ref, w0_b, a0_b, bw_b, ba_b, ex_b, True)
    rt, kh, bh, kt, vv, gam = (jnp.concatenate([sl_f[n], sl_b[n]], axis=0)
                               for n in ("rt", "kh", "bh", "kt", "vv", "gam"))
    nch = cur_f.shape[0] // CHUNK
    gdir = 2 * nch
    n_slab = 2 * gdir

    mshape = (2, 1, CHUNK, 128)
    lag = ((lax.broadcasted_iota(jnp.int32, mshape, 2) - (lax.broadcasted_iota(jnp.int32, mshape, 3) & (CHUNK - 1)))
           * jnp.where(lax.broadcasted_iota(jnp.int32, mshape, 0) >= 1, -1, 1))
    strict, incl = lag > 0, lag >= 0

    def masked(mask, x):
        return jnp.where(mask, x.reshape(2, gdir, CHUNK, 128), 0.0).reshape(n_slab, CHUNK, 128)

    t_i = lax.broadcasted_iota(jnp.int32, (1, CHUNK, 128), 1)
    s_i = lax.broadcasted_iota(jnp.int32, (1, CHUNK, 128), 2) & (CHUNK - 1)
    blk = (s_i // SUB) == (t_i // SUB)
    eye = jnp.where(s_i == t_i, 1.0, 0.0)

    lhs = jnp.concatenate([kt, rt.astype(BF16)], axis=1)
    kb_bd = jnp.concatenate([_slab_bd(kh.astype(BF16)), _slab_bd(bh.astype(BF16))], axis=1)
    mkb = _bmm_nt(lhs, kb_bd)
    m_mat = masked(strict, mkb[:, :CHUNK, :128])
    p_mat = masked(incl, mkb[:, CHUNK:, :128])
    n_mat = masked(strict, mkb[:, :CHUNK, 128:])
    q_mat = masked(incl, mkb[:, CHUNK:, 128:])
    t_mat = _tri_inverse(n_mat, eye, blk)
    mpv = _mm(jnp.concatenate([m_mat, p_mat], axis=1), vv)
    mv, pv = mpv[:, :CHUNK], mpv[:, CHUNK:]
    tk, uv = _mm_pair(t_mat, kt, mv)
    qtk, quv = _mm_pair(q_mat, tk, uv)
    rc = rt - qtk
    oc = pv - quv
    bd_r = lax.broadcasted_iota(jnp.int32, (1, 128, 128), 1) // 64
    bd_c = lax.broadcasted_iota(jnp.int32, (1, 128, 128), 2) // 64
    bd_mask = bd_r == bd_c
    d_mat = jnp.where(bd_mask, _bmm_tn(tk, bh), 0.0)
    braw = jnp.where(bd_mask,
                     _bmm_tn(jnp.concatenate([vv, uv.astype(BF16)], axis=1),
                             jnp.concatenate([kh, -bh], axis=1)), 0.0)

    cps = nch // spb
    dirs = ((0, False, s0_f, o_f, sfin_f), (1, True, s0_b, o_b, sfin_b))
    s_cur = [s_scr[0], s_scr[1]] if spb == 1 else [None, None]
    for ci in range(nch):
        for d, rev, s0_ref, o_ref, sfin_ref in dirs:
            c = (nch - 1 - ci) if rev else ci
            q, ci_seq = divmod(ci, cps)
            q = (spb - 1 - q) if rev else q
            if spb > 1 and ci_seq == 0:
                s_cur[d] = _rwkv_state_to_bd(s0_ref[q])
            g = slice(d * gdir + 2 * c, d * gdir + 2 * c + 2)
            o = _bmm_nt(rc[g], s_cur[d]) + oc[g]
            o_ref[c * CHUNK:(c + 1) * CHUNK, 0:128] = o[0]
            o_ref[c * CHUNK:(c + 1) * CHUNK, 128:256] = o[1]
            s_cur[d] = (s_cur[d] - _bmm(s_cur[d], d_mat[g]) + braw[g]) * gam[g]
            if spb > 1 and ci_seq == cps - 1:
                _rwkv_state_store(sfin_ref, q, s_cur[d])
    if spb == 1:
        s_scr[0] = s_cur[0]
        s_scr[1] = s_cur[1]

        @pl.when(j == nb - 1)
        def _():
            _rwkv_state_store(sfin_f, 0, s_scr[0])
            _rwkv_state_store(sfin_b, 0, s_scr[1])


def _scan_blocking(nseq, seq_len, max_rows):
    spb = max(1, min(nseq, max_rows // seq_len))
    assert nseq % spb == 0
    tb = min(seq_len, max_rows) * spb
    return spb, tb, (seq_len * spb) // tb


def _rwkv_scan(u_r, prm, s0, nseq, seq_len):
    spb, tb, nb = _scan_blocking(nseq, seq_len, RWKV_TB)
    shared = nb == 1
    hb = tb // POOL_HALO
    nh = u_r.shape[0] // POOL_HALO
    t = u_r.shape[0]
    vec = _full((1, MIX_W))
    low = _full((128, MIX_W))
    st = pl.BlockSpec((spb, 4, 64, 64), lambda b, j: (b, 0, 0, 0))

    def block_specs(jmap):
        blk = lambda b, j: b * nb + jmap(j)
        return [pl.BlockSpec((tb, 1024), lambda b, j: (blk(b, j), 0)),
                pl.BlockSpec((POOL_HALO, 768), lambda b, j: (jnp.maximum(blk(b, j) * hb - 1, 0), 0)),
                pl.BlockSpec((POOL_HALO, 768), lambda b, j: (jnp.minimum((blk(b, j) + 1) * hb, nh - 1), 0))]

    fwd_map = lambda b, j: (b * nb + j, 0)
    bwd_map = lambda b, j: (b * nb + nb - 1 - j, 0)
    in_specs = block_specs(lambda j: j)
    operands = [u_r, u_r, u_r]
    if not shared:
        in_specs += block_specs(lambda j: nb - 1 - j)
        operands += [u_r, u_r, u_r]
    in_specs += [_full((1, 768)), vec, vec, vec] + [vec, vec, low, low] * 2 + [st, st]
    f, b = prm["rwkv0"], prm["rwkv1"]
    operands += [f["mu"], f["kk"], f["ka"], f["rk"], f["w0"], f["a0"], f["bw"], f["ba"],
                 b["w0"], b["a0"], b["bw"], b["ba"], s0[0], s0[1]]
    out_tok = jax.ShapeDtypeStruct((t, MIX_W), F32)
    out_st = jax.ShapeDtypeStruct((nseq, 4, 64, 64), F32)
    return pl.pallas_call(
        functools.partial(_rwkv_kernel, nb=nb, spb=spb, shared=shared),
        grid=(nseq // spb, nb),
        in_specs=in_specs,
        out_specs=[pl.BlockSpec((tb, MIX_W), fwd_map), pl.BlockSpec((tb, MIX_W), fwd_map), st,
                   pl.BlockSpec((tb, MIX_W), bwd_map), pl.BlockSpec((tb, MIX_W), bwd_map), st],
        out_shape=[out_tok, out_tok, out_st, out_tok, out_tok, out_st],
        scratch_shapes=[pltpu.VMEM((2, 2, 128, 128), F32)],
        compiler_params=_cparams(("parallel", "arbitrary")),
        name="rwkv_scan",
    )(*operands)


def _rows_bd4(y, width):
    lane = lax.broadcasted_iota(jnp.int32, y.shape, 2) // width
    z = jnp.zeros_like(y)
    return jnp.concatenate([jnp.where(lane == h, y, z) for h in range(4)], axis=1)


def _block_scans(x, levels):
    rows, width = x.shape
    row = lax.broadcasted_iota(jnp.int32, (rows, width), 0)
    pre, suf = [x], [x]
    for m in range(1, levels + 1):
        size, half = 1 << m, 1 << (m - 1)
        p, s = pre[-1], suf[-1]
        if half < 8:
            off = row & (size - 1)
            p3 = p.reshape(rows // 8, 8, width)
            s3 = s.reshape(rows // 8, 8, width)
            addp = jnp.zeros_like(x)
            adds = jnp.zeros_like(x)
            for o in range(half):
                addp = jnp.where(off == half + o, pltpu.roll(p3, o + 1, 1).reshape(rows, width), addp)
                adds = jnp.where(off == o, pltpu.roll(s3, 8 - (half - o), 1).reshape(rows, width), adds)
        else:
            p4 = p.reshape(rows // size, size, width)
            s4 = s.reshape(rows // size, size, width)
            second = (row & half) != 0
            addp = jnp.where(second, jnp.broadcast_to(p4[:, half - 1:half, :], p4.shape).reshape(rows, width), 0.0)
            adds = jnp.where(second, 0.0, jnp.broadcast_to(s4[:, half:half + 1, :], s4.shape).reshape(rows, width))
        pre.append(p + addp)
        suf.append(s + adds)
    return pre, suf


def _gla_state_in(s4):
    sp = s4.reshape(4 * GLA_DK, 64)
    st = jnp.concatenate([sp, sp], axis=1).T[0:64, :]
    lane_h = lax.broadcasted_iota(jnp.int32, (64, 128), 1) // GLA_DK
    return jnp.concatenate([jnp.where(lane_h == h, st, 0.0) for h in range(4)], axis=0)


def _gla_state_store(sfin_ref, q, s_t):
    m = s_t[0:64] + s_t[64:128] + s_t[128:192] + s_t[192:256]
    nat = jnp.concatenate([m, jnp.zeros_like(m)], axis=0).T
    sfin_ref[q] = nat[:, 0:64].reshape(4, GLA_DK, 64)


def _gla_dir_parts(u_ref, ab_ref, abias_ref, rev):
    la = _log_sigmoid(_dot(u_ref[:, 768:896], ab_ref[...]) + abias_ref[...]) * (1.0 / GLA_GATE_NORM)
    pre, suf = _block_scans(la, 6)
    near, far = (suf, pre) if rev else (pre, suf)
    return dict(la=la, near=near, far=far, b=near[6], q=u_ref[:, 0:128] * float(GLA_DK ** -0.5),
                kg=u_ref[:, 128:256], vg=u_ref[:, 256:512])


def _gla_kernel(*refs, nb, spb, shared):
    if shared:
        u_f, ab_f, abias_f, ab_b, abias_b, s0_f, s0_b, o_f, sfin_f, o_b, sfin_b, s_scr = refs
        u_b = u_f
    else:
        u_f, u_b, ab_f, abias_f, ab_b, abias_b, s0_f, s0_b, o_f, sfin_f, o_b, sfin_b, s_scr = refs
    j = pl.program_id(1)

    if spb == 1:
        @pl.when(j == 0)
        def _():
            s_scr[0] = _gla_state_in(s0_f[0])
            s_scr[1] = _gla_state_in(s0_b[0])

    parts = (_gla_dir_parts(u_f, ab_f, abias_f, False), _gla_dir_parts(u_b, ab_b, abias_b, True))
    nch = u_f.shape[0] // CHUNK
    n3 = 2 * nch

    def cat3(xs):
        return jnp.concatenate([x.reshape(nch, CHUNK, x.shape[1]) for x in xs], axis=0)

    mshape = (2, 1, CHUNK, MIX_W)
    t_i = lax.broadcasted_iota(jnp.int32, mshape, 2)
    s_i = lax.broadcasted_iota(jnp.int32, mshape, 3) & (CHUNK - 1)
    sign = jnp.where(lax.broadcasted_iota(jnp.int32, mshape, 0) >= 1, -1, 1)

    def select(mask, x, other):
        return jnp.where(mask, x.reshape(2, nch, CHUNK, MIX_W),
                         other.reshape(2, nch, CHUNK, MIX_W)).reshape(n3, CHUNK, MIX_W)

    q3 = cat3([p["q"] for p in parts])
    k3 = cat3([p["kg"] for p in parts])
    v3 = cat3([p["vg"] for p in parts])
    att = select(s_i == t_i, _bmm_nt(q3, _rows_bd4(k3.astype(BF16), GLA_DK)), jnp.zeros((n3, CHUNK, MIX_W), F32))
    for lev in range(1, 7):
        size, half = 1 << lev, 1 << (lev - 1)
        same = (t_i // size) == (s_i // size)
        t_late = jnp.where((t_i & (size - 1)) >= half, 1, 0)
        s_late = jnp.where((s_i & (size - 1)) >= half, 1, 0)
        mask = same & ((t_late - s_late) * sign == 1)
        qe = cat3([p["q"] * jnp.exp(p["near"][lev - 1]) for p in parts])
        ke = cat3([p["kg"] * jnp.exp(p["far"][lev - 1] - p["la"]) for p in parts])
        att = select(mask, _bmm_nt(qe, _rows_bd4(ke.astype(BF16), GLA_DK)), att)
    o_intra = _bmm(att, _rows_bd4(v3.astype(BF16), 64))

    b3 = cat3([p["b"] for p in parts])
    b_last = jnp.concatenate([b3[:nch, CHUNK - 1:CHUNK, :], b3[nch:, 0:1, :]], axis=0)
    qb = q3 * jnp.exp(b3)
    bd_r = lax.broadcasted_iota(jnp.int32, (1, MIX_W, 128), 1) // 64
    bd_c = lax.broadcasted_iota(jnp.int32, (1, MIX_W, 128), 2) // GLA_DK
    inc = jnp.where(bd_r == bd_c, _bmm_tn(v3, k3 * jnp.exp(b_last - b3)), 0.0)
    decay = jnp.exp(b_last)

    cps = nch // spb
    dirs = ((0, False, s0_f, o_f, sfin_f), (1, True, s0_b, o_b, sfin_b))
    s_cur = [s_scr[0], s_scr[1]] if spb == 1 else [None, None]
    for ci in range(nch):
        for d, rev, s0_ref, o_ref, sfin_ref in dirs:
            c = (nch - 1 - ci) if rev else ci
            qs, ci_seq = divmod(ci, cps)
            qs = (spb - 1 - qs) if rev else qs
            if spb > 1 and ci_seq == 0:
                s_cur[d] = _gla_state_in(s0_ref[qs])
            g = d * nch + c
            o_ref[c * CHUNK:(c + 1) * CHUNK, :] = o_intra[g] + _dot_nt(qb[g], s_cur[d])
            s_cur[d] = s_cur[d] * decay[g] + inc[g]
            if spb > 1 and ci_seq == cps - 1:
                _gla_state_store(sfin_ref, qs, s_cur[d])
    if spb == 1:
        s_scr[0] = s_cur[0]
        s_scr[1] = s_cur[1]

        @pl.when(j == nb - 1)
        def _():
            _gla_state_store(sfin_f, 0, s_scr[0])
            _gla_state_store(sfin_b, 0, s_scr[1])


def _gla_scan(u_g, prm, s0, nseq, seq_len):
    spb, tb, nb = _scan_blocking(nseq, seq_len, GLA_TB)
    shared = nb == 1
    t = u_g.shape[0]
    fwd_map = lambda b, j: (b * nb + j, 0)
    bwd_map = lambda b, j: (b * nb + nb - 1 - j, 0)
    st = pl.BlockSpec((spb, 4, GLA_DK, 64), lambda b, j: (b, 0, 0, 0))
    in_specs = [pl.BlockSpec((tb, 896), fwd_map)]
    operands = [u_g]
    if not shared:
        in_specs.append(pl.BlockSpec((tb, 896), bwd_map))
        operands.append(u_g)
    in_specs += [_full((128, 128)), _full((1, 128))] * 2 + [st, st]
    operands += [prm["gla_ab0"], prm["gla_abias0"], prm["gla_ab1"], prm["gla_abias1"], s0[0], s0[1]]
    out_tok = jax.ShapeDtypeStruct((t, MIX_W), F32)
    out_st = jax.ShapeDtypeStruct((nseq, 4, GLA_DK, 64), F32)
    return pl.pallas_call(
        functools.partial(_gla_kernel, nb=nb, spb=spb, shared=shared),
        grid=(nseq // spb, nb),
        in_specs=in_specs,
        out_specs=[pl.BlockSpec((tb, MIX_W), fwd_map), st, pl.BlockSpec((tb, MIX_W), bwd_map), st],
        out_shape=[out_tok, out_st, out_tok, out_st],
        scratch_shapes=[pltpu.VMEM((2, 256, 128), F32)],
        compiler_params=_cparams(("parallel", "arbitrary")),
        name="gla_scan",
    )(*operands)


def _merge_kernel(x_ref, mod_ref, g_ref, ya_ref, yb_ref, of_ref, ob_ref, exf_ref, exb_ref, cag_ref,
                  gf_ref, gb_ref, gout_ref, gn_ref, bg_ref, gnorm_ref, wg_ref, wb_ref, wo_ref, x1_ref):
    mod = mod_ref[0]
    x = x_ref[...]
    h = _rms_mod(x, g_ref[...], mod[:, D_MODEL:2 * D_MODEL], mod[:, 0:D_MODEL]).astype(BF16)
    mean64 = _seg_ones(MIX_W, 64, 1.0 / 64.0)
    o = of_ref[...] + ob_ref[...]
    mu = _dot_exact_rhs(o, mean64)
    oc = o - mu
    var = _dot_exact_rhs(oc * oc, mean64)
    gate_c = _dot(jax.nn.sigmoid(cag_ref[...]), bg_ref[...])
    y_c = (oc * lax.rsqrt(var + RWKV_GN_EPS) * gn_ref[...] + exf_ref[...] + exb_ref[...]) * gate_c
    og = gf_ref[...] + gb_ref[...]
    gout = gout_ref[...]
    y_d = (og * lax.rsqrt(_dot_exact_rhs(og * og, mean64) + EPS) * gnorm_ref[...]
           * (gout * jax.nn.sigmoid(gout)))
    ys = (ya_ref[...], yb_ref[...], y_c, y_d)
    merged = jnp.zeros((TM, D_MODEL), F32)
    for i in range(4):
        g0 = P_MIX + i * D_MODEL
        gate = jax.nn.sigmoid(_dot_nt(h, wg_ref[g0:g0 + D_MODEL, :]))
        merged = merged + gate * _dot(ys[i], wb_ref[i])
    out = _dot(merged, wo_ref[...])
    x1_ref[...] = x + mod[:, 2 * D_MODEL:3 * D_MODEL] * out


def _merge(x, mod3, g, y_a, y_b, o_f, o_b, ex_f, ex_b, u_r, g_f, g_b, u_g, prm, big, layer,
           cond_base, rows_per_cond):
    t = x.shape[0]
    row = lambda i: (i, 0)
    mix = pl.BlockSpec((TM, MIX_W), row)
    vec = _full((1, MIX_W))
    lay3 = lambda i: (layer, 0, 0)
    return pl.pallas_call(
        _merge_kernel,
        grid=(t // TM,),
        in_specs=[pl.BlockSpec((TM, D_MODEL), row),
                  pl.BlockSpec((1, 1, 6 * D_MODEL), _cond_map(cond_base, rows_per_cond)),
                  _full((1, D_MODEL)),
                  mix, mix, mix, mix, mix, mix,
                  pl.BlockSpec((TM, 128), lambda i: (i, 7)),
                  mix, mix,
                  pl.BlockSpec((TM, MIX_W), lambda i: (i, 2)),
                  vec, _full((128, MIX_W)), vec,
                  pl.BlockSpec((None, big["w_in_t"].shape[1], D_MODEL), lay3, pipeline_mode=pl.Buffered(1)),
                  pl.BlockSpec((None, 4, MIX_W, D_MODEL), lambda i: (layer, 0, 0, 0)),
                  pl.BlockSpec((None, D_MODEL, D_MODEL), lay3)],
        out_specs=pl.BlockSpec((TM, D_MODEL), row),
        out_shape=jax.ShapeDtypeStruct((t, D_MODEL), F32),
        compiler_params=_cparams(("parallel",)),
        name="merge_out",
    )(x, mod3, g, y_a, y_b, o_f, o_b, ex_f, ex_b, u_r, g_f, g_b, u_g,
      prm["gn"], prm["bg"], prm["gla_norm"], big["w_in_t"], big["w_branch"], big["w_out"])


def _mlp_kernel(x_ref, mod_ref, g_ref, w1_ref, w2_ref, fg_ref, out_ref, *, final):
    mod = mod_ref[0]
    x = x_ref[...]
    h = _rms_mod(x, g_ref[...], mod[:, 4 * D_MODEL:5 * D_MODEL], mod[:, 3 * D_MODEL:4 * D_MODEL]).astype(BF16)
    ff = jnp.zeros(x.shape, F32)
    for c in range(D_FF // D_MODEL):
        cols = slice(c * D_MODEL, (c + 1) * D_MODEL)
        a = jnp.maximum(jnp.dot(h, w1_ref[:, cols], preferred_element_type=F32), 0.0)
        ff = ff + _dot(a * a, w2_ref[cols, :])
    x2 = x + mod[:, 5 * D_MODEL:6 * D_MODEL] * ff
    if final:
        x2 = x2 * lax.rsqrt(jnp.mean(x2 * x2, axis=-1, keepdims=True) + EPS) * fg_ref[...]
    out_ref[...] = x2


def _mlp(x1, mod3, g, w1, w2, layer, final_g, cond_base, rows_per_cond):
    t = x1.shape[0]
    row = lambda i: (i, 0)
    lay3 = lambda i: (layer, 0, 0)
    return pl.pallas_call(
        functools.partial(_mlp_kernel, final=layer == DEPTH - 1),
        grid=(t // TM_WIDE,),
        in_specs=[pl.BlockSpec((TM_WIDE, D_MODEL), row),
                  pl.BlockSpec((1, 1, 6 * D_MODEL), _cond_map(cond_base, rows_per_cond, TM_WIDE)),
                  _full((1, D_MODEL)),
                  pl.BlockSpec((None, D_MODEL, D_FF), lay3, pipeline_mode=pl.Buffered(1)),
                  pl.BlockSpec((None, D_FF, D_MODEL), lay3, pipeline_mode=pl.Buffered(1)),
                  _full((1, D_MODEL))],
        out_specs=pl.BlockSpec((TM_WIDE, D_MODEL), row),
        out_shape=jax.ShapeDtypeStruct((t, D_MODEL), F32),
        compiler_params=_cparams(("parallel",)),
        name="mlp",
    )(x1, mod3, g, w1, w2, final_g)


def _pos_tables(n_tok, d):
    rows = n_tok // GRID_W
    assert rows == GRID_W
    quarter = d // 4
    omega = 1.0 / (POS_BASE ** (jnp.arange(quarter, dtype=F32) / quarter))
    ar = jnp.arange(rows, dtype=F32)[:, None] * omega
    ac = jnp.arange(GRID_W, dtype=F32)[:, None] * omega
    return (jnp.concatenate([jnp.sin(ar), jnp.cos(ar)], axis=-1),
            jnp.concatenate([jnp.sin(ac), jnp.cos(ac)], axis=-1))


def _pad_rows(m, start, total):
    return jnp.zeros((total, m.shape[1]), m.dtype).at[start:start + m.shape[0]].set(m)


def _layer_params(l, p):
    out = {
        "n1": p["norm1_g"][l].reshape(1, D_MODEL),
        "n2": p["norm2_g"][l].reshape(1, D_MODEL),
        "pool_w": jnp.einsum("gcd,gh->gchd", p["pool_w"][l], jnp.eye(4, dtype=F32)).reshape(MIX_W, MIX_W).astype(BF16),
        "pool_scale": p["pool_scale"][l].reshape(1, MIX_W),
        "gn": p["rwkv_gn"][l].reshape(1, MIX_W),
        "bg": _pad_rows(p["rwkv_bg"][l], 64, 128).astype(BF16),
        "gla_norm": p["gla_norm"][l].reshape(1, MIX_W),
    }
    for d in range(2):
        out["rwkv%d" % d] = {
            "mu": p["rwkv_mu"][l].reshape(1, 768),
            "kk": p["rwkv_kk"][l].reshape(1, MIX_W),
            "ka": p["rwkv_ka"][l].reshape(1, MIX_W),
            "rk": p["rwkv_rk"][l].reshape(1, MIX_W),
            "w0": p["rwkv_w0"][l, d].reshape(1, MIX_W),
            "a0": p["rwkv_a0"][l, d].reshape(1, MIX_W),
            "bw": _pad_rows(p["rwkv_bw"][l, d], 64 * d, 128).astype(BF16),
            "ba": _pad_rows(p["rwkv_ba"][l, d], 32 * d, 128).astype(BF16),
        }
        out["gla_ab%d" % d] = _pad_rows(p["gla_ab"][l, d], 16 * d, 128).astype(BF16)
        out["gla_abias%d" % d] = p["gla_abias"][l, d].reshape(1, 128)
    return out


def _run_layer(x, mod3, prm, big, layer, final_g, s_rwkv0, s_gla0, nseq, seq_len, cond_base, rows_per_cond,
               pos_tables=None):
    if pos_tables is None:
        u_p, u_f, u_r, u_g = _inproj(x, mod3, prm["n1"], big["w_in_t"], layer, cond_base, rows_per_cond)
    else:
        x, u_p, u_f, u_r, u_g = _inproj(x, mod3, prm["n1"], big["w_in_t"], layer, cond_base, rows_per_cond,
                                        pos_tables, seq_len)
    y_a = _pool(u_p, prm["pool_w"], prm["pool_scale"], nseq, seq_len)
    if seq_len == FFT_N1 * FFT_N1:
        y_b = _four_large(u_f, nseq, seq_len)
    else:
        y_b = _four_small(u_f, nseq, seq_len)
    o_f, ex_f, sr_f, o_b, ex_b, sr_b = _rwkv_scan(u_r, prm, s_rwkv0, nseq, seq_len)
    g_f, sg_f, g_b, sg_b = _gla_scan(u_g, prm, s_gla0, nseq, seq_len)
    x1 = _merge(x, mod3, prm["n1"], y_a, y_b, o_f, o_b, ex_f, ex_b, u_r, g_f, g_b, u_g, prm, big, layer,
                cond_base, rows_per_cond)
    x2 = _mlp(x1, mod3, prm["n2"], big["w1"], big["w2"], layer, final_g, cond_base, rows_per_cond)
    return x2, jnp.stack([sr_f, sr_b], axis=1), jnp.stack([sg_f, sg_b], axis=1)


def kernel(x_prompt, x_sample, state_rwkv, state_gla, c, c_ctx, ada_w, ada_b, norm1_g, norm2_g, w_in, pool_w, pool_scale, rwkv_mu, rwkv_w0, rwkv_bw, rwkv_a0, rwkv_ba, rwkv_kk, rwkv_ka, rwkv_bg, rwkv_rk, rwkv_gn, gla_ab, gla_abias, gla_norm, w_branch, w_out, mlp_w1, mlp_w2, final_g):
    p = dict(pool_w=pool_w, pool_scale=pool_scale, rwkv_mu=rwkv_mu, rwkv_w0=rwkv_w0,
             rwkv_bw=rwkv_bw, rwkv_a0=rwkv_a0, rwkv_ba=rwkv_ba, rwkv_kk=rwkv_kk, rwkv_ka=rwkv_ka,
             rwkv_bg=rwkv_bg, rwkv_rk=rwkv_rk, rwkv_gn=rwkv_gn, gla_ab=gla_ab, gla_abias=gla_abias,
             gla_norm=gla_norm, norm1_g=norm1_g, norm2_g=norm2_g)
    big = dict(w_in_t=jnp.swapaxes(w_in, 1, 2).astype(BF16),
               w_branch=w_branch.astype(BF16), w_out=w_out.astype(BF16),
               w1=mlp_w1.astype(BF16), w2=mlp_w2.astype(BF16))
    bp, lp, _ = x_prompt.shape
    bs, ls, _ = x_sample.shape
    cond8 = jnp.zeros((8, D_MODEL), F32).at[0].set(c_ctx).at[1:1 + bs].set(c)
    mods = _ada(cond8, ada_w, ada_b)
    fg = final_g.reshape(1, D_MODEL)

    xp = x_prompt.reshape(bp * lp, D_MODEL)
    xs = x_sample.reshape(bs * ls, D_MODEL)
    pos_tables = _pos_tables(ls, D_MODEL)
    zr = jnp.zeros((2, bp) + state_rwkv.shape[3:], F32)
    zg = jnp.zeros((2, bp) + state_gla.shape[3:], F32)
    new_r, new_g = [], []
    for l in range(DEPTH):
        prm = _layer_params(l, p)
        mod3 = mods[l].reshape(8, 1, 6 * D_MODEL)
        xp, s_r, s_g = _run_layer(xp, mod3, prm, big, l, fg, zr, zg, bp, lp, 0, bp * lp)
        new_r.append(s_r)
        new_g.append(s_g)
        xs, _, _ = _run_layer(xs, mod3, prm, big, l, fg, jnp.swapaxes(state_rwkv[:, l], 0, 1),
                              jnp.swapaxes(state_gla[:, l], 0, 1), bs, ls, 1, ls,
                              pos_tables if l == 0 else None)
    y_prompt = xp.reshape(bp, lp, D_MODEL)
    y_sample = xs.reshape(bs, ls, D_MODEL)
    return (y_prompt, y_sample, jnp.stack(new_r, axis=1), jnp.stack(new_g, axis=1))
```

```python
import functools

import numpy as np
import jax
import jax.numpy as jnp
from jax import lax
from jax.experimental import pallas as pl
from jax.experimental.pallas import tpu as pltpu

F32 = jnp.float32
BF16 = jnp.bfloat16

D_MODEL = 1024
DEPTH = 2
GRID_W = 64
POS_BASE = 10000.0
MIX_W = 256
POOL_WINDOWS = (2, 4, 8, 16)
POOL_HALO = 8
RWKV_DECAY_SCALE = 0.606531
RWKV_GN_EPS = 64e-5
GLA_DK = 32
GLA_GATE_NORM = 16.0
D_FF = 4 * D_MODEL
EPS = 1e-6
P_MIX = 2336
P_MIX_PAD = 2432
CHUNK = 64
SUB = 8

TM = 512
TM_WIDE = 1024
POOL_TB = 1024
RWKV_TB = 512
GLA_TB = 1024
FFT_N1 = 64
VMEM_LIMIT = 56 * 1024 * 1024


def _cparams(sem):
    return pltpu.CompilerParams(dimension_semantics=sem, vmem_limit_bytes=VMEM_LIMIT)


def _full(shape):
    n = len(shape)
    return pl.BlockSpec(shape, lambda *_: (0,) * n)


def _dot(a, b):
    return jnp.dot(a.astype(BF16), b.astype(BF16), preferred_element_type=F32)


def _dot_nt(a, b):
    return lax.dot_general(a.astype(BF16), b.astype(BF16), (((1,), (1,)), ((), ())),
                           preferred_element_type=F32)


def _split2(a):
    hi = a.astype(BF16)
    lo = (a - hi.astype(F32)).astype(BF16)
    return hi, lo


def _dot_exact_rhs(a, w):
    hi, lo = _split2(a)
    return (jnp.dot(hi, w, preferred_element_type=F32) + jnp.dot(lo, w, preferred_element_type=F32))


def _dot_exact_lhs(w, a):
    hi, lo = _split2(a)
    return jnp.dot(w, hi, preferred_element_type=F32) + jnp.dot(w, lo, preferred_element_type=F32)


def _seg_ones(n, seg, scale=1.0):
    r = lax.broadcasted_iota(jnp.int32, (n, n), 0)
    c = lax.broadcasted_iota(jnp.int32, (n, n), 1)
    return jnp.where((r // seg) == (c // seg), scale, 0.0).astype(BF16)


def _log_sigmoid(x):
    return jnp.minimum(x, 0.0) - jnp.log(1.0 + jnp.exp(-jnp.abs(x)))


def _rms_mod(x, g, sc, sh):
    ms = jnp.mean(x * x, axis=-1, keepdims=True)
    return x * lax.rsqrt(ms + EPS) * g * (1.0 + sc) + sh


def _ada_kernel(c_ref, w_ref, b_ref, o_ref):
    c = c_ref[...]
    s = c * jax.nn.sigmoid(c)
    hi, lo = _split2(s)
    w = w_ref[0].astype(BF16)
    o_ref[0] = (jnp.dot(hi, w, preferred_element_type=F32) + jnp.dot(lo, w, preferred_element_type=F32)
                + b_ref[0])


def _ada(cond8, ada_w, ada_b):
    tn = 3072
    n = 6 * D_MODEL
    return pl.pallas_call(
        _ada_kernel,
        grid=(DEPTH, n // tn),
        in_specs=[_full((8, D_MODEL)),
                  pl.BlockSpec((1, D_MODEL, tn), lambda l, j: (l, 0, j)),
                  pl.BlockSpec((1, 1, tn), lambda l, j: (l, 0, j))],
        out_specs=pl.BlockSpec((1, 8, tn), lambda l, j: (l, 0, j)),
        out_shape=jax.ShapeDtypeStruct((DEPTH, 8, n), F32),
        compiler_params=_cparams(("parallel", "parallel")),
        name="ada_mod",
    )(cond8, ada_w, ada_b.reshape(DEPTH, 1, n))


def _inproj_kernel(x_ref, mod_ref, g_ref, w_ref, up_ref, uf_ref, ur_ref, ug_ref):
    _inproj_body(x_ref[...], mod_ref, g_ref, w_ref, up_ref, uf_ref, ur_ref, ug_ref)


def _inproj_pos_kernel(x_ref, rr_ref, cc_ref, mod_ref, g_ref, w_ref, xo_ref, up_ref, uf_ref, ur_ref, ug_ref,
                       *, seq_len):
    tile = x_ref.shape[0]
    nrow = tile // GRID_W
    g0 = pl.multiple_of(((pl.program_id(0) * tile) % seq_len) // GRID_W, nrow)
    half = D_MODEL // 2
    rpart = jnp.broadcast_to(rr_ref[pl.ds(g0, nrow), :][:, None, :], (nrow, GRID_W, half)).reshape(tile, half)
    cpart = jnp.concatenate([cc_ref[...]] * nrow, axis=0)
    x = x_ref[...] + jnp.concatenate([rpart, cpart], axis=1)
    xo_ref[...] = x
    _inproj_body(x, mod_ref, g_ref, w_ref, up_ref, uf_ref, ur_ref, ug_ref)


def _inproj_body(x, mod_ref, g_ref, w_ref, up_ref, uf_ref, ur_ref, ug_ref):
    mod = mod_ref[0]
    h = _rms_mod(x, g_ref[...], mod[:, D_MODEL:2 * D_MODEL], mod[:, 0:D_MODEL]).astype(BF16)
    up_ref[...] = _dot_nt(h, w_ref[0:256, :])
    uf_ref[...] = _dot_nt(h, w_ref[256:512, :])
    ur_ref[...] = _dot_nt(h, w_ref[512:1536, :])
    ug_ref[...] = _dot_nt(h, w_ref[1536:P_MIX_PAD, :])


def _cond_map(cond_base, rows_per_cond, tile=TM):
    return lambda i: (cond_base + (i * tile) // rows_per_cond, 0, 0)


def _inproj(x, mod3, g, w_t, layer, cond_base, rows_per_cond, pos_tables=None, seq_len=None):
    t = x.shape[0]
    row = lambda i: (i, 0)
    tm = TM_WIDE
    in_specs = [pl.BlockSpec((1, 1, 6 * D_MODEL), _cond_map(cond_base, rows_per_cond, tm)),
                _full((1, D_MODEL)),
                pl.BlockSpec((None, P_MIX_PAD, D_MODEL), lambda i: (layer, 0, 0), pipeline_mode=pl.Buffered(1))]
    out_specs = [pl.BlockSpec((tm, 256), row), pl.BlockSpec((tm, 256), row),
                 pl.BlockSpec((tm, 1024), row), pl.BlockSpec((tm, 896), row)]
    out_shape = [jax.ShapeDtypeStruct((t, 256), F32), jax.ShapeDtypeStruct((t, 256), F32),
                 jax.ShapeDtypeStruct((t, 1024), F32), jax.ShapeDtypeStruct((t, 896), F32)]
    x_spec = pl.BlockSpec((tm, D_MODEL), row)
    if pos_tables is None:
        return pl.pallas_call(
            _inproj_kernel, grid=(t // tm,), in_specs=[x_spec] + in_specs, out_specs=out_specs,
            out_shape=out_shape, compiler_params=_cparams(("parallel",)), name="in_proj",
        )(x, mod3, g, w_t)
    tab = _full((GRID_W, D_MODEL // 2))
    return pl.pallas_call(
        functools.partial(_inproj_pos_kernel, seq_len=seq_len), grid=(t // tm,),
        in_specs=[x_spec, tab, tab] + in_specs, out_specs=[x_spec] + out_specs,
        out_shape=[jax.ShapeDtypeStruct((t, D_MODEL), F32)] + out_shape,
        compiler_params=_cparams(("parallel",)), name="in_proj_pos",
    )(x, pos_tables[0], pos_tables[1], mod3, g, w_t)


def _pool_kernel(prev_ref, cur_ref, next_ref, wbd_ref, scale_ref, y_ref, *, seq_len, spb):
    j = pl.program_id(1)
    tb = cur_ref.shape[0]
    rows = tb // spb
    nb = seq_len // rows
    cur = cur_ref[...]
    if spb == 1:
        prev = jnp.where(j > 0, prev_ref[...], 0.0)[None]
        nxt = jnp.where(j < nb - 1, next_ref[...], 0.0)[None]
    else:
        prev = nxt = jnp.zeros((spb, POOL_HALO, MIX_W), F32)
    ext = jnp.concatenate([prev, cur.reshape(spb, rows, MIX_W), nxt], axis=1)
    n_ext = rows + 2 * POOL_HALO

    def shift(x, k):
        return pltpu.roll(x, k % n_ext, 1)

    sums = [ext + shift(ext, 1)]
    for k in (1, 2, 4):
        sums.append(shift(sums[-1], -k) + shift(sums[-1], k))
    assert POOL_WINDOWS == (2, 4, 8, 16)
    pos = (j * tb + lax.broadcasted_iota(jnp.int32, (tb, MIX_W), 0)) % seq_len
    group = lax.broadcasted_iota(jnp.int32, (tb, MIX_W), 1) // (MIX_W // len(POOL_WINDOWS))
    pooled = jnp.zeros((tb, MIX_W), F32)
    for gi, win in enumerate(POOL_WINDOWS):
        before, after = win // 2, win - win // 2 - 1
        s = sums[gi][:, POOL_HALO:POOL_HALO + rows].reshape(tb, MIX_W)
        cnt = (jnp.minimum(pos + after, seq_len - 1) - jnp.maximum(pos - before, 0) + 1).astype(F32)
        pooled = jnp.where(group == gi, s / cnt - cur, pooled)
    y_ref[...] = _dot(pooled, wbd_ref[...]) * scale_ref[...]


def _pool(u_p, wbd, scale, nseq, seq_len):
    spb, tb, nb = _scan_blocking(nseq, seq_len, POOL_TB)
    hb = tb // POOL_HALO
    nh = u_p.shape[0] // POOL_HALO
    return pl.pallas_call(
        functools.partial(_pool_kernel, seq_len=seq_len, spb=spb),
        grid=(nseq // spb, nb),
        in_specs=[pl.BlockSpec((POOL_HALO, MIX_W), lambda b, j: (jnp.maximum((b * nb + j) * hb - 1, 0), 0)),
                  pl.BlockSpec((tb, MIX_W), lambda b, j: (b * nb + j, 0)),
                  pl.BlockSpec((POOL_HALO, MIX_W), lambda b, j: (jnp.minimum((b * nb + j + 1) * hb, nh - 1), 0)),
                  _full((MIX_W, MIX_W)), _full((1, MIX_W))],
        out_specs=pl.BlockSpec((tb, MIX_W), lambda b, j: (b * nb + j, 0)),
        out_shape=jax.ShapeDtypeStruct(u_p.shape, F32),
        compiler_params=_cparams(("parallel", "parallel")),
        name="pool_mixer",
    )(u_p, u_p, u_p, wbd, scale)


def _dft_tables(n):
    t = np.arange(n)
    ang = 2.0 * np.pi * ((np.outer(t, t)) % n) / n
    return np.cos(ang), np.sin(ang)


def _bd_np(m, copies):
    k = m.shape[0]
    out = np.zeros((k * copies, k * copies), m.dtype)
    for i in range(copies):
        out[i * k:(i + 1) * k, i * k:(i + 1) * k] = m
    return out


def _four_small_kernel(z_ref, cs_ref, cbd_ref, sbd_ref, y_ref, *, seq_len, norm):
    cs = cs_ref[...].astype(BF16)
    zr, zi = [], []
    for g in range(z_ref.shape[0] // seq_len):
        zf = _dot(cs, z_ref[g * seq_len:(g + 1) * seq_len, :])
        zr.append(zf[:seq_len])
        zi.append(zf[seq_len:])
    y = _dot(jnp.concatenate(zr, axis=0), cbd_ref[...]) + _dot(jnp.concatenate(zi, axis=0), sbd_ref[...])
    y_ref[...] = y * norm


def _four_small(u_f, nseq, seq_len):
    c, s = _dft_tables(seq_len)
    cs = jnp.asarray(np.concatenate([c, -s], axis=0), F32)
    c64, s64 = _dft_tables(MIX_W // 4)
    cbd = jnp.asarray(_bd_np(c64, 4), F32)
    sbd = jnp.asarray(_bd_np(s64, 4), F32)
    per_step = 4 if nseq % 4 == 0 else 1
    rows = per_step * seq_len
    return pl.pallas_call(
        functools.partial(_four_small_kernel, seq_len=seq_len, norm=float((seq_len * 64) ** -0.5)),
        grid=(nseq // per_step,),
        in_specs=[pl.BlockSpec((rows, MIX_W), lambda b: (b, 0)),
                  _full((2 * seq_len, seq_len)), _full((MIX_W, MIX_W)), _full((MIX_W, MIX_W))],
        out_specs=pl.BlockSpec((rows, MIX_W), lambda b: (b, 0)),
        out_shape=jax.ShapeDtypeStruct(u_f.shape, F32),
        compiler_params=_cparams(("parallel",)),
        name="fourier_small",
    )(u_f, cs, cbd, sbd)


def _four_large_kernel(z0_ref, z1_ref, fs_ref, g_ref, cbd_ref, sbd_ref, y_ref, a0_scr, a1_scr, xr_scr, xi_scr,
                       y0_scr, y1_scr, *, norm):
    n1 = FFT_N1
    fs = fs_ref[...].astype(BF16)

    def rows(ref0, ref1, start, stride):
        sl = pl.ds(start, n1, stride=stride)
        return jnp.concatenate([ref0[sl, :], ref1[sl, :]], axis=1)

    for t2 in range(n1):
        a = _dot(fs, rows(z0_ref, z1_ref, t2, n1))
        a0_scr[t2 * 2 * n1:(t2 + 1) * 2 * n1, :] = a[:, :128]
        a1_scr[t2 * 2 * n1:(t2 + 1) * 2 * n1, :] = a[:, 128:]
    for f1 in range(n1):
        a = jnp.concatenate([rows(a0_scr, a1_scr, f1, 2 * n1), rows(a0_scr, a1_scr, n1 + f1, 2 * n1)],
                            axis=0)
        x = _dot(g_ref[f1], a)
        xr_scr[f1 * n1:(f1 + 1) * n1, :] = x[:n1]
        xi_scr[f1 * n1:(f1 + 1) * n1, :] = x[n1:]
    y = (_dot(xr_scr[...], cbd_ref[...]) + _dot(xi_scr[...], sbd_ref[...])) * norm
    y0_scr[...] = y[:, :128]
    y1_scr[...] = y[:, 128:]
    for f2 in range(n1):
        y_ref[f2 * n1:(f2 + 1) * n1, :] = rows(y0_scr, y1_scr, f2, n1)


def _four_large(u_f, nseq, seq_len):
    n1 = FFT_N1
    assert seq_len == n1 * n1
    c64, s64 = _dft_tables(n1)
    fs = jnp.asarray(np.concatenate([c64, -s64], axis=0), F32)
    f1 = np.arange(n1)[:, None, None]
    f2 = np.arange(n1)[None, :, None]
    t2 = np.arange(n1)[None, None, :]
    ang = 2.0 * np.pi * ((t2 * (f1 + n1 * f2)) % seq_len) / seq_len
    gr, gi = np.cos(ang), -np.sin(ang)
    g = jnp.asarray(np.concatenate([np.concatenate([gr, -gi], axis=2),
                                    np.concatenate([gi, gr], axis=2)], axis=1), F32)
    cbd = jnp.asarray(_bd_np(c64, 4), F32)
    sbd = jnp.asarray(_bd_np(s64, 4), F32)
    half = MIX_W // 2
    return pl.pallas_call(
        functools.partial(_four_large_kernel, norm=float((seq_len * 64) ** -0.5)),
        grid=(nseq,),
        in_specs=[pl.BlockSpec((seq_len, half), lambda b: (b, 0)), pl.BlockSpec((seq_len, half), lambda b: (b, 1)),
                  _full((2 * n1, n1)), _full((n1, 2 * n1, 2 * n1)), _full((MIX_W, MIX_W)), _full((MIX_W, MIX_W))],
        out_specs=pl.BlockSpec((seq_len, MIX_W), lambda b: (b, 0)),
        out_shape=jax.ShapeDtypeStruct(u_f.shape, F32),
        scratch_shapes=[pltpu.VMEM((2 * seq_len, half), F32), pltpu.VMEM((2 * seq_len, half), F32),
                        pltpu.VMEM((seq_len, MIX_W), F32), pltpu.VMEM((seq_len, MIX_W), F32),
                        pltpu.VMEM((seq_len, half), F32), pltpu.VMEM((seq_len, half), F32)],
        compiler_params=_cparams(("parallel",)),
        name="fourier_large",
    )(u_f, u_f, fs, g, cbd, sbd)


def _bmm(x, y):
    return lax.dot_general(x.astype(BF16), y.astype(BF16), (((2,), (1,)), ((0,), (0,))),
                           preferred_element_type=F32)


def _bmm_nt(x, y):
    return lax.dot_general(x.astype(BF16), y.astype(BF16), (((2,), (2,)), ((0,), (0,))),
                           preferred_element_type=F32)


def _bmm_tn(x, y):
    return lax.dot_general(x.astype(BF16), y.astype(BF16), (((1,), (1,)), ((0,), (0,))),
                           preferred_element_type=F32)


def _slab_bd(y):
    lane = lax.broadcasted_iota(jnp.int32, y.shape, 2)
    z = jnp.zeros_like(y)
    return jnp.concatenate([jnp.where(lane < 64, y, z), jnp.where(lane >= 64, y, z)], axis=1)


def _mm(x, y):
    return _bmm(x, _slab_bd(y.astype(BF16)))


def _mm_pair(x, y1, y2):
    w = jnp.concatenate([_slab_bd(y1.astype(BF16)), _slab_bd(y2.astype(BF16))], axis=2)
    out = _bmm(x, w)
    return out[:, :, :128], out[:, :, 128:]


def _mm3(xs, y):
    yh, yl = _split2(y)
    bh, bl = _slab_bd(yh), _slab_bd(yl)
    parts = [_split2(x) for x in xs]
    xh = jnp.concatenate([p[0] for p in parts], axis=1)
    xl = jnp.concatenate([p[1] for p in parts], axis=1)
    hx = _bmm(xh, jnp.concatenate([bh, bl], axis=2))
    out = hx[:, :, :128] + (hx[:, :, 128:] + _bmm(xl, bh))
    return [out[:, i * CHUNK:(i + 1) * CHUNK] for i in range(len(xs))]


def _to_slabs(x):
    nch = x.shape[0] // CHUNK
    x4 = x.reshape(nch, CHUNK, MIX_W)
    return jnp.stack([x4[:, :, :128], x4[:, :, 128:]], axis=1).reshape(2 * nch, CHUNK, 128)


def _tri_inverse(n_mat, eye, blk):
    nd = jnp.where(blk, n_mat, 0.0)
    no = n_mat - nd
    n2, = _mm3([nd], nd)
    n4, y = _mm3([n2, nd], n2)
    x = n2 - nd - y
    y, = _mm3([x], n4)
    x = x + n4 + y
    td = eye + x
    e = _mm(td, no)
    e2 = _mm(e, e)
    e4 = _mm(e2, e2)
    g = e2 - e - _mm(e, e2)
    g = g + e4 + _mm(g, e4)
    return td + _mm(g, td)


def _rwkv_state_to_bd(s4):
    bd_r = lax.broadcasted_iota(jnp.int32, (128, 128), 0) // 64
    bd_c = lax.broadcasted_iota(jnp.int32, (128, 128), 1) // 64
    pairs = []
    for hp in range(2):
        sp = s4[2 * hp:2 * hp + 2].reshape(128, 64)
        pairs.append(jnp.where(bd_r == bd_c, jnp.concatenate([sp, sp], axis=1), 0.0))
    return jnp.stack(pairs, axis=0)


def _rwkv_state_store(sfin_ref, q, s_bd):
    for hp in range(2):
        sfin_ref[q, 2 * hp] = s_bd[hp, 0:64, 0:64]
        sfin_ref[q, 2 * hp + 1] = s_bd[hp, 64:128, 64:128]


def _rwkv_prep(cur_ref, prev_ref, next_ref, mu_ref, kkw_ref, jj, nb, spb):
    tb = cur_ref.shape[0]
    seq_rows = tb // spb
    z = cur_ref[:, 0:768]
    rowid = lax.broadcasted_iota(jnp.int32, (tb, 768), 0)
    prow = jnp.where(rowid == 0, jnp.where(jj > 0, prev_ref[POOL_HALO - 1:POOL_HALO, :], 0.0), 0.0)
    nrow = jnp.where(rowid == tb - 1, jnp.where(jj < nb - 1, next_ref[0:1, :], 0.0), 0.0)
    in_seq = rowid & (seq_rows - 1)
    up = jnp.where(in_seq == 0, prow, pltpu.roll(z, 1, 0))
    dn = jnp.where(in_seq == seq_rows - 1, nrow, pltpu.roll(z, tb - 1, 0))
    zm = z + mu_ref[...] * (0.5 * (up + dn) - z)
    r = zm[:, 0:256]
    k = zm[:, 256:512]
    v = zm[:, 512:768]
    kk = k * kkw_ref[...]
    kk = kk * lax.rsqrt(_dot(kk * kk, _seg_ones(MIX_W, 64)) + EPS)
    return r, k, v, kk


def _rwkv_dir_slabs(cur_ref, r, k, v, kk, ka_ref, rk_ref, w0_ref, a0_ref, bw_ref, ba_ref, ex_ref, rev):
    tb = cur_ref.shape[0]
    w_log = w0_ref[...] + _dot(jnp.tanh(cur_ref[:, 768:896]), bw_ref[...])
    lw = -RWKV_DECAY_SCALE * jax.nn.sigmoid(w_log)
    a = jax.nn.sigmoid(a0_ref[...] + _dot(cur_ref[:, 896:1024], ba_ref[...]))
    kd = k * (1.0 + (a - 1.0) * ka_ref[...])
    ex_ref[...] = _dot(r * kd * rk_ref[...], _seg_ones(MIX_W, 64)) * v
    tr = lax.broadcasted_iota(jnp.int32, (tb, tb), 0)
    tc = lax.broadcasted_iota(jnp.int32, (tb, tb), 1)
    same_chunk = (tr // CHUNK) == (tc // CHUNK)
    tri = jnp.where(same_chunk & ((tc >= tr) if rev else (tc <= tr)), 1.0, 0.0).astype(BF16)
    cum = _dot_exact_lhs(tri, lw)
    e_in = jnp.exp(cum)
    e_out = jnp.exp(-cum)
    last = 0 if rev else CHUNK - 1
    return dict(rt=_to_slabs(r * e_in), kh=_to_slabs((kd * e_out).astype(BF16)),
                bh=_to_slabs((kk * a * e_out).astype(BF16)),
                kt=_to_slabs((kk * jnp.exp(cum - lw)).astype(BF16)), vv=_to_slabs(v.astype(BF16)),
                gam=_to_slabs(e_in)[:, last:last + 1, :])


def _rwkv_kernel(*refs, nb, spb, shared):
    if shared:
        (cur_f, prev_f, next_f, mu_ref, kkw_ref, ka_ref, rk_ref, w0_f, a0_f, bw_f, ba_f, w0_b, a0_b, bw_b, ba_b,
         s0_f, s0_b, o_f, ex_f, sfin_f, o_b, ex_b, sfin_b, s_scr) = refs
        cur_b = cur_f
    else:
        (cur_f, prev_f, next_f, cur_b, prev_b, next_b, mu_ref, kkw_ref, ka_ref, rk_ref, w0_f, a0_f, bw_f, ba_f,
         w0_b, a0_b, bw_b, ba_b, s0_f, s0_b, o_f, ex_f, sfin_f, o_b, ex_b, sfin_b, s_scr) = refs
    j = pl.program_id(1)

    if spb == 1:
        @pl.when(j == 0)
        def _():
            s_scr[0] = _rwkv_state_to_bd(s0_f[0])
            s_scr[1] = _rwkv_state_to_bd(s0_b[0])

    pf = _rwkv_prep(cur_f, prev_f, next_f, mu_ref, kkw_ref, j, nb, spb)
    pb = pf if shared else _rwkv_prep(cur_b, prev_b, next_b, mu_ref, kkw_ref, nb - 1 - j, nb, spb)
    sl_f = _rwkv_dir_slabs(cur_f, *pf, ka_ref, rk_ref, w0_f, a0_f, bw_f, ba_f, ex_f, False)
    sl_b = _rwkv_dir_slabs(cur_b, *pb, ka_ref, rk_ref, w0_b, a0_b, bw_b, ba_b, ex_b, True)
    rt, kh, bh, kt, vv, gam = (jnp.concatenate([sl_f[n], sl_b[n]], axis=0)
                               for n in ("rt", "kh", "bh", "kt", "vv", "gam"))
    nch = cur_f.shape[0] // CHUNK
    gdir = 2 * nch
    n_slab = 2 * gdir

    mshape = (2, 1, CHUNK, 128)
    lag = ((lax.broadcasted_iota(jnp.int32, mshape, 2) - (lax.broadcasted_iota(jnp.int32, mshape, 3) & (CHUNK - 1)))
           * jnp.where(lax.broadcasted_iota(jnp.int32, mshape, 0) >= 1, -1, 1))
    strict, incl = lag > 0, lag >= 0

    def masked(mask, x):
        return jnp.where(mask, x.reshape(2, gdir, CHUNK, 128), 0.0).reshape(n_slab, CHUNK, 128)

    t_i = lax.broadcasted_iota(jnp.int32, (1, CHUNK, 128), 1)
    s_i = lax.broadcasted_iota(jnp.int32, (1, CHUNK, 128), 2) & (CHUNK - 1)
    blk = (s_i // SUB) == (t_i // SUB)
    eye = jnp.where(s_i == t_i, 1.0, 0.0)

    lhs = jnp.concatenate([kt, rt.astype(BF16)], axis=1)
    kb_bd = jnp.concatenate([_slab_bd(kh.astype(BF16)), _slab_bd(bh.astype(BF16))], axis=1)
    mkb = _bmm_nt(lhs, kb_bd)
    m_mat = masked(strict, mkb[:, :CHUNK, :128])
    p_mat = masked(incl, mkb[:, CHUNK:, :128])
    n_mat = masked(strict, mkb[:, :CHUNK, 128:])
    q_mat = masked(incl, mkb[:, CHUNK:, 128:])
    t_mat = _tri_inverse(n_mat, eye, blk)
    mpv = _mm(jnp.concatenate([m_mat, p_mat], axis=1), vv)
    mv, pv = mpv[:, :CHUNK], mpv[:, CHUNK:]
    tk, uv = _mm_pair(t_mat, kt, mv)
    qtk, quv = _mm_pair(q_mat, tk, uv)
    rc = rt - qtk
    oc = pv - quv
    bd_r = lax.broadcasted_iota(jnp.int32, (1, 128, 128), 1) // 64
    bd_c = lax.broadcasted_iota(jnp.int32, (1, 128, 128), 2) // 64
    bd_mask = bd_r == bd_c
    d_mat = jnp.where(bd_mask, _bmm_tn(tk, bh), 0.0)
    braw = jnp.where(bd_mask,
                     _bmm_tn(jnp.concatenate([vv, uv.astype(BF16)], axis=1),
                             jnp.concatenate([kh, -bh], axis=1)), 0.0)

    cps = nch // spb
    dirs = ((0, False, s0_f, o_f, sfin_f), (1, True, s0_b, o_b, sfin_b))
    s_cur = [s_scr[0], s_scr[1]] if spb == 1 else [None, None]
    for ci in range(nch):
        for d, rev, s0_ref, o_ref, sfin_ref in dirs:
            c = (nch - 1 - ci) if rev else ci
            q, ci_seq = divmod(ci, cps)
            q = (spb - 1 - q) if rev else q
            if spb > 1 and ci_seq == 0:
                s_cur[d] = _rwkv_state_to_bd(s0_ref[q])
            g = slice(d * gdir + 2 * c, d * gdir + 2 * c + 2)
            o = _bmm_nt(rc[g], s_cur[d]) + oc[g]
            o_ref[c * CHUNK:(c + 1) * CHUNK, 0:128] = o[0]
            o_ref[c * CHUNK:(c + 1) * CHUNK, 128:256] = o[1]
            s_cur[d] = (s_cur[d] - _bmm(s_cur[d], d_mat[g]) + braw[g]) * gam[g]
            if spb > 1 and ci_seq == cps - 1:
                _rwkv_state_store(sfin_ref, q, s_cur[d])
    if spb == 1:
        s_scr[0] = s_cur[0]
        s_scr[1] = s_cur[1]

        @pl.when(j == nb - 1)
        def _():
            _rwkv_state_store(sfin_f, 0, s_scr[0])
            _rwkv_state_store(sfin_b, 0, s_scr[1])


def _scan_blocking(nseq, seq_len, max_rows):
    spb = max(1, min(nseq, max_rows // seq_len))
    assert nseq % spb == 0
    tb = min(seq_len, max_rows) * spb
    return spb, tb, (seq_len * spb) // tb


def _rwkv_scan(u_r, prm, s0, nseq, seq_len):
    spb, tb, nb = _scan_blocking(nseq, seq_len, RWKV_TB)
    shared = nb == 1
    hb = tb // POOL_HALO
    nh = u_r.shape[0] // POOL_HALO
    t = u_r.shape[0]
    vec = _full((1, MIX_W))
    low = _full((128, MIX_W))
    st = pl.BlockSpec((spb, 4, 64, 64), lambda b, j: (b, 0, 0, 0))

    def block_specs(jmap):
        blk = lambda b, j: b * nb + jmap(j)
        return [pl.BlockSpec((tb, 1024), lambda b, j: (blk(b, j), 0)),
                pl.BlockSpec((POOL_HALO, 768), lambda b, j: (jnp.maximum(blk(b, j) * hb - 1, 0), 0)),
                pl.BlockSpec((POOL_HALO, 768), lambda b, j: (jnp.minimum((blk(b, j) + 1) * hb, nh - 1), 0))]

    fwd_map = lambda b, j: (b * nb + j, 0)
    bwd_map = lambda b, j: (b * nb + nb - 1 - j, 0)
    in_specs = block_specs(lambda j: j)
    operands = [u_r, u_r, u_r]
    if not shared:
        in_specs += block_specs(lambda j: nb - 1 - j)
        operands += [u_r, u_r, u_r]
    in_specs += [_full((1, 768)), vec, vec, vec] + [vec, vec, low, low] * 2 + [st, st]
    f, b = prm["rwkv0"], prm["rwkv1"]
    operands += [f["mu"], f["kk"], f["ka"], f["rk"], f["w0"], f["a0"], f["bw"], f["ba"],
                 b["w0"], b["a0"], b["bw"], b["ba"], s0[0], s0[1]]
    out_tok = jax.ShapeDtypeStruct((t, MIX_W), F32)
    out_st = jax.ShapeDtypeStruct((nseq, 4, 64, 64), F32)
    return pl.pallas_call(
        functools.partial(_rwkv_kernel, nb=nb, spb=spb, shared=shared),
        grid=(nseq // spb, nb),
        in_specs=in_specs,
        out_specs=[pl.BlockSpec((tb, MIX_W), fwd_map), pl.BlockSpec((tb, MIX_W), fwd_map), st,
                   pl.BlockSpec((tb, MIX_W), bwd_map), pl.BlockSpec((tb, MIX_W), bwd_map), st],
        out_shape=[out_tok, out_tok, out_st, out_tok, out_tok, out_st],
        scratch_shapes=[pltpu.VMEM((2, 2, 128, 128), F32)],
        compiler_params=_cparams(("parallel", "arbitrary")),
        name="rwkv_scan",
    )(*operands)


def _rows_bd4(y, width):
    lane = lax.broadcasted_iota(jnp.int32, y.shape, 2) // width
    z = jnp.zeros_like(y)
    return jnp.concatenate([jnp.where(lane == h, y, z) for h in range(4)], axis=1)


def _block_scans(x, levels):
    rows, width = x.shape
    row = lax.broadcasted_iota(jnp.int32, (rows, width), 0)
    pre, suf = [x], [x]
    for m in range(1, levels + 1):
        size, half = 1 << m, 1 << (m - 1)
        p, s = pre[-1], suf[-1]
        if half < 8:
            off = row & (size - 1)
            p3 = p.reshape(rows // 8, 8, width)
            s3 = s.reshape(rows // 8, 8, width)
            addp = jnp.zeros_like(x)
            adds = jnp.zeros_like(x)
            for o in range(half):
                addp = jnp.where(off == half + o, pltpu.roll(p3, o + 1, 1).reshape(rows, width), addp)
                adds = jnp.where(off == o, pltpu.roll(s3, 8 - (half - o), 1).reshape(rows, width), adds)
        else:
            p4 = p.reshape(rows // size, size, width)
            s4 = s.reshape(rows // size, size, width)
            second = (row & half) != 0
            addp = jnp.where(second, jnp.broadcast_to(p4[:, half - 1:half, :], p4.shape).reshape(rows, width), 0.0)
            adds = jnp.where(second, 0.0, jnp.broadcast_to(s4[:, half:half + 1, :], s4.shape).reshape(rows, width))
        pre.append(p + addp)
        suf.append(s + adds)
    return pre, suf


def _gla_state_in(s4):
    sp = s4.reshape(4 * GLA_DK, 64)
    st = jnp.concatenate([sp, sp], axis=1).T[0:64, :]
    lane_h = lax.broadcasted_iota(jnp.int32, (64, 128), 1) // GLA_DK
    return jnp.concatenate([jnp.where(lane_h == h, st, 0.0) for h in range(4)], axis=0)


def _gla_state_store(sfin_ref, q, s_t):
    m = s_t[0:64] + s_t[64:128] + s_t[128:192] + s_t[192:256]
    nat = jnp.concatenate([m, jnp.zeros_like(m)], axis=0).T
    sfin_ref[q] = nat[:, 0:64].reshape(4, GLA_DK, 64)


def _gla_dir_parts(u_ref, ab_ref, abias_ref, rev):
    la = _log_sigmoid(_dot(u_ref[:, 768:896], ab_ref[...]) + abias_ref[...]) * (1.0 / GLA_GATE_NORM)
    pre, suf = _block_scans(la, 6)
    near, far = (suf, pre) if rev else (pre, suf)
    return dict(la=la, near=near, far=far, b=near[6], q=u_ref[:, 0:128] * float(GLA_DK ** -0.5),
                kg=u_ref[:, 128:256], vg=u_ref[:, 256:512])


def _gla_kernel(*refs, nb, spb, shared):
    if shared:
        u_f, ab_f, abias_f, ab_b, abias_b, s0_f, s0_b, o_f, sfin_f, o_b, sfin_b, s_scr = refs
        u_b = u_f
    else:
        u_f, u_b, ab_f, abias_f, ab_b, abias_b, s0_f, s0_b, o_f, sfin_f, o_b, sfin_b, s_scr = refs
    j = pl.program_id(1)

    if spb == 1:
        @pl.when(j == 0)
        def _():
            s_scr[0] = _gla_state_in(s0_f[0])
            s_scr[1] = _gla_state_in(s0_b[0])

    parts = (_gla_dir_parts(u_f, ab_f, abias_f, False), _gla_dir_parts(u_b, ab_b, abias_b, True))
    nch = u_f.shape[0] // CHUNK
    n3 = 2 * nch

    def cat3(xs):
        return jnp.concatenate([x.reshape(nch, CHUNK, x.shape[1]) for x in xs], axis=0)

    mshape = (2, 1, CHUNK, MIX_W)
    t_i = lax.broadcasted_iota(jnp.int32, mshape, 2)
    s_i = lax.broadcasted_iota(jnp.int32, mshape, 3) & (CHUNK - 1)
    sign = jnp.where(lax.broadcasted_iota(jnp.int32, mshape, 0) >= 1, -1, 1)

    def select(mask, x, other):
        return jnp.where(mask, x.reshape(2, nch, CHUNK, MIX_W),
                         other.reshape(2, nch, CHUNK, MIX_W)).reshape(n3, CHUNK, MIX_W)

    q3 = cat3([p["q"] for p in parts])
    k3 = cat3([p["kg"] for p in parts])
    v3 = cat3([p["vg"] for p in parts])
    att = select(s_i == t_i, _bmm_nt(q3, _rows_bd4(k3.astype(BF16), GLA_DK)), jnp.zeros((n3, CHUNK, MIX_W), F32))
    for lev in range(1, 7):
        size, half = 1 << lev, 1 << (lev - 1)
        same = (t_i // size) == (s_i // size)
        t_late = jnp.where((t_i & (size - 1)) >= half, 1, 0)
        s_late = jnp.where((s_i & (size - 1)) >= half, 1, 0)
        mask = same & ((t_late - s_late) * sign == 1)
        qe = cat3([p["q"] * jnp.exp(p["near"][lev - 1]) for p in parts])
        ke = cat3([p["kg"] * jnp.exp(p["far"][lev - 1] - p["la"]) for p in parts])
        att = select(mask, _bmm_nt(qe, _rows_bd4(ke.astype(BF16), GLA_DK)), att)
    o_intra = _bmm(att, _rows_bd4(v3.astype(BF16), 64))

    b3 = cat3([p["b"] for p in parts])
    b_last = jnp.concatenate([b3[:nch, CHUNK - 1:CHUNK, :], b3[nch:, 0:1, :]], axis=0)
    qb = q3 * jnp.exp(b3)
    bd_r = lax.broadcasted_iota(jnp.int32, (1, MIX_W, 128), 1) // 64
    bd_c = lax.broadcasted_iota(jnp.int32, (1, MIX_W, 128), 2) // GLA_DK
    inc = jnp.where(bd_r == bd_c, _bmm_tn(v3, k3 * jnp.exp(b_last - b3)), 0.0)
    decay = jnp.exp(b_last)

    cps = nch // spb
    dirs = ((0, False, s0_f, o_f, sfin_f), (1, True, s0_b, o_b, sfin_b))
    s_cur = [s_scr[0], s_scr[1]] if spb == 1 else [None, None]
    for ci in range(nch):
        for d, rev, s0_ref, o_ref, sfin_ref in dirs:
            c = (nch - 1 - ci) if rev else ci
            qs, ci_seq = divmod(ci, cps)
            qs = (spb - 1 - qs) if rev else qs
            if spb > 1 and ci_seq == 0:
                s_cur[d] = _gla_state_in(s0_ref[qs])
            g = d * nch + c
            o_ref[c * CHUNK:(c + 1) * CHUNK, :] = o_intra[g] + _dot_nt(qb[g], s_cur[d])
            s_cur[d] = s_cur[d] * decay[g] + inc[g]
            if spb > 1 and ci_seq == cps - 1:
                _gla_state_store(sfin_ref, qs, s_cur[d])
    if spb == 1:
        s_scr[0] = s_cur[0]
        s_scr[1] = s_cur[1]

        @pl.when(j == nb - 1)
        def _():
            _gla_state_store(sfin_f, 0, s_scr[0])
            _gla_state_store(sfin_b, 0, s_scr[1])


def _gla_scan(u_g, prm, s0, nseq, seq_len):
    spb, tb, nb = _scan_blocking(nseq, seq_len, GLA_TB)
    shared = nb == 1
    t = u_g.shape[0]
    fwd_map = lambda b, j: (b * nb + j, 0)
    bwd_map = lambda b, j: (b * nb + nb - 1 - j, 0)
    st = pl.BlockSpec((spb, 4, GLA_DK, 64), lambda b, j: (b, 0, 0, 0))
    in_specs = [pl.BlockSpec((tb, 896), fwd_map)]
    operands = [u_g]
    if not shared:
        in_specs.append(pl.BlockSpec((tb, 896), bwd_map))
        operands.append(u_g)
    in_specs += [_full((128, 128)), _full((1, 128))] * 2 + [st, st]
    operands += [prm["gla_ab0"], prm["gla_abias0"], prm["gla_ab1"], prm["gla_abias1"], s0[0], s0[1]]
    out_tok = jax.ShapeDtypeStruct((t, MIX_W), F32)
    out_st = jax.ShapeDtypeStruct((nseq, 4, GLA_DK, 64), F32)
    return pl.pallas_call(
        functools.partial(_gla_kernel, nb=nb, spb=spb, shared=shared),
        grid=(nseq // spb, nb),
        in_specs=in_specs,
        out_specs=[pl.BlockSpec((tb, MIX_W), fwd_map), st, pl.BlockSpec((tb, MIX_W), bwd_map), st],
        out_shape=[out_tok, out_st, out_tok, out_st],
        scratch_shapes=[pltpu.VMEM((2, 256, 128), F32)],
        compiler_params=_cparams(("parallel", "arbitrary")),
        name="gla_scan",
    )(*operands)


def _merge_kernel(x_ref, mod_ref, g_ref, ya_ref, yb_ref, of_ref, ob_ref, exf_ref, exb_ref, cag_ref,
                  gf_ref, gb_ref, gout_ref, gn_ref, bg_ref, gnorm_ref, wg_ref, wb_ref, wo_ref, x1_ref):
    mod = mod_ref[0]
    x = x_ref[...]
    h = _rms_mod(x, g_ref[...], mod[:, D_MODEL:2 * D_MODEL], mod[:, 0:D_MODEL]).astype(BF16)
    mean64 = _seg_ones(MIX_W, 64, 1.0 / 64.0)
    o = of_ref[...] + ob_ref[...]
    mu = _dot_exact_rhs(o, mean64)
    oc = o - mu
    var = _dot_exact_rhs(oc * oc, mean64)
    gate_c = _dot(jax.nn.sigmoid(cag_ref[...]), bg_ref[...])
    y_c = (oc * lax.rsqrt(var + RWKV_GN_EPS) * gn_ref[...] + exf_ref[...] + exb_ref[...]) * gate_c
    og = gf_ref[...] + gb_ref[...]
    gout = gout_ref[...]
    y_d = (og * lax.rsqrt(_dot_exact_rhs(og * og, mean64) + EPS) * gnorm_ref[...]
           * (gout * jax.nn.sigmoid(gout)))
    ys = (ya_ref[...], yb_ref[...], y_c, y_d)
    merged = jnp.zeros((TM, D_MODEL), F32)
    for i in range(4):
        g0 = P_MIX + i * D_MODEL
        gate = jax.nn.sigmoid(_dot_nt(h, wg_ref[g0:g0 + D_MODEL, :]))
        merged = merged + gate * _dot(ys[i], wb_ref[i])
    out = _dot(merged, wo_ref[...])
    x1_ref[...] = x + mod[:, 2 * D_MODEL:3 * D_MODEL] * out


def _merge(x, mod3, g, y_a, y_b, o_f, o_b, ex_f, ex_b, u_r, g_f, g_b, u_g, prm, big, layer,
           cond_base, rows_per_cond):
    t = x.shape[0]
    row = lambda i: (i, 0)
    mix = pl.BlockSpec((TM, MIX_W), row)
    vec = _full((1, MIX_W))
    lay3 = lambda i: (layer, 0, 0)
    return pl.pallas_call(
        _merge_kernel,
        grid=(t // TM,),
        in_specs=[pl.BlockSpec((TM, D_MODEL), row),
                  pl.BlockSpec((1, 1, 6 * D_MODEL), _cond_map(cond_base, rows_per_cond)),
                  _full((1, D_MODEL)),
                  mix, mix, mix, mix, mix, mix,
                  pl.BlockSpec((TM, 128), lambda i: (i, 7)),
                  mix, mix,
                  pl.BlockSpec((TM, MIX_W), lambda i: (i, 2)),
                  vec, _full((128, MIX_W)), vec,
                  pl.BlockSpec((None, big["w_in_t"].shape[1], D_MODEL), lay3, pipeline_mode=pl.Buffered(1)),
                  pl.BlockSpec((None, 4, MIX_W, D_MODEL), lambda i: (layer, 0, 0, 0)),
                  pl.BlockSpec((None, D_MODEL, D_MODEL), lay3)],
        out_specs=pl.BlockSpec((TM, D_MODEL), row),
        out_shape=jax.ShapeDtypeStruct((t, D_MODEL), F32),
        compiler_params=_cparams(("parallel",)),
        name="merge_out",
    )(x, mod3, g, y_a, y_b, o_f, o_b, ex_f, ex_b, u_r, g_f, g_b, u_g,
      prm["gn"], prm["bg"], prm["gla_norm"], big["w_in_t"], big["w_branch"], big["w_out"])


def _mlp_kernel(x_ref, mod_ref, g_ref, w1_ref, w2_ref, fg_ref, out_ref, *, final):
    mod = mod_ref[0]
    x = x_ref[...]
    h = _rms_mod(x, g_ref[...], mod[:, 4 * D_MODEL:5 * D_MODEL], mod[:, 3 * D_MODEL:4 * D_MODEL]).astype(BF16)
    ff = jnp.zeros(x.shape, F32)
    for c in range(D_FF // D_MODEL):
        cols = slice(c * D_MODEL, (c + 1) * D_MODEL)
        a = jnp.maximum(jnp.dot(h, w1_ref[:, cols], preferred_element_type=F32), 0.0)
        ff = ff + _dot(a * a, w2_ref[cols, :])
    x2 = x + mod[:, 5 * D_MODEL:6 * D_MODEL] * ff
    if final:
        x2 = x2 * lax.rsqrt(jnp.mean(x2 * x2, axis=-1, keepdims=True) + EPS) * fg_ref[...]
    out_ref[...] = x2


def _mlp(x1, mod3, g, w1, w2, layer, final_g, cond_base, rows_per_cond):
    t = x1.shape[0]
    row = lambda i: (i, 0)
    lay3 = lambda i: (layer, 0, 0)
    return pl.pallas_call(
        functools.partial(_mlp_kernel, final=layer == DEPTH - 1),
        grid=(t // TM_WIDE,),
        in_specs=[pl.BlockSpec((TM_WIDE, D_MODEL), row),
                  pl.BlockSpec((1, 1, 6 * D_MODEL), _cond_map(cond_base, rows_per_cond, TM_WIDE)),
                  _full((1, D_MODEL)),
                  pl.BlockSpec((None, D_MODEL, D_FF), lay3, pipeline_mode=pl.Buffered(1)),
                  pl.BlockSpec((None, D_FF, D_MODEL), lay3, pipeline_mode=pl.Buffered(1)),
                  _full((1, D_MODEL))],
        out_specs=pl.BlockSpec((TM_WIDE, D_MODEL), row),
        out_shape=jax.ShapeDtypeStruct((t, D_MODEL), F32),
        compiler_params=_cparams(("parallel",)),
        name="mlp",
    )(x1, mod3, g, w1, w2, final_g)


def _pos_tables(n_tok, d):
    rows = n_tok // GRID_W
    assert rows == GRID_W
    quarter = d // 4
    omega = 1.0 / (POS_BASE ** (jnp.arange(quarter, dtype=F32) / quarter))
    ar = jnp.arange(rows, dtype=F32)[:, None] * omega
    ac = jnp.arange(GRID_W, dtype=F32)[:, None] * omega
    return (jnp.concatenate([jnp.sin(ar), jnp.cos(ar)], axis=-1),
            jnp.concatenate([jnp.sin(ac), jnp.cos(ac)], axis=-1))


def _pad_rows(m, start, total):
    return jnp.zeros((total, m.shape[1]), m.dtype).at[start:start + m.shape[0]].set(m)


def _layer_params(l, p):
    out = {
        "n1": p["norm1_g"][l].reshape(1, D_MODEL),
        "n2": p["norm2_g"][l].reshape(1, D_MODEL),
        "pool_w": jnp.einsum("gcd,gh->gchd", p["pool_w"][l], jnp.eye(4, dtype=F32)).reshape(MIX_W, MIX_W).astype(BF16),
        "pool_scale": p["pool_scale"][l].reshape(1, MIX_W),
        "gn": p["rwkv_gn"][l].reshape(1, MIX_W),
        "bg": _pad_rows(p["rwkv_bg"][l], 64, 128).astype(BF16),
        "gla_norm": p["gla_norm"][l].reshape(1, MIX_W),
    }
    for d in range(2):
        out["rwkv%d" % d] = {
            "mu": p["rwkv_mu"][l].reshape(1, 768),
            "kk": p["rwkv_kk"][l].reshape(1, MIX_W),
            "ka": p["rwkv_ka"][l].reshape(1, MIX_W),
            "rk": p["rwkv_rk"][l].reshape(1, MIX_W),
            "w0": p["rwkv_w0"][l, d].reshape(1, MIX_W),
            "a0": p["rwkv_a0"][l, d].reshape(1, MIX_W),
            "bw": _pad_rows(p["rwkv_bw"][l, d], 64 * d, 128).astype(BF16),
            "ba": _pad_rows(p["rwkv_ba"][l, d], 32 * d, 128).astype(BF16),
        }
        out["gla_ab%d" % d] = _pad_rows(p["gla_ab"][l, d], 16 * d, 128).astype(BF16)
        out["gla_abias%d" % d] = p["gla_abias"][l, d].reshape(1, 128)
    return out


def _run_layer(x, mod3, prm, big, layer, final_g, s_rwkv0, s_gla0, nseq, seq_len, cond_base, rows_per_cond,
               pos_tables=None):
    if pos_tables is None:
        u_p, u_f, u_r, u_g = _inproj(x, mod3, prm["n1"], big["w_in_t"], layer, cond_base, rows_per_cond)
    else:
        x, u_p, u_f, u_r, u_g = _inproj(x, mod3, prm["n1"], big["w_in_t"], layer, cond_base, rows_per_cond,
                                        pos_tables, seq_len)
    y_a = _pool(u_p, prm["pool_w"], prm["pool_scale"], nseq, seq_len)
    if seq_len == FFT_N1 * FFT_N1:
        y_b = _four_large(u_f, nseq, seq_len)
    else:
        y_b = _four_small(u_f, nseq, seq_len)
    o_f, ex_f, sr_f, o_b, ex_b, sr_b = _rwkv_scan(u_r, prm, s_rwkv0, nseq, seq_len)
    g_f, sg_f, g_b, sg_b = _gla_scan(u_g, prm, s_gla0, nseq, seq_len)
    x1 = _merge(x, mod3, prm["n1"], y_a, y_b, o_f, o_b, ex_f, ex_b, u_r, g_f, g_b, u_g, prm, big, layer,
                cond_base, rows_per_cond)
    x2 = _mlp(x1, mod3, prm["n2"], big["w1"], big["w2"], layer, final_g, cond_base, rows_per_cond)
    return x2, jnp.stack([sr_f, sr_b], axis=1), jnp.stack([sg_f, sg_b], axis=1)


def kernel(x_prompt, x_sample, state_rwkv, state_gla, c, c_ctx, ada_w, ada_b, norm1_g, norm2_g, w_in, pool_w, pool_scale, rwkv_mu, rwkv_w0, rwkv_bw, rwkv_a0, rwkv_ba, rwkv_kk, rwkv_ka, rwkv_bg, rwkv_rk, rwkv_gn, gla_ab, gla_abias, gla_norm, w_branch, w_out, mlp_w1, mlp_w2, final_g):
    p = dict(pool_w=pool_w, pool_scale=pool_scale, rwkv_mu=rwkv_mu, rwkv_w0=rwkv_w0,
             rwkv_bw=rwkv_bw, rwkv_a0=rwkv_a0, rwkv_ba=rwkv_ba, rwkv_kk=rwkv_kk, rwkv_ka=rwkv_ka,
             rwkv_bg=rwkv_bg, rwkv_rk=rwkv_rk, rwkv_gn=rwkv_gn, gla_ab=gla_ab, gla_abias=gla_abias,
             gla_norm=gla_norm, norm1_g=norm1_g, norm2_g=norm2_g)
    big = dict(w_in_t=jnp.swapaxes(w_in, 1, 2).astype(BF16),
               w_branch=w_branch.astype(BF16), w_out=w_out.astype(BF16),
               w1=mlp_w1.astype(BF16), w2=mlp_w2.astype(BF16))
    bp, lp, _ = x_prompt.shape
    bs, ls, _ = x_sample.shape
    cond8 = jnp.zeros((8, D_MODEL), F32).at[0].set(c_ctx).at[1:1 + bs].set(c)
    mods = _ada(cond8, ada_w, ada_b)
    fg = final_g.reshape(1, D_MODEL)

    xp = x_prompt.reshape(bp * lp, D_MODEL)
    xs = x_sample.reshape(bs * ls, D_MODEL)
    pos_tables = _pos_tables(ls, D_MODEL)
    zr = jnp.zeros((2, bp) + state_rwkv.shape[3:], F32)
    zg = jnp.zeros((2, bp) + state_gla.shape[3:], F32)
    new_r, new_g = [], []
    for l in range(DEPTH):
        prm = _layer_params(l, p)
        mod3 = mods[l].reshape(8, 1, 6 * D_MODEL)
        xp, s_r, s_g = _run_layer(xp, mod3, prm, big, l, fg, zr, zg, bp, lp, 0, bp * lp)
        new_r.append(s_r)
        new_g.append(s_g)
        xs, _, _ = _run_layer(xs, mod3, prm, big, l, fg, jnp.swapaxes(state_rwkv[:, l], 0, 1),
                              jnp.swapaxes(state_gla[:, l], 0, 1), bs, ls, 1, ls,
                              pos_tables if l == 0 else None)
    y_prompt = xp.reshape(bp, lp, D_MODEL)
    y_sample = xs.reshape(bs, ls, D_MODEL)
    return (y_prompt, y_sample, jnp.stack(new_r, axis=1), jnp.stack(new_g, axis=1))
```
